```python
import math
import jax
import jax.numpy as jnp
from jax import lax
import numpy as np

D_MODEL = 1024
BATCH = 1
SEQ = 16384
DEPTH = 2
DEC_BATCH = 8
DEC_SEQ = 32
PAST_LEN = 4096

CHUNK = 64
EPS = 1e-6
H_A = 4
DK_A = 64
DV_A = 128
GATE_RANK_A = 16
GATE_NORM_A = 16.0
W_A = H_A * DV_A
H_B = 8
KVH_B = 2
G_B = H_B // KVH_B
HD_B = 64
W_B = H_B * HD_B
WINDOW = 128
WINDOW_CHUNKS = WINDOW // CHUNK
N_BUCKETS = 32
MAX_DISTANCE = 128
D_INNER_C = 2 * D_MODEL
P_C = 64
H_C = D_INNER_C // P_C
G_C = 4
HPG_C = H_C // G_C
N_C = 128
CONV_W = 4
CONV_DIM = D_INNER_C + 2 * G_C * N_C
N_AB = (DEPTH + 1) // 2
N_CL = DEPTH // 2
SPLIT_AB = (H_A * DK_A, H_A * DK_A, W_A, GATE_RANK_A, W_A, W_B, KVH_B * HD_B, KVH_B * HD_B, W_B)
D_IN_AB = 2 * H_A * DK_A + 2 * W_A + GATE_RANK_A + 2 * W_B + 2 * KVH_B * HD_B
SPLIT_C = (D_INNER_C, CONV_DIM, H_C)
D_IN_C = D_INNER_C + CONV_DIM + H_C

kernel_name = 'hybrid_gla_swa_ssd_stream_step'

F32 = jnp.float32


def _split(p, sizes):
    idx, acc = [], 0
    for s in sizes[:-1]:
        acc += s
        idx.append(acc)
    return jnp.split(p, idx, axis=-1)


def rms_norm(x, g):
    xf = x.astype(F32)
    y = xf * lax.rsqrt(jnp.mean(xf * xf, axis=-1, keepdims=True) + EPS)
    return (y * g.astype(F32)).astype(x.dtype)


def rel_pos_bias(table, q_off, k_off):
    n = q_off[:, None] - k_off[None, :]
    half = N_BUCKETS // 2
    max_exact = half // 2
    side = jnp.where(n < 0, half, 0)
    n = jnp.abs(n)
    nf = jnp.maximum(n, max_exact).astype(F32)
    large = max_exact + (jnp.log(nf / max_exact) / math.log(MAX_DISTANCE / max_exact) * (half - max_exact)).astype(jnp.int32)
    large = jnp.minimum(large, half - 1)
    bucket = side + jnp.where(n < max_exact, n, large)
    return jnp.moveaxis(table[bucket].astype(F32), -1, 0)


def sink_softmax(s, sink):
    m = jnp.maximum(jnp.max(s, axis=-1, keepdims=True), sink)
    p = jnp.exp(s - m)
    return p / (jnp.sum(p, axis=-1, keepdims=True) + jnp.exp(sink - m))


def swa_prompt(q, k, v, sink, table):
    B, L = q.shape[:2]
    nc = L // CHUNK
    qc = q.reshape(B, nc, CHUNK, KVH_B, G_B, HD_B)
    pad = jnp.zeros((B, WINDOW_CHUNKS, CHUNK, KVH_B, HD_B), k.dtype)
    kp = jnp.concatenate([pad, k.reshape(B, nc, CHUNK, KVH_B, HD_B)], axis=1)
    vp = jnp.concatenate([pad, v.reshape(B, nc, CHUNK, KVH_B, HD_B)], axis=1)
    kband = jnp.concatenate([kp[:, j:j + nc] for j in range(WINDOW_CHUNKS + 1)], axis=2)
    vband = jnp.concatenate([vp[:, j:j + nc] for j in range(WINDOW_CHUNKS + 1)], axis=2)
    kb = kband.shape[2]
    bias = rel_pos_bias(table, jnp.arange(CHUNK), jnp.arange(kb) - WINDOW_CHUNKS * CHUNK)
    s = jnp.einsum('bnqhgd,bnkhd->bnhgqk', qc, kband) * HD_B ** -0.5 + bias.reshape(KVH_B, G_B, CHUNK, kb)
    valid = (jnp.arange(nc)[:, None] - WINDOW_CHUNKS + jnp.arange(kb)[None, :] // CHUNK) >= 0
    s = jnp.where(valid[None, :, None, None, None, :], s, -jnp.inf)
    p = sink_softmax(s, sink.astype(F32).reshape(KVH_B, G_B, 1, 1))
    o = jnp.einsum('bnhgqk,bnkhd->bnqhgd', p, vband)
    return o.reshape(B, L, W_B)


def swa_sample(q, k, v, k_cache, v_cache, sink, table):
    B, S = q.shape[:2]
    wc = k_cache.shape[1]
    kall = jnp.concatenate([k_cache.astype(F32), k], axis=1)
    vall = jnp.concatenate([v_cache.astype(F32), v], axis=1)
    bias = rel_pos_bias(table, jnp.arange(S), jnp.arange(wc + S) - wc)
    qg = q.reshape(B, S, KVH_B, G_B, HD_B)
    s = jnp.einsum('bqhgd,bkhd->bhgqk', qg, kall) * HD_B ** -0.5 + bias.reshape(KVH_B, G_B, S, wc + S)
    p = sink_softmax(s, sink.astype(F32).reshape(KVH_B, G_B, 1, 1))
    o = jnp.einsum('bhgqk,bkhd->bqhgd', p, vall)
    return o.reshape(B, S, W_B)


def gla_scan(q, k, v, g, S0):
    B, L, H = q.shape[:3]
    C = min(CHUNK, L)
    nc = L // C
    causal = jnp.tril(jnp.ones((C, C), bool))

    def to_chunks(a):
        return jnp.moveaxis(a.reshape(B, nc, C, *a.shape[2:]), 1, 0)

    def body(S, inp):
        qc, kc, vc, gc = inp
        b = jnp.cumsum(gc, axis=1)
        qe = qc * jnp.exp(b)
        att = jnp.einsum('bthk,bshk->bhts', qe, kc * jnp.exp(-b))
        att = jnp.where(causal, att, 0.0)
        o = jnp.einsum('bhts,bshv->bthv', att, vc) + jnp.einsum('bthk,bhkv->bthv', qe, S)
        bl = b[:, -1]
        S = jnp.exp(bl)[..., None] * S + jnp.einsum('bshk,bshv->bhkv', kc * jnp.exp(bl[:, None] - b), vc)
        return S, o

    S, o = lax.scan(body, S0, (to_chunks(q), to_chunks(k), to_chunks(v), to_chunks(g)))
    return jnp.moveaxis(o, 0, 1).reshape(B, L, H, v.shape[-1]), S


def ssd_scan(x, Bm, Cm, dt, A, S0):
    Bsz, L = x.shape[:2]
    C = min(CHUNK, L)
    nc = L // C
    causal = jnp.tril(jnp.ones((C, C), bool))[None, :, :, None, None]

    def to_chunks(a):
        return jnp.moveaxis(a.reshape(Bsz, nc, C, *a.shape[2:]), 1, 0)

    def body(S, inp):
        xc, bc, cc, dtc = inp
        acum = jnp.cumsum(dtc * A, axis=1)
        seg = acum[:, :, None] - acum[:, None, :]
        lm = jnp.exp(jnp.where(causal, seg, -jnp.inf))
        cb = jnp.einsum('btgn,bsgn->btsg', cc, bc)
        y = jnp.einsum('btsg,btsgh,bsgh,bsghp->btghp', cb, lm, dtc, xc)
        y = y + jnp.einsum('btgn,bghpn->btghp', cc, S) * jnp.exp(acum)[..., None]
        al = acum[:, -1]
        S = jnp.exp(al)[..., None, None] * S + jnp.einsum('bsgn,bsgh,bsghp->bghpn', bc, jnp.exp(al[:, None] - acum) * dtc, xc)
        return S, y

    S, y = lax.scan(body, S0, (to_chunks(x), to_chunks(Bm), to_chunks(Cm), to_chunks(dt)))
    return jnp.moveaxis(y, 0, 1).reshape(x.shape), S


def causal_conv(u, conv_state, w, b):
    up = jnp.concatenate([conv_state.astype(u.dtype), u], axis=1)
    y = lax.conv_general_dilated(up, w.astype(u.dtype)[:, None, :], window_strides=(1,), padding='VALID',
                                 dimension_numbers=('NWC', 'WIO', 'NWC'), feature_group_count=u.shape[-1])
    return y + b.astype(u.dtype), up[:, -(CONV_W - 1):]


def ab_mixer(x, rms_g, w_in, w_gate_up, b_gate, g_out_a, g_q, g_k, sink, table, w_out, S0, k_cache, v_cache):
    B, L, _ = x.shape
    h = rms_norm(x, rms_g)
    q_a, k_a, v_a, glow, z_a, q_b, k_b, v_b, z_b = _split(h @ w_in, SPLIT_AB)
    q_a = q_a.reshape(B, L, H_A, DK_A).astype(F32) * DK_A ** -0.5
    k_a = k_a.reshape(B, L, H_A, DK_A).astype(F32)
    v_a = v_a.reshape(B, L, H_A, DV_A).astype(F32)
    g_a = jax.nn.log_sigmoid((glow @ w_gate_up + b_gate).astype(F32)) / GATE_NORM_A
    o_a, S = gla_scan(q_a, k_a, v_a, g_a.reshape(B, L, H_A, DK_A), S0.astype(F32))
    o_a = rms_norm(o_a, g_out_a) * jax.nn.silu(z_a.astype(F32)).reshape(B, L, H_A, DV_A)
    o_a = o_a.reshape(B, L, W_A)
    q_b = rms_norm(q_b.reshape(B, L, H_B, HD_B).astype(F32), g_q)
    k_b = rms_norm(k_b.reshape(B, L, KVH_B, HD_B).astype(F32), g_k)
    v_b = v_b.reshape(B, L, KVH_B, HD_B).astype(F32)
    if k_cache is None:
        o_b = swa_prompt(q_b, k_b, v_b, sink, table)
        rows = min(WINDOW, L)
        k_rows, v_rows = k_b[:, -rows:], v_b[:, -rows:]
    else:
        o_b = swa_sample(q_b, k_b, v_b, k_cache, v_cache, sink, table)
        k_rows, v_rows = k_b, v_b
    o_b = o_b * jax.nn.silu(z_b.astype(F32))
    o = jnp.concatenate([o_a, o_b], axis=-1).astype(x.dtype)
    return x + o @ w_out, S, k_rows, v_rows


def c_mixer(x, rms_g, w_in, conv_w, conv_b, dt_bias, a_log, d_skip, g_y, w_out, S0, conv_state):
    B, L, _ = x.shape
    h = rms_norm(x, rms_g)
    z, xbc, dt = _split(h @ w_in, SPLIT_C)
    xbc, new_conv = causal_conv(xbc, conv_state, conv_w, conv_b)
    xbc = jax.nn.silu(xbc.astype(F32))
    xs, Bm, Cm = _split(xbc, (D_INNER_C, G_C * N_C, G_C * N_C))
    xs = xs.reshape(B, L, G_C, HPG_C, P_C)
    Bm = Bm.reshape(B, L, G_C, N_C)
    Cm = Cm.reshape(B, L, G_C, N_C)
    dt = jax.nn.softplus(dt.astype(F32) + dt_bias.astype(F32)).reshape(B, L, G_C, HPG_C)
    A = -jnp.exp(a_log.astype(F32)).reshape(G_C, HPG_C)
    S0 = S0.astype(F32).reshape(S0.shape[0], G_C, HPG_C, P_C, N_C)
    y, S = ssd_scan(xs, Bm, Cm, dt, A, S0)
    y = y + d_skip.astype(F32).reshape(G_C, HPG_C)[..., None] * xs
    y = y.reshape(B, L, G_C, D_INNER_C // G_C) * jax.nn.silu(z.astype(F32)).reshape(B, L, G_C, D_INNER_C // G_C)
    y = rms_norm(y, g_y.reshape(G_C, D_INNER_C // G_C)).reshape(B, L, D_INNER_C)
    return x + y.astype(x.dtype) @ w_out, S.reshape(B, H_C, P_C, N_C), new_conv


def setup_inputs(seed: int = 0) -> dict:
    key = jax.random.key(seed)
    ks = jax.random.split(key, 32)

    def nrm(i, shape, scale):
        return jax.random.normal(ks[i], shape, F32) * scale

    win_rows = min(WINDOW, PAST_LEN)
    dt0 = jnp.exp(jax.random.uniform(ks[20], (N_CL, H_C), F32) * (math.log(0.1) - math.log(0.001)) + math.log(0.001))
    dt_bias = dt0 + jnp.log(-jnp.expm1(-dt0))
    return {
        'x_prompt': nrm(0, (BATCH, SEQ, D_MODEL), 1.0),
        'x_sample': nrm(1, (DEC_BATCH, DEC_SEQ, D_MODEL), 1.0),
        'cache_swa_k': nrm(2, (N_AB, DEC_BATCH, win_rows, KVH_B, HD_B), 1.0),
        'cache_swa_v': nrm(3, (N_AB, DEC_BATCH, win_rows, KVH_B, HD_B), 1.0),
        'state_gla': nrm(4, (N_AB, DEC_BATCH, H_A, DK_A, DV_A), 0.5),
        'state_ssd': nrm(5, (N_CL, DEC_BATCH, H_C, P_C, N_C), 0.5),
        'state_conv': nrm(6, (N_CL, DEC_BATCH, CONV_W - 1, CONV_DIM), 1.0),
        'rms_g': 1.0 + nrm(7, (DEPTH, D_MODEL), 0.02),
        'w_in_ab': nrm(8, (N_AB, D_MODEL, D_IN_AB), D_MODEL ** -0.5),
        'w_gate_up_a': nrm(9, (N_AB, GATE_RANK_A, H_A * DK_A), GATE_RANK_A ** -0.5),
        'b_gate_a': nrm(10, (N_AB, H_A * DK_A), 0.1),
        'g_out_a': 1.0 + nrm(11, (N_AB, DV_A), 0.02),
        'g_q_b': 1.0 + nrm(12, (N_AB, HD_B), 0.02),
        'g_k_b': 1.0 + nrm(13, (N_AB, HD_B), 0.02),
        'sink_b': nrm(14, (N_AB, H_B), 0.5),
        'rel_bias': nrm(15, (N_BUCKETS, H_B), 0.5),
        'w_out_ab': nrm(16, (N_AB, W_A + W_B, D_MODEL), (W_A + W_B) ** -0.5),
        'w_in_c': nrm(17, (N_CL, D_MODEL, D_IN_C), D_MODEL ** -0.5),
        'conv_w_c': nrm(18, (N_CL, CONV_W, CONV_DIM), CONV_W ** -0.5),
        'conv_b_c': nrm(19, (N_CL, CONV_DIM), 0.02),
        'dt_bias_c': dt_bias,
        'a_log_c': jnp.log(jax.random.uniform(ks[21], (N_CL, H_C), F32, 1.0, 16.0)),
        'd_skip_c': 1.0 + nrm(22, (N_CL, H_C), 0.1),
        'g_y_c': 1.0 + nrm(23, (N_CL, D_INNER_C), 0.02),
        'w_out_c': nrm(24, (N_CL, D_INNER_C, D_MODEL), D_INNER_C ** -0.5),
    }


def reference(x_prompt, x_sample, cache_swa_k, cache_swa_v, state_gla, state_ssd, state_conv,
              rms_g, w_in_ab, w_gate_up_a, b_gate_a, g_out_a, g_q_b, g_k_b, sink_b, rel_bias, w_out_ab,
              w_in_c, conv_w_c, conv_b_c, dt_bias_c, a_log_c, d_skip_c, g_y_c, w_out_c):
    yp, ys = x_prompt, x_sample
    bp = x_prompt.shape[0]
    gla_p, kp, vp, ssd_p, conv_p = [], [], [], [], []
    gla_s, ks_, vs_, ssd_s, conv_s = [], [], [], [], []
    for layer in range(DEPTH):
        if layer % 2 == 0:
            e = layer // 2
            w = (rms_g[layer], w_in_ab[e], w_gate_up_a[e], b_gate_a[e], g_out_a[e], g_q_b[e], g_k_b[e],
                 sink_b[e], rel_bias, w_out_ab[e])
            yp, S, kr, vr = ab_mixer(yp, *w, jnp.zeros((bp, H_A, DK_A, DV_A), F32), None, None)
            gla_p.append(S)
            kp.append(kr)
            vp.append(vr)
            ys, S, kr, vr = ab_mixer(ys, *w, state_gla[e], cache_swa_k[e], cache_swa_v[e])
            gla_s.append(S)
            ks_.append(kr)
            vs_.append(vr)
        else:
            o = layer // 2
            w = (rms_g[layer], w_in_c[o], conv_w_c[o], conv_b_c[o], dt_bias_c[o], a_log_c[o], d_skip_c[o],
                 g_y_c[o], w_out_c[o])
            yp, S, cv = c_mixer(yp, *w, jnp.zeros((bp, H_C, P_C, N_C), F32),
                                jnp.zeros((bp, CONV_W - 1, CONV_DIM), x_prompt.dtype))
            ssd_p.append(S)
            conv_p.append(cv)
            ys, S, cv = c_mixer(ys, *w, state_ssd[o], state_conv[o])
            ssd_s.append(S)
            conv_s.append(cv)
    return (yp, ys,
            jnp.stack(gla_p), jnp.stack(kp), jnp.stack(vp), jnp.stack(ssd_p), jnp.stack(conv_p),
            jnp.stack(gla_s), jnp.stack(ks_), jnp.stack(vs_), jnp.stack(ssd_s), jnp.stack(conv_s))
```

```python
import functools
import math

import numpy as np
import jax
import jax.numpy as jnp
from jax import lax
from jax.experimental import pallas as pl
from jax.experimental.pallas import tpu as pltpu

F32 = jnp.float32
BF16 = jnp.bfloat16

D_MODEL = 1024
CHUNK = 64
EPS = 1e-6
H_A = 4
DK_A = 64
DV_A = 128
GATE_RANK_A = 16
GATE_NORM_A = 16.0
H_B = 8
KVH_B = 2
G_B = H_B // KVH_B
HD_B = 64
WINDOW = 128
N_BUCKETS = 32
MAX_DISTANCE = 128
D_INNER_C = 2048
P_C = 64
H_C = 32
G_C = 4
HPG_C = 8
N_C = 128
CONV_W = 4
CONV_DIM = D_INNER_C + 2 * G_C * N_C

LANES = 128

A_Q, A_K, A_V, A_Z = 0, 256, 512, 1024
B_Q, B_K, B_V, B_Z = 1536, 2048, 2176, 2304
A_G = 2816
AB_COLS = 2944
C_Z, C_X, C_B, C_C, C_DT = 0, 2048, 4096, 4608, 5120
C_COLS = 5248

PROMPT_BLOCK = 256
VMEM_LIMIT = 56 * 1024 * 1024


def _dot(a, b):
    return jnp.dot(a.astype(BF16), b.astype(BF16), preferred_element_type=F32)


def _dot_nt(a, b):
    return lax.dot_general(a.astype(BF16), b.astype(BF16), (((1,), (1,)), ((), ())),
                           preferred_element_type=F32)


def _dot_tn(a, b):
    return lax.dot_general(a.astype(BF16), b.astype(BF16), (((0,), (0,)), ((), ())),
                           preferred_element_type=F32)


def _split_bf16(x, terms):
    out = []
    r = x
    for _ in range(terms):
        h = r.astype(BF16)
        out.append(h)
        r = r - h.astype(F32)
    return out


def _sel_dot(x, sel, terms):
    acc = None
    for h in _split_bf16(x, terms):
        d = jnp.dot(h, sel, preferred_element_type=F32)
        acc = d if acc is None else acc + d
    return acc


def _sel_dot_left(sel, x, terms):
    acc = None
    for h in _split_bf16(x, terms):
        d = jnp.dot(sel, h, preferred_element_type=F32)
        acc = d if acc is None else acc + d
    return acc


def _sel_dot_tn(x, sel, terms):
    acc = None
    for h in _split_bf16(x, terms):
        d = lax.dot_general(h, sel, (((0,), (0,)), ((), ())), preferred_element_type=F32)
        acc = d if acc is None else acc + d
    return acc


def _rms_rows(x):
    return x * lax.rsqrt(jnp.mean(x * x, axis=-1, keepdims=True) + EPS)


def _silu(x):
    return x * (1.0 / (1.0 + jnp.exp(-x)))


def _softplus(x):
    return jnp.maximum(x, 0.0) + jnp.log(1.0 + jnp.exp(-jnp.abs(x)))


def _log_sigmoid(x):
    return jnp.minimum(x, 0.0) - jnp.log(1.0 + jnp.exp(-jnp.abs(x)))


def _tril(n):
    r = lax.broadcasted_iota(jnp.int32, (n, n), 0)
    c = lax.broadcasted_iota(jnp.int32, (n, n), 1)
    return r >= c


def _ab_dense_in(x, rms_g, w_in_ref, gsum512, gsum128, gq, gk, proj_ref, qn_ref):
    h = _rms_rows(x) * rms_g
    proj_ref[...] = jnp.dot(h.astype(BF16), w_in_ref[...], preferred_element_type=F32)
    qb = proj_ref[:, B_Q:B_Q + 512]
    msq = _sel_dot(qb * qb, gsum512, 2) * (1.0 / HD_B)
    qn_ref[...] = qb * lax.rsqrt(msq + EPS) * gq
    kb = proj_ref[:, B_K:B_K + 128]
    msk = _sel_dot(kb * kb, gsum128, 2) * (1.0 / HD_B)
    kn = kb * lax.rsqrt(msk + EPS) * gk
    vb = proj_ref[:, B_V:B_V + 128]
    return kn, vb


def _ab_chunk(r, c, kb, proj_ref, qn_ref, o_ref, s_in, k_band, v_band, bias_ref, sink_ref,
              w_gu, b_gate, g_out, first_valid_col):
    rows = pl.ds(r, c)
    gate = _dot(proj_ref[rows, A_G:A_G + LANES], w_gu) + b_gate
    g = _log_sigmoid(gate) * (1.0 / GATE_NORM_A)
    tril = _tril(c)
    b = _sel_dot_left(tril.astype(BF16), g, 3)
    bl = b[c - 1:c, :]
    q = proj_ref[rows, A_Q:A_Q + 256] * (DK_A ** -0.5)
    k = proj_ref[rows, A_K:A_K + 256]
    v = proj_ref[rows, A_V:A_V + 512]
    qe = q * jnp.exp(b)
    kd = (k * jnp.exp(-b)).astype(BF16)
    kd2 = (k * jnp.exp(bl - b)).astype(BF16)
    bl_rows = _sel_dot_tn(g, jnp.ones((c, LANES), BF16), 3)
    lane128 = lax.broadcasted_iota(jnp.int32, (c, LANES), 1)
    row128 = lax.broadcasted_iota(jnp.int32, (LANES, LANES), 0)
    s_out = []
    for p in range(2):
        qp = qe[:, p * 128:(p + 1) * 128]
        kp = kd[:, p * 128:(p + 1) * 128]
        vp = v[:, p * 256:(p + 1) * 256].astype(BF16)
        sp = s_in[p * 128:(p + 1) * 128, :]
        sp_bf = sp.astype(BF16)
        for j in range(2):
            hd = 2 * p + j
            qm = jnp.where((lane128 // DK_A) == j, qp, 0.0).astype(BF16)
            att = jnp.where(tril, _dot_nt(qm, kp), 0.0)
            oh = _dot(att, vp[:, j * 128:(j + 1) * 128]) + _dot(qm, sp_bf)
            z = proj_ref[rows, A_Z + hd * 128:A_Z + (hd + 1) * 128]
            oh = _rms_rows(oh) * g_out * _silu(z)
            o_ref[rows, hd * 128:(hd + 1) * 128] = oh.astype(BF16)
        upd = _dot_tn(kd2[:, p * 128:(p + 1) * 128], vp)
        upd = jnp.where(row128 < DK_A, upd[:, :128], upd[:, 128:])
        s_out.append(jnp.exp(bl_rows[p * 128:(p + 1) * 128, :]) * sp + upd)
    lane256 = lax.broadcasted_iota(jnp.int32, (c, 256), 1) // HD_B
    srow = lax.broadcasted_iota(jnp.int32, (G_B * c, 1), 0) // c
    for kvh in range(KVH_B):
        qn = qn_ref[rows, kvh * 256:(kvh + 1) * 256]
        qs = jnp.concatenate([jnp.where(lane256 == gq_, qn, 0.0) for gq_ in range(G_B)], axis=0)
        s = _dot_nt(qs, k_band[kvh]) * (HD_B ** -0.5) + bias_ref[kvh]
        if first_valid_col is not None:
            col = lax.broadcasted_iota(jnp.int32, (G_B * c, kb), 1)
            s = jnp.where(col >= first_valid_col, s, -jnp.inf)
        sink = jnp.zeros((G_B * c, 1), F32)
        for gq_ in range(G_B):
            sink = jnp.where(srow == gq_, sink_ref[kvh * G_B + gq_], sink)
        m = jnp.maximum(jnp.max(s, axis=-1, keepdims=True), sink)
        pr = jnp.exp(s - m)
        den = jnp.sum(pr, axis=-1, keepdims=True) + jnp.exp(sink - m)
        ost = _dot(pr, v_band[kvh]) / den
        ob = jnp.zeros((c, 256), F32)
        for gq_ in range(G_B):
            ob = ob + jnp.where(lane256 == gq_, ost[gq_ * c:(gq_ + 1) * c, :], 0.0)
        z = proj_ref[rows, B_Z + kvh * 256:B_Z + (kvh + 1) * 256]
        o_ref[rows, 512 + kvh * 256:512 + (kvh + 1) * 256] = (ob * _silu(z)).astype(BF16)
    return jnp.concatenate(s_out, axis=0)


def _ab_prompt_kernel(sink_ref, x_ref, rms_g_ref, w_in_ref, w_gu_ref, b_gate_ref, g_out_ref, gq_ref,
                      gk_ref, bias_ref, w_out_ref, gsum512_ref, gsum128_ref, tile_ref,
                      y_ref, s_out_ref, k_out_ref, v_out_ref,
                      proj_ref, qn_ref, o_ref, kband_ref, vband_ref, s_ref):
    t = x_ref.shape[0]
    c = CHUNK
    kb = WINDOW + c
    nchunk = t // c
    step = pl.program_id(0)

    @pl.when(step == 0)
    def _():
        s_ref[...] = jnp.zeros_like(s_ref)
        kband_ref[:, 0:WINDOW, :] = jnp.zeros((KVH_B, WINDOW, 256), BF16)
        vband_ref[:, 0:WINDOW, :] = jnp.zeros((KVH_B, WINDOW, 256), BF16)

    x = x_ref[...]
    kn, vb = _ab_dense_in(x, rms_g_ref[...], w_in_ref, gsum512_ref[...], gsum128_ref[...],
                          gq_ref[...], gk_ref[...], proj_ref, qn_ref)
    k_out_ref[...] = kn[t - WINDOW:, :]
    v_out_ref[...] = vb[t - WINDOW:, :]
    for kvh in range(KVH_B):
        kband_ref[kvh, WINDOW:WINDOW + t, :] = jnp.dot(
            kn.astype(BF16), tile_ref[kvh], preferred_element_type=F32).astype(BF16)
        vband_ref[kvh, WINDOW:WINDOW + t, :] = jnp.dot(
            vb.astype(BF16), tile_ref[kvh], preferred_element_type=F32).astype(BF16)

    w_gu = w_gu_ref[...]
    b_gate = b_gate_ref[...]
    g_out = g_out_ref[...]

    def chunk(i, carry):
        r = pl.multiple_of(i * c, c)
        k_band = [kband_ref[kvh, pl.ds(r, kb), :] for kvh in range(KVH_B)]
        v_band = [vband_ref[kvh, pl.ds(r, kb), :] for kvh in range(KVH_B)]
        first_valid = (2 - (step * nchunk + i)) * c
        s_ref[...] = _ab_chunk(r, c, kb, proj_ref, qn_ref, o_ref, s_ref[...], k_band, v_band,
                               bias_ref, sink_ref, w_gu, b_gate, g_out, first_valid)
        return carry

    lax.fori_loop(0, nchunk, chunk, 0)
    for kvh in range(KVH_B):
        kband_ref[kvh, 0:WINDOW, :] = kband_ref[kvh, t:t + WINDOW, :]
        vband_ref[kvh, 0:WINDOW, :] = vband_ref[kvh, t:t + WINDOW, :]
    y_ref[...] = x + jnp.dot(o_ref[...], w_out_ref[...], preferred_element_type=F32)
    s_out_ref[...] = s_ref[...]


def _ab_sample_kernel(sink_ref, x_ref, rms_g_ref, w_in_ref, w_gu_ref, b_gate_ref, g_out_ref, gq_ref,
                      gk_ref, bias_ref, w_out_ref, gsum512_ref, gsum128_ref, tile_ref,
                      s_in_ref, kc_ref, vc_ref,
                      y_ref, s_out_ref, k_out_ref, v_out_ref,
                      proj_ref, qn_ref, o_ref, kband_ref, vband_ref):
    nb, wc = kc_ref.shape[0], kc_ref.shape[1]
    t = x_ref.shape[0]
    c = t // nb
    kb = wc + c
    x = x_ref[...]
    kn, vb = _ab_dense_in(x, rms_g_ref[...], w_in_ref, gsum512_ref[...], gsum128_ref[...],
                          gq_ref[...], gk_ref[...], proj_ref, qn_ref)
    k_out_ref[...] = kn
    v_out_ref[...] = vb
    for kvh in range(KVH_B):
        kt = jnp.dot(kn.astype(BF16), tile_ref[kvh], preferred_element_type=F32).astype(BF16)
        vt = jnp.dot(vb.astype(BF16), tile_ref[kvh], preferred_element_type=F32).astype(BF16)
        for bi in range(nb):
            kband_ref[kvh, bi, 0:wc, :] = jnp.dot(
                kc_ref[bi].astype(BF16), tile_ref[kvh], preferred_element_type=F32).astype(BF16)
            vband_ref[kvh, bi, 0:wc, :] = jnp.dot(
                vc_ref[bi].astype(BF16), tile_ref[kvh], preferred_element_type=F32).astype(BF16)
            kband_ref[kvh, bi, wc:kb, :] = kt[bi * c:(bi + 1) * c, :]
            vband_ref[kvh, bi, wc:kb, :] = vt[bi * c:(bi + 1) * c, :]

    w_gu = w_gu_ref[...]
    b_gate = b_gate_ref[...]
    g_out = g_out_ref[...]

    def seq(bi, carry):
        r = pl.multiple_of(bi * c, c)
        k_band = [kband_ref[kvh, bi] for kvh in range(KVH_B)]
        v_band = [vband_ref[kvh, bi] for kvh in range(KVH_B)]
        s_out_ref[bi] = _ab_chunk(r, c, kb, proj_ref, qn_ref, o_ref, s_in_ref[bi], k_band, v_band,
                                  bias_ref, sink_ref, w_gu, b_gate, g_out, None)
        return carry

    lax.fori_loop(0, nb, seq, 0)
    y_ref[...] = x + jnp.dot(o_ref[...], w_out_ref[...], preferred_element_type=F32)


def _c_chunk(r, c, dt_ref, da_ref, act_ref, proj_ref, o_ref, st_in, e_s, e_p, dskip, g_y):
    rows = pl.ds(r, c)
    dtc = dt_ref[rows, :]
    acum = _sel_dot_left(_tril(c).astype(BF16), da_ref[rows, :], 3)
    ws = H_C * c
    xa_s = _sel_dot(acum, e_s, 3)
    dt_s = _sel_dot(dtc, e_s, 2)
    row = lax.broadcasted_iota(jnp.int32, (c, ws), 0)
    lane_s = lax.broadcasted_iota(jnp.int32, (c, ws), 1) % c
    diag = row == lane_s
    a_row = jnp.sum(jnp.where(diag, xa_s, 0.0), axis=0, keepdims=True)
    dt_row = jnp.sum(jnp.where(diag, dt_s, 0.0), axis=0, keepdims=True)
    wmat = jnp.exp(jnp.where(row >= lane_s, xa_s - a_row, -jnp.inf)) * dt_row
    if c == P_C:
        xa_p, dt_p = xa_s, dt_s
    else:
        xa_p = _sel_dot(acum, e_p, 3)
        dt_p = _sel_dot(dtc, e_p, 2)
    al_p = xa_p[c - 1:c, :]
    ea_p = jnp.exp(xa_p)
    xs = act_ref[rows, 0:D_INNER_C]
    xw = (xs * (jnp.exp(al_p - xa_p) * dt_p)).astype(BF16)
    dec = jnp.exp(al_p)
    gw = D_INNER_C // G_C
    rowblk = lax.broadcasted_iota(jnp.int32, (4 * c, 256), 0) // c
    colblk = lax.broadcasted_iota(jnp.int32, (4 * c, 256), 1) // P_C
    blockdiag = rowblk == colblk
    st_out = []
    for g in range(G_C):
        bg = act_ref[rows, D_INNER_C + g * N_C:D_INNER_C + (g + 1) * N_C].astype(BF16)
        cg = act_ref[rows, D_INNER_C + G_C * N_C + g * N_C:
                     D_INNER_C + G_C * N_C + (g + 1) * N_C].astype(BF16)
        cb = _dot_nt(cg, jnp.concatenate([bg] * HPG_C, axis=0))
        mg = (cb * wmat[:, g * HPG_C * c:(g + 1) * HPG_C * c]).astype(BF16)
        st_g = st_in[:, g * gw:(g + 1) * gw]
        y_inter = _dot(cg, st_g) * ea_p[:, g * gw:(g + 1) * gw]
        ys = []
        for j in range(2):
            xsj = xs[:, g * gw + j * 256:g * gw + (j + 1) * 256]
            bd = jnp.where(blockdiag, jnp.concatenate([xsj] * 4, axis=0), 0.0).astype(BF16)
            ys.append(jnp.dot(mg[:, j * 4 * c:(j + 1) * 4 * c], bd, preferred_element_type=F32))
        y = jnp.concatenate(ys, axis=1) + y_inter
        xg = xs[:, g * gw:(g + 1) * gw]
        y = y + dskip[:, g * gw:(g + 1) * gw] * xg
        y = y * _silu(proj_ref[rows, C_Z + g * gw:C_Z + (g + 1) * gw])
        y = _rms_rows(y) * g_y[:, g * gw:(g + 1) * gw]
        o_ref[rows, g * gw:(g + 1) * gw] = y.astype(BF16)
        st_out.append(st_g * dec[:, g * gw:(g + 1) * gw] + _dot_tn(bg, xw[:, g * gw:(g + 1) * gw]))
    return jnp.concatenate(st_out, axis=1)


def _c_dense_in(x, rms_g, w_in_ref, dt_bias, a_log, proj_ref, dt_ref, da_ref):
    h = _rms_rows(x) * rms_g
    proj_ref[...] = jnp.dot(h.astype(BF16), w_in_ref[...], preferred_element_type=F32)
    dt = _softplus(proj_ref[:, C_DT:C_DT + LANES] + dt_bias)
    dt_ref[...] = dt
    da_ref[...] = dt * (-jnp.exp(a_log))


def _conv_rows(ubuf, nrows, conv_w, conv_b):
    acc = conv_b
    for i in range(CONV_W):
        acc = acc + conv_w[i:i + 1, :] * ubuf[pl.ds(8 - (CONV_W - 1) + i, nrows), :]
    return _silu(acc)


def _c_prompt_kernel(x_ref, rms_g_ref, w_in_ref, conv_w_ref, conv_b_ref, dt_bias_ref, a_log_ref,
                     dskip_ref, g_y_ref, w_out_ref, e_s_ref,
                     y_ref, st_out_ref, conv_out_ref,
                     proj_ref, ubuf_ref, act_ref, dt_ref, da_ref, o_ref, st_ref):
    t = x_ref.shape[0]
    c = CHUNK
    step = pl.program_id(0)

    @pl.when(step == 0)
    def _():
        st_ref[...] = jnp.zeros_like(st_ref)
        ubuf_ref[0:8, :] = jnp.zeros((8, CONV_DIM), F32)

    x = x_ref[...]
    _c_dense_in(x, rms_g_ref[...], w_in_ref, dt_bias_ref[...], a_log_ref[...], proj_ref, dt_ref, da_ref)
    ubuf_ref[8:8 + t, :] = proj_ref[:, C_X:C_X + CONV_DIM]
    act_ref[...] = _conv_rows(ubuf_ref, t, conv_w_ref[...], conv_b_ref[...])
    tail = ubuf_ref[t:t + 8, :]
    conv_out_ref[...] = tail
    ubuf_ref[0:8, :] = tail

    e_s = e_s_ref[...]
    dskip = dskip_ref[...]
    g_y = g_y_ref[...]

    def chunk(i, carry):
        r = pl.multiple_of(i * c, c)
        st_ref[...] = _c_chunk(r, c, dt_ref, da_ref, act_ref, proj_ref, o_ref, st_ref[...],
                               e_s, e_s, dskip, g_y)
        return carry

    lax.fori_loop(0, t // c, chunk, 0)
    y_ref[...] = x + jnp.dot(o_ref[...], w_out_ref[...], preferred_element_type=F32)

    @pl.when(step == pl.num_programs(0) - 1)
    def _():
        st_out_ref[...] = st_ref[...].T


def _c_sample_kernel(x_ref, rms_g_ref, w_in_ref, conv_w_ref, conv_b_ref, dt_bias_ref, a_log_ref,
                     dskip_ref, g_y_ref, w_out_ref, e_s_ref, e_p_ref, st_in_ref, conv_in_ref,
                     y_ref, st_out_ref, conv_out_ref,
                     proj_ref, ubuf_ref, act_ref, dt_ref, da_ref, o_ref):
    nb = st_in_ref.shape[0]
    t = x_ref.shape[0]
    c = t // nb
    x = x_ref[...]
    _c_dense_in(x, rms_g_ref[...], w_in_ref, dt_bias_ref[...], a_log_ref[...], proj_ref, dt_ref, da_ref)
    conv_w = conv_w_ref[...]
    conv_b = conv_b_ref[...]
    ubuf_ref[0:8, :] = jnp.zeros((8, CONV_DIM), F32)
    for bi in range(nb):
        ubuf_ref[8 - (CONV_W - 1):8, :] = conv_in_ref[bi]
        ubuf_ref[8:8 + c, :] = proj_ref[bi * c:(bi + 1) * c, C_X:C_X + CONV_DIM]
        act_ref[bi * c:(bi + 1) * c, :] = _conv_rows(ubuf_ref, c, conv_w, conv_b)
        conv_out_ref[bi] = ubuf_ref[8 + c - (CONV_W - 1):8 + c, :]

    e_s = e_s_ref[...]
    e_p = e_p_ref[...]
    dskip = dskip_ref[...]
    g_y = g_y_ref[...]

    def seq(bi, carry):
        r = pl.multiple_of(bi * c, c)
        st = _c_chunk(r, c, dt_ref, da_ref, act_ref, proj_ref, o_ref, st_in_ref[bi].T,
                      e_s, e_p, dskip, g_y)
        st_out_ref[bi] = st.T
        return carry

    lax.fori_loop(0, nb, seq, 0)
    y_ref[...] = x + jnp.dot(o_ref[...], w_out_ref[...], preferred_element_type=F32)


def _bucket_table(q_off, k_off):
    n = q_off[:, None] - k_off[None, :]
    half = N_BUCKETS // 2
    max_exact = half // 2
    side = np.where(n < 0, half, 0)
    n = np.abs(n)
    nf = np.maximum(n, max_exact).astype(np.float32)
    large = max_exact + (np.log(nf / np.float32(max_exact)) / np.float32(math.log(MAX_DISTANCE / max_exact))
                         * np.float32(half - max_exact)).astype(np.int32)
    large = np.minimum(large, half - 1)
    return side + np.where(n < max_exact, n, large)


def _bias_rows(rel_bias, q_off, k_off):
    bucket = _bucket_table(q_off, k_off)
    bias = jnp.moveaxis(rel_bias[bucket].astype(F32), -1, 0)
    return bias.reshape(KVH_B, G_B * q_off.shape[0], k_off.shape[0])


def _group_sum_matrix(width, group):
    idx = np.arange(width) // group
    return jnp.asarray(idx[:, None] == idx[None, :], BF16)


def _kv_tile_matrices():
    m = np.zeros((KVH_B, KVH_B * HD_B, G_B * HD_B), np.float32)
    for kvh in range(KVH_B):
        for gq in range(G_B):
            m[kvh, kvh * HD_B + np.arange(HD_B), gq * HD_B + np.arange(HD_B)] = 1.0
    return jnp.asarray(m, BF16)


def _head_expand_matrix(per_head):
    m = np.zeros((LANES, H_C * per_head), np.float32)
    for h in range(H_C):
        m[h, h * per_head:(h + 1) * per_head] = 1.0
    return jnp.asarray(m, BF16)


def _full(shape):
    return pl.BlockSpec(shape, lambda *_: (0,) * len(shape))


def _const(shape):
    return pl.BlockSpec(shape, lambda *_: (0,) * len(shape), pipeline_mode=pl.Buffered(1))


def _ab_weights(rms_g, w_in, w_gate_up, b_gate, g_out, g_q, g_k, w_out):
    w_in_r = jnp.concatenate(
        [w_in[:, :1024], w_in[:, 1024 + GATE_RANK_A:], w_in[:, 1024:1024 + GATE_RANK_A],
         jnp.zeros((D_MODEL, AB_COLS - w_in.shape[1]), w_in.dtype)], axis=1).astype(BF16)
    w_gu = jnp.concatenate([w_gate_up, jnp.zeros((LANES - GATE_RANK_A, H_A * DK_A), w_gate_up.dtype)],
                           axis=0).astype(BF16)
    return (rms_g.reshape(1, D_MODEL), w_in_r, w_gu, b_gate.reshape(1, -1), g_out.reshape(1, DV_A),
            jnp.tile(g_q, H_B).reshape(1, -1), jnp.tile(g_k, KVH_B).reshape(1, -1), w_out.astype(BF16))


def _ab_prompt(x, sink, rel_bias, weights):
    rms_g, w_in, w_gu, b_gate, g_out, gq, gk, w_out = weights
    length = x.shape[0]
    t = PROMPT_BLOCK
    c = CHUNK
    kb = WINDOW + c
    bias = _bias_rows(rel_bias, np.arange(c), np.arange(kb) - WINDOW)
    row_blk = lambda i: (i, 0)
    in_specs = [
        pl.BlockSpec(memory_space=pltpu.SMEM),
        pl.BlockSpec((t, D_MODEL), row_blk),
        _const((1, D_MODEL)), _const((D_MODEL, AB_COLS)), _const((LANES, 256)), _const((1, 256)),
        _const((1, DV_A)), _const((1, 512)), _const((1, 128)), _const((KVH_B, G_B * c, kb)),
        _const((D_MODEL, D_MODEL)), _const((512, 512)), _const((128, 128)), _const((KVH_B, 128, 256)),
    ]
    out_shape = (jax.ShapeDtypeStruct((length, D_MODEL), F32),
                 jax.ShapeDtypeStruct((H_A * DK_A, DV_A), F32),
                 jax.ShapeDtypeStruct((WINDOW, KVH_B * HD_B), F32),
                 jax.ShapeDtypeStruct((WINDOW, KVH_B * HD_B), F32))
    out_specs = (pl.BlockSpec((t, D_MODEL), row_blk), _full((H_A * DK_A, DV_A)),
                 _full((WINDOW, KVH_B * HD_B)), _full((WINDOW, KVH_B * HD_B)))
    scratch = [pltpu.VMEM((t, AB_COLS), F32), pltpu.VMEM((t, 512), F32), pltpu.VMEM((t, D_MODEL), BF16),
               pltpu.VMEM((KVH_B, t + WINDOW, 256), BF16), pltpu.VMEM((KVH_B, t + WINDOW, 256), BF16),
               pltpu.VMEM((H_A * DK_A, DV_A), F32)]
    return pl.pallas_call(
        _ab_prompt_kernel, grid=(length // t,), in_specs=in_specs, out_specs=out_specs,
        out_shape=out_shape, scratch_shapes=scratch, name="ab_prompt",
        compiler_params=pltpu.CompilerParams(dimension_semantics=("arbitrary",),
                                             vmem_limit_bytes=VMEM_LIMIT),
    )(sink, x, rms_g, w_in, w_gu, b_gate, g_out, gq, gk, bias, w_out,
      _group_sum_matrix(512, HD_B), _group_sum_matrix(128, HD_B), _kv_tile_matrices())


def _ab_sample(x, sink, rel_bias, weights, s_in, k_cache, v_cache):
    rms_g, w_in, w_gu, b_gate, g_out, gq, gk, w_out = weights
    nb, c, _ = x.shape
    wc = k_cache.shape[1]
    kb = wc + c
    t = nb * c
    bias = _bias_rows(rel_bias, np.arange(c), np.arange(kb) - wc)
    vm = pl.BlockSpec(memory_space=pltpu.VMEM)
    in_specs = [pl.BlockSpec(memory_space=pltpu.SMEM)] + [vm] * 16
    out_shape = (jax.ShapeDtypeStruct((t, D_MODEL), F32),
                 jax.ShapeDtypeStruct((nb, H_A * DK_A, DV_A), F32),
                 jax.ShapeDtypeStruct((t, KVH_B * HD_B), F32),
                 jax.ShapeDtypeStruct((t, KVH_B * HD_B), F32))
    scratch = [pltpu.VMEM((t, AB_COLS), F32), pltpu.VMEM((t, 512), F32), pltpu.VMEM((t, D_MODEL), BF16),
               pltpu.VMEM((KVH_B, nb, kb, 256), BF16), pltpu.VMEM((KVH_B, nb, kb, 256), BF16)]
    return pl.pallas_call(
        _ab_sample_kernel, in_specs=in_specs, out_specs=(vm,) * 4, out_shape=out_shape,
        scratch_shapes=scratch, name="ab_sample",
        compiler_params=pltpu.CompilerParams(vmem_limit_bytes=VMEM_LIMIT),
    )(sink, x.reshape(t, D_MODEL), rms_g, w_in, w_gu, b_gate, g_out, gq, gk, bias, w_out,
      _group_sum_matrix(512, HD_B), _group_sum_matrix(128, HD_B), _kv_tile_matrices(),
      s_in.reshape(nb, H_A * DK_A, DV_A), k_cache.reshape(nb, wc, KVH_B * HD_B),
      v_cache.reshape(nb, wc, KVH_B * HD_B))


def _c_weights(rms_g, w_in, conv_w, conv_b, dt_bias, a_log, d_skip, g_y, w_out):
    w_in_p = jnp.concatenate([w_in, jnp.zeros((D_MODEL, C_COLS - w_in.shape[1]), w_in.dtype)],
                             axis=1).astype(BF16)
    pad = lambda v: jnp.concatenate([v, jnp.zeros((LANES - H_C,), v.dtype)]).reshape(1, LANES)
    return (rms_g.reshape(1, D_MODEL), w_in_p, conv_w, conv_b.reshape(1, CONV_DIM), pad(dt_bias), pad(a_log),
            jnp.repeat(d_skip, P_C).reshape(1, D_INNER_C), g_y.reshape(1, D_INNER_C), w_out.astype(BF16))


def _c_prompt(x, weights):
    length = x.shape[0]
    t = PROMPT_BLOCK
    row_blk = lambda i: (i, 0)
    in_specs = [
        pl.BlockSpec((t, D_MODEL), row_blk),
        _const((1, D_MODEL)), _const((D_MODEL, C_COLS)), _const((CONV_W, CONV_DIM)), _const((1, CONV_DIM)),
        _const((1, LANES)), _const((1, LANES)), _const((1, D_INNER_C)), _const((1, D_INNER_C)),
        _const((D_INNER_C, D_MODEL)), _const((LANES, H_C * CHUNK)),
    ]
    out_shape = (jax.ShapeDtypeStruct((length, D_MODEL), F32),
                 jax.ShapeDtypeStruct((H_C * P_C, N_C), F32),
                 jax.ShapeDtypeStruct((8, CONV_DIM), F32))
    out_specs = (pl.BlockSpec((t, D_MODEL), row_blk), _full((H_C * P_C, N_C)), _full((8, CONV_DIM)))
    scratch = [pltpu.VMEM((t, C_COLS), F32), pltpu.VMEM((t + 8, CONV_DIM), F32),
               pltpu.VMEM((t, CONV_DIM), F32), pltpu.VMEM((t, LANES), F32), pltpu.VMEM((t, LANES), F32),
               pltpu.VMEM((t, D_INNER_C), BF16), pltpu.VMEM((N_C, H_C * P_C), F32)]
    return pl.pallas_call(
        _c_prompt_kernel, grid=(length // t,), in_specs=in_specs, out_specs=out_specs,
        out_shape=out_shape, scratch_shapes=scratch, name="c_prompt",
        compiler_params=pltpu.CompilerParams(dimension_semantics=("arbitrary",),
                                             vmem_limit_bytes=VMEM_LIMIT),
    )(x, *weights, _head_expand_matrix(CHUNK))


def _c_sample(x, weights, st_in, conv_in):
    nb, c, _ = x.shape
    t = nb * c
    vm = pl.BlockSpec(memory_space=pltpu.VMEM)
    out_shape = (jax.ShapeDtypeStruct((t, D_MODEL), F32),
                 jax.ShapeDtypeStruct((nb, H_C * P_C, N_C), F32),
                 jax.ShapeDtypeStruct((nb, CONV_W - 1, CONV_DIM), F32))
    scratch = [pltpu.VMEM((t, C_COLS), F32), pltpu.VMEM((c + 8, CONV_DIM), F32),
               pltpu.VMEM((t, CONV_DIM), F32), pltpu.VMEM((t, LANES), F32), pltpu.VMEM((t, LANES), F32),
               pltpu.VMEM((t, D_INNER_C), BF16)]
    return pl.pallas_call(
        _c_sample_kernel, in_specs=[vm] * 14, out_specs=(vm,) * 3, out_shape=out_shape,
        scratch_shapes=scratch, name="c_sample",
        compiler_params=pltpu.CompilerParams(vmem_limit_bytes=VMEM_LIMIT),
    )(x.reshape(t, D_MODEL), *weights, _head_expand_matrix(c), _head_expand_matrix(P_C),
      st_in.reshape(nb, H_C * P_C, N_C), conv_in)


def kernel(x_prompt, x_sample, cache_swa_k, cache_swa_v, state_gla, state_ssd, state_conv, rms_g, w_in_ab, w_gate_up_a, b_gate_a, g_out_a, g_q_b, g_k_b, sink_b, rel_bias, w_out_ab, w_in_c, conv_w_c, conv_b_c, dt_bias_c, a_log_c, d_skip_c, g_y_c, w_out_c):
    bp, seq_len, _ = x_prompt.shape
    nb, dec_len, _ = x_sample.shape
    assert bp == 1 and seq_len % PROMPT_BLOCK == 0 and seq_len >= WINDOW
    wab = _ab_weights(rms_g[0], w_in_ab[0], w_gate_up_a[0], b_gate_a[0], g_out_a[0], g_q_b[0], g_k_b[0],
                      w_out_ab[0])
    yp, gla_p, k_p, v_p = _ab_prompt(x_prompt[0], sink_b[0], rel_bias, wab)
    ys, gla_s, k_s, v_s = _ab_sample(x_sample, sink_b[0], rel_bias, wab, state_gla[0], cache_swa_k[0],
                                     cache_swa_v[0])
    wc = _c_weights(rms_g[1], w_in_c[0], conv_w_c[0], conv_b_c[0], dt_bias_c[0], a_log_c[0], d_skip_c[0],
                    g_y_c[0], w_out_c[0])
    yp, ssd_p, conv_p = _c_prompt(yp, wc)
    ys, ssd_s, conv_s = _c_sample(ys.reshape(nb, dec_len, D_MODEL), wc, state_ssd[0], state_conv[0])
    return (
        yp.reshape(1, seq_len, D_MODEL),
        ys.reshape(nb, dec_len, D_MODEL),
        gla_p.reshape(1, 1, H_A, DK_A, DV_A),
        k_p.reshape(1, 1, WINDOW, KVH_B, HD_B),
        v_p.reshape(1, 1, WINDOW, KVH_B, HD_B),
        ssd_p.reshape(1, 1, H_C, P_C, N_C),
        conv_p[8 - (CONV_W - 1):].reshape(1, 1, CONV_W - 1, CONV_DIM),
        gla_s.reshape(1, nb, H_A, DK_A, DV_A),
        k_s.reshape(1, nb, dec_len, KVH_B, HD_B),
        v_s.reshape(1, nb, dec_len, KVH_B, HD_B),
        ssd_s.reshape(1, nb, H_C, P_C, N_C),
        conv_s.reshape(1, nb, CONV_W - 1, CONV_DIM),
    )
```

```python
import functools
import math

import numpy as np
import jax
import jax.numpy as jnp
from jax import lax
from jax.experimental import pallas as pl
from jax.experimental.pallas import tpu as pltpu

F32 = jnp.float32
BF16 = jnp.bfloat16

D_MODEL = 1024
CHUNK = 64
EPS = 1e-6
H_A = 4
DK_A = 64
DV_A = 128
GATE_RANK_A = 16
GATE_NORM_A = 16.0
H_B = 8
KVH_B = 2
G_B = H_B // KVH_B
HD_B = 64
WINDOW = 128
N_BUCKETS = 32
MAX_DISTANCE = 128
D_INNER_C = 2048
P_C = 64
H_C = 32
G_C = 4
HPG_C = 8
N_C = 128
CONV_W = 4
CONV_DIM = D_INNER_C + 2 * G_C * N_C

LANES = 128

A_Q, A_K, A_V, A_Z = 0, 256, 512, 1024
B_Q, B_K, B_V, B_Z = 1536, 2048, 2176, 2304
A_G = 2816
AB_COLS = 2944
C_Z, C_X, C_B, C_C, C_DT = 0, 2048, 4096, 4608, 5120
C_COLS = 5248

PROMPT_BLOCK = 256
VMEM_LIMIT = 56 * 1024 * 1024


def _dot(a, b):
    return jnp.dot(a.astype(BF16), b.astype(BF16), preferred_element_type=F32)


def _dot_nt(a, b):
    return lax.dot_general(a.astype(BF16), b.astype(BF16), (((1,), (1,)), ((), ())),
                           preferred_element_type=F32)


def _dot_tn(a, b):
    return lax.dot_general(a.astype(BF16), b.astype(BF16), (((0,), (0,)), ((), ())),
                           preferred_element_type=F32)


def _split_bf16(x, terms):
    out = []
    r = x
    for _ in range(terms):
        h = r.astype(BF16)
        out.append(h)
        r = r - h.astype(F32)
    return out


def _sel_dot(x, sel, terms):
    acc = None
    for h in _split_bf16(x, terms):
        d = jnp.dot(h, sel, preferred_element_type=F32)
        acc = d if acc is None else acc + d
    return acc


def _sel_dot_left(sel, x, terms):
    acc = None
    for h in _split_bf16(x, terms):
        d = jnp.dot(sel, h, preferred_element_type=F32)
        acc = d if acc is None else acc + d
    return acc


def _sel_dot_tn(x, sel, terms):
    acc = None
    for h in _split_bf16(x, terms):
        d = lax.dot_general(h, sel, (((0,), (0,)), ((), ())), preferred_element_type=F32)
        acc = d if acc is None else acc + d
    return acc


def _rms_rows(x):
    return x * lax.rsqrt(jnp.mean(x * x, axis=-1, keepdims=True) + EPS)


def _silu(x):
    return x * (1.0 / (1.0 + jnp.exp(-x)))


def _softplus(x):
    return jnp.maximum(x, 0.0) + jnp.log(1.0 + jnp.exp(-jnp.abs(x)))


def _log_sigmoid(x):
    return jnp.minimum(x, 0.0) - jnp.log(1.0 + jnp.exp(-jnp.abs(x)))


def _tril(n):
    r = lax.broadcasted_iota(jnp.int32, (n, n), 0)
    c = lax.broadcasted_iota(jnp.int32, (n, n), 1)
    return r >= c


def _ab_dense_in(x, rms_g, w_in_ref, gsum512, gsum128, gq, gk, proj_ref, qn_ref):
    h = _rms_rows(x) * rms_g
    proj_ref[...] = jnp.dot(h.astype(BF16), w_in_ref[...], preferred_element_type=F32)
    qb = proj_ref[:, B_Q:B_Q + 512]
    msq = _sel_dot(qb * qb, gsum512, 2) * (1.0 / HD_B)
    qn_ref[...] = qb * lax.rsqrt(msq + EPS) * gq
    kb = proj_ref[:, B_K:B_K + 128]
    msk = _sel_dot(kb * kb, gsum128, 2) * (1.0 / HD_B)
    kn = kb * lax.rsqrt(msk + EPS) * gk
    vb = proj_ref[:, B_V:B_V + 128]
    return kn, vb


def _ab_chunk(r, c, kb, proj_ref, qn_ref, o_ref, s_in, k_band, v_band, bias_ref, sink_ref,
              w_gu, b_gate, g_out, first_valid_col):
    rows = pl.ds(r, c)
    gate = _dot(proj_ref[rows, A_G:A_G + LANES], w_gu) + b_gate
    g = _log_sigmoid(gate) * (1.0 / GATE_NORM_A)
    tril = _tril(c)
    b = _sel_dot_left(tril.astype(BF16), g, 3)
    bl = b[c - 1:c, :]
    q = proj_ref[rows, A_Q:A_Q + 256] * (DK_A ** -0.5)
    k = proj_ref[rows, A_K:A_K + 256]
    v = proj_ref[rows, A_V:A_V + 512]
    qe = q * jnp.exp(b)
    kd = (k * jnp.exp(-b)).astype(BF16)
    kd2 = (k * jnp.exp(bl - b)).astype(BF16)
    bl_rows = _sel_dot_tn(g, jnp.ones((c, LANES), BF16), 3)
    lane128 = lax.broadcasted_iota(jnp.int32, (c, LANES), 1)
    row128 = lax.broadcasted_iota(jnp.int32, (LANES, LANES), 0)
    s_out = []
    for p in range(2):
        qp = qe[:, p * 128:(p + 1) * 128]
        kp = kd[:, p * 128:(p + 1) * 128]
        vp = v[:, p * 256:(p + 1) * 256].astype(BF16)
        sp = s_in[p * 128:(p + 1) * 128, :]
        sp_bf = sp.astype(BF16)
        for j in range(2):
            hd = 2 * p + j
            qm = jnp.where((lane128 // DK_A) == j, qp, 0.0).astype(BF16)
            att = jnp.where(tril, _dot_nt(qm, kp), 0.0)
            oh = _dot(att, vp[:, j * 128:(j + 1) * 128]) + _dot(qm, sp_bf)
            z = proj_ref[rows, A_Z + hd * 128:A_Z + (hd + 1) * 128]
            oh = _rms_rows(oh) * g_out * _silu(z)
            o_ref[rows, hd * 128:(hd + 1) * 128] = oh.astype(BF16)
        upd = _dot_tn(kd2[:, p * 128:(p + 1) * 128], vp)
        upd = jnp.where(row128 < DK_A, upd[:, :128], upd[:, 128:])
        s_out.append(jnp.exp(bl_rows[p * 128:(p + 1) * 128, :]) * sp + upd)
    lane256 = lax.broadcasted_iota(jnp.int32, (c, 256), 1) // HD_B
    srow = lax.broadcasted_iota(jnp.int32, (G_B * c, 1), 0) // c
    for kvh in range(KVH_B):
        qn = qn_ref[rows, kvh * 256:(kvh + 1) * 256]
        qs = jnp.concatenate([jnp.where(lane256 == gq_, qn, 0.0) for gq_ in range(G_B)], axis=0)
        s = _dot_nt(qs, k_band[kvh]) * (HD_B ** -0.5) + bias_ref[kvh]
        if first_valid_col is not None:
            col = lax.broadcasted_iota(jnp.int32, (G_B * c, kb), 1)
            s = jnp.where(col >= first_valid_col, s, -jnp.inf)
        sink = jnp.zeros((G_B * c, 1), F32)
        for gq_ in range(G_B):
            sink = jnp.where(srow == gq_, sink_ref[kvh * G_B + gq_], sink)
        m = jnp.maximum(jnp.max(s, axis=-1, keepdims=True), sink)
        pr = jnp.exp(s - m)
        den = jnp.sum(pr, axis=-1, keepdims=True) + jnp.exp(sink - m)
        ost = _dot(pr, v_band[kvh]) / den
        ob = jnp.zeros((c, 256), F32)
        for gq_ in range(G_B):
            ob = ob + jnp.where(lane256 == gq_, ost[gq_ * c:(gq_ + 1) * c, :], 0.0)
        z = proj_ref[rows, B_Z + kvh * 256:B_Z + (kvh + 1) * 256]
        o_ref[rows, 512 + kvh * 256:512 + (kvh + 1) * 256] = (ob * _silu(z)).astype(BF16)
    return jnp.concatenate(s_out, axis=0)


def _ab_prompt_kernel(sink_ref, x_ref, rms_g_ref, w_in_ref, w_gu_ref, b_gate_ref, g_out_ref, gq_ref,
                      gk_ref, bias_ref, w_out_ref, gsum512_ref, gsum128_ref, tile_ref,
                      y_ref, s_out_ref, k_out_ref, v_out_ref,
                      proj_ref, qn_ref, o_ref, kband_ref, vband_ref, s_ref):
    t = x_ref.shape[0]
    c = CHUNK
    kb = WINDOW + c
    nchunk = t // c
    step = pl.program_id(0)

    @pl.when(step == 0)
    def _():
        s_ref[...] = jnp.zeros_like(s_ref)
        kband_ref[:, t:t + WINDOW, :] = jnp.zeros((KVH_B, WINDOW, 256), BF16)
        vband_ref[:, t:t + WINDOW, :] = jnp.zeros((KVH_B, WINDOW, 256), BF16)

    for kvh in range(KVH_B):
        kband_ref[kvh, 0:WINDOW, :] = kband_ref[kvh, t:t + WINDOW, :]
        vband_ref[kvh, 0:WINDOW, :] = vband_ref[kvh, t:t + WINDOW, :]

    x = x_ref[...]
    kn, vb = _ab_dense_in(x, rms_g_ref[...], w_in_ref, gsum512_ref[...], gsum128_ref[...],
                          gq_ref[...], gk_ref[...], proj_ref, qn_ref)
    k_out_ref[...] = kn[t - WINDOW:, :]
    v_out_ref[...] = vb[t - WINDOW:, :]
    for kvh in range(KVH_B):
        kband_ref[kvh, WINDOW:WINDOW + t, :] = jnp.dot(
            kn.astype(BF16), tile_ref[kvh], preferred_element_type=F32).astype(BF16)
        vband_ref[kvh, WINDOW:WINDOW + t, :] = jnp.dot(
            vb.astype(BF16), tile_ref[kvh], preferred_element_type=F32).astype(BF16)

    w_gu = w_gu_ref[...]
    b_gate = b_gate_ref[...]
    g_out = g_out_ref[...]

    state = s_ref[...]
    for i in range(nchunk):
        r = i * c
        k_band = [kband_ref[kvh, pl.ds(r, kb), :] for kvh in range(KVH_B)]
        v_band = [vband_ref[kvh, pl.ds(r, kb), :] for kvh in range(KVH_B)]
        first_valid = (2 - (step * nchunk + i)) * c
        state = _ab_chunk(r, c, kb, proj_ref, qn_ref, o_ref, state, k_band, v_band,
                          bias_ref, sink_ref, w_gu, b_gate, g_out, first_valid)
    s_ref[...] = state
    y_ref[...] = x + jnp.dot(o_ref[...], w_out_ref[...], preferred_element_type=F32)
    s_out_ref[...] = s_ref[...]


def _ab_sample_kernel(sink_ref, x_ref, rms_g_ref, w_in_ref, w_gu_ref, b_gate_ref, g_out_ref, gq_ref,
                      gk_ref, bias_ref, w_out_ref, gsum512_ref, gsum128_ref, tile_ref,
                      s_in_ref, kc_ref, vc_ref,
                      y_ref, s_out_ref, k_out_ref, v_out_ref,
                      proj_ref, qn_ref, o_ref, kband_ref, vband_ref):
    nb, wc = kc_ref.shape[0], kc_ref.shape[1]
    t = x_ref.shape[0]
    c = t // nb
    kb = wc + c
    x = x_ref[...]
    kn, vb = _ab_dense_in(x, rms_g_ref[...], w_in_ref, gsum512_ref[...], gsum128_ref[...],
                          gq_ref[...], gk_ref[...], proj_ref, qn_ref)
    k_out_ref[...] = kn
    v_out_ref[...] = vb
    for kvh in range(KVH_B):
        kt = jnp.dot(kn.astype(BF16), tile_ref[kvh], preferred_element_type=F32).astype(BF16)
        vt = jnp.dot(vb.astype(BF16), tile_ref[kvh], preferred_element_type=F32).astype(BF16)
        for bi in range(nb):
            kband_ref[kvh, bi, 0:wc, :] = jnp.dot(
                kc_ref[bi].astype(BF16), tile_ref[kvh], preferred_element_type=F32).astype(BF16)
            vband_ref[kvh, bi, 0:wc, :] = jnp.dot(
                vc_ref[bi].astype(BF16), tile_ref[kvh], preferred_element_type=F32).astype(BF16)
            kband_ref[kvh, bi, wc:kb, :] = kt[bi * c:(bi + 1) * c, :]
            vband_ref[kvh, bi, wc:kb, :] = vt[bi * c:(bi + 1) * c, :]

    w_gu = w_gu_ref[...]
    b_gate = b_gate_ref[...]
    g_out = g_out_ref[...]

    def seq(bi, carry):
        r = pl.multiple_of(bi * c, c)
        k_band = [kband_ref[kvh, bi] for kvh in range(KVH_B)]
        v_band = [vband_ref[kvh, bi] for kvh in range(KVH_B)]
        s_out_ref[bi] = _ab_chunk(r, c, kb, proj_ref, qn_ref, o_ref, s_in_ref[bi], k_band, v_band,
                                  bias_ref, sink_ref, w_gu, b_gate, g_out, None)
        return carry

    lax.fori_loop(0, nb, seq, 0)
    y_ref[...] = x + jnp.dot(o_ref[...], w_out_ref[...], preferred_element_type=F32)


def _c_chunk(r, c, dt_ref, da_ref, act_ref, proj_ref, o_ref, st_in, e_s, e_p, dskip, g_y):
    rows = pl.ds(r, c)
    dtc = dt_ref[rows, :]
    acum = _sel_dot_left(_tril(c).astype(BF16), da_ref[rows, :], 3)
    ws = H_C * c
    xa_s = _sel_dot(acum, e_s, 3)
    dt_s = _sel_dot(dtc, e_s, 2)
    row = lax.broadcasted_iota(jnp.int32, (c, ws), 0)
    lane_s = lax.broadcasted_iota(jnp.int32, (c, ws), 1) % c
    diag = row == lane_s
    a_row = jnp.sum(jnp.where(diag, xa_s, 0.0), axis=0, keepdims=True)
    dt_row = jnp.sum(jnp.where(diag, dt_s, 0.0), axis=0, keepdims=True)
    wmat = jnp.exp(jnp.where(row >= lane_s, xa_s - a_row, -jnp.inf)) * dt_row
    if c == P_C:
        xa_p, dt_p = xa_s, dt_s
    else:
        xa_p = _sel_dot(acum, e_p, 3)
        dt_p = _sel_dot(dtc, e_p, 2)
    al_p = xa_p[c - 1:c, :]
    ea_p = jnp.exp(xa_p)
    xs = act_ref[rows, 0:D_INNER_C]
    xw = (xs * (jnp.exp(al_p - xa_p) * dt_p)).astype(BF16)
    dec = jnp.exp(al_p)
    gw = D_INNER_C // G_C
    rowblk = lax.broadcasted_iota(jnp.int32, (4 * c, 256), 0) // c
    colblk = lax.broadcasted_iota(jnp.int32, (4 * c, 256), 1) // P_C
    blockdiag = rowblk == colblk
    st_out = []
    for g in range(G_C):
        bg = act_ref[rows, D_INNER_C + g * N_C:D_INNER_C + (g + 1) * N_C].astype(BF16)
        cg = act_ref[rows, D_INNER_C + G_C * N_C + g * N_C:
                     D_INNER_C + G_C * N_C + (g + 1) * N_C].astype(BF16)
        cb = _dot_nt(cg, jnp.concatenate([bg] * HPG_C, axis=0))
        mg = (cb * wmat[:, g * HPG_C * c:(g + 1) * HPG_C * c]).astype(BF16)
        st_g = st_in[:, g * gw:(g + 1) * gw]
        y_inter = _dot(cg, st_g) * ea_p[:, g * gw:(g + 1) * gw]
        ys = []
        for j in range(2):
            xsj = xs[:, g * gw + j * 256:g * gw + (j + 1) * 256]
            bd = jnp.where(blockdiag, jnp.concatenate([xsj] * 4, axis=0), 0.0).astype(BF16)
            ys.append(jnp.dot(mg[:, j * 4 * c:(j + 1) * 4 * c], bd, preferred_element_type=F32))
        y = jnp.concatenate(ys, axis=1) + y_inter
        xg = xs[:, g * gw:(g + 1) * gw]
        y = y + dskip[:, g * gw:(g + 1) * gw] * xg
        y = y * _silu(proj_ref[rows, C_Z + g * gw:C_Z + (g + 1) * gw])
        y = _rms_rows(y) * g_y[:, g * gw:(g + 1) * gw]
        o_ref[rows, g * gw:(g + 1) * gw] = y.astype(BF16)
        st_out.append(st_g * dec[:, g * gw:(g + 1) * gw] + _dot_tn(bg, xw[:, g * gw:(g + 1) * gw]))
    return jnp.concatenate(st_out, axis=1)


def _c_dense_in(x, rms_g, w_in_ref, dt_bias, a_log, proj_ref, dt_ref, da_ref):
    h = _rms_rows(x) * rms_g
    proj_ref[...] = jnp.dot(h.astype(BF16), w_in_ref[...], preferred_element_type=F32)
    dt = _softplus(proj_ref[:, C_DT:C_DT + LANES] + dt_bias)
    dt_ref[...] = dt
    da_ref[...] = dt * (-jnp.exp(a_log))


def _conv_rows(ubuf, nrows, conv_w, conv_b):
    acc = conv_b
    for i in range(CONV_W):
        acc = acc + conv_w[i:i + 1, :] * ubuf[pl.ds(8 - (CONV_W - 1) + i, nrows), :]
    return _silu(acc)


def _c_prompt_kernel(x_ref, rms_g_ref, w_in_ref, conv_w_ref, conv_b_ref, dt_bias_ref, a_log_ref,
                     dskip_ref, g_y_ref, w_out_ref, e_s_ref,
                     y_ref, st_out_ref, conv_out_ref,
                     proj_ref, ubuf_ref, act_ref, dt_ref, da_ref, o_ref, st_ref):
    t = x_ref.shape[0]
    c = CHUNK
    step = pl.program_id(0)

    @pl.when(step == 0)
    def _():
        st_ref[...] = jnp.zeros_like(st_ref)
        ubuf_ref[0:8, :] = jnp.zeros((8, CONV_DIM), F32)

    x = x_ref[...]
    _c_dense_in(x, rms_g_ref[...], w_in_ref, dt_bias_ref[...], a_log_ref[...], proj_ref, dt_ref, da_ref)
    ubuf_ref[8:8 + t, :] = proj_ref[:, C_X:C_X + CONV_DIM]
    act_ref[...] = _conv_rows(ubuf_ref, t, conv_w_ref[...], conv_b_ref[...])
    tail = ubuf_ref[t:t + 8, :]
    conv_out_ref[...] = tail
    ubuf_ref[0:8, :] = tail

    e_s = e_s_ref[...]
    dskip = dskip_ref[...]
    g_y = g_y_ref[...]

    state = st_ref[...]
    for i in range(t // c):
        state = _c_chunk(i * c, c, dt_ref, da_ref, act_ref, proj_ref, o_ref, state,
                         e_s, e_s, dskip, g_y)
    st_ref[...] = state
    y_ref[...] = x + jnp.dot(o_ref[...], w_out_ref[...], preferred_element_type=F32)

    @pl.when(step == pl.num_programs(0) - 1)
    def _():
        st_out_ref[...] = st_ref[...].T


def _c_sample_kernel(x_ref, rms_g_ref, w_in_ref, conv_w_ref, conv_b_ref, dt_bias_ref, a_log_ref,
                     dskip_ref, g_y_ref, w_out_ref, e_s_ref, e_p_ref, st_in_ref, conv_in_ref,
                     y_ref, st_out_ref, conv_out_ref,
                     proj_ref, ubuf_ref, act_ref, dt_ref, da_ref, o_ref):
    nb = st_in_ref.shape[0]
    t = x_ref.shape[0]
    c = t // nb
    x = x_ref[...]
    _c_dense_in(x, rms_g_ref[...], w_in_ref, dt_bias_ref[...], a_log_ref[...], proj_ref, dt_ref, da_ref)
    conv_w = conv_w_ref[...]
    conv_b = conv_b_ref[...]
    ubuf_ref[0:8, :] = jnp.zeros((8, CONV_DIM), F32)
    for bi in range(nb):
        ubuf_ref[8 - (CONV_W - 1):8, :] = conv_in_ref[bi]
        ubuf_ref[8:8 + c, :] = proj_ref[bi * c:(bi + 1) * c, C_X:C_X + CONV_DIM]
        act_ref[bi * c:(bi + 1) * c, :] = _conv_rows(ubuf_ref, c, conv_w, conv_b)
        conv_out_ref[bi] = ubuf_ref[8 + c - (CONV_W - 1):8 + c, :]

    e_s = e_s_ref[...]
    e_p = e_p_ref[...]
    dskip = dskip_ref[...]
    g_y = g_y_ref[...]

    def seq(bi, carry):
        r = pl.multiple_of(bi * c, c)
        st = _c_chunk(r, c, dt_ref, da_ref, act_ref, proj_ref, o_ref, st_in_ref[bi].T,
                      e_s, e_p, dskip, g_y)
        st_out_ref[bi] = st.T
        return carry

    lax.fori_loop(0, nb, seq, 0)
    y_ref[...] = x + jnp.dot(o_ref[...], w_out_ref[...], preferred_element_type=F32)


def _bucket_table(q_off, k_off):
    n = q_off[:, None] - k_off[None, :]
    half = N_BUCKETS // 2
    max_exact = half // 2
    side = np.where(n < 0, half, 0)
    n = np.abs(n)
    nf = np.maximum(n, max_exact).astype(np.float32)
    large = max_exact + (np.log(nf / np.float32(max_exact)) / np.float32(math.log(MAX_DISTANCE / max_exact))
                         * np.float32(half - max_exact)).astype(np.int32)
    large = np.minimum(large, half - 1)
    return side + np.where(n < max_exact, n, large)


def _bias_rows(rel_bias, q_off, k_off):
    bucket = _bucket_table(q_off, k_off)
    bias = jnp.moveaxis(rel_bias[bucket].astype(F32), -1, 0)
    return bias.reshape(KVH_B, G_B * q_off.shape[0], k_off.shape[0])


def _group_sum_matrix(width, group):
    idx = np.arange(width) // group
    return jnp.asarray(idx[:, None] == idx[None, :], BF16)


def _kv_tile_matrices():
    m = np.zeros((KVH_B, KVH_B * HD_B, G_B * HD_B), np.float32)
    for kvh in range(KVH_B):
        for gq in range(G_B):
            m[kvh, kvh * HD_B + np.arange(HD_B), gq * HD_B + np.arange(HD_B)] = 1.0
    return jnp.asarray(m, BF16)


def _head_expand_matrix(per_head):
    m = np.zeros((LANES, H_C * per_head), np.float32)
    for h in range(H_C):
        m[h, h * per_head:(h + 1) * per_head] = 1.0
    return jnp.asarray(m, BF16)


def _full(shape):
    return pl.BlockSpec(shape, lambda *_: (0,) * len(shape))


def _const(shape):
    return pl.BlockSpec(shape, lambda *_: (0,) * len(shape), pipeline_mode=pl.Buffered(1))


def _ab_weights(rms_g, w_in, w_gate_up, b_gate, g_out, g_q, g_k, w_out):
    w_in_r = jnp.concatenate(
        [w_in[:, :1024], w_in[:, 1024 + GATE_RANK_A:], w_in[:, 1024:1024 + GATE_RANK_A],
         jnp.zeros((D_MODEL, AB_COLS - w_in.shape[1]), w_in.dtype)], axis=1).astype(BF16)
    w_gu = jnp.concatenate([w_gate_up, jnp.zeros((LANES - GATE_RANK_A, H_A * DK_A), w_gate_up.dtype)],
                           axis=0).astype(BF16)
    return (rms_g.reshape(1, D_MODEL), w_in_r, w_gu, b_gate.reshape(1, -1), g_out.reshape(1, DV_A),
            jnp.tile(g_q, H_B).reshape(1, -1), jnp.tile(g_k, KVH_B).reshape(1, -1), w_out.astype(BF16))


def _ab_prompt(x, sink, rel_bias, weights):
    rms_g, w_in, w_gu, b_gate, g_out, gq, gk, w_out = weights
    length = x.shape[0]
    t = PROMPT_BLOCK
    c = CHUNK
    kb = WINDOW + c
    bias = _bias_rows(rel_bias, np.arange(c), np.arange(kb) - WINDOW)
    row_blk = lambda i: (i, 0)
    in_specs = [
        pl.BlockSpec(memory_space=pltpu.SMEM),
        pl.BlockSpec((t, D_MODEL), row_blk),
        _const((1, D_MODEL)), _const((D_MODEL, AB_COLS)), _const((LANES, 256)), _const((1, 256)),
        _const((1, DV_A)), _const((1, 512)), _const((1, 128)), _const((KVH_B, G_B * c, kb)),
        _const((D_MODEL, D_MODEL)), _const((512, 512)), _const((128, 128)), _const((KVH_B, 128, 256)),
    ]
    out_shape = (jax.ShapeDtypeStruct((length, D_MODEL), F32),
                 jax.ShapeDtypeStruct((H_A * DK_A, DV_A), F32),
                 jax.ShapeDtypeStruct((WINDOW, KVH_B * HD_B), F32),
                 jax.ShapeDtypeStruct((WINDOW, KVH_B * HD_B), F32))
    out_specs = (pl.BlockSpec((t, D_MODEL), row_blk), _full((H_A * DK_A, DV_A)),
                 _full((WINDOW, KVH_B * HD_B)), _full((WINDOW, KVH_B * HD_B)))
    scratch = [pltpu.VMEM((t, AB_COLS), F32), pltpu.VMEM((t, 512), F32), pltpu.VMEM((t, D_MODEL), BF16),
               pltpu.VMEM((KVH_B, t + WINDOW, 256), BF16), pltpu.VMEM((KVH_B, t + WINDOW, 256), BF16),
               pltpu.VMEM((H_A * DK_A, DV_A), F32)]
    return pl.pallas_call(
        _ab_prompt_kernel, grid=(length // t,), in_specs=in_specs, out_specs=out_specs,
        out_shape=out_shape, scratch_shapes=scratch, name="ab_prompt",
        compiler_params=pltpu.CompilerParams(dimension_semantics=("arbitrary",),
                                             vmem_limit_bytes=VMEM_LIMIT),
    )(sink, x, rms_g, w_in, w_gu, b_gate, g_out, gq, gk, bias, w_out,
      _group_sum_matrix(512, HD_B), _group_sum_matrix(128, HD_B), _kv_tile_matrices())


def _ab_sample(x, sink, rel_bias, weights, s_in, k_cache, v_cache):
    rms_g, w_in, w_gu, b_gate, g_out, gq, gk, w_out = weights
    nb, c, _ = x.shape
    wc = k_cache.shape[1]
    kb = wc + c
    t = nb * c
    bias = _bias_rows(rel_bias, np.arange(c), np.arange(kb) - wc)
    vm = pl.BlockSpec(memory_space=pltpu.VMEM)
    in_specs = [pl.BlockSpec(memory_space=pltpu.SMEM)] + [vm] * 16
    out_shape = (jax.ShapeDtypeStruct((t, D_MODEL), F32),
                 jax.ShapeDtypeStruct((nb, H_A * DK_A, DV_A), F32),
                 jax.ShapeDtypeStruct((t, KVH_B * HD_B), F32),
                 jax.ShapeDtypeStruct((t, KVH_B * HD_B), F32))
    scratch = [pltpu.VMEM((t, AB_COLS), F32), pltpu.VMEM((t, 512), F32), pltpu.VMEM((t, D_MODEL), BF16),
               pltpu.VMEM((KVH_B, nb, kb, 256), BF16), pltpu.VMEM((KVH_B, nb, kb, 256), BF16)]
    return pl.pallas_call(
        _ab_sample_kernel, in_specs=in_specs, out_specs=(vm,) * 4, out_shape=out_shape,
        scratch_shapes=scratch, name="ab_sample",
        compiler_params=pltpu.CompilerParams(vmem_limit_bytes=VMEM_LIMIT),
    )(sink, x.reshape(t, D_MODEL), rms_g, w_in, w_gu, b_gate, g_out, gq, gk, bias, w_out,
      _group_sum_matrix(512, HD_B), _group_sum_matrix(128, HD_B), _kv_tile_matrices(),
      s_in.reshape(nb, H_A * DK_A, DV_A), k_cache.reshape(nb, wc, KVH_B * HD_B),
      v_cache.reshape(nb, wc, KVH_B * HD_B))


def _c_weights(rms_g, w_in, conv_w, conv_b, dt_bias, a_log, d_skip, g_y, w_out):
    w_in_p = jnp.concatenate([w_in, jnp.zeros((D_MODEL, C_COLS - w_in.shape[1]), w_in.dtype)],
                             axis=1).astype(BF16)
    pad = lambda v: jnp.concatenate([v, jnp.zeros((LANES - H_C,), v.dtype)]).reshape(1, LANES)
    return (rms_g.reshape(1, D_MODEL), w_in_p, conv_w, conv_b.reshape(1, CONV_DIM), pad(dt_bias), pad(a_log),
            jnp.repeat(d_skip, P_C).reshape(1, D_INNER_C), g_y.reshape(1, D_INNER_C), w_out.astype(BF16))


def _c_prompt(x, weights):
    length = x.shape[0]
    t = PROMPT_BLOCK
    row_blk = lambda i: (i, 0)
    in_specs = [
        pl.BlockSpec((t, D_MODEL), row_blk),
        _const((1, D_MODEL)), _const((D_MODEL, C_COLS)), _const((CONV_W, CONV_DIM)), _const((1, CONV_DIM)),
        _const((1, LANES)), _const((1, LANES)), _const((1, D_INNER_C)), _const((1, D_INNER_C)),
        _const((D_INNER_C, D_MODEL)), _const((LANES, H_C * CHUNK)),
    ]
    out_shape = (jax.ShapeDtypeStruct((length, D_MODEL), F32),
                 jax.ShapeDtypeStruct((H_C * P_C, N_C), F32),
                 jax.ShapeDtypeStruct((8, CONV_DIM), F32))
    out_specs = (pl.BlockSpec((t, D_MODEL), row_blk), _full((H_C * P_C, N_C)), _full((8, CONV_DIM)))
    scratch = [pltpu.VMEM((t, C_COLS), F32), pltpu.VMEM((t + 8, CONV_DIM), F32),
               pltpu.VMEM((t, CONV_DIM), F32), pltpu.VMEM((t, LANES), F32), pltpu.VMEM((t, LANES), F32),
               pltpu.VMEM((t, D_INNER_C), BF16), pltpu.VMEM((N_C, H_C * P_C), F32)]
    return pl.pallas_call(
        _c_prompt_kernel, grid=(length // t,), in_specs=in_specs, out_specs=out_specs,
        out_shape=out_shape, scratch_shapes=scratch, name="c_prompt",
        compiler_params=pltpu.CompilerParams(dimension_semantics=("arbitrary",),
                                             vmem_limit_bytes=VMEM_LIMIT),
    )(x, *weights, _head_expand_matrix(CHUNK))


def _c_sample(x, weights, st_in, conv_in):
    nb, c, _ = x.shape
    t = nb * c
    vm = pl.BlockSpec(memory_space=pltpu.VMEM)
    out_shape = (jax.ShapeDtypeStruct((t, D_MODEL), F32),
                 jax.ShapeDtypeStruct((nb, H_C * P_C, N_C), F32),
                 jax.ShapeDtypeStruct((nb, CONV_W - 1, CONV_DIM), F32))
    scratch = [pltpu.VMEM((t, C_COLS), F32), pltpu.VMEM((c + 8, CONV_DIM), F32),
               pltpu.VMEM((t, CONV_DIM), F32), pltpu.VMEM((t, LANES), F32), pltpu.VMEM((t, LANES), F32),
               pltpu.VMEM((t, D_INNER_C), BF16)]
    return pl.pallas_call(
        _c_sample_kernel, in_specs=[vm] * 14, out_specs=(vm,) * 3, out_shape=out_shape,
        scratch_shapes=scratch, name="c_sample",
        compiler_params=pltpu.CompilerParams(vmem_limit_bytes=VMEM_LIMIT),
    )(x.reshape(t, D_MODEL), *weights, _head_expand_matrix(c), _head_expand_matrix(P_C),
      st_in.reshape(nb, H_C * P_C, N_C), conv_in)


def kernel(x_prompt, x_sample, cache_swa_k, cache_swa_v, state_gla, state_ssd, state_conv, rms_g, w_in_ab, w_gate_up_a, b_gate_a, g_out_a, g_q_b, g_k_b, sink_b, rel_bias, w_out_ab, w_in_c, conv_w_c, conv_b_c, dt_bias_c, a_log_c, d_skip_c, g_y_c, w_out_c):
    bp, seq_len, _ = x_prompt.shape
    nb, dec_len, _ = x_sample.shape
    assert bp == 1 and seq_len % PROMPT_BLOCK == 0 and seq_len >= WINDOW
    wab = _ab_weights(rms_g[0], w_in_ab[0], w_gate_up_a[0], b_gate_a[0], g_out_a[0], g_q_b[0], g_k_b[0],
                      w_out_ab[0])
    yp, gla_p, k_p, v_p = _ab_prompt(x_prompt[0], sink_b[0], rel_bias, wab)
    ys, gla_s, k_s, v_s = _ab_sample(x_sample, sink_b[0], rel_bias, wab, state_gla[0], cache_swa_k[0],
                                     cache_swa_v[0])
    wc = _c_weights(rms_g[1], w_in_c[0], conv_w_c[0], conv_b_c[0], dt_bias_c[0], a_log_c[0], d_skip_c[0],
                    g_y_c[0], w_out_c[0])
    yp, ssd_p, conv_p = _c_prompt(yp, wc)
    ys, ssd_s, conv_s = _c_sample(ys.reshape(nb, dec_len, D_MODEL), wc, state_ssd[0], state_conv[0])
    return (
        yp.reshape(1, seq_len, D_MODEL),
        ys.reshape(nb, dec_len, D_MODEL),
        gla_p.reshape(1, 1, H_A, DK_A, DV_A),
        k_p.reshape(1, 1, WINDOW, KVH_B, HD_B),
        v_p.reshape(1, 1, WINDOW, KVH_B, HD_B),
        ssd_p.reshape(1, 1, H_C, P_C, N_C),
        conv_p[8 - (CONV_W - 1):].reshape(1, 1, CONV_W - 1, CONV_DIM),
        gla_s.reshape(1, nb, H_A, DK_A, DV_A),
        k_s.reshape(1, nb, dec_len, KVH_B, HD_B),
        v_s.reshape(1, nb, dec_len, KVH_B, HD_B),
        ssd_s.reshape(1, nb, H_C, P_C, N_C),
        conv_s.reshape(1, nb, CONV_W - 1, CONV_DIM),
    )
```

```python
import functools
import math

import numpy as np
import jax
import jax.numpy as jnp
from jax import lax
from jax.experimental import pallas as pl
from jax.experimental.pallas import tpu as pltpu

F32 = jnp.float32
BF16 = jnp.bfloat16

D_MODEL = 1024
CHUNK = 64
EPS = 1e-6
H_A = 4
DK_A = 64
DV_A = 128
GATE_RANK_A = 16
GATE_NORM_A = 16.0
H_B = 8
KVH_B = 2
G_B = H_B // KVH_B
HD_B = 64
WINDOW = 128
N_BUCKETS = 32
MAX_DISTANCE = 128
D_INNER_C = 2048
P_C = 64
H_C = 32
G_C = 4
HPG_C = 8
N_C = 128
CONV_W = 4
CONV_DIM = D_INNER_C + 2 * G_C * N_C

LANES = 128

A_Q, A_K, A_V, A_Z = 0, 256, 512, 1024
B_Q, B_K, B_V, B_Z = 1536, 2048, 2176, 2304
A_G = 2816
AB_COLS = 2944
C_Z, C_X, C_B, C_C, C_DT = 0, 2048, 4096, 4608, 5120
C_COLS = 5248

PROMPT_BLOCK = 256
VMEM_LIMIT = 56 * 1024 * 1024


def _dot(a, b):
    return jnp.dot(a.astype(BF16), b.astype(BF16), preferred_element_type=F32)


def _dot_nt(a, b):
    return lax.dot_general(a.astype(BF16), b.astype(BF16), (((1,), (1,)), ((), ())),
                           preferred_element_type=F32)


def _dot_tn(a, b):
    return lax.dot_general(a.astype(BF16), b.astype(BF16), (((0,), (0,)), ((), ())),
                           preferred_element_type=F32)


def _split_bf16(x, terms):
    out = []
    r = x
    for _ in range(terms):
        h = r.astype(BF16)
        out.append(h)
        r = r - h.astype(F32)
    return out


def _sel_dot(x, sel, terms):
    acc = None
    for h in _split_bf16(x, terms):
        d = jnp.dot(h, sel, preferred_element_type=F32)
        acc = d if acc is None else acc + d
    return acc


def _sel_dot_left(sel, x, terms):
    acc = None
    for h in _split_bf16(x, terms):
        d = jnp.dot(sel, h, preferred_element_type=F32)
        acc = d if acc is None else acc + d
    return acc


def _sel_dot_tn(x, sel, terms):
    acc = None
    for h in _split_bf16(x, terms):
        d = lax.dot_general(h, sel, (((0,), (0,)), ((), ())), preferred_element_type=F32)
        acc = d if acc is None else acc + d
    return acc


def _rms_rows(x):
    return x * lax.rsqrt(jnp.mean(x * x, axis=-1, keepdims=True) + EPS)


def _silu(x):
    return x * (1.0 / (1.0 + jnp.exp(-x)))


def _softplus(x):
    return jnp.maximum(x, 0.0) + jnp.log(1.0 + jnp.exp(-jnp.abs(x)))


def _log_sigmoid(x):
    return jnp.minimum(x, 0.0) - jnp.log(1.0 + jnp.exp(-jnp.abs(x)))


def _tril(n):
    r = lax.broadcasted_iota(jnp.int32, (n, n), 0)
    c = lax.broadcasted_iota(jnp.int32, (n, n), 1)
    return r >= c


def _ab_dense_in(x, rms_g, w_in_ref, gsum512, gsum128, gq, gk, proj_ref, qn_ref):
    h = _rms_rows(x) * rms_g
    proj_ref[...] = jnp.dot(h.astype(BF16), w_in_ref[...], preferred_element_type=F32)
    qb = proj_ref[:, B_Q:B_Q + 512]
    msq = _sel_dot(qb * qb, gsum512, 2) * (1.0 / HD_B)
    qn_ref[...] = qb * lax.rsqrt(msq + EPS) * gq
    kb = proj_ref[:, B_K:B_K + 128]
    msk = _sel_dot(kb * kb, gsum128, 2) * (1.0 / HD_B)
    kn = kb * lax.rsqrt(msk + EPS) * gk
    vb = proj_ref[:, B_V:B_V + 128]
    return kn, vb


def _interleave(chunks):
    chunks = list(chunks)
    while chunks:
        alive = []
        for gen in chunks:
            try:
                next(gen)
                alive.append(gen)
            except StopIteration:
                pass
        chunks = alive


def _ab_chunk(r, c, kb, proj_ref, qn_ref, o_ref, state, k_band, v_band, bias_ref, sink_ref,
              w_gu, b_gate, g_out, first_valid_col):
    rows = pl.ds(r, c)
    gate = _dot(proj_ref[rows, A_G:A_G + LANES], w_gu) + b_gate
    yield
    g = _log_sigmoid(gate) * (1.0 / GATE_NORM_A)
    tril = _tril(c)
    b = _sel_dot_left(tril.astype(BF16), g, 3)
    bl_rows = _sel_dot_tn(g, jnp.ones((c, LANES), BF16), 3)
    yield
    bl = b[c - 1:c, :]
    q = proj_ref[rows, A_Q:A_Q + 256] * (DK_A ** -0.5)
    k = proj_ref[rows, A_K:A_K + 256]
    v = proj_ref[rows, A_V:A_V + 512].astype(BF16)
    qe = q * jnp.exp(b)
    kd = (k * jnp.exp(-b)).astype(BF16)
    kd2 = (k * jnp.exp(bl - b)).astype(BF16)
    lane128 = lax.broadcasted_iota(jnp.int32, (c, LANES), 1)
    row128 = lax.broadcasted_iota(jnp.int32, (LANES, LANES), 0)
    qm, att = [], []
    for hd in range(H_A):
        p, j = divmod(hd, 2)
        qm.append(jnp.where((lane128 // DK_A) == j, qe[:, p * 128:(p + 1) * 128], 0.0).astype(BF16))
        att.append(_dot_nt(qm[hd], kd[:, p * 128:(p + 1) * 128]))
    upd = [_dot_tn(kd2[:, p * 128:(p + 1) * 128], v[:, p * 256:(p + 1) * 256]) for p in range(2)]
    yield
    oh = [_dot(jnp.where(tril, att[hd], 0.0), v[:, hd * 128:(hd + 1) * 128]) for hd in range(H_A)]
    yield
    s_prev = state[0]
    s_new = []
    for p in range(2):
        sp = s_prev[p * 128:(p + 1) * 128, :]
        sp_bf = sp.astype(BF16)
        for j in range(2):
            oh[2 * p + j] = oh[2 * p + j] + _dot(qm[2 * p + j], sp_bf)
        u = jnp.where(row128 < DK_A, upd[p][:, :128], upd[p][:, 128:])
        s_new.append(jnp.exp(bl_rows[p * 128:(p + 1) * 128, :]) * sp + u)
    state[0] = jnp.concatenate(s_new, axis=0)
    yield
    for hd in range(H_A):
        z = proj_ref[rows, A_Z + hd * 128:A_Z + (hd + 1) * 128]
        o_ref[rows, hd * 128:(hd + 1) * 128] = (_rms_rows(oh[hd]) * g_out * _silu(z)).astype(BF16)
    yield
    lane256 = lax.broadcasted_iota(jnp.int32, (c, 256), 1) // HD_B
    srow = lax.broadcasted_iota(jnp.int32, (G_B * c, 1), 0) // c
    scores = []
    for kvh in range(KVH_B):
        qn = qn_ref[rows, kvh * 256:(kvh + 1) * 256]
        qs = jnp.concatenate([jnp.where(lane256 == gq_, qn, 0.0) for gq_ in range(G_B)], axis=0)
        scores.append(_dot_nt(qs, k_band(kvh)))
    yield
    probs, dens = [], []
    for kvh in range(KVH_B):
        s = scores[kvh] * (HD_B ** -0.5) + bias_ref[kvh]
        if first_valid_col is not None:
            col = lax.broadcasted_iota(jnp.int32, (G_B * c, kb), 1)
            s = jnp.where(col >= first_valid_col, s, -jnp.inf)
        sink = jnp.zeros((G_B * c, 1), F32)
        for gq_ in range(G_B):
            sink = jnp.where(srow == gq_, sink_ref[kvh * G_B + gq_], sink)
        m = jnp.maximum(jnp.max(s, axis=-1, keepdims=True), sink)
        pr = jnp.exp(s - m)
        dens.append(jnp.sum(pr, axis=-1, keepdims=True) + jnp.exp(sink - m))
        probs.append(pr.astype(BF16))
    yield
    outs = [_dot(probs[kvh], v_band(kvh)) for kvh in range(KVH_B)]
    yield
    for kvh in range(KVH_B):
        ost = outs[kvh] / dens[kvh]
        ob = jnp.zeros((c, 256), F32)
        for gq_ in range(G_B):
            ob = ob + jnp.where(lane256 == gq_, ost[gq_ * c:(gq_ + 1) * c, :], 0.0)
        z = proj_ref[rows, B_Z + kvh * 256:B_Z + (kvh + 1) * 256]
        o_ref[rows, 512 + kvh * 256:512 + (kvh + 1) * 256] = (ob * _silu(z)).astype(BF16)


def _ab_prompt_kernel(sink_ref, x_ref, rms_g_ref, w_in_ref, w_gu_ref, b_gate_ref, g_out_ref, gq_ref,
                      gk_ref, bias_ref, w_out_ref, gsum512_ref, gsum128_ref, tile_ref,
                      y_ref, s_out_ref, k_out_ref, v_out_ref,
                      proj_ref, qn_ref, o_ref, kband_ref, vband_ref, s_ref):
    t = x_ref.shape[0]
    c = CHUNK
    kb = WINDOW + c
    nchunk = t // c
    step = pl.program_id(0)

    @pl.when(step == 0)
    def _():
        s_ref[...] = jnp.zeros_like(s_ref)
        kband_ref[:, t:t + WINDOW, :] = jnp.zeros((KVH_B, WINDOW, 256), BF16)
        vband_ref[:, t:t + WINDOW, :] = jnp.zeros((KVH_B, WINDOW, 256), BF16)

    for kvh in range(KVH_B):
        kband_ref[kvh, 0:WINDOW, :] = kband_ref[kvh, t:t + WINDOW, :]
        vband_ref[kvh, 0:WINDOW, :] = vband_ref[kvh, t:t + WINDOW, :]

    x = x_ref[...]
    kn, vb = _ab_dense_in(x, rms_g_ref[...], w_in_ref, gsum512_ref[...], gsum128_ref[...],
                          gq_ref[...], gk_ref[...], proj_ref, qn_ref)
    k_out_ref[...] = kn[t - WINDOW:, :]
    v_out_ref[...] = vb[t - WINDOW:, :]
    for kvh in range(KVH_B):
        kband_ref[kvh, WINDOW:WINDOW + t, :] = jnp.dot(
            kn.astype(BF16), tile_ref[kvh], preferred_element_type=F32).astype(BF16)
        vband_ref[kvh, WINDOW:WINDOW + t, :] = jnp.dot(
            vb.astype(BF16), tile_ref[kvh], preferred_element_type=F32).astype(BF16)

    w_gu = w_gu_ref[...]
    b_gate = b_gate_ref[...]
    g_out = g_out_ref[...]

    state = [s_ref[...]]

    def chunk(i):
        r = i * c
        return _ab_chunk(r, c, kb, proj_ref, qn_ref, o_ref, state,
                         lambda kvh: kband_ref[kvh, pl.ds(r, kb), :],
                         lambda kvh: vband_ref[kvh, pl.ds(r, kb), :],
                         bias_ref, sink_ref, w_gu, b_gate, g_out, (2 - (step * nchunk + i)) * c)

    _interleave(chunk(i) for i in range(nchunk))
    s_ref[...] = state[0]
    y_ref[...] = x + jnp.dot(o_ref[...], w_out_ref[...], preferred_element_type=F32)
    s_out_ref[...] = s_ref[...]


def _ab_sample_kernel(sink_ref, x_ref, rms_g_ref, w_in_ref, w_gu_ref, b_gate_ref, g_out_ref, gq_ref,
                      gk_ref, bias_ref, w_out_ref, gsum512_ref, gsum128_ref, tile_ref,
                      s_in_ref, kc_ref, vc_ref,
                      y_ref, s_out_ref, k_out_ref, v_out_ref,
                      proj_ref, qn_ref, o_ref, kband_ref, vband_ref):
    nb, wc = kc_ref.shape[0], kc_ref.shape[1]
    t = x_ref.shape[0]
    c = t // nb
    kb = wc + c
    x = x_ref[...]
    kn, vb = _ab_dense_in(x, rms_g_ref[...], w_in_ref, gsum512_ref[...], gsum128_ref[...],
                          gq_ref[...], gk_ref[...], proj_ref, qn_ref)
    k_out_ref[...] = kn
    v_out_ref[...] = vb
    for kvh in range(KVH_B):
        kt = jnp.dot(kn.astype(BF16), tile_ref[kvh], preferred_element_type=F32).astype(BF16)
        vt = jnp.dot(vb.astype(BF16), tile_ref[kvh], preferred_element_type=F32).astype(BF16)
        for bi in range(nb):
            kband_ref[kvh, bi, 0:wc, :] = jnp.dot(
                kc_ref[bi].astype(BF16), tile_ref[kvh], preferred_element_type=F32).astype(BF16)
            vband_ref[kvh, bi, 0:wc, :] = jnp.dot(
                vc_ref[bi].astype(BF16), tile_ref[kvh], preferred_element_type=F32).astype(BF16)
            kband_ref[kvh, bi, wc:kb, :] = kt[bi * c:(bi + 1) * c, :]
            vband_ref[kvh, bi, wc:kb, :] = vt[bi * c:(bi + 1) * c, :]

    w_gu = w_gu_ref[...]
    b_gate = b_gate_ref[...]
    g_out = g_out_ref[...]

    def seq(bi, carry):
        r = pl.multiple_of(bi * c, c)
        state = [s_in_ref[bi]]
        _interleave([_ab_chunk(r, c, kb, proj_ref, qn_ref, o_ref, state,
                               lambda kvh: kband_ref[kvh, bi], lambda kvh: vband_ref[kvh, bi],
                               bias_ref, sink_ref, w_gu, b_gate, g_out, None)])
        s_out_ref[bi] = state[0]
        return carry

    lax.fori_loop(0, nb, seq, 0)
    y_ref[...] = x + jnp.dot(o_ref[...], w_out_ref[...], preferred_element_type=F32)


def _c_chunk(r, c, dt_ref, da_ref, act_ref, proj_ref, o_ref, state, e_s, e_p, dskip, g_y):
    rows = pl.ds(r, c)
    dtc = dt_ref[rows, :]
    acum = _sel_dot_left(_tril(c).astype(BF16), da_ref[rows, :], 3)
    yield
    ws = H_C * c
    xa_s = _sel_dot(acum, e_s, 3)
    dt_s = _sel_dot(dtc, e_s, 2)
    if c == P_C:
        xa_p, dt_p = xa_s, dt_s
    else:
        xa_p = _sel_dot(acum, e_p, 3)
        dt_p = _sel_dot(dtc, e_p, 2)
    yield
    row = lax.broadcasted_iota(jnp.int32, (c, ws), 0)
    lane_s = lax.broadcasted_iota(jnp.int32, (c, ws), 1) % c
    diag = row == lane_s
    a_row = jnp.sum(jnp.where(diag, xa_s, 0.0), axis=0, keepdims=True)
    dt_row = jnp.sum(jnp.where(diag, dt_s, 0.0), axis=0, keepdims=True)
    wmat = jnp.exp(jnp.where(row >= lane_s, xa_s - a_row, -jnp.inf)) * dt_row
    al_p = xa_p[c - 1:c, :]
    ea_p = jnp.exp(xa_p)
    xs = act_ref[rows, 0:D_INNER_C]
    xw = (xs * (jnp.exp(al_p - xa_p) * dt_p)).astype(BF16)
    dec = jnp.exp(al_p)
    gw = D_INNER_C // G_C
    bg, cg, cb = [], [], []
    for g in range(G_C):
        bg.append(act_ref[rows, D_INNER_C + g * N_C:D_INNER_C + (g + 1) * N_C].astype(BF16))
        cg.append(act_ref[rows, D_INNER_C + G_C * N_C + g * N_C:
                          D_INNER_C + G_C * N_C + (g + 1) * N_C].astype(BF16))
        cb.append(_dot_nt(cg[g], jnp.concatenate([bg[g]] * HPG_C, axis=0)))
    yield
    rowblk = lax.broadcasted_iota(jnp.int32, (4 * c, 256), 0) // c
    colblk = lax.broadcasted_iota(jnp.int32, (4 * c, 256), 1) // P_C
    blockdiag = rowblk == colblk
    ys = []
    for g in range(G_C):
        mg = (cb[g] * wmat[:, g * HPG_C * c:(g + 1) * HPG_C * c]).astype(BF16)
        for j in range(2):
            xsj = xs[:, g * gw + j * 256:g * gw + (j + 1) * 256]
            bd = jnp.where(blockdiag, jnp.concatenate([xsj] * 4, axis=0), 0.0).astype(BF16)
            ys.append(jnp.dot(mg[:, j * 4 * c:(j + 1) * 4 * c], bd, preferred_element_type=F32))
    upd = [_dot_tn(bg[g], xw[:, g * gw:(g + 1) * gw]) for g in range(G_C)]
    yield
    st_prev = state[0]
    y_inter, st_new = [], []
    for g in range(G_C):
        st_g = st_prev[:, g * gw:(g + 1) * gw]
        y_inter.append(_dot(cg[g], st_g))
        st_new.append(st_g * dec[:, g * gw:(g + 1) * gw] + upd[g])
    state[0] = jnp.concatenate(st_new, axis=1)
    yield
    for g in range(G_C):
        sl = slice(g * gw, (g + 1) * gw)
        y = jnp.concatenate(ys[2 * g:2 * g + 2], axis=1) + y_inter[g] * ea_p[:, sl]
        y = y + dskip[:, sl] * xs[:, sl]
        y = y * _silu(proj_ref[rows, C_Z + g * gw:C_Z + (g + 1) * gw])
        o_ref[rows, sl] = (_rms_rows(y) * g_y[:, sl]).astype(BF16)


def _c_dense_in(x, rms_g, w_in_ref, dt_bias, a_log, proj_ref, dt_ref, da_ref):
    h = _rms_rows(x) * rms_g
    proj_ref[...] = jnp.dot(h.astype(BF16), w_in_ref[...], preferred_element_type=F32)
    dt = _softplus(proj_ref[:, C_DT:C_DT + LANES] + dt_bias)
    dt_ref[...] = dt
    da_ref[...] = dt * (-jnp.exp(a_log))


def _conv_rows(ubuf, nrows, conv_w, conv_b):
    acc = conv_b
    for i in range(CONV_W):
        acc = acc + conv_w[i:i + 1, :] * ubuf[pl.ds(8 - (CONV_W - 1) + i, nrows), :]
    return _silu(acc)


def _c_prompt_kernel(x_ref, rms_g_ref, w_in_ref, conv_w_ref, conv_b_ref, dt_bias_ref, a_log_ref,
                     dskip_ref, g_y_ref, w_out_ref, e_s_ref,
                     y_ref, st_out_ref, conv_out_ref,
                     proj_ref, ubuf_ref, act_ref, dt_ref, da_ref, o_ref, st_ref):
    t = x_ref.shape[0]
    c = CHUNK
    step = pl.program_id(0)

    @pl.when(step == 0)
    def _():
        st_ref[...] = jnp.zeros_like(st_ref)
        ubuf_ref[0:8, :] = jnp.zeros((8, CONV_DIM), F32)

    x = x_ref[...]
    _c_dense_in(x, rms_g_ref[...], w_in_ref, dt_bias_ref[...], a_log_ref[...], proj_ref, dt_ref, da_ref)
    ubuf_ref[8:8 + t, :] = proj_ref[:, C_X:C_X + CONV_DIM]
    act_ref[...] = _conv_rows(ubuf_ref, t, conv_w_ref[...], conv_b_ref[...])
    tail = ubuf_ref[t:t + 8, :]
    conv_out_ref[...] = tail
    ubuf_ref[0:8, :] = tail

    e_s = e_s_ref[...]
    dskip = dskip_ref[...]
    g_y = g_y_ref[...]

    state = [st_ref[...]]
    _interleave(_c_chunk(i * c, c, dt_ref, da_ref, act_ref, proj_ref, o_ref, state, e_s, e_s, dskip, g_y)
                for i in range(t // c))
    st_ref[...] = state[0]
    y_ref[...] = x + jnp.dot(o_ref[...], w_out_ref[...], preferred_element_type=F32)

    @pl.when(step == pl.num_programs(0) - 1)
    def _():
        st_out_ref[...] = st_ref[...].T


def _c_sample_kernel(x_ref, rms_g_ref, w_in_ref, conv_w_ref, conv_b_ref, dt_bias_ref, a_log_ref,
                     dskip_ref, g_y_ref, w_out_ref, e_s_ref, e_p_ref, st_in_ref, conv_in_ref,
                     y_ref, st_out_ref, conv_out_ref,
                     proj_ref, ubuf_ref, act_ref, dt_ref, da_ref, o_ref):
    nb = st_in_ref.shape[0]
    t = x_ref.shape[0]
    c = t // nb
    x = x_ref[...]
    _c_dense_in(x, rms_g_ref[...], w_in_ref, dt_bias_ref[...], a_log_ref[...], proj_ref, dt_ref, da_ref)
    conv_w = conv_w_ref[...]
    conv_b = conv_b_ref[...]
    ubuf_ref[0:8, :] = jnp.zeros((8, CONV_DIM), F32)
    for bi in range(nb):
        ubuf_ref[8 - (CONV_W - 1):8, :] = conv_in_ref[bi]
        ubuf_ref[8:8 + c, :] = proj_ref[bi * c:(bi + 1) * c, C_X:C_X + CONV_DIM]
        act_ref[bi * c:(bi + 1) * c, :] = _conv_rows(ubuf_ref, c, conv_w, conv_b)
        conv_out_ref[bi] = ubuf_ref[8 + c - (CONV_W - 1):8 + c, :]

    e_s = e_s_ref[...]
    e_p = e_p_ref[...]
    dskip = dskip_ref[...]
    g_y = g_y_ref[...]

    def seq(bi, carry):
        r = pl.multiple_of(bi * c, c)
        state = [st_in_ref[bi].T]
        _interleave([_c_chunk(r, c, dt_ref, da_ref, act_ref, proj_ref, o_ref, state, e_s, e_p, dskip, g_y)])
        st_out_ref[bi] = state[0].T
        return carry

    lax.fori_loop(0, nb, seq, 0)
    y_ref[...] = x + jnp.dot(o_ref[...], w_out_ref[...], preferred_element_type=F32)


def _bucket_table(q_off, k_off):
    n = q_off[:, None] - k_off[None, :]
    half = N_BUCKETS // 2
    max_exact = half // 2
    side = np.where(n < 0, half, 0)
    n = np.abs(n)
    nf = np.maximum(n, max_exact).astype(np.float32)
    large = max_exact + (np.log(nf / np.float32(max_exact)) / np.float32(math.log(MAX_DISTANCE / max_exact))
                         * np.float32(half - max_exact)).astype(np.int32)
    large = np.minimum(large, half - 1)
    return side + np.where(n < max_exact, n, large)


def _bias_rows(rel_bias, q_off, k_off):
    bucket = _bucket_table(q_off, k_off)
    bias = jnp.moveaxis(rel_bias[bucket].astype(F32), -1, 0)
    return bias.reshape(KVH_B, G_B * q_off.shape[0], k_off.shape[0])


def _group_sum_matrix(width, group):
    idx = np.arange(width) // group
    return jnp.asarray(idx[:, None] == idx[None, :], BF16)


def _kv_tile_matrices():
    m = np.zeros((KVH_B, KVH_B * HD_B, G_B * HD_B), np.float32)
    for kvh in range(KVH_B):
        for gq in range(G_B):
            m[kvh, kvh * HD_B + np.arange(HD_B), gq * HD_B + np.arange(HD_B)] = 1.0
    return jnp.asarray(m, BF16)


def _head_expand_matrix(per_head):
    m = np.zeros((LANES, H_C * per_head), np.float32)
    for h in range(H_C):
        m[h, h * per_head:(h + 1) * per_head] = 1.0
    return jnp.asarray(m, BF16)


def _full(shape):
    return pl.BlockSpec(shape, lambda *_: (0,) * len(shape))


def _const(shape):
    return pl.BlockSpec(shape, lambda *_: (0,) * len(shape), pipeline_mode=pl.Buffered(1))


def _ab_weights(rms_g, w_in, w_gate_up, b_gate, g_out, g_q, g_k, w_out):
    w_in_r = jnp.concatenate(
        [w_in[:, :1024], w_in[:, 1024 + GATE_RANK_A:], w_in[:, 1024:1024 + GATE_RANK_A],
         jnp.zeros((D_MODEL, AB_COLS - w_in.shape[1]), w_in.dtype)], axis=1).astype(BF16)
    w_gu = jnp.concatenate([w_gate_up, jnp.zeros((LANES - GATE_RANK_A, H_A * DK_A), w_gate_up.dtype)],
                           axis=0).astype(BF16)
    return (rms_g.reshape(1, D_MODEL), w_in_r, w_gu, b_gate.reshape(1, -1), g_out.reshape(1, DV_A),
            jnp.tile(g_q, H_B).reshape(1, -1), jnp.tile(g_k, KVH_B).reshape(1, -1), w_out.astype(BF16))


def _ab_prompt(x, sink, rel_bias, weights):
    rms_g, w_in, w_gu, b_gate, g_out, gq, gk, w_out = weights
    length = x.shape[0]
    t = PROMPT_BLOCK
    c = CHUNK
    kb = WINDOW + c
    bias = _bias_rows(rel_bias, np.arange(c), np.arange(kb) - WINDOW)
    row_blk = lambda i: (i, 0)
    in_specs = [
        pl.BlockSpec(memory_space=pltpu.SMEM),
        pl.BlockSpec((t, D_MODEL), row_blk),
        _const((1, D_MODEL)), _const((D_MODEL, AB_COLS)), _const((LANES, 256)), _const((1, 256)),
        _const((1, DV_A)), _const((1, 512)), _const((1, 128)), _const((KVH_B, G_B * c, kb)),
        _const((D_MODEL, D_MODEL)), _const((512, 512)), _const((128, 128)), _const((KVH_B, 128, 256)),
    ]
    out_shape = (jax.ShapeDtypeStruct((length, D_MODEL), F32),
                 jax.ShapeDtypeStruct((H_A * DK_A, DV_A), F32),
                 jax.ShapeDtypeStruct((WINDOW, KVH_B * HD_B), F32),
                 jax.ShapeDtypeStruct((WINDOW, KVH_B * HD_B), F32))
    out_specs = (pl.BlockSpec((t, D_MODEL), row_blk), _full((H_A * DK_A, DV_A)),
                 _full((WINDOW, KVH_B * HD_B)), _full((WINDOW, KVH_B * HD_B)))
    scratch = [pltpu.VMEM((t, AB_COLS), F32), pltpu.VMEM((t, 512), F32), pltpu.VMEM((t, D_MODEL), BF16),
               pltpu.VMEM((KVH_B, t + WINDOW, 256), BF16), pltpu.VMEM((KVH_B, t + WINDOW, 256), BF16),
               pltpu.VMEM((H_A * DK_A, DV_A), F32)]
    return pl.pallas_call(
        _ab_prompt_kernel, grid=(length // t,), in_specs=in_specs, out_specs=out_specs,
        out_shape=out_shape, scratch_shapes=scratch, name="ab_prompt",
        compiler_params=pltpu.CompilerParams(dimension_semantics=("arbitrary",),
                                             vmem_limit_bytes=VMEM_LIMIT),
    )(sink, x, rms_g, w_in, w_gu, b_gate, g_out, gq, gk, bias, w_out,
      _group_sum_matrix(512, HD_B), _group_sum_matrix(128, HD_B), _kv_tile_matrices())


def _ab_sample(x, sink, rel_bias, weights, s_in, k_cache, v_cache):
    rms_g, w_in, w_gu, b_gate, g_out, gq, gk, w_out = weights
    nb, c, _ = x.shape
    wc = k_cache.shape[1]
    kb = wc + c
    t = nb * c
    bias = _bias_rows(rel_bias, np.arange(c), np.arange(kb) - wc)
    vm = pl.BlockSpec(memory_space=pltpu.VMEM)
    in_specs = [pl.BlockSpec(memory_space=pltpu.SMEM)] + [vm] * 16
    out_shape = (jax.ShapeDtypeStruct((t, D_MODEL), F32),
                 jax.ShapeDtypeStruct((nb, H_A * DK_A, DV_A), F32),
                 jax.ShapeDtypeStruct((t, KVH_B * HD_B), F32),
                 jax.ShapeDtypeStruct((t, KVH_B * HD_B), F32))
    scratch = [pltpu.VMEM((t, AB_COLS), F32), pltpu.VMEM((t, 512), F32), pltpu.VMEM((t, D_MODEL), BF16),
               pltpu.VMEM((KVH_B, nb, kb, 256), BF16), pltpu.VMEM((KVH_B, nb, kb, 256), BF16)]
    return pl.pallas_call(
        _ab_sample_kernel, in_specs=in_specs, out_specs=(vm,) * 4, out_shape=out_shape,
        scratch_shapes=scratch, name="ab_sample",
        compiler_params=pltpu.CompilerParams(vmem_limit_bytes=VMEM_LIMIT),
    )(sink, x.reshape(t, D_MODEL), rms_g, w_in, w_gu, b_gate, g_out, gq, gk, bias, w_out,
      _group_sum_matrix(512, HD_B), _group_sum_matrix(128, HD_B), _kv_tile_matrices(),
      s_in.reshape(nb, H_A * DK_A, DV_A), k_cache.reshape(nb, wc, KVH_B * HD_B),
      v_cache.reshape(nb, wc, KVH_B * HD_B))


def _c_weights(rms_g, w_in, conv_w, conv_b, dt_bias, a_log, d_skip, g_y, w_out):
    w_in_p = jnp.concatenate([w_in, jnp.zeros((D_MODEL, C_COLS - w_in.shape[1]), w_in.dtype)],
                             axis=1).astype(BF16)
    pad = lambda v: jnp.concatenate([v, jnp.zeros((LANES - H_C,), v.dtype)]).reshape(1, LANES)
    return (rms_g.reshape(1, D_MODEL), w_in_p, conv_w, conv_b.reshape(1, CONV_DIM), pad(dt_bias), pad(a_log),
            jnp.repeat(d_skip, P_C).reshape(1, D_INNER_C), g_y.reshape(1, D_INNER_C), w_out.astype(BF16))


def _c_prompt(x, weights):
    length = x.shape[0]
    t = PROMPT_BLOCK
    row_blk = lambda i: (i, 0)
    in_specs = [
        pl.BlockSpec((t, D_MODEL), row_blk),
        _const((1, D_MODEL)), _const((D_MODEL, C_COLS)), _const((CONV_W, CONV_DIM)), _const((1, CONV_DIM)),
        _const((1, LANES)), _const((1, LANES)), _const((1, D_INNER_C)), _const((1, D_INNER_C)),
        _const((D_INNER_C, D_MODEL)), _const((LANES, H_C * CHUNK)),
    ]
    out_shape = (jax.ShapeDtypeStruct((length, D_MODEL), F32),
                 jax.ShapeDtypeStruct((H_C * P_C, N_C), F32),
                 jax.ShapeDtypeStruct((8, CONV_DIM), F32))
    out_specs = (pl.BlockSpec((t, D_MODEL), row_blk), _full((H_C * P_C, N_C)), _full((8, CONV_DIM)))
    scratch = [pltpu.VMEM((t, C_COLS), F32), pltpu.VMEM((t + 8, CONV_DIM), F32),
               pltpu.VMEM((t, CONV_DIM), F32), pltpu.VMEM((t, LANES), F32), pltpu.VMEM((t, LANES), F32),
               pltpu.VMEM((t, D_INNER_C), BF16), pltpu.VMEM((N_C, H_C * P_C), F32)]
    return pl.pallas_call(
        _c_prompt_kernel, grid=(length // t,), in_specs=in_specs, out_specs=out_specs,
        out_shape=out_shape, scratch_shapes=scratch, name="c_prompt",
        compiler_params=pltpu.CompilerParams(dimension_semantics=("arbitrary",),
                                             vmem_limit_bytes=VMEM_LIMIT),
    )(x, *weights, _head_expand_matrix(CHUNK))


def _c_sample(x, weights, st_in, conv_in):
    nb, c, _ = x.shape
    t = nb * c
    vm = pl.BlockSpec(memory_space=pltpu.VMEM)
    out_shape = (jax.ShapeDtypeStruct((t, D_MODEL), F32),
                 jax.ShapeDtypeStruct((nb, H_C * P_C, N_C), F32),
                 jax.ShapeDtypeStruct((nb, CONV_W - 1, CONV_DIM), F32))
    scratch = [pltpu.VMEM((t, C_COLS), F32), pltpu.VMEM((c + 8, CONV_DIM), F32),
               pltpu.VMEM((t, CONV_DIM), F32), pltpu.VMEM((t, LANES), F32), pltpu.VMEM((t, LANES), F32),
               pltpu.VMEM((t, D_INNER_C), BF16)]
    return pl.pallas_call(
        _c_sample_kernel, in_specs=[vm] * 14, out_specs=(vm,) * 3, out_shape=out_shape,
        scratch_shapes=scratch, name="c_sample",
        compiler_params=pltpu.CompilerParams(vmem_limit_bytes=VMEM_LIMIT),
    )(x.reshape(t, D_MODEL), *weights, _head_expand_matrix(c), _head_expand_matrix(P_C),
      st_in.reshape(nb, H_C * P_C, N_C), conv_in)


def kernel(x_prompt, x_sample, cache_swa_k, cache_swa_v, state_gla, state_ssd, state_conv, rms_g, w_in_ab, w_gate_up_a, b_gate_a, g_out_a, g_q_b, g_k_b, sink_b, rel_bias, w_out_ab, w_in_c, conv_w_c, conv_b_c, dt_bias_c, a_log_c, d_skip_c, g_y_c, w_out_c):
    bp, seq_len, _ = x_prompt.shape
    nb, dec_len, _ = x_sample.shape
    assert bp == 1 and seq_len % PROMPT_BLOCK == 0 and seq_len >= WINDOW
    wab = _ab_weights(rms_g[0], w_in_ab[0], w_gate_up_a[0], b_gate_a[0], g_out_a[0], g_q_b[0], g_k_b[0],
                      w_out_ab[0])
    yp, gla_p, k_p, v_p = _ab_prompt(x_prompt[0], sink_b[0], rel_bias, wab)
    ys, gla_s, k_s, v_s = _ab_sample(x_sample, sink_b[0], rel_bias, wab, state_gla[0], cache_swa_k[0],
                                     cache_swa_v[0])
    wc = _c_weights(rms_g[1], w_in_c[0], conv_w_c[0], conv_b_c[0], dt_bias_c[0], a_log_c[0], d_skip_c[0],
                    g_y_c[0], w_out_c[0])
    yp, ssd_p, conv_p = _c_prompt(yp, wc)
    ys, ssd_s, conv_s = _c_sample(ys.reshape(nb, dec_len, D_MODEL), wc, state_ssd[0], state_conv[0])
    return (
        yp.reshape(1, seq_len, D_MODEL),
        ys.reshape(nb, dec_len, D_MODEL),
        gla_p.reshape(1, 1, H_A, DK_A, DV_A),
        k_p.reshape(1, 1, WINDOW, KVH_B, HD_B),
        v_p.reshape(1, 1, WINDOW, KVH_B, HD_B),
        ssd_p.reshape(1, 1, H_C, P_C, N_C),
        conv_p[8 - (CONV_W - 1):].reshape(1, 1, CONV_W - 1, CONV_DIM),
        gla_s.reshape(1, nb, H_A, DK_A, DV_A),
        k_s.reshape(1, nb, dec_len, KVH_B, HD_B),
        v_s.reshape(1, nb, dec_len, KVH_B, HD_B),
        ssd_s.reshape(1, nb, H_C, P_C, N_C),
        conv_s.reshape(1, nb, CONV_W - 1, CONV_DIM),
    )
```

```python
import math
import types

import numpy as np
import jax
import jax.numpy as jnp
from jax import lax
from jax.experimental import pallas as pl
from jax.experimental.pallas import tpu as pltpu

F32 = jnp.float32
BF16 = jnp.bfloat16

D_MODEL = 1024
CHUNK = 64
EPS = 1e-6
H_A = 4
DK_A = 64
DV_A = 128
GATE_RANK_A = 16
GATE_NORM_A = 16.0
H_B = 8
KVH_B = 2
G_B = H_B // KVH_B
HD_B = 64
WINDOW = 128
N_BUCKETS = 32
MAX_DISTANCE = 128
D_INNER_C = 2048
P_C = 64
H_C = 32
G_C = 4
HPG_C = 8
N_C = 128
CONV_W = 4
CONV_DIM = D_INNER_C + 2 * G_C * N_C

LANES = 128

A_Q, A_K, A_V, A_Z = 0, 256, 512, 1024
B_Q, B_K, B_V, B_Z = 1536, 2048, 2176, 2304
A_G = 2816
AB_COLS = 2944
C_Z, C_X, C_B, C_C, C_DT = 0, 2048, 4096, 4608, 5120
C_COLS = 5248

PROMPT_BLOCK = 256
VMEM_LIMIT = 60 * 1024 * 1024


def _dot(a, b):
    return jnp.dot(a.astype(BF16), b.astype(BF16), preferred_element_type=F32)


def _dot_nt(a, b):
    return lax.dot_general(a.astype(BF16), b.astype(BF16), (((1,), (1,)), ((), ())),
                           preferred_element_type=F32)


def _dot_tn(a, b):
    return lax.dot_general(a.astype(BF16), b.astype(BF16), (((0,), (0,)), ((), ())),
                           preferred_element_type=F32)


def _split_bf16(x, terms):
    out = []
    r = x
    for _ in range(terms):
        h = r.astype(BF16)
        out.append(h)
        r = r - h.astype(F32)
    return out


def _sel_dot(x, sel, terms):
    acc = None
    for h in _split_bf16(x, terms):
        d = jnp.dot(h, sel, preferred_element_type=F32)
        acc = d if acc is None else acc + d
    return acc


def _sel_dot_left(sel, x, terms):
    acc = None
    for h in _split_bf16(x, terms):
        d = jnp.dot(sel, h, preferred_element_type=F32)
        acc = d if acc is None else acc + d
    return acc


def _sel_dot_tn(x, sel, terms):
    acc = None
    for h in _split_bf16(x, terms):
        d = lax.dot_general(h, sel, (((0,), (0,)), ((), ())), preferred_element_type=F32)
        acc = d if acc is None else acc + d
    return acc


def _rms_rows(x):
    return x * lax.rsqrt(jnp.mean(x * x, axis=-1, keepdims=True) + EPS)


def _silu(x):
    return x * (1.0 / (1.0 + jnp.exp(-x)))


def _softplus(x):
    return jnp.maximum(x, 0.0) + jnp.log(1.0 + jnp.exp(-jnp.abs(x)))


def _log_sigmoid(x):
    return jnp.minimum(x, 0.0) - jnp.log(1.0 + jnp.exp(-jnp.abs(x)))


def _tril(n):
    r = lax.broadcasted_iota(jnp.int32, (n, n), 0)
    c = lax.broadcasted_iota(jnp.int32, (n, n), 1)
    return r >= c


def _ab_dense_in(x, rms_g, w_in_ref, gsum512, gsum128, gq, gk, proj_ref, qn_ref):
    h = _rms_rows(x) * rms_g
    proj_ref[...] = jnp.dot(h.astype(BF16), w_in_ref[...], preferred_element_type=F32)
    qb = proj_ref[:, B_Q:B_Q + 512]
    msq = _sel_dot(qb * qb, gsum512, 2) * (1.0 / HD_B)
    qn_ref[...] = qb * lax.rsqrt(msq + EPS) * gq
    kb = proj_ref[:, B_K:B_K + 128]
    msk = _sel_dot(kb * kb, gsum128, 2) * (1.0 / HD_B)
    kn = kb * lax.rsqrt(msk + EPS) * gk
    vb = proj_ref[:, B_V:B_V + 128]
    return kn, vb


def _advance(gens):
    alive = []
    for gen in gens:
        try:
            next(gen)
            alive.append(gen)
        except StopIteration:
            pass
    return alive


def _interleave(gens):
    gens = list(gens)
    while gens:
        gens = _advance(gens)


def _ab_chunk(r, c, kb, proj_ref, qn_ref, o_ref, state, k_band, v_band, bias_ref, sink_ref,
              w_gu, b_gate, g_out, first_valid_col):
    rows = pl.ds(r, c)
    gate = _dot(proj_ref[rows, A_G:A_G + LANES], w_gu) + b_gate
    yield
    g = _log_sigmoid(gate) * (1.0 / GATE_NORM_A)
    tril = _tril(c)
    b = _sel_dot_left(tril.astype(BF16), g, 3)
    bl_rows = _sel_dot_tn(g, jnp.ones((c, LANES), BF16), 3)
    yield
    bl = b[c - 1:c, :]
    q = proj_ref[rows, A_Q:A_Q + 256] * (DK_A ** -0.5)
    k = proj_ref[rows, A_K:A_K + 256]
    v = proj_ref[rows, A_V:A_V + 512].astype(BF16)
    qe = q * jnp.exp(b)
    kd = (k * jnp.exp(-b)).astype(BF16)
    kd2 = (k * jnp.exp(bl - b)).astype(BF16)
    lane128 = lax.broadcasted_iota(jnp.int32, (c, LANES), 1)
    row128 = lax.broadcasted_iota(jnp.int32, (LANES, LANES), 0)
    qm, att = [], []
    for hd in range(H_A):
        p, j = divmod(hd, 2)
        qm.append(jnp.where((lane128 // DK_A) == j, qe[:, p * 128:(p + 1) * 128], 0.0).astype(BF16))
        att.append(_dot_nt(qm[hd], kd[:, p * 128:(p + 1) * 128]))
    upd = [_dot_tn(kd2[:, p * 128:(p + 1) * 128], v[:, p * 256:(p + 1) * 256]) for p in range(2)]
    yield
    oh = [_dot(jnp.where(tril, att[hd], 0.0), v[:, hd * 128:(hd + 1) * 128]) for hd in range(H_A)]
    yield
    s_prev = state[0]
    s_new = []
    for p in range(2):
        sp = s_prev[p * 128:(p + 1) * 128, :]
        sp_bf = sp.astype(BF16)
        for j in range(2):
            oh[2 * p + j] = oh[2 * p + j] + _dot(qm[2 * p + j], sp_bf)
        u = jnp.where(row128 < DK_A, upd[p][:, :128], upd[p][:, 128:])
        s_new.append(jnp.exp(bl_rows[p * 128:(p + 1) * 128, :]) * sp + u)
    state[0] = jnp.concatenate(s_new, axis=0)
    yield
    for hd in range(H_A):
        z = proj_ref[rows, A_Z + hd * 128:A_Z + (hd + 1) * 128]
        o_ref[rows, hd * 128:(hd + 1) * 128] = (_rms_rows(oh[hd]) * g_out * _silu(z)).astype(BF16)
    yield
    lane256 = lax.broadcasted_iota(jnp.int32, (c, 256), 1) // HD_B
    srow = lax.broadcasted_iota(jnp.int32, (G_B * c, 1), 0) // c
    scores = []
    for kvh in range(KVH_B):
        qn = qn_ref[rows, kvh * 256:(kvh + 1) * 256]
        qs = jnp.concatenate([jnp.where(lane256 == gq_, qn, 0.0) for gq_ in range(G_B)], axis=0)
        scores.append(_dot_nt(qs, k_band(kvh)))
    yield
    probs, dens = [], []
    for kvh in range(KVH_B):
        s = scores[kvh] * (HD_B ** -0.5) + bias_ref[kvh]
        if first_valid_col is not None:
            col = lax.broadcasted_iota(jnp.int32, (G_B * c, kb), 1)
            s = jnp.where(col >= first_valid_col, s, -jnp.inf)
        sink = jnp.zeros((G_B * c, 1), F32)
        for gq_ in range(G_B):
            sink = jnp.where(srow == gq_, sink_ref[kvh * G_B + gq_], sink)
        m = jnp.maximum(jnp.max(s, axis=-1, keepdims=True), sink)
        pr = jnp.exp(s - m)
        dens.append(jnp.sum(pr, axis=-1, keepdims=True) + jnp.exp(sink - m))
        probs.append(pr.astype(BF16))
    yield
    outs = [_dot(probs[kvh], v_band(kvh)) for kvh in range(KVH_B)]
    yield
    for kvh in range(KVH_B):
        ost = outs[kvh] / dens[kvh]
        ob = jnp.zeros((c, 256), F32)
        for gq_ in range(G_B):
            ob = ob + jnp.where(lane256 == gq_, ost[gq_ * c:(gq_ + 1) * c, :], 0.0)
        z = proj_ref[rows, B_Z + kvh * 256:B_Z + (kvh + 1) * 256]
        o_ref[rows, 512 + kvh * 256:512 + (kvh + 1) * 256] = (ob * _silu(z)).astype(BF16)


def _ab_prompt_kernel(sink_ref, x_ref, rms_g_ref, w_in_ref, w_gu_ref, b_gate_ref, g_out_ref, gq_ref,
                      gk_ref, bias_ref, w_out_ref, gsum512_ref, gsum128_ref, tile_ref,
                      y_ref, s_out_ref, k_out_ref, v_out_ref,
                      proj_ref, qn_ref, o_ref, kband_ref, vband_ref, s_ref):
    t = x_ref.shape[0]
    c = CHUNK
    kb = WINDOW + c
    nchunk = t // c
    step = pl.program_id(0)

    @pl.when(step == 0)
    def _():
        s_ref[...] = jnp.zeros_like(s_ref)
        kband_ref[:, t:t + WINDOW, :] = jnp.zeros((KVH_B, WINDOW, 256), BF16)
        vband_ref[:, t:t + WINDOW, :] = jnp.zeros((KVH_B, WINDOW, 256), BF16)

    for kvh in range(KVH_B):
        kband_ref[kvh, 0:WINDOW, :] = kband_ref[kvh, t:t + WINDOW, :]
        vband_ref[kvh, 0:WINDOW, :] = vband_ref[kvh, t:t + WINDOW, :]

    x = x_ref[...]
    kn, vb = _ab_dense_in(x, rms_g_ref[...], w_in_ref, gsum512_ref[...], gsum128_ref[...],
                          gq_ref[...], gk_ref[...], proj_ref, qn_ref)
    k_out_ref[...] = kn[t - WINDOW:, :]
    v_out_ref[...] = vb[t - WINDOW:, :]
    for kvh in range(KVH_B):
        kband_ref[kvh, WINDOW:WINDOW + t, :] = jnp.dot(
            kn.astype(BF16), tile_ref[kvh], preferred_element_type=F32).astype(BF16)
        vband_ref[kvh, WINDOW:WINDOW + t, :] = jnp.dot(
            vb.astype(BF16), tile_ref[kvh], preferred_element_type=F32).astype(BF16)

    w_gu = w_gu_ref[...]
    b_gate = b_gate_ref[...]
    g_out = g_out_ref[...]

    state = [s_ref[...]]

    def chunk(i):
        r = i * c
        return _ab_chunk(r, c, kb, proj_ref, qn_ref, o_ref, state,
                         lambda kvh: kband_ref[kvh, pl.ds(r, kb), :],
                         lambda kvh: vband_ref[kvh, pl.ds(r, kb), :],
                         bias_ref, sink_ref, w_gu, b_gate, g_out, (2 - (step * nchunk + i)) * c)

    _interleave(chunk(i) for i in range(nchunk))
    s_ref[...] = state[0]
    y_ref[...] = x + jnp.dot(o_ref[...], w_out_ref[...], preferred_element_type=F32)
    s_out_ref[...] = s_ref[...]


def _ab_sample_kernel(sink_ref, x_ref, rms_g_ref, w_in_ref, w_gu_ref, b_gate_ref, g_out_ref, gq_ref,
                      gk_ref, bias_ref, w_out_ref, gsum512_ref, gsum128_ref, tile_ref,
                      s_in_ref, kc_ref, vc_ref,
                      y_ref, s_out_ref, k_out_ref, v_out_ref,
                      proj_ref, qn_ref, o_ref, kband_ref, vband_ref):
    nb, wc = kc_ref.shape[0], kc_ref.shape[1]
    t = x_ref.shape[0]
    c = t // nb
    kb = wc + c
    x = x_ref[...]
    kn, vb = _ab_dense_in(x, rms_g_ref[...], w_in_ref, gsum512_ref[...], gsum128_ref[...],
                          gq_ref[...], gk_ref[...], proj_ref, qn_ref)
    k_out_ref[...] = kn
    v_out_ref[...] = vb
    for kvh in range(KVH_B):
        kt = jnp.dot(kn.astype(BF16), tile_ref[kvh], preferred_element_type=F32).astype(BF16)
        vt = jnp.dot(vb.astype(BF16), tile_ref[kvh], preferred_element_type=F32).astype(BF16)
        for bi in range(nb):
            kband_ref[kvh, bi, 0:wc, :] = jnp.dot(
                kc_ref[bi].astype(BF16), tile_ref[kvh], preferred_element_type=F32).astype(BF16)
            vband_ref[kvh, bi, 0:wc, :] = jnp.dot(
                vc_ref[bi].astype(BF16), tile_ref[kvh], preferred_element_type=F32).astype(BF16)
            kband_ref[kvh, bi, wc:kb, :] = kt[bi * c:(bi + 1) * c, :]
            vband_ref[kvh, bi, wc:kb, :] = vt[bi * c:(bi + 1) * c, :]

    w_gu = w_gu_ref[...]
    b_gate = b_gate_ref[...]
    g_out = g_out_ref[...]

    def seq(bi, carry):
        r = pl.multiple_of(bi * c, c)
        state = [s_in_ref[bi]]
        _interleave([_ab_chunk(r, c, kb, proj_ref, qn_ref, o_ref, state,
                               lambda kvh: kband_ref[kvh, bi], lambda kvh: vband_ref[kvh, bi],
                               bias_ref, sink_ref, w_gu, b_gate, g_out, None)])
        s_out_ref[bi] = state[0]
        return carry

    lax.fori_loop(0, nb, seq, 0)
    y_ref[...] = x + jnp.dot(o_ref[...], w_out_ref[...], preferred_element_type=F32)


GROUP_W = D_INNER_C // G_C


def _c_chunk(r, c, io, state, e_s, e_p, diag, neg_mask, bd_mask, dskip, g_y):
    dtc = io.dt(r)
    acum = _sel_dot_left(_tril(c).astype(BF16), io.da(r), 3)
    yield
    xa_s = _sel_dot(acum, e_s, 2)
    xa_p = xa_s if c == P_C else _sel_dot(acum, e_p, 2)
    dt_p = _sel_dot(dtc, e_p, 2)
    yield
    a_row = jnp.sum(xa_s * diag, axis=0, keepdims=True)
    wmat = jnp.exp((xa_s - a_row) + neg_mask)
    al_p = xa_p[c - 1:c, :]
    ea_p = jnp.exp(xa_p)
    xs = io.x(r)
    xdt = xs * dt_p
    xdt_bf = xdt.astype(BF16)
    xw = (xdt * jnp.exp(al_p - xa_p)).astype(BF16)
    dec = jnp.exp(al_p)
    bg, cg, cb = [], [], []
    for g in range(G_C):
        bg.append(io.b(r, g).astype(BF16))
        cg.append(io.c(r, g).astype(BF16))
        cb.append(_dot_nt(cg[g], jnp.concatenate([bg[g]] * HPG_C, axis=0)))
    yield
    ys = []
    for g in range(G_C):
        mg = (cb[g] * wmat[:, g * HPG_C * c:(g + 1) * HPG_C * c]).astype(BF16)
        for j in range(2):
            xj = xdt_bf[:, g * GROUP_W + j * 256:g * GROUP_W + (j + 1) * 256]
            bd = jnp.concatenate([xj] * 4, axis=0) * bd_mask
            ys.append(jnp.dot(mg[:, j * 4 * c:(j + 1) * 4 * c], bd, preferred_element_type=F32))
    upd = [_dot_tn(bg[g], xw[:, g * GROUP_W:(g + 1) * GROUP_W]) for g in range(G_C)]
    yield
    st_prev = state[0]
    y_inter, st_new = [], []
    for g in range(G_C):
        sl = slice(g * GROUP_W, (g + 1) * GROUP_W)
        y_inter.append(_dot(cg[g], st_prev[:, sl]))
        st_new.append(st_prev[:, sl] * dec[:, sl] + upd[g])
    state[0] = jnp.concatenate(st_new, axis=1)
    yield
    for g in range(G_C):
        sl = slice(g * GROUP_W, (g + 1) * GROUP_W)
        y = jnp.concatenate(ys[2 * g:2 * g + 2], axis=1) + y_inter[g] * ea_p[:, sl]
        y = y + dskip[:, sl] * xs[:, sl]
        y = y * _silu(io.z(r, g))
        io.put_o(r, g, (_rms_rows(y) * g_y[:, sl]).astype(BF16))


def _c_dt(dt_cols, dt_bias, a_log):
    dt = _softplus(dt_cols + dt_bias)
    return dt, dt * (-jnp.exp(a_log))


CONV_PITCH = PROMPT_BLOCK // 8 + 1
CONV_ROWS = 8 * CONV_PITCH
CONV_TILES = CONV_DIM // LANES
C_ROUNDS = 12
C_IN_UNITS = (4, 4, 4, 4, 4, 4, 4, 2, 4, 2, 2, 3)
C_OUT_ROUNDS = (3, 5, 8, 10)
assert len(C_IN_UNITS) == C_ROUNDS and sum(C_IN_UNITS) * LANES == C_COLS


def _conv_tile(ubuf_ref, act_ref, j, conv_w, conv_b):
    w = [jnp.broadcast_to(conv_w[i:i + 1, j * LANES:(j + 1) * LANES], (8, LANES)) for i in range(CONV_W)]
    b = jnp.broadcast_to(conv_b[:, j * LANES:(j + 1) * LANES], (8, LANES))
    for a in range(CONV_PITCH):
        acc = b
        for i in range(CONV_W):
            acc = acc + w[i] * ubuf_ref[j, pl.ds(8 - (CONV_W - 1) + i + a, 8, stride=CONV_PITCH), :]
        act_ref[j, pl.ds(a, 8, stride=CONV_PITCH), :] = _silu(acc)


def _c_prompt_kernel(xin_ref, xres_ref, rms_g_ref, w_in_ref, conv_w_ref, conv_b_ref, dt_bias_ref,
                     a_log_ref, dskip_ref, g_y_ref, w_out_ref, e_s_ref, diag_ref, neg_mask_ref, bd_mask_ref,
                     y_ref, st_out_ref, conv_out_ref,
                     proj_ref, ubuf_ref, act_ref, dt_ref, da_ref, o_ref, st_ref):
    t = xin_ref.shape[0]
    c = CHUNK
    s = pl.program_id(0)
    n_blocks = pl.num_programs(0) - 2

    @pl.when(s == 0)
    def _():
        proj_ref[1] = jnp.zeros(proj_ref.shape[1:], F32)
        dt_ref[1] = jnp.zeros(dt_ref.shape[1:], F32)
        da_ref[1] = jnp.zeros(da_ref.shape[1:], F32)
        o_ref[...] = jnp.zeros_like(o_ref)
        ubuf_ref[...] = jnp.zeros_like(ubuf_ref)

    @pl.when(s <= 1)
    def _():
        st_ref[...] = jnp.zeros_like(st_ref)
        ubuf_ref[:, CONV_ROWS:CONV_ROWS + 8, :] = jnp.zeros((CONV_TILES, 8, LANES), F32)

    def in_stage(slot_in):
        h = (_rms_rows(xin_ref[...]) * rms_g_ref[...]).astype(BF16)
        lo = 0
        for units in C_IN_UNITS:
            if lo:
                yield
            hi = lo + units * LANES
            slab = jnp.dot(h, w_in_ref[:, lo:hi], preferred_element_type=F32)
            proj_ref[slot_in, :, lo:hi] = slab
            if hi == C_COLS:
                dt, da = _c_dt(slab[:, C_DT - lo:], dt_bias_ref[...], a_log_ref[...])
                dt_ref[slot_in] = dt
                da_ref[slot_in] = da
            lo = hi

    def act_rows(r, j):
        return act_ref[j, pl.ds(8 + r, c), :]

    n_x = D_INNER_C // LANES
    n_g = N_C // LANES

    def mix_stage(slot_mix):
        def put_o(r, g, value):
            o_ref[slot_mix, pl.ds(r, c), g * GROUP_W:(g + 1) * GROUP_W] = value

        io = types.SimpleNamespace(
            dt=lambda r: dt_ref[slot_mix, pl.ds(r, c), :],
            da=lambda r: da_ref[slot_mix, pl.ds(r, c), :],
            x=lambda r: jnp.concatenate([act_rows(r, j) for j in range(n_x)], axis=1),
            b=lambda r, g: act_rows(r, n_x + g * n_g),
            c=lambda r, g: act_rows(r, n_x + (G_C + g) * n_g),
            z=lambda r, g: proj_ref[slot_mix, pl.ds(r, c), C_Z + g * GROUP_W:C_Z + (g + 1) * GROUP_W],
            put_o=put_o)
        conv_w = conv_w_ref[...]
        conv_b = conv_b_ref[...]
        conv_rounds = C_ROUNDS - 6
        per_round = CONV_TILES // conv_rounds
        for rnd in range(conv_rounds):
            if rnd:
                yield
            for j in range(rnd * per_round, (rnd + 1) * per_round):
                ubuf_ref[j, 8:16, :] = ubuf_ref[j, CONV_ROWS:CONV_ROWS + 8, :]
                ubuf_ref[j, 16:16 + t, :] = proj_ref[slot_mix, :, C_X + j * LANES:C_X + (j + 1) * LANES]
                _conv_tile(ubuf_ref, act_ref, j, conv_w, conv_b)
        state = [st_ref[...]]
        e_s = e_s_ref[...]
        chunks = [_c_chunk(i * c, c, io, state, e_s, e_s, diag_ref[...], neg_mask_ref[...], bd_mask_ref[...],
                           dskip_ref[...], g_y_ref[...]) for i in range(t // c)]
        while chunks:
            yield
            chunks = _advance(chunks)
        st_ref[...] = state[0]

    def out_stage(slot_in):
        quarter = 0
        for rnd in range(C_ROUNDS):
            if rnd:
                yield
            if rnd in C_OUT_ROUNDS:
                cols = slice(quarter * 256, (quarter + 1) * 256)
                y_ref[:, cols] = xres_ref[:, cols] + jnp.dot(o_ref[slot_in], w_out_ref[:, cols],
                                                             preferred_element_type=F32)
                quarter += 1

    slot_in = s % 2
    _interleave([in_stage(slot_in), mix_stage(1 - slot_in), out_stage(slot_in)])

    @pl.when(s == n_blocks)
    def _():
        st_out_ref[...] = st_ref[...].T
        conv_out_ref[...] = proj_ref[1 - s % 2, t - 8:t, C_X:C_X + CONV_DIM]


def _conv_rows(ubuf, nrows, conv_w, conv_b):
    acc = conv_b
    for i in range(CONV_W):
        acc = acc + conv_w[i:i + 1, :] * ubuf[pl.ds(8 - (CONV_W - 1) + i, nrows), :]
    return _silu(acc)


def _c_sample_kernel(x_ref, rms_g_ref, w_in_ref, conv_w_ref, conv_b_ref, dt_bias_ref, a_log_ref,
                     dskip_ref, g_y_ref, w_out_ref, e_s_ref, e_p_ref, diag_ref, neg_mask_ref, bd_mask_ref,
                     st_in_ref, conv_in_ref,
                     y_ref, st_out_ref, conv_out_ref,
                     proj_ref, ubuf_ref, act_ref, dt_ref, da_ref, o_ref):
    nb = st_in_ref.shape[0]
    t = x_ref.shape[0]
    c = t // nb
    x = x_ref[...]
    h = (_rms_rows(x) * rms_g_ref[...]).astype(BF16)
    proj_ref[...] = jnp.dot(h, w_in_ref[...], preferred_element_type=F32)
    dt, da = _c_dt(proj_ref[:, C_DT:C_DT + LANES], dt_bias_ref[...], a_log_ref[...])
    dt_ref[...] = dt
    da_ref[...] = da
    conv_w = conv_w_ref[...]
    conv_b = conv_b_ref[...]
    ubuf_ref[0:8, :] = jnp.zeros((8, CONV_DIM), F32)
    for bi in range(nb):
        ubuf_ref[8 - (CONV_W - 1):8, :] = conv_in_ref[bi]
        ubuf_ref[8:8 + c, :] = proj_ref[bi * c:(bi + 1) * c, C_X:C_X + CONV_DIM]
        act_ref[bi * c:(bi + 1) * c, :] = _conv_rows(ubuf_ref, c, conv_w, conv_b)
        conv_out_ref[bi] = ubuf_ref[8 + c - (CONV_W - 1):8 + c, :]

    def put_o(r, g, value):
        o_ref[pl.ds(r, c), g * GROUP_W:(g + 1) * GROUP_W] = value

    io = types.SimpleNamespace(
        dt=lambda r: dt_ref[pl.ds(r, c), :],
        da=lambda r: da_ref[pl.ds(r, c), :],
        x=lambda r: act_ref[pl.ds(r, c), 0:D_INNER_C],
        b=lambda r, g: act_ref[pl.ds(r, c), D_INNER_C + g * N_C:D_INNER_C + (g + 1) * N_C],
        c=lambda r, g: act_ref[pl.ds(r, c), D_INNER_C + (G_C + g) * N_C:D_INNER_C + (G_C + g + 1) * N_C],
        z=lambda r, g: proj_ref[pl.ds(r, c), C_Z + g * GROUP_W:C_Z + (g + 1) * GROUP_W],
        put_o=put_o)
    e_s = e_s_ref[...]
    e_p = e_p_ref[...]
    diag = diag_ref[...]
    neg_mask = neg_mask_ref[...]
    bd_mask = bd_mask_ref[...]
    dskip = dskip_ref[...]
    g_y = g_y_ref[...]

    def seq(bi, carry):
        r = pl.multiple_of(bi * c, c)
        state = [st_in_ref[bi].T]
        _interleave([_c_chunk(r, c, io, state, e_s, e_p, diag, neg_mask, bd_mask, dskip, g_y)])
        st_out_ref[bi] = state[0].T
        return carry

    lax.fori_loop(0, nb, seq, 0)
    y_ref[...] = x + jnp.dot(o_ref[...], w_out_ref[...], preferred_element_type=F32)


def _bucket_table(q_off, k_off):
    n = q_off[:, None] - k_off[None, :]
    half = N_BUCKETS // 2
    max_exact = half // 2
    side = np.where(n < 0, half, 0)
    n = np.abs(n)
    nf = np.maximum(n, max_exact).astype(np.float32)
    large = max_exact + (np.log(nf / np.float32(max_exact)) / np.float32(math.log(MAX_DISTANCE / max_exact))
                         * np.float32(half - max_exact)).astype(np.int32)
    large = np.minimum(large, half - 1)
    return side + np.where(n < max_exact, n, large)


def _bias_rows(rel_bias, q_off, k_off):
    bucket = _bucket_table(q_off, k_off)
    bias = jnp.moveaxis(rel_bias[bucket].astype(F32), -1, 0)
    return bias.reshape(KVH_B, G_B * q_off.shape[0], k_off.shape[0])


def _group_sum_matrix(width, group):
    idx = np.arange(width) // group
    return jnp.asarray(idx[:, None] == idx[None, :], BF16)


def _kv_tile_matrices():
    m = np.zeros((KVH_B, KVH_B * HD_B, G_B * HD_B), np.float32)
    for kvh in range(KVH_B):
        for gq in range(G_B):
            m[kvh, kvh * HD_B + np.arange(HD_B), gq * HD_B + np.arange(HD_B)] = 1.0
    return jnp.asarray(m, BF16)


def _head_expand_matrix(per_head):
    m = np.zeros((LANES, H_C * per_head), np.float32)
    for h in range(H_C):
        m[h, h * per_head:(h + 1) * per_head] = 1.0
    return jnp.asarray(m, BF16)


def _full(shape):
    return pl.BlockSpec(shape, lambda *_: (0,) * len(shape))


def _const(shape):
    return pl.BlockSpec(shape, lambda *_: (0,) * len(shape), pipeline_mode=pl.Buffered(1))


def _ab_weights(rms_g, w_in, w_gate_up, b_gate, g_out, g_q, g_k, w_out):
    w_in_r = jnp.concatenate(
        [w_in[:, :1024], w_in[:, 1024 + GATE_RANK_A:], w_in[:, 1024:1024 + GATE_RANK_A],
         jnp.zeros((D_MODEL, AB_COLS - w_in.shape[1]), w_in.dtype)], axis=1).astype(BF16)
    w_gu = jnp.concatenate([w_gate_up, jnp.zeros((LANES - GATE_RANK_A, H_A * DK_A), w_gate_up.dtype)],
                           axis=0).astype(BF16)
    return (rms_g.reshape(1, D_MODEL), w_in_r, w_gu, b_gate.reshape(1, -1), g_out.reshape(1, DV_A),
            jnp.tile(g_q, H_B).reshape(1, -1), jnp.tile(g_k, KVH_B).reshape(1, -1), w_out.astype(BF16))


def _ab_prompt(x, sink, rel_bias, weights):
    rms_g, w_in, w_gu, b_gate, g_out, gq, gk, w_out = weights
    length = x.shape[0]
    t = PROMPT_BLOCK
    c = CHUNK
    kb = WINDOW + c
    bias = _bias_rows(rel_bias, np.arange(c), np.arange(kb) - WINDOW)
    row_blk = lambda i: (i, 0)
    in_specs = [
        pl.BlockSpec(memory_space=pltpu.SMEM),
        pl.BlockSpec((t, D_MODEL), row_blk),
        _const((1, D_MODEL)), _const((D_MODEL, AB_COLS)), _const((LANES, 256)), _const((1, 256)),
        _const((1, DV_A)), _const((1, 512)), _const((1, 128)), _const((KVH_B, G_B * c, kb)),
        _const((D_MODEL, D_MODEL)), _const((512, 512)), _const((128, 128)), _const((KVH_B, 128, 256)),
    ]
    out_shape = (jax.ShapeDtypeStruct((length, D_MODEL), F32),
                 jax.ShapeDtypeStruct((H_A * DK_A, DV_A), F32),
                 jax.ShapeDtypeStruct((WINDOW, KVH_B * HD_B), F32),
                 jax.ShapeDtypeStruct((WINDOW, KVH_B * HD_B), F32))
    out_specs = (pl.BlockSpec((t, D_MODEL), row_blk), _full((H_A * DK_A, DV_A)),
                 _full((WINDOW, KVH_B * HD_B)), _full((WINDOW, KVH_B * HD_B)))
    scratch = [pltpu.VMEM((t, AB_COLS), F32), pltpu.VMEM((t, 512), F32), pltpu.VMEM((t, D_MODEL), BF16),
               pltpu.VMEM((KVH_B, t + WINDOW, 256), BF16), pltpu.VMEM((KVH_B, t + WINDOW, 256), BF16),
               pltpu.VMEM((H_A * DK_A, DV_A), F32)]
    return pl.pallas_call(
        _ab_prompt_kernel, grid=(length // t,), in_specs=in_specs, out_specs=out_specs,
        out_shape=out_shape, scratch_shapes=scratch, name="ab_prompt",
        compiler_params=pltpu.CompilerParams(dimension_semantics=("arbitrary",),
                                             vmem_limit_bytes=VMEM_LIMIT),
    )(sink, x, rms_g, w_in, w_gu, b_gate, g_out, gq, gk, bias, w_out,
      _group_sum_matrix(512, HD_B), _group_sum_matrix(128, HD_B), _kv_tile_matrices())


def _ab_sample(x, sink, rel_bias, weights, s_in, k_cache, v_cache):
    rms_g, w_in, w_gu, b_gate, g_out, gq, gk, w_out = weights
    nb, c, _ = x.shape
    wc = k_cache.shape[1]
    kb = wc + c
    t = nb * c
    bias = _bias_rows(rel_bias, np.arange(c), np.arange(kb) - wc)
    vm = pl.BlockSpec(memory_space=pltpu.VMEM)
    in_specs = [pl.BlockSpec(memory_space=pltpu.SMEM)] + [vm] * 16
    out_shape = (jax.ShapeDtypeStruct((t, D_MODEL), F32),
                 jax.ShapeDtypeStruct((nb, H_A * DK_A, DV_A), F32),
                 jax.ShapeDtypeStruct((t, KVH_B * HD_B), F32),
                 jax.ShapeDtypeStruct((t, KVH_B * HD_B), F32))
    scratch = [pltpu.VMEM((t, AB_COLS), F32), pltpu.VMEM((t, 512), F32), pltpu.VMEM((t, D_MODEL), BF16),
               pltpu.VMEM((KVH_B, nb, kb, 256), BF16), pltpu.VMEM((KVH_B, nb, kb, 256), BF16)]
    return pl.pallas_call(
        _ab_sample_kernel, in_specs=in_specs, out_specs=(vm,) * 4, out_shape=out_shape,
        scratch_shapes=scratch, name="ab_sample",
        compiler_params=pltpu.CompilerParams(vmem_limit_bytes=VMEM_LIMIT),
    )(sink, x.reshape(t, D_MODEL), rms_g, w_in, w_gu, b_gate, g_out, gq, gk, bias, w_out,
      _group_sum_matrix(512, HD_B), _group_sum_matrix(128, HD_B), _kv_tile_matrices(),
      s_in.reshape(nb, H_A * DK_A, DV_A), k_cache.reshape(nb, wc, KVH_B * HD_B),
      v_cache.reshape(nb, wc, KVH_B * HD_B))


def _c_weights(rms_g, w_in, conv_w, conv_b, dt_bias, a_log, d_skip, g_y, w_out):
    w_in_p = jnp.concatenate([w_in, jnp.zeros((D_MODEL, C_COLS - w_in.shape[1]), w_in.dtype)],
                             axis=1).astype(BF16)
    pad = lambda v: jnp.concatenate([v, jnp.zeros((LANES - H_C,), v.dtype)]).reshape(1, LANES)
    return (rms_g.reshape(1, D_MODEL), w_in_p, conv_w, conv_b.reshape(1, CONV_DIM), pad(dt_bias), pad(a_log),
            jnp.repeat(d_skip, P_C).reshape(1, D_INNER_C), g_y.reshape(1, D_INNER_C), w_out.astype(BF16))


def _ssd_masks(c):
    tok = np.arange(c)[:, None]
    src = np.arange(H_C * c)[None, :] % c
    diag = (tok == src).astype(np.float32)
    neg = np.where(src <= tok, 0.0, -np.inf).astype(np.float32)
    blk = (np.arange(4 * c)[:, None] // c) == (np.arange(256)[None, :] // P_C)
    return jnp.asarray(diag), jnp.asarray(neg), jnp.asarray(blk, BF16)


def _c_prompt(x, weights):
    length = x.shape[0]
    t = PROMPT_BLOCK
    c = CHUNK
    n_blocks = length // t
    in_blk = lambda s: (jnp.minimum(s, n_blocks - 1), 0)
    out_blk = lambda s: (jnp.clip(s - 2, 0, n_blocks - 1), 0)
    in_specs = [
        pl.BlockSpec((t, D_MODEL), in_blk), pl.BlockSpec((t, D_MODEL), out_blk),
        _const((1, D_MODEL)), _const((D_MODEL, C_COLS)), _const((CONV_W, CONV_DIM)), _const((1, CONV_DIM)),
        _const((1, LANES)), _const((1, LANES)), _const((1, D_INNER_C)), _const((1, D_INNER_C)),
        _const((D_INNER_C, D_MODEL)), _const((LANES, H_C * c)),
        _const((c, H_C * c)), _const((c, H_C * c)), _const((4 * c, 256)),
    ]
    out_shape = (jax.ShapeDtypeStruct((length, D_MODEL), F32),
                 jax.ShapeDtypeStruct((H_C * P_C, N_C), F32),
                 jax.ShapeDtypeStruct((8, CONV_DIM), F32))
    out_specs = (pl.BlockSpec((t, D_MODEL), out_blk), _full((H_C * P_C, N_C)), _full((8, CONV_DIM)))
    scratch = [pltpu.VMEM((2, t, C_COLS), F32), pltpu.VMEM((CONV_TILES, CONV_ROWS + 8, LANES), F32),
               pltpu.VMEM((CONV_TILES, CONV_ROWS, LANES), F32), pltpu.VMEM((2, t, LANES), F32),
               pltpu.VMEM((2, t, LANES), F32), pltpu.VMEM((2, t, D_INNER_C), BF16),
               pltpu.VMEM((N_C, H_C * P_C), F32)]
    return pl.pallas_call(
        _c_prompt_kernel, grid=(n_blocks + 2,), in_specs=in_specs, out_specs=out_specs,
        out_shape=out_shape, scratch_shapes=scratch, name="c_prompt",
        compiler_params=pltpu.CompilerParams(dimension_semantics=("arbitrary",),
                                             vmem_limit_bytes=VMEM_LIMIT),
    )(x, x, *weights, _head_expand_matrix(c), *_ssd_masks(c))


def _c_sample(x, weights, st_in, conv_in):
    nb, c, _ = x.shape
    t = nb * c
    vm = pl.BlockSpec(memory_space=pltpu.VMEM)
    out_shape = (jax.ShapeDtypeStruct((t, D_MODEL), F32),
                 jax.ShapeDtypeStruct((nb, H_C * P_C, N_C), F32),
                 jax.ShapeDtypeStruct((nb, CONV_W - 1, CONV_DIM), F32))
    scratch = [pltpu.VMEM((t, C_COLS), F32), pltpu.VMEM((c + 8, CONV_DIM), F32),
               pltpu.VMEM((t, CONV_DIM), F32), pltpu.VMEM((t, LANES), F32), pltpu.VMEM((t, LANES), F32),
               pltpu.VMEM((t, D_INNER_C), BF16)]
    return pl.pallas_call(
        _c_sample_kernel, in_specs=[vm] * 17, out_specs=(vm,) * 3, out_shape=out_shape,
        scratch_shapes=scratch, name="c_sample",
        compiler_params=pltpu.CompilerParams(vmem_limit_bytes=VMEM_LIMIT),
    )(x.reshape(t, D_MODEL), *weights, _head_expand_matrix(c), _head_expand_matrix(P_C), *_ssd_masks(c),
      st_in.reshape(nb, H_C * P_C, N_C), conv_in)


def kernel(x_prompt, x_sample, cache_swa_k, cache_swa_v, state_gla, state_ssd, state_conv, rms_g, w_in_ab, w_gate_up_a, b_gate_a, g_out_a, g_q_b, g_k_b, sink_b, rel_bias, w_out_ab, w_in_c, conv_w_c, conv_b_c, dt_bias_c, a_log_c, d_skip_c, g_y_c, w_out_c):
    bp, seq_len, _ = x_prompt.shape
    nb, dec_len, _ = x_sample.shape
    assert bp == 1 and seq_len % PROMPT_BLOCK == 0 and seq_len >= WINDOW
    wab = _ab_weights(rms_g[0], w_in_ab[0], w_gate_up_a[0], b_gate_a[0], g_out_a[0], g_q_b[0], g_k_b[0],
                      w_out_ab[0])
    yp, gla_p, k_p, v_p = _ab_prompt(x_prompt[0], sink_b[0], rel_bias, wab)
    ys, gla_s, k_s, v_s = _ab_sample(x_sample, sink_b[0], rel_bias, wab, state_gla[0], cache_swa_k[0],
                                     cache_swa_v[0])
    wc = _c_weights(rms_g[1], w_in_c[0], conv_w_c[0], conv_b_c[0], dt_bias_c[0], a_log_c[0], d_skip_c[0],
                    g_y_c[0], w_out_c[0])
    yp, ssd_p, conv_p = _c_prompt(yp, wc)
    ys, ssd_s, conv_s = _c_sample(ys.reshape(nb, dec_len, D_MODEL), wc, state_ssd[0], state_conv[0])
    return (
        yp.reshape(1, seq_len, D_MODEL),
        ys.reshape(nb, dec_len, D_MODEL),
        gla_p.reshape(1, 1, H_A, DK_A, DV_A),
        k_p.reshape(1, 1, WINDOW, KVH_B, HD_B),
        v_p.reshape(1, 1, WINDOW, KVH_B, HD_B),
        ssd_p.reshape(1, 1, H_C, P_C, N_C),
        conv_p[8 - (CONV_W - 1):].reshape(1, 1, CONV_W - 1, CONV_DIM),
        gla_s.reshape(1, nb, H_A, DK_A, DV_A),
        k_s.reshape(1, nb, dec_len, KVH_B, HD_B),
        v_s.reshape(1, nb, dec_len, KVH_B, HD_B),
        ssd_s.reshape(1, nb, H_C, P_C, N_C),
        conv_s.reshape(1, nb, CONV_W - 1, CONV_DIM),
    )
```

```python
import math
import types

import numpy as np
import jax
import jax.numpy as jnp
from jax import lax
from jax.experimental import pallas as pl
from jax.experimental.pallas import tpu as pltpu

F32 = jnp.float32
BF16 = jnp.bfloat16

D_MODEL = 1024
CHUNK = 64
EPS = 1e-6
H_A = 4
DK_A = 64
DV_A = 128
GATE_RANK_A = 16
GATE_NORM_A = 16.0
H_B = 8
KVH_B = 2
G_B = H_B // KVH_B
HD_B = 64
WINDOW = 128
N_BUCKETS = 32
MAX_DISTANCE = 128
D_INNER_C = 2048
P_C = 64
H_C = 32
G_C = 4
HPG_C = 8
N_C = 128
CONV_W = 4
CONV_DIM = D_INNER_C + 2 * G_C * N_C

LANES = 128

A_Q, A_K, A_V, A_Z = 0, 256, 512, 1024
B_Q, B_K, B_V, B_Z = 1536, 2048, 2176, 2304
A_G = 2816
AB_COLS = 2944
C_Z, C_X, C_B, C_C, C_DT = 0, 2048, 4096, 4608, 5120
C_COLS = 5248

PROMPT_BLOCK = 256
AB_PROMPT_BLOCK = 512
VMEM_LIMIT = 60 * 1024 * 1024


def _dot(a, b):
    return jnp.dot(a.astype(BF16), b.astype(BF16), preferred_element_type=F32)


def _dot_nt(a, b):
    return lax.dot_general(a.astype(BF16), b.astype(BF16), (((1,), (1,)), ((), ())),
                           preferred_element_type=F32)


def _dot_tn(a, b):
    return lax.dot_general(a.astype(BF16), b.astype(BF16), (((0,), (0,)), ((), ())),
                           preferred_element_type=F32)


def _split_bf16(x, terms):
    out = []
    r = x
    for _ in range(terms):
        h = r.astype(BF16)
        out.append(h)
        r = r - h.astype(F32)
    return out


def _sel_dot(x, sel, terms):
    acc = None
    for h in _split_bf16(x, terms):
        d = jnp.dot(h, sel, preferred_element_type=F32)
        acc = d if acc is None else acc + d
    return acc


def _sel_dot_left(sel, x, terms):
    acc = None
    for h in _split_bf16(x, terms):
        d = jnp.dot(sel, h, preferred_element_type=F32)
        acc = d if acc is None else acc + d
    return acc


def _sel_dot_tn(x, sel, terms):
    acc = None
    for h in _split_bf16(x, terms):
        d = lax.dot_general(h, sel, (((0,), (0,)), ((), ())), preferred_element_type=F32)
        acc = d if acc is None else acc + d
    return acc


def _rms_rows(x):
    return x * lax.rsqrt(jnp.mean(x * x, axis=-1, keepdims=True) + EPS)


def _silu(x):
    return x * (1.0 / (1.0 + jnp.exp(-x)))


def _softplus(x):
    return jnp.maximum(x, 0.0) + jnp.log(1.0 + jnp.exp(-jnp.abs(x)))


def _log_sigmoid(x):
    return jnp.minimum(x, 0.0) - jnp.log(1.0 + jnp.exp(-jnp.abs(x)))


def _tril(n):
    r = lax.broadcasted_iota(jnp.int32, (n, n), 0)
    c = lax.broadcasted_iota(jnp.int32, (n, n), 1)
    return r >= c


def _ab_dense_in(x, rms_g, w_in_ref, gsum512, gsum128, gq, gk, proj_ref, qn_ref):
    h = (_rms_rows(x) * rms_g).astype(BF16)
    for lo in range(0, AB_COLS, 256):
        hi = min(lo + 256, AB_COLS)
        proj_ref[:, lo:hi] = jnp.dot(h, w_in_ref[:, lo:hi], preferred_element_type=F32)
    qb = proj_ref[:, B_Q:B_Q + 512]
    msq = _sel_dot(qb * qb, gsum512, 2) * (1.0 / HD_B)
    qn_ref[...] = qb * lax.rsqrt(msq + EPS) * gq
    kb = proj_ref[:, B_K:B_K + 128]
    msk = _sel_dot(kb * kb, gsum128, 2) * (1.0 / HD_B)
    kn = kb * lax.rsqrt(msk + EPS) * gk
    vb = proj_ref[:, B_V:B_V + 128]
    return kn, vb


def _advance(gens, yielded=None):
    alive = []
    for gen in gens:
        try:
            value = next(gen)
            alive.append(gen)
            if yielded is not None and value is not None:
                yielded.append(value)
        except StopIteration:
            pass
    return alive


def _interleave(gens):
    gens = list(gens)
    while gens:
        gens = _advance(gens)


def _ab_chunk(r, c, kb, proj_ref, qn_ref, o_ref, state, k_band, v_band, bias_ref, sink_ref,
              w_gu, b_gate, g_out, first_valid_col):
    rows = pl.ds(r, c)
    gate = _dot(proj_ref[rows, A_G:A_G + LANES], w_gu) + b_gate
    yield
    g = _log_sigmoid(gate) * (1.0 / GATE_NORM_A)
    tril = _tril(c)
    b = _sel_dot_left(tril.astype(BF16), g, 3)
    bl_rows = _sel_dot_tn(g, jnp.ones((c, LANES), BF16), 3)
    yield
    bl = b[c - 1:c, :]
    q = proj_ref[rows, A_Q:A_Q + 256] * (DK_A ** -0.5)
    k = proj_ref[rows, A_K:A_K + 256]
    v = proj_ref[rows, A_V:A_V + 512].astype(BF16)
    qe = q * jnp.exp(b)
    kd = (k * jnp.exp(-b)).astype(BF16)
    kd2 = (k * jnp.exp(bl - b)).astype(BF16)
    lane128 = lax.broadcasted_iota(jnp.int32, (c, LANES), 1)
    row128 = lax.broadcasted_iota(jnp.int32, (LANES, LANES), 0)
    qm, att = [], []
    for hd in range(H_A):
        p, j = divmod(hd, 2)
        qm.append(jnp.where((lane128 // DK_A) == j, qe[:, p * 128:(p + 1) * 128], 0.0).astype(BF16))
        att.append(_dot_nt(qm[hd], kd[:, p * 128:(p + 1) * 128]))
    upd = [_dot_tn(kd2[:, p * 128:(p + 1) * 128], v[:, p * 256:(p + 1) * 256]) for p in range(2)]
    yield
    oh = [_dot(jnp.where(tril, att[hd], 0.0), v[:, hd * 128:(hd + 1) * 128]) for hd in range(H_A)]
    yield
    s_prev = state[0]
    s_new = []
    for p in range(2):
        sp = s_prev[p * 128:(p + 1) * 128, :]
        sp_bf = sp.astype(BF16)
        for j in range(2):
            oh[2 * p + j] = oh[2 * p + j] + _dot(qm[2 * p + j], sp_bf)
        u = jnp.where(row128 < DK_A, upd[p][:, :128], upd[p][:, 128:])
        s_new.append(jnp.exp(bl_rows[p * 128:(p + 1) * 128, :]) * sp + u)
    state[0] = jnp.concatenate(s_new, axis=0)
    yield
    for hd in range(H_A):
        z = proj_ref[rows, A_Z + hd * 128:A_Z + (hd + 1) * 128]
        o_ref[rows, hd * 128:(hd + 1) * 128] = (_rms_rows(oh[hd]) * g_out * _silu(z)).astype(BF16)
    yield
    lane256 = lax.broadcasted_iota(jnp.int32, (c, 256), 1) // HD_B
    srow = lax.broadcasted_iota(jnp.int32, (G_B * c, 1), 0) // c
    scores = []
    for kvh in range(KVH_B):
        qn = qn_ref[rows, kvh * 256:(kvh + 1) * 256]
        qs = jnp.concatenate([jnp.where(lane256 == gq_, qn, 0.0) for gq_ in range(G_B)], axis=0)
        scores.append(_dot_nt(qs, k_band(kvh)))
    yield
    probs, dens = [], []
    for kvh in range(KVH_B):
        s = scores[kvh] * (HD_B ** -0.5) + bias_ref[kvh]
        if first_valid_col is not None:
            col = lax.broadcasted_iota(jnp.int32, (G_B * c, kb), 1)
            s = jnp.where(col >= first_valid_col, s, -jnp.inf)
        sink = jnp.zeros((G_B * c, 1), F32)
        for gq_ in range(G_B):
            sink = jnp.where(srow == gq_, sink_ref[kvh * G_B + gq_], sink)
        m = jnp.maximum(jnp.max(s, axis=-1, keepdims=True), sink)
        pr = jnp.exp(s - m)
        dens.append(jnp.sum(pr, axis=-1, keepdims=True) + jnp.exp(sink - m))
        probs.append(pr.astype(BF16))
    yield
    outs = [_dot(probs[kvh], v_band(kvh)) for kvh in range(KVH_B)]
    yield
    for kvh in range(KVH_B):
        ost = outs[kvh] / dens[kvh]
        ob = jnp.zeros((c, 256), F32)
        for gq_ in range(G_B):
            ob = ob + jnp.where(lane256 == gq_, ost[gq_ * c:(gq_ + 1) * c, :], 0.0)
        z = proj_ref[rows, B_Z + kvh * 256:B_Z + (kvh + 1) * 256]
        o_ref[rows, 512 + kvh * 256:512 + (kvh + 1) * 256] = (ob * _silu(z)).astype(BF16)


def _ab_prompt_kernel(sink_ref, x_ref, rms_g_ref, w_in_ref, w_gu_ref, b_gate_ref, g_out_ref, gq_ref,
                      gk_ref, bias_ref, w_out_ref, gsum512_ref, gsum128_ref, tile_ref,
                      y_ref, s_out_ref, k_out_ref, v_out_ref,
                      proj_ref, qn_ref, o_ref, kband_ref, vband_ref, s_ref):
    t = x_ref.shape[0]
    c = CHUNK
    kb = WINDOW + c
    nchunk = t // c
    step = pl.program_id(0)

    @pl.when(step == 0)
    def _():
        s_ref[...] = jnp.zeros_like(s_ref)
        kband_ref[:, t:t + WINDOW, :] = jnp.zeros((KVH_B, WINDOW, 256), BF16)
        vband_ref[:, t:t + WINDOW, :] = jnp.zeros((KVH_B, WINDOW, 256), BF16)

    for kvh in range(KVH_B):
        kband_ref[kvh, 0:WINDOW, :] = kband_ref[kvh, t:t + WINDOW, :]
        vband_ref[kvh, 0:WINDOW, :] = vband_ref[kvh, t:t + WINDOW, :]

    x = x_ref[...]
    kn, vb = _ab_dense_in(x, rms_g_ref[...], w_in_ref, gsum512_ref[...], gsum128_ref[...],
                          gq_ref[...], gk_ref[...], proj_ref, qn_ref)
    k_out_ref[...] = kn[t - WINDOW:, :]
    v_out_ref[...] = vb[t - WINDOW:, :]
    for kvh in range(KVH_B):
        kband_ref[kvh, WINDOW:WINDOW + t, :] = jnp.dot(
            kn.astype(BF16), tile_ref[kvh], preferred_element_type=F32).astype(BF16)
        vband_ref[kvh, WINDOW:WINDOW + t, :] = jnp.dot(
            vb.astype(BF16), tile_ref[kvh], preferred_element_type=F32).astype(BF16)

    w_gu = w_gu_ref[...]
    b_gate = b_gate_ref[...]
    g_out = g_out_ref[...]

    state = [s_ref[...]]

    def chunk(i):
        r = i * c
        return _ab_chunk(r, c, kb, proj_ref, qn_ref, o_ref, state,
                         lambda kvh: kband_ref[kvh, pl.ds(r, kb), :],
                         lambda kvh: vband_ref[kvh, pl.ds(r, kb), :],
                         bias_ref, sink_ref, w_gu, b_gate, g_out, (2 - (step * nchunk + i)) * c)

    _interleave(chunk(i) for i in range(nchunk))
    s_ref[...] = state[0]
    y_ref[...] = x + jnp.dot(o_ref[...], w_out_ref[...], preferred_element_type=F32)
    s_out_ref[...] = s_ref[...]


def _ab_sample_kernel(sink_ref, x_ref, rms_g_ref, w_in_ref, w_gu_ref, b_gate_ref, g_out_ref, gq_ref,
                      gk_ref, bias_ref, w_out_ref, gsum512_ref, gsum128_ref, tile_ref,
                      s_in_ref, kc_ref, vc_ref,
                      y_ref, s_out_ref, k_out_ref, v_out_ref,
                      proj_ref, qn_ref, o_ref, kband_ref, vband_ref):
    nb, wc = kc_ref.shape[0], kc_ref.shape[1]
    t = x_ref.shape[0]
    c = t // nb
    kb = wc + c
    x = x_ref[...]
    kn, vb = _ab_dense_in(x, rms_g_ref[...], w_in_ref, gsum512_ref[...], gsum128_ref[...],
                          gq_ref[...], gk_ref[...], proj_ref, qn_ref)
    k_out_ref[...] = kn
    v_out_ref[...] = vb
    for kvh in range(KVH_B):
        kt = jnp.dot(kn.astype(BF16), tile_ref[kvh], preferred_element_type=F32).astype(BF16)
        vt = jnp.dot(vb.astype(BF16), tile_ref[kvh], preferred_element_type=F32).astype(BF16)
        for bi in range(nb):
            kband_ref[kvh, bi, 0:wc, :] = jnp.dot(
                kc_ref[bi].astype(BF16), tile_ref[kvh], preferred_element_type=F32).astype(BF16)
            vband_ref[kvh, bi, 0:wc, :] = jnp.dot(
                vc_ref[bi].astype(BF16), tile_ref[kvh], preferred_element_type=F32).astype(BF16)
            kband_ref[kvh, bi, wc:kb, :] = kt[bi * c:(bi + 1) * c, :]
            vband_ref[kvh, bi, wc:kb, :] = vt[bi * c:(bi + 1) * c, :]

    w_gu = w_gu_ref[...]
    b_gate = b_gate_ref[...]
    g_out = g_out_ref[...]

    states = [[s_in_ref[bi]] for bi in range(nb)]

    def seq(bi):
        return _ab_chunk(bi * c, c, kb, proj_ref, qn_ref, o_ref, states[bi],
                         lambda kvh: kband_ref[kvh, bi], lambda kvh: vband_ref[kvh, bi],
                         bias_ref, sink_ref, w_gu, b_gate, g_out, None)

    _interleave(seq(bi) for bi in range(nb))
    for bi in range(nb):
        s_out_ref[bi] = states[bi][0]
    y_ref[...] = x + jnp.dot(o_ref[...], w_out_ref[...], preferred_element_type=F32)


GROUP_W = D_INNER_C // G_C


def _c_chunk(r, c, io, state, e_s, e_p, diag, neg_mask, bd_mask, dskip, g_y):
    dtc = io.dt(r)
    acum = _sel_dot_left(_tril(c).astype(BF16), io.da(r), 3)
    yield acum
    lhs = jnp.concatenate([jnp.concatenate(_split_bf16(acum, 2), axis=1),
                           jnp.concatenate(_split_bf16(dtc, 2), axis=1)], axis=0)
    both_p = jnp.dot(lhs, e_p, preferred_element_type=F32)
    xa_p, dt_p = both_p[:c], both_p[c:]
    xa_s = xa_p if c == P_C else jnp.dot(lhs[:c], e_s, preferred_element_type=F32)
    yield dt_p
    a_row = jnp.sum(xa_s * diag, axis=0, keepdims=True)
    wmat = jnp.exp((xa_s - a_row) + neg_mask)
    al_p = xa_p[c - 1:c, :]
    ea_p = jnp.exp(xa_p)
    xs = io.x(r)
    xdt = xs * dt_p
    xdt_bf = xdt.astype(BF16)
    xw = (xdt * jnp.exp(al_p - xa_p)).astype(BF16)
    dec = jnp.exp(al_p)
    bg, cg, cb = [], [], []
    for g in range(G_C):
        bg.append(io.b(r, g).astype(BF16))
        cg.append(io.c(r, g).astype(BF16))
        cb.append(_dot_nt(cg[g], jnp.concatenate([bg[g]] * HPG_C, axis=0)))
    yield cb[-1]
    ys = []
    for g in range(G_C):
        mg = (cb[g] * wmat[:, g * HPG_C * c:(g + 1) * HPG_C * c]).astype(BF16)
        for j in range(2):
            xj = xdt_bf[:, g * GROUP_W + j * 256:g * GROUP_W + (j + 1) * 256]
            bd = jnp.concatenate([xj] * 4, axis=0) * bd_mask
            ys.append(jnp.dot(mg[:, j * 4 * c:(j + 1) * 4 * c], bd, preferred_element_type=F32))
    upd = [_dot_tn(bg[g], xw[:, g * GROUP_W:(g + 1) * GROUP_W]) for g in range(G_C)]
    yield upd[-1]
    st_prev = state[0]
    y_inter, st_new = [], []
    for g in range(G_C):
        sl = slice(g * GROUP_W, (g + 1) * GROUP_W)
        y_inter.append(_dot(cg[g], st_prev[:, sl]))
        st_new.append(st_prev[:, sl] * dec[:, sl] + upd[g])
    state[0] = jnp.concatenate(st_new, axis=1)
    yield st_new[-1]
    for g in range(G_C):
        sl = slice(g * GROUP_W, (g + 1) * GROUP_W)
        y = jnp.concatenate(ys[2 * g:2 * g + 2], axis=1) + y_inter[g] * ea_p[:, sl]
        y = y + dskip[:, sl] * xs[:, sl]
        y = y * _silu(io.z(r, g))
        io.put_o(r, g, (_rms_rows(y) * g_y[:, sl]).astype(BF16))


def _c_dt(dt_cols, dt_bias, a_log):
    dt = _softplus(dt_cols + dt_bias)
    return dt, dt * (-jnp.exp(a_log))


CONV_PITCH = PROMPT_BLOCK // 8 + 1
CONV_ROWS = 8 * CONV_PITCH
CONV_TILES = CONV_DIM // LANES
C_ROUNDS = 12
C_IN_UNITS = (4, 4, 4, 4, 4, 4, 4, 2, 4, 2, 2, 3)
C_OUT_ROUNDS = (3, 5, 8, 10)
assert len(C_IN_UNITS) == C_ROUNDS and sum(C_IN_UNITS) * LANES == C_COLS


def _conv_tile(ubuf_ref, act_ref, j, conv_w, conv_b):
    w = [jnp.broadcast_to(conv_w[i:i + 1, j * LANES:(j + 1) * LANES], (8, LANES)) for i in range(CONV_W)]
    b = jnp.broadcast_to(conv_b[:, j * LANES:(j + 1) * LANES], (8, LANES))
    for a in range(CONV_PITCH):
        acc = b
        for i in range(CONV_W):
            acc = acc + w[i] * ubuf_ref[j, pl.ds(8 - (CONV_W - 1) + i + a, 8, stride=CONV_PITCH), :]
        act_ref[j, pl.ds(a, 8, stride=CONV_PITCH), :] = _silu(acc)


def _c_prompt_kernel(xin_ref, xres_ref, rms_g_ref, w_in_ref, conv_w_ref, conv_b_ref, dt_bias_ref,
                     a_log_ref, dskip_ref, g_y_ref, w_out_ref, e_s_ref, diag_ref, neg_mask_ref, bd_mask_ref,
                     y_ref, st_out_ref, conv_out_ref,
                     proj_ref, ubuf_ref, act_ref, dt_ref, da_ref, o_ref, st_ref):
    t = xin_ref.shape[0]
    c = CHUNK
    s = pl.program_id(0)
    n_blocks = pl.num_programs(0) - 2

    @pl.when(s == 0)
    def _():
        proj_ref[1] = jnp.zeros(proj_ref.shape[1:], F32)
        dt_ref[1] = jnp.zeros(dt_ref.shape[1:], F32)
        da_ref[1] = jnp.zeros(da_ref.shape[1:], F32)
        o_ref[...] = jnp.zeros_like(o_ref)
        ubuf_ref[...] = jnp.zeros_like(ubuf_ref)

    @pl.when(s <= 1)
    def _():
        st_ref[...] = jnp.zeros_like(st_ref)
        ubuf_ref[:, CONV_ROWS:CONV_ROWS + 8, :] = jnp.zeros((CONV_TILES, 8, LANES), F32)

    def in_stage(slot_in):
        h = (_rms_rows(xin_ref[...]) * rms_g_ref[...]).astype(BF16)
        lo = 0
        for units in C_IN_UNITS:
            if lo:
                yield
            hi = lo + units * LANES
            slab = jnp.dot(h, w_in_ref[:, lo:hi], preferred_element_type=F32)
            proj_ref[slot_in, :, lo:hi] = slab
            if hi == C_COLS:
                dt, da = _c_dt(slab[:, C_DT - lo:], dt_bias_ref[...], a_log_ref[...])
                dt_ref[slot_in] = dt
                da_ref[slot_in] = da
            lo = hi

    def act_rows(r, j):
        return act_ref[j, pl.ds(8 + r, c), :]

    n_x = D_INNER_C // LANES
    n_g = N_C // LANES

    def mix_stage(slot_mix):
        def put_o(r, g, value):
            o_ref[slot_mix, pl.ds(r, c), g * GROUP_W:(g + 1) * GROUP_W] = value

        io = types.SimpleNamespace(
            dt=lambda r: dt_ref[slot_mix, pl.ds(r, c), :],
            da=lambda r: da_ref[slot_mix, pl.ds(r, c), :],
            x=lambda r: jnp.concatenate([act_rows(r, j) for j in range(n_x)], axis=1),
            b=lambda r, g: act_rows(r, n_x + g * n_g),
            c=lambda r, g: act_rows(r, n_x + (G_C + g) * n_g),
            z=lambda r, g: proj_ref[slot_mix, pl.ds(r, c), C_Z + g * GROUP_W:C_Z + (g + 1) * GROUP_W],
            put_o=put_o)
        conv_w = conv_w_ref[...]
        conv_b = conv_b_ref[...]
        conv_rounds = C_ROUNDS - 6
        per_round = CONV_TILES // conv_rounds
        for rnd in range(conv_rounds):
            if rnd:
                yield
            for j in range(rnd * per_round, (rnd + 1) * per_round):
                ubuf_ref[j, 8:16, :] = ubuf_ref[j, CONV_ROWS:CONV_ROWS + 8, :]
                ubuf_ref[j, 16:16 + t, :] = proj_ref[slot_mix, :, C_X + j * LANES:C_X + (j + 1) * LANES]
                _conv_tile(ubuf_ref, act_ref, j, conv_w, conv_b)
        state = [st_ref[...]]
        e_s = e_s_ref[...]
        chunks = [_c_chunk(i * c, c, io, state, e_s, e_s, diag_ref[...], neg_mask_ref[...], bd_mask_ref[...],
                           dskip_ref[...], g_y_ref[...]) for i in range(t // c)]
        while chunks:
            yield
            chunks = _advance(chunks)
        st_ref[...] = state[0]

    def out_stage(slot_in):
        quarter = 0
        for rnd in range(C_ROUNDS):
            if rnd:
                yield
            if rnd in C_OUT_ROUNDS:
                cols = slice(quarter * 256, (quarter + 1) * 256)
                y_ref[:, cols] = xres_ref[:, cols] + jnp.dot(o_ref[slot_in], w_out_ref[:, cols],
                                                             preferred_element_type=F32)
                quarter += 1

    slot_in = s % 2
    _interleave([in_stage(slot_in), mix_stage(1 - slot_in), out_stage(slot_in)])

    @pl.when(s == n_blocks)
    def _():
        st_out_ref[...] = st_ref[...].T
        conv_out_ref[...] = proj_ref[1 - s % 2, t - 8:t, C_X:C_X + CONV_DIM]


def _conv_rows(ubuf, nrows, conv_w, conv_b):
    acc = conv_b
    for i in range(CONV_W):
        acc = acc + conv_w[i:i + 1, :] * ubuf[pl.ds(8 - (CONV_W - 1) + i, nrows), :]
    return _silu(acc)


def _c_sample_kernel(x_ref, rms_g_ref, w_in_ref, conv_w_ref, conv_b_ref, dt_bias_ref, a_log_ref,
                     dskip_ref, g_y_ref, w_out_ref, e_s_ref, e_p_ref, diag_ref, neg_mask_ref, bd_mask_ref,
                     st_in_ref, conv_in_ref,
                     y_ref, st_out_ref, conv_out_ref,
                     proj_ref, ubuf_ref, act_ref, dt_ref, da_ref, o_ref):
    nb = st_in_ref.shape[0]
    t = x_ref.shape[0]
    c = t // nb
    x = x_ref[...]
    h = (_rms_rows(x) * rms_g_ref[...]).astype(BF16)
    proj_ref[...] = jnp.dot(h, w_in_ref[...], preferred_element_type=F32)
    dt, da = _c_dt(proj_ref[:, C_DT:C_DT + LANES], dt_bias_ref[...], a_log_ref[...])
    dt_ref[...] = dt
    da_ref[...] = da
    conv_w = conv_w_ref[...]
    conv_b = conv_b_ref[...]
    ubuf_ref[0:8, :] = jnp.zeros((8, CONV_DIM), F32)
    for bi in range(nb):
        ubuf_ref[8 - (CONV_W - 1):8, :] = conv_in_ref[bi]
        ubuf_ref[8:8 + c, :] = proj_ref[bi * c:(bi + 1) * c, C_X:C_X + CONV_DIM]
        act_ref[bi * c:(bi + 1) * c, :] = _conv_rows(ubuf_ref, c, conv_w, conv_b)
        conv_out_ref[bi] = ubuf_ref[8 + c - (CONV_W - 1):8 + c, :]

    def put_o(r, g, value):
        o_ref[pl.ds(r, c), g * GROUP_W:(g + 1) * GROUP_W] = value

    io = types.SimpleNamespace(
        dt=lambda r: dt_ref[pl.ds(r, c), :],
        da=lambda r: da_ref[pl.ds(r, c), :],
        x=lambda r: act_ref[pl.ds(r, c), 0:D_INNER_C],
        b=lambda r, g: act_ref[pl.ds(r, c), D_INNER_C + g * N_C:D_INNER_C + (g + 1) * N_C],
        c=lambda r, g: act_ref[pl.ds(r, c), D_INNER_C + (G_C + g) * N_C:D_INNER_C + (G_C + g + 1) * N_C],
        z=lambda r, g: proj_ref[pl.ds(r, c), C_Z + g * GROUP_W:C_Z + (g + 1) * GROUP_W],
        put_o=put_o)
    e_s = e_s_ref[...]
    e_p = e_p_ref[...]
    diag = diag_ref[...]
    neg_mask = neg_mask_ref[...]
    bd_mask = bd_mask_ref[...]
    dskip = dskip_ref[...]
    g_y = g_y_ref[...]

    states = [[st_in_ref[bi].T] for bi in range(nb)]
    _interleave(_c_chunk(bi * c, c, io, states[bi], e_s, e_p, diag, neg_mask, bd_mask, dskip, g_y)
                for bi in range(nb))
    for bi in range(nb):
        st_out_ref[bi] = states[bi][0].T
    y_ref[...] = x + jnp.dot(o_ref[...], w_out_ref[...], preferred_element_type=F32)


def _bucket_table(q_off, k_off):
    n = q_off[:, None] - k_off[None, :]
    half = N_BUCKETS // 2
    max_exact = half // 2
    side = np.where(n < 0, half, 0)
    n = np.abs(n)
    nf = np.maximum(n, max_exact).astype(np.float32)
    large = max_exact + (np.log(nf / np.float32(max_exact)) / np.float32(math.log(MAX_DISTANCE / max_exact))
                         * np.float32(half - max_exact)).astype(np.int32)
    large = np.minimum(large, half - 1)
    return side + np.where(n < max_exact, n, large)


def _bias_rows(rel_bias, q_off, k_off):
    bucket = _bucket_table(q_off, k_off)
    onehot = jnp.asarray(np.eye(N_BUCKETS, dtype=np.float32)[bucket])
    bias = jnp.einsum('qkb,bh->hqk', onehot, rel_bias.astype(F32), precision=lax.Precision.HIGHEST)
    return bias.reshape(KVH_B, G_B * q_off.shape[0], k_off.shape[0])


def _group_sum_matrix(width, group):
    idx = np.arange(width) // group
    return jnp.asarray(idx[:, None] == idx[None, :], BF16)


def _kv_tile_matrices():
    m = np.zeros((KVH_B, KVH_B * HD_B, G_B * HD_B), np.float32)
    for kvh in range(KVH_B):
        for gq in range(G_B):
            m[kvh, kvh * HD_B + np.arange(HD_B), gq * HD_B + np.arange(HD_B)] = 1.0
    return jnp.asarray(m, BF16)


def _head_expand_matrix(per_head):
    m = np.zeros((LANES, H_C * per_head), np.float32)
    for h in range(H_C):
        m[h, h * per_head:(h + 1) * per_head] = 1.0
    return jnp.asarray(np.concatenate([m, m], axis=0), BF16)


def _full(shape):
    return pl.BlockSpec(shape, lambda *_: (0,) * len(shape))


def _const(shape):
    return pl.BlockSpec(shape, lambda *_: (0,) * len(shape), pipeline_mode=pl.Buffered(1))


def _ab_weights(rms_g, w_in, w_gate_up, b_gate, g_out, g_q, g_k, w_out):
    w_in_r = jnp.concatenate(
        [w_in[:, :1024], w_in[:, 1024 + GATE_RANK_A:], w_in[:, 1024:1024 + GATE_RANK_A],
         jnp.zeros((D_MODEL, AB_COLS - w_in.shape[1]), w_in.dtype)], axis=1).astype(BF16)
    w_gu = jnp.concatenate([w_gate_up, jnp.zeros((LANES - GATE_RANK_A, H_A * DK_A), w_gate_up.dtype)],
                           axis=0).astype(BF16)
    return (rms_g.reshape(1, D_MODEL), w_in_r, w_gu, b_gate.reshape(1, -1), g_out.reshape(1, DV_A),
            jnp.tile(g_q, H_B).reshape(1, -1), jnp.tile(g_k, KVH_B).reshape(1, -1), w_out.astype(BF16))


def _ab_prompt(x, sink, rel_bias, weights):
    rms_g, w_in, w_gu, b_gate, g_out, gq, gk, w_out = weights
    length = x.shape[0]
    t = AB_PROMPT_BLOCK
    c = CHUNK
    kb = WINDOW + c
    bias = _bias_rows(rel_bias, np.arange(c), np.arange(kb) - WINDOW)
    row_blk = lambda i: (i, 0)
    in_specs = [
        pl.BlockSpec(memory_space=pltpu.SMEM),
        pl.BlockSpec((t, D_MODEL), row_blk),
        _const((1, D_MODEL)), _const((D_MODEL, AB_COLS)), _const((LANES, 256)), _const((1, 256)),
        _const((1, DV_A)), _const((1, 512)), _const((1, 128)), _const((KVH_B, G_B * c, kb)),
        _const((D_MODEL, D_MODEL)), _const((512, 512)), _const((128, 128)), _const((KVH_B, 128, 256)),
    ]
    out_shape = (jax.ShapeDtypeStruct((length, D_MODEL), F32),
                 jax.ShapeDtypeStruct((H_A * DK_A, DV_A), F32),
                 jax.ShapeDtypeStruct((WINDOW, KVH_B * HD_B), F32),
                 jax.ShapeDtypeStruct((WINDOW, KVH_B * HD_B), F32))
    out_specs = (pl.BlockSpec((t, D_MODEL), row_blk), _full((H_A * DK_A, DV_A)),
                 _full((WINDOW, KVH_B * HD_B)), _full((WINDOW, KVH_B * HD_B)))
    scratch = [pltpu.VMEM((t, AB_COLS), F32), pltpu.VMEM((t, 512), F32), pltpu.VMEM((t, D_MODEL), BF16),
               pltpu.VMEM((KVH_B, t + WINDOW, 256), BF16), pltpu.VMEM((KVH_B, t + WINDOW, 256), BF16),
               pltpu.VMEM((H_A * DK_A, DV_A), F32)]
    return pl.pallas_call(
        _ab_prompt_kernel, grid=(length // t,), in_specs=in_specs, out_specs=out_specs,
        out_shape=out_shape, scratch_shapes=scratch, name="ab_prompt",
        compiler_params=pltpu.CompilerParams(dimension_semantics=("arbitrary",),
                                             vmem_limit_bytes=VMEM_LIMIT),
    )(sink, x, rms_g, w_in, w_gu, b_gate, g_out, gq, gk, bias, w_out,
      _group_sum_matrix(512, HD_B), _group_sum_matrix(128, HD_B), _kv_tile_matrices())


def _ab_sample(x, sink, rel_bias, weights, s_in, k_cache, v_cache):
    rms_g, w_in, w_gu, b_gate, g_out, gq, gk, w_out = weights
    nb, c, _ = x.shape
    wc = k_cache.shape[1]
    kb = wc + c
    t = nb * c
    bias = _bias_rows(rel_bias, np.arange(c), np.arange(kb) - wc)
    vm = pl.BlockSpec(memory_space=pltpu.VMEM)
    in_specs = [pl.BlockSpec(memory_space=pltpu.SMEM)] + [vm] * 16
    out_shape = (jax.ShapeDtypeStruct((t, D_MODEL), F32),
                 jax.ShapeDtypeStruct((nb, H_A * DK_A, DV_A), F32),
                 jax.ShapeDtypeStruct((t, KVH_B * HD_B), F32),
                 jax.ShapeDtypeStruct((t, KVH_B * HD_B), F32))
    scratch = [pltpu.VMEM((t, AB_COLS), F32), pltpu.VMEM((t, 512), F32), pltpu.VMEM((t, D_MODEL), BF16),
               pltpu.VMEM((KVH_B, nb, kb, 256), BF16), pltpu.VMEM((KVH_B, nb, kb, 256), BF16)]
    return pl.pallas_call(
        _ab_sample_kernel, in_specs=in_specs, out_specs=(vm,) * 4, out_shape=out_shape,
        scratch_shapes=scratch, name="ab_sample",
        compiler_params=pltpu.CompilerParams(vmem_limit_bytes=VMEM_LIMIT),
    )(sink, x.reshape(t, D_MODEL), rms_g, w_in, w_gu, b_gate, g_out, gq, gk, bias, w_out,
      _group_sum_matrix(512, HD_B), _group_sum_matrix(128, HD_B), _kv_tile_matrices(),
      s_in.reshape(nb, H_A * DK_A, DV_A), k_cache.reshape(nb, wc, KVH_B * HD_B),
      v_cache.reshape(nb, wc, KVH_B * HD_B))


def _c_weights(rms_g, w_in, conv_w, conv_b, dt_bias, a_log, d_skip, g_y, w_out):
    w_in_p = jnp.concatenate([w_in, jnp.zeros((D_MODEL, C_COLS - w_in.shape[1]), w_in.dtype)],
                             axis=1).astype(BF16)
    pad = lambda v: jnp.concatenate([v, jnp.zeros((LANES - H_C,), v.dtype)]).reshape(1, LANES)
    return (rms_g.reshape(1, D_MODEL), w_in_p, conv_w, conv_b.reshape(1, CONV_DIM), pad(dt_bias), pad(a_log),
            jnp.repeat(d_skip, P_C).reshape(1, D_INNER_C), g_y.reshape(1, D_INNER_C), w_out.astype(BF16))


def _ssd_masks(c):
    tok = np.arange(c)[:, None]
    src = np.arange(H_C * c)[None, :] % c
    diag = (tok == src).astype(np.float32)
    neg = np.where(src <= tok, 0.0, -np.inf).astype(np.float32)
    blk = (np.arange(4 * c)[:, None] // c) == (np.arange(256)[None, :] // P_C)
    return jnp.asarray(diag), jnp.asarray(neg), jnp.asarray(blk, BF16)


def _c_prompt(x, weights):
    length = x.shape[0]
    t = PROMPT_BLOCK
    c = CHUNK
    n_blocks = length // t
    in_blk = lambda s: (jnp.minimum(s, n_blocks - 1), 0)
    out_blk = lambda s: (jnp.clip(s - 2, 0, n_blocks - 1), 0)
    in_specs = [
        pl.BlockSpec((t, D_MODEL), in_blk), pl.BlockSpec((t, D_MODEL), out_blk),
        _const((1, D_MODEL)), _const((D_MODEL, C_COLS)), _const((CONV_W, CONV_DIM)), _const((1, CONV_DIM)),
        _const((1, LANES)), _const((1, LANES)), _const((1, D_INNER_C)), _const((1, D_INNER_C)),
        _const((D_INNER_C, D_MODEL)), _const((2 * LANES, H_C * c)),
        _const((c, H_C * c)), _const((c, H_C * c)), _const((4 * c, 256)),
    ]
    out_shape = (jax.ShapeDtypeStruct((length, D_MODEL), F32),
                 jax.ShapeDtypeStruct((H_C * P_C, N_C), F32),
                 jax.ShapeDtypeStruct((8, CONV_DIM), F32))
    out_specs = (pl.BlockSpec((t, D_MODEL), out_blk), _full((H_C * P_C, N_C)), _full((8, CONV_DIM)))
    scratch = [pltpu.VMEM((2, t, C_COLS), F32), pltpu.VMEM((CONV_TILES, CONV_ROWS + 8, LANES), F32),
               pltpu.VMEM((CONV_TILES, CONV_ROWS, LANES), F32), pltpu.VMEM((2, t, LANES), F32),
               pltpu.VMEM((2, t, LANES), F32), pltpu.VMEM((2, t, D_INNER_C), BF16),
               pltpu.VMEM((N_C, H_C * P_C), F32)]
    return pl.pallas_call(
        _c_prompt_kernel, grid=(n_blocks + 2,), in_specs=in_specs, out_specs=out_specs,
        out_shape=out_shape, scratch_shapes=scratch, name="c_prompt",
        compiler_params=pltpu.CompilerParams(dimension_semantics=("arbitrary",),
                                             vmem_limit_bytes=VMEM_LIMIT),
    )(x, x, *weights, _head_expand_matrix(c), *_ssd_masks(c))


def _c_sample(x, weights, st_in, conv_in):
    nb, c, _ = x.shape
    t = nb * c
    vm = pl.BlockSpec(memory_space=pltpu.VMEM)
    out_shape = (jax.ShapeDtypeStruct((t, D_MODEL), F32),
                 jax.ShapeDtypeStruct((nb, H_C * P_C, N_C), F32),
                 jax.ShapeDtypeStruct((nb, CONV_W - 1, CONV_DIM), F32))
    scratch = [pltpu.VMEM((t, C_COLS), F32), pltpu.VMEM((c + 8, CONV_DIM), F32),
               pltpu.VMEM((t, CONV_DIM), F32), pltpu.VMEM((t, LANES), F32), pltpu.VMEM((t, LANES), F32),
               pltpu.VMEM((t, D_INNER_C), BF16)]
    return pl.pallas_call(
        _c_sample_kernel, in_specs=[vm] * 17, out_specs=(vm,) * 3, out_shape=out_shape,
        scratch_shapes=scratch, name="c_sample",
        compiler_params=pltpu.CompilerParams(vmem_limit_bytes=VMEM_LIMIT),
    )(x.reshape(t, D_MODEL), *weights, _head_expand_matrix(c), _head_expand_matrix(P_C), *_ssd_masks(c),
      st_in.reshape(nb, H_C * P_C, N_C), conv_in)


def kernel(x_prompt, x_sample, cache_swa_k, cache_swa_v, state_gla, state_ssd, state_conv, rms_g, w_in_ab, w_gate_up_a, b_gate_a, g_out_a, g_q_b, g_k_b, sink_b, rel_bias, w_out_ab, w_in_c, conv_w_c, conv_b_c, dt_bias_c, a_log_c, d_skip_c, g_y_c, w_out_c):
    bp, seq_len, _ = x_prompt.shape
    nb, dec_len, _ = x_sample.shape
    assert bp == 1 and seq_len % PROMPT_BLOCK == 0 and seq_len % AB_PROMPT_BLOCK == 0 and seq_len >= WINDOW
    wab = _ab_weights(rms_g[0], w_in_ab[0], w_gate_up_a[0], b_gate_a[0], g_out_a[0], g_q_b[0], g_k_b[0],
                      w_out_ab[0])
    yp, gla_p, k_p, v_p = _ab_prompt(x_prompt[0], sink_b[0], rel_bias, wab)
    ys, gla_s, k_s, v_s = _ab_sample(x_sample, sink_b[0], rel_bias, wab, state_gla[0], cache_swa_k[0],
                                     cache_swa_v[0])
    wc = _c_weights(rms_g[1], w_in_c[0], conv_w_c[0], conv_b_c[0], dt_bias_c[0], a_log_c[0], d_skip_c[0],
                    g_y_c[0], w_out_c[0])
    yp, ssd_p, conv_p = _c_prompt(yp, wc)
    ys, ssd_s, conv_s = _c_sample(ys.reshape(nb, dec_len, D_MODEL), wc, state_ssd[0], state_conv[0])
    return (
        yp.reshape(1, seq_len, D_MODEL),
        ys.reshape(nb, dec_len, D_MODEL),
        gla_p.reshape(1, 1, H_A, DK_A, DV_A),
        k_p.reshape(1, 1, WINDOW, KVH_B, HD_B),
        v_p.reshape(1, 1, WINDOW, KVH_B, HD_B),
        ssd_p.reshape(1, 1, H_C, P_C, N_C),
        conv_p[8 - (CONV_W - 1):].reshape(1, 1, CONV_W - 1, CONV_DIM),
        gla_s.reshape(1, nb, H_A, DK_A, DV_A),
        k_s.reshape(1, nb, dec_len, KVH_B, HD_B),
        v_s.reshape(1, nb, dec_len, KVH_B, HD_B),
        ssd_s.reshape(1, nb, H_C, P_C, N_C),
        conv_s.reshape(1, nb, CONV_W - 1, CONV_DIM),
    )
```

```python
import math
import types

import numpy as np
import jax
import jax.numpy as jnp
from jax import lax
from jax.experimental import pallas as pl
from jax.experimental.pallas import tpu as pltpu

F32 = jnp.float32
BF16 = jnp.bfloat16

D_MODEL = 1024
CHUNK = 64
EPS = 1e-6
H_A = 4
DK_A = 64
DV_A = 128
GATE_RANK_A = 16
GATE_NORM_A = 16.0
H_B = 8
KVH_B = 2
G_B = H_B // KVH_B
HD_B = 64
WINDOW = 128
N_BUCKETS = 32
MAX_DISTANCE = 128
D_INNER_C = 2048
P_C = 64
H_C = 32
G_C = 4
HPG_C = 8
N_C = 128
CONV_W = 4
CONV_DIM = D_INNER_C + 2 * G_C * N_C

LANES = 128

A_Q, A_K, A_V, A_Z = 0, 256, 512, 1024
B_Q, B_K, B_V, B_Z = 1536, 2048, 2176, 2304
A_G = 2816
AB_COLS = 2944
C_Z, C_X, C_B, C_C, C_DT = 0, 2048, 4096, 4608, 5120
C_COLS = 5248

PROMPT_BLOCK = 256
AB_PROMPT_BLOCK = 512
VMEM_LIMIT = 60 * 1024 * 1024


def _dot(a, b):
    return jnp.dot(a.astype(BF16), b.astype(BF16), preferred_element_type=F32)


def _dot_nt(a, b):
    return lax.dot_general(a.astype(BF16), b.astype(BF16), (((1,), (1,)), ((), ())),
                           preferred_element_type=F32)


def _dot_tn(a, b):
    return lax.dot_general(a.astype(BF16), b.astype(BF16), (((0,), (0,)), ((), ())),
                           preferred_element_type=F32)


def _split_bf16(x, terms):
    out = []
    r = x
    for _ in range(terms):
        h = r.astype(BF16)
        out.append(h)
        r = r - h.astype(F32)
    return out


def _sel_dot(x, sel, terms):
    acc = None
    for h in _split_bf16(x, terms):
        d = jnp.dot(h, sel, preferred_element_type=F32)
        acc = d if acc is None else acc + d
    return acc


def _sel_dot_left(sel, x, terms):
    acc = None
    for h in _split_bf16(x, terms):
        d = jnp.dot(sel, h, preferred_element_type=F32)
        acc = d if acc is None else acc + d
    return acc


def _sel_dot_tn(x, sel, terms):
    acc = None
    for h in _split_bf16(x, terms):
        d = lax.dot_general(h, sel, (((0,), (0,)), ((), ())), preferred_element_type=F32)
        acc = d if acc is None else acc + d
    return acc


def _rms_rows(x):
    return x * lax.rsqrt(jnp.mean(x * x, axis=-1, keepdims=True) + EPS)


def _silu(x):
    return x * (1.0 / (1.0 + jnp.exp(-x)))


def _softplus(x):
    return jnp.maximum(x, 0.0) + jnp.log(1.0 + jnp.exp(-jnp.abs(x)))


def _log_sigmoid(x):
    return jnp.minimum(x, 0.0) - jnp.log(1.0 + jnp.exp(-jnp.abs(x)))


def _tril(n):
    r = lax.broadcasted_iota(jnp.int32, (n, n), 0)
    c = lax.broadcasted_iota(jnp.int32, (n, n), 1)
    return r >= c


def _ab_dense_in(x, rms_g, w_in_ref, gsum512, gsum128, gq, gk, proj_ref, qn_ref):
    h = (_rms_rows(x) * rms_g).astype(BF16)
    for lo in range(0, AB_COLS, 256):
        hi = min(lo + 256, AB_COLS)
        proj_ref[:, lo:hi] = jnp.dot(h, w_in_ref[:, lo:hi], preferred_element_type=F32)
    qb = proj_ref[:, B_Q:B_Q + 512]
    msq = _sel_dot(qb * qb, gsum512, 2) * (1.0 / HD_B)
    qn_ref[...] = qb * lax.rsqrt(msq + EPS) * gq
    kb = proj_ref[:, B_K:B_K + 128]
    msk = _sel_dot(kb * kb, gsum128, 2) * (1.0 / HD_B)
    kn = kb * lax.rsqrt(msk + EPS) * gk
    vb = proj_ref[:, B_V:B_V + 128]
    return kn, vb


def _advance(gens, yielded=None):
    alive = []
    for gen in gens:
        try:
            value = next(gen)
            alive.append(gen)
            if yielded is not None and value is not None:
                yielded.append(value)
        except StopIteration:
            pass
    return alive


def _interleave(gens):
    gens = list(gens)
    while gens:
        gens = _advance(gens)


def _ab_chunk(r, c, kb, proj_ref, qn_ref, o_ref, state, k_band, v_band, bias_ref, sink_ref,
              w_gu, b_gate, g_out, first_valid_col):
    rows = pl.ds(r, c)
    gate = _dot(proj_ref[rows, A_G:A_G + LANES], w_gu) + b_gate
    yield
    g = _log_sigmoid(gate) * (1.0 / GATE_NORM_A)
    tril = _tril(c)
    b = _sel_dot_left(tril.astype(BF16), g, 3)
    bl = b[c - 1:c, :]
    bl_tile = jnp.broadcast_to(bl, (LANES, H_A * DK_A))
    bl_rows = jnp.concatenate([bl_tile[:, i * LANES:(i + 1) * LANES].T
                               for i in range(H_A * DK_A // LANES)], axis=0)
    yield
    q = proj_ref[rows, A_Q:A_Q + 256] * (DK_A ** -0.5)
    k = proj_ref[rows, A_K:A_K + 256]
    v = proj_ref[rows, A_V:A_V + 512].astype(BF16)
    qe = q * jnp.exp(b)
    kd = (k * jnp.exp(-b)).astype(BF16)
    kd2 = (k * jnp.exp(bl - b)).astype(BF16)
    lane128 = lax.broadcasted_iota(jnp.int32, (c, LANES), 1)
    row128 = lax.broadcasted_iota(jnp.int32, (LANES, LANES), 0)
    qm, att = [], []
    for hd in range(H_A):
        p, j = divmod(hd, 2)
        qm.append(jnp.where((lane128 // DK_A) == j, qe[:, p * 128:(p + 1) * 128], 0.0).astype(BF16))
        att.append(_dot_nt(qm[hd], kd[:, p * 128:(p + 1) * 128]))
    upd = [_dot_tn(kd2[:, p * 128:(p + 1) * 128], v[:, p * 256:(p + 1) * 256]) for p in range(2)]
    yield
    oh = [_dot(jnp.where(tril, att[hd], 0.0), v[:, hd * 128:(hd + 1) * 128]) for hd in range(H_A)]
    yield
    s_prev = state[0]
    s_new = []
    for p in range(2):
        sp = s_prev[p * 128:(p + 1) * 128, :]
        sp_bf = sp.astype(BF16)
        for j in range(2):
            oh[2 * p + j] = oh[2 * p + j] + _dot(qm[2 * p + j], sp_bf)
        u = jnp.where(row128 < DK_A, upd[p][:, :128], upd[p][:, 128:])
        s_new.append(jnp.exp(bl_rows[p * 128:(p + 1) * 128, :]) * sp + u)
    state[0] = jnp.concatenate(s_new, axis=0)
    yield
    for hd in range(H_A):
        z = proj_ref[rows, A_Z + hd * 128:A_Z + (hd + 1) * 128]
        o_ref[rows, hd * 128:(hd + 1) * 128] = (_rms_rows(oh[hd]) * g_out * _silu(z)).astype(BF16)
    yield
    lane256 = lax.broadcasted_iota(jnp.int32, (c, 256), 1) // HD_B
    srow = lax.broadcasted_iota(jnp.int32, (G_B * c, 1), 0) // c
    scores = []
    for kvh in range(KVH_B):
        qn = qn_ref[rows, kvh * 256:(kvh + 1) * 256]
        qs = jnp.concatenate([jnp.where(lane256 == gq_, qn, 0.0) for gq_ in range(G_B)], axis=0)
        scores.append(_dot_nt(qs, k_band(kvh)))
    yield
    probs, dens = [], []
    for kvh in range(KVH_B):
        s = scores[kvh] * (HD_B ** -0.5) + bias_ref[kvh]
        if first_valid_col is not None:
            col = lax.broadcasted_iota(jnp.int32, (G_B * c, kb), 1)
            s = jnp.where(col >= first_valid_col, s, -jnp.inf)
        sink = jnp.zeros((G_B * c, 1), F32)
        for gq_ in range(G_B):
            sink = jnp.where(srow == gq_, sink_ref[kvh * G_B + gq_], sink)
        m = jnp.maximum(jnp.max(s, axis=-1, keepdims=True), sink)
        pr = jnp.exp(s - m)
        dens.append(jnp.sum(pr, axis=-1, keepdims=True) + jnp.exp(sink - m))
        probs.append(pr.astype(BF16))
    yield
    outs = [_dot(probs[kvh], v_band(kvh)) for kvh in range(KVH_B)]
    yield
    for kvh in range(KVH_B):
        ost = outs[kvh] / dens[kvh]
        ob = jnp.zeros((c, 256), F32)
        for gq_ in range(G_B):
            ob = ob + jnp.where(lane256 == gq_, ost[gq_ * c:(gq_ + 1) * c, :], 0.0)
        z = proj_ref[rows, B_Z + kvh * 256:B_Z + (kvh + 1) * 256]
        o_ref[rows, 512 + kvh * 256:512 + (kvh + 1) * 256] = (ob * _silu(z)).astype(BF16)


def _ab_prompt_kernel(sink_ref, x_ref, rms_g_ref, w_in_ref, w_gu_ref, b_gate_ref, g_out_ref, gq_ref,
                      gk_ref, bias_ref, w_out_ref, gsum512_ref, gsum128_ref, tile_ref,
                      y_ref, s_out_ref, k_out_ref, v_out_ref,
                      proj_ref, qn_ref, o_ref, kband_ref, vband_ref, s_ref):
    t = x_ref.shape[0]
    c = CHUNK
    kb = WINDOW + c
    nchunk = t // c
    step = pl.program_id(0)

    @pl.when(step == 0)
    def _():
        s_ref[...] = jnp.zeros_like(s_ref)
        kband_ref[:, t:t + WINDOW, :] = jnp.zeros((KVH_B, WINDOW, 256), BF16)
        vband_ref[:, t:t + WINDOW, :] = jnp.zeros((KVH_B, WINDOW, 256), BF16)

    for kvh in range(KVH_B):
        kband_ref[kvh, 0:WINDOW, :] = kband_ref[kvh, t:t + WINDOW, :]
        vband_ref[kvh, 0:WINDOW, :] = vband_ref[kvh, t:t + WINDOW, :]

    x = x_ref[...]
    kn, vb = _ab_dense_in(x, rms_g_ref[...], w_in_ref, gsum512_ref[...], gsum128_ref[...],
                          gq_ref[...], gk_ref[...], proj_ref, qn_ref)
    k_out_ref[...] = kn[t - WINDOW:, :]
    v_out_ref[...] = vb[t - WINDOW:, :]
    for kvh in range(KVH_B):
        kband_ref[kvh, WINDOW:WINDOW + t, :] = jnp.dot(
            kn.astype(BF16), tile_ref[kvh], preferred_element_type=F32).astype(BF16)
        vband_ref[kvh, WINDOW:WINDOW + t, :] = jnp.dot(
            vb.astype(BF16), tile_ref[kvh], preferred_element_type=F32).astype(BF16)

    w_gu = w_gu_ref[...]
    b_gate = b_gate_ref[...]
    g_out = g_out_ref[...]

    state = [s_ref[...]]

    def chunk(i):
        r = i * c
        return _ab_chunk(r, c, kb, proj_ref, qn_ref, o_ref, state,
                         lambda kvh: kband_ref[kvh, pl.ds(r, kb), :],
                         lambda kvh: vband_ref[kvh, pl.ds(r, kb), :],
                         bias_ref, sink_ref, w_gu, b_gate, g_out, (2 - (step * nchunk + i)) * c)

    _interleave(chunk(i) for i in range(nchunk))
    s_ref[...] = state[0]
    y_ref[...] = x + jnp.dot(o_ref[...], w_out_ref[...], preferred_element_type=F32)
    s_out_ref[...] = s_ref[...]


def _ab_sample_kernel(sink_ref, x_ref, rms_g_ref, w_in_ref, w_gu_ref, b_gate_ref, g_out_ref, gq_ref,
                      gk_ref, bias_ref, w_out_ref, gsum512_ref, gsum128_ref, tile_ref,
                      s_in_ref, kc_ref, vc_ref,
                      y_ref, s_out_ref, k_out_ref, v_out_ref,
                      proj_ref, qn_ref, o_ref, kband_ref, vband_ref):
    nb, wc = kc_ref.shape[0], kc_ref.shape[1]
    t = x_ref.shape[0]
    c = t // nb
    kb = wc + c
    x = x_ref[...]
    kn, vb = _ab_dense_in(x, rms_g_ref[...], w_in_ref, gsum512_ref[...], gsum128_ref[...],
                          gq_ref[...], gk_ref[...], proj_ref, qn_ref)
    k_out_ref[...] = kn
    v_out_ref[...] = vb
    for kvh in range(KVH_B):
        kt = jnp.dot(kn.astype(BF16), tile_ref[kvh], preferred_element_type=F32).astype(BF16)
        vt = jnp.dot(vb.astype(BF16), tile_ref[kvh], preferred_element_type=F32).astype(BF16)
        for bi in range(nb):
            kband_ref[kvh, bi, 0:wc, :] = jnp.dot(
                kc_ref[bi].astype(BF16), tile_ref[kvh], preferred_element_type=F32).astype(BF16)
            vband_ref[kvh, bi, 0:wc, :] = jnp.dot(
                vc_ref[bi].astype(BF16), tile_ref[kvh], preferred_element_type=F32).astype(BF16)
            kband_ref[kvh, bi, wc:kb, :] = kt[bi * c:(bi + 1) * c, :]
            vband_ref[kvh, bi, wc:kb, :] = vt[bi * c:(bi + 1) * c, :]

    w_gu = w_gu_ref[...]
    b_gate = b_gate_ref[...]
    g_out = g_out_ref[...]

    states = [[s_in_ref[bi]] for bi in range(nb)]

    def seq(bi):
        return _ab_chunk(bi * c, c, kb, proj_ref, qn_ref, o_ref, states[bi],
                         lambda kvh: kband_ref[kvh, bi], lambda kvh: vband_ref[kvh, bi],
                         bias_ref, sink_ref, w_gu, b_gate, g_out, None)

    _interleave(seq(bi) for bi in range(nb))
    for bi in range(nb):
        s_out_ref[bi] = states[bi][0]
    y_ref[...] = x + jnp.dot(o_ref[...], w_out_ref[...], preferred_element_type=F32)


GROUP_W = D_INNER_C // G_C


def _c_chunk(r, c, io, state, e_s, e_p, diag, neg_mask, bd_mask, dskip, g_y):
    dtc = io.dt(r)
    acum = _sel_dot_left(_tril(c).astype(BF16), io.da(r), 3)
    yield acum
    lhs = jnp.concatenate([jnp.concatenate(_split_bf16(acum, 2), axis=1),
                           jnp.concatenate(_split_bf16(dtc, 2), axis=1)], axis=0)
    both_p = jnp.dot(lhs, e_p, preferred_element_type=F32)
    xa_p, dt_p = both_p[:c], both_p[c:]
    xa_s = xa_p if c == P_C else jnp.dot(lhs[:c], e_s, preferred_element_type=F32)
    yield dt_p
    a_row = jnp.sum(xa_s * diag, axis=0, keepdims=True)
    wmat = jnp.exp((xa_s - a_row) + neg_mask)
    al_p = xa_p[c - 1:c, :]
    ea_p = jnp.exp(xa_p)
    xs = io.x(r)
    xdt = xs * dt_p
    xdt_bf = xdt.astype(BF16)
    xw = (xdt * jnp.exp(al_p - xa_p)).astype(BF16)
    dec = jnp.exp(al_p)
    bg, cg, cb = [], [], []
    for g in range(G_C):
        bg.append(io.b(r, g).astype(BF16))
        cg.append(io.c(r, g).astype(BF16))
        cb.append(_dot_nt(cg[g], jnp.concatenate([bg[g]] * HPG_C, axis=0)))
    yield cb[-1]
    ys = []
    for g in range(G_C):
        mg = (cb[g] * wmat[:, g * HPG_C * c:(g + 1) * HPG_C * c]).astype(BF16)
        for j in range(2):
            xj = xdt_bf[:, g * GROUP_W + j * 256:g * GROUP_W + (j + 1) * 256]
            bd = jnp.concatenate([xj] * 4, axis=0) * bd_mask
            ys.append(jnp.dot(mg[:, j * 4 * c:(j + 1) * 4 * c], bd, preferred_element_type=F32))
    upd = [_dot_tn(bg[g], xw[:, g * GROUP_W:(g + 1) * GROUP_W]) for g in range(G_C)]
    yield upd[-1]
    st_prev = state[0]
    y_inter, st_new = [], []
    for g in range(G_C):
        sl = slice(g * GROUP_W, (g + 1) * GROUP_W)
        y_inter.append(_dot(cg[g], st_prev[:, sl]))
        st_new.append(st_prev[:, sl] * dec[:, sl] + upd[g])
    state[0] = jnp.concatenate(st_new, axis=1)
    yield st_new[-1]
    for g in range(G_C):
        sl = slice(g * GROUP_W, (g + 1) * GROUP_W)
        y = jnp.concatenate(ys[2 * g:2 * g + 2], axis=1) + y_inter[g] * ea_p[:, sl]
        y = y + dskip[:, sl] * xs[:, sl]
        y = y * _silu(io.z(r, g))
        io.put_o(r, g, (_rms_rows(y) * g_y[:, sl]).astype(BF16))


def _c_dt(dt_cols, dt_bias, a_log):
    dt = _softplus(dt_cols + dt_bias)
    return dt, dt * (-jnp.exp(a_log))


CONV_PITCH = PROMPT_BLOCK // 8 + 1
CONV_ROWS = 8 * CONV_PITCH
CONV_TILES = CONV_DIM // LANES
C_ROUNDS = 12
C_IN_UNITS = (4, 4, 4, 4, 4, 4, 4, 2, 4, 2, 2, 3)
C_OUT_ROUNDS = (3, 5, 8, 10)
assert len(C_IN_UNITS) == C_ROUNDS and sum(C_IN_UNITS) * LANES == C_COLS


def _conv_tile(ubuf_ref, act_ref, j, conv_w, conv_b):
    w = [jnp.broadcast_to(conv_w[i:i + 1, j * LANES:(j + 1) * LANES], (8, LANES)) for i in range(CONV_W)]
    b = jnp.broadcast_to(conv_b[:, j * LANES:(j + 1) * LANES], (8, LANES))
    for a in range(CONV_PITCH):
        acc = b
        for i in range(CONV_W):
            acc = acc + w[i] * ubuf_ref[j, pl.ds(8 - (CONV_W - 1) + i + a, 8, stride=CONV_PITCH), :]
        act_ref[j, pl.ds(a, 8, stride=CONV_PITCH), :] = _silu(acc)


def _c_prompt_kernel(xin_ref, xres_ref, rms_g_ref, w_in_ref, conv_w_ref, conv_b_ref, dt_bias_ref,
                     a_log_ref, dskip_ref, g_y_ref, w_out_ref, e_s_ref, diag_ref, neg_mask_ref, bd_mask_ref,
                     y_ref, st_out_ref, conv_out_ref,
                     proj_ref, ubuf_ref, act_ref, dt_ref, da_ref, o_ref, st_ref):
    t = xin_ref.shape[0]
    c = CHUNK
    s = pl.program_id(0)
    n_blocks = pl.num_programs(0) - 2

    @pl.when(s == 0)
    def _():
        proj_ref[1] = jnp.zeros(proj_ref.shape[1:], F32)
        dt_ref[1] = jnp.zeros(dt_ref.shape[1:], F32)
        da_ref[1] = jnp.zeros(da_ref.shape[1:], F32)
        o_ref[...] = jnp.zeros_like(o_ref)
        ubuf_ref[...] = jnp.zeros_like(ubuf_ref)

    @pl.when(s <= 1)
    def _():
        st_ref[...] = jnp.zeros_like(st_ref)
        ubuf_ref[:, CONV_ROWS:CONV_ROWS + 8, :] = jnp.zeros((CONV_TILES, 8, LANES), F32)

    def in_stage(slot_in):
        h = (_rms_rows(xin_ref[...]) * rms_g_ref[...]).astype(BF16)
        lo = 0
        for units in C_IN_UNITS:
            if lo:
                yield
            hi = lo + units * LANES
            slab = jnp.dot(h, w_in_ref[:, lo:hi], preferred_element_type=F32)
            proj_ref[slot_in, :, lo:hi] = slab
            if hi == C_COLS:
                dt, da = _c_dt(slab[:, C_DT - lo:], dt_bias_ref[...], a_log_ref[...])
                dt_ref[slot_in] = dt
                da_ref[slot_in] = da
            lo = hi

    def act_rows(r, j):
        return act_ref[j, pl.ds(8 + r, c), :]

    n_x = D_INNER_C // LANES
    n_g = N_C // LANES

    def mix_stage(slot_mix):
        def put_o(r, g, value):
            o_ref[slot_mix, pl.ds(r, c), g * GROUP_W:(g + 1) * GROUP_W] = value

        io = types.SimpleNamespace(
            dt=lambda r: dt_ref[slot_mix, pl.ds(r, c), :],
            da=lambda r: da_ref[slot_mix, pl.ds(r, c), :],
            x=lambda r: jnp.concatenate([act_rows(r, j) for j in range(n_x)], axis=1),
            b=lambda r, g: act_rows(r, n_x + g * n_g),
            c=lambda r, g: act_rows(r, n_x + (G_C + g) * n_g),
            z=lambda r, g: proj_ref[slot_mix, pl.ds(r, c), C_Z + g * GROUP_W:C_Z + (g + 1) * GROUP_W],
            put_o=put_o)
        conv_w = conv_w_ref[...]
        conv_b = conv_b_ref[...]
        conv_rounds = C_ROUNDS - 6
        per_round = CONV_TILES // conv_rounds
        for rnd in range(conv_rounds):
            if rnd:
                yield
            for j in range(rnd * per_round, (rnd + 1) * per_round):
                ubuf_ref[j, 8:16, :] = ubuf_ref[j, CONV_ROWS:CONV_ROWS + 8, :]
                ubuf_ref[j, 16:16 + t, :] = proj_ref[slot_mix, :, C_X + j * LANES:C_X + (j + 1) * LANES]
                _conv_tile(ubuf_ref, act_ref, j, conv_w, conv_b)
        state = [st_ref[...]]
        e_s = e_s_ref[...]
        chunks = [_c_chunk(i * c, c, io, state, e_s, e_s, diag_ref[...], neg_mask_ref[...], bd_mask_ref[...],
                           dskip_ref[...], g_y_ref[...]) for i in range(t // c)]
        while chunks:
            yield
            chunks = _advance(chunks)
        st_ref[...] = state[0]

    def out_stage(slot_in):
        quarter = 0
        for rnd in range(C_ROUNDS):
            if rnd:
                yield
            if rnd in C_OUT_ROUNDS:
                cols = slice(quarter * 256, (quarter + 1) * 256)
                y_ref[:, cols] = xres_ref[:, cols] + jnp.dot(o_ref[slot_in], w_out_ref[:, cols],
                                                             preferred_element_type=F32)
                quarter += 1

    slot_in = s % 2
    _interleave([in_stage(slot_in), mix_stage(1 - slot_in), out_stage(slot_in)])

    @pl.when(s == n_blocks)
    def _():
        st_out_ref[...] = st_ref[...].T
        conv_out_ref[...] = proj_ref[1 - s % 2, t - 8:t, C_X:C_X + CONV_DIM]


def _conv_rows(ubuf, nrows, conv_w, conv_b):
    acc = conv_b
    for i in range(CONV_W):
        acc = acc + conv_w[i:i + 1, :] * ubuf[pl.ds(8 - (CONV_W - 1) + i, nrows), :]
    return _silu(acc)


def _c_sample_kernel(x_ref, rms_g_ref, w_in_ref, conv_w_ref, conv_b_ref, dt_bias_ref, a_log_ref,
                     dskip_ref, g_y_ref, w_out_ref, e_s_ref, e_p_ref, diag_ref, neg_mask_ref, bd_mask_ref,
                     st_in_ref, conv_in_ref,
                     y_ref, st_out_ref, conv_out_ref,
                     proj_ref, ubuf_ref, act_ref, dt_ref, da_ref, o_ref):
    nb = st_in_ref.shape[0]
    t = x_ref.shape[0]
    c = t // nb
    x = x_ref[...]
    h = (_rms_rows(x) * rms_g_ref[...]).astype(BF16)
    proj_ref[...] = jnp.dot(h, w_in_ref[...], preferred_element_type=F32)
    dt, da = _c_dt(proj_ref[:, C_DT:C_DT + LANES], dt_bias_ref[...], a_log_ref[...])
    dt_ref[...] = dt
    da_ref[...] = da
    conv_w = conv_w_ref[...]
    conv_b = conv_b_ref[...]
    ubuf_ref[0:8, :] = jnp.zeros((8, CONV_DIM), F32)
    for bi in range(nb):
        ubuf_ref[8 - (CONV_W - 1):8, :] = conv_in_ref[bi]
        ubuf_ref[8:8 + c, :] = proj_ref[bi * c:(bi + 1) * c, C_X:C_X + CONV_DIM]
        act_ref[bi * c:(bi + 1) * c, :] = _conv_rows(ubuf_ref, c, conv_w, conv_b)
        conv_out_ref[bi] = ubuf_ref[8 + c - (CONV_W - 1):8 + c, :]

    def put_o(r, g, value):
        o_ref[pl.ds(r, c), g * GROUP_W:(g + 1) * GROUP_W] = value

    io = types.SimpleNamespace(
        dt=lambda r: dt_ref[pl.ds(r, c), :],
        da=lambda r: da_ref[pl.ds(r, c), :],
        x=lambda r: act_ref[pl.ds(r, c), 0:D_INNER_C],
        b=lambda r, g: act_ref[pl.ds(r, c), D_INNER_C + g * N_C:D_INNER_C + (g + 1) * N_C],
        c=lambda r, g: act_ref[pl.ds(r, c), D_INNER_C + (G_C + g) * N_C:D_INNER_C + (G_C + g + 1) * N_C],
        z=lambda r, g: proj_ref[pl.ds(r, c), C_Z + g * GROUP_W:C_Z + (g + 1) * GROUP_W],
        put_o=put_o)
    e_s = e_s_ref[...]
    e_p = e_p_ref[...]
    diag = diag_ref[...]
    neg_mask = neg_mask_ref[...]
    bd_mask = bd_mask_ref[...]
    dskip = dskip_ref[...]
    g_y = g_y_ref[...]

    states = [[st_in_ref[bi].T] for bi in range(nb)]
    _interleave(_c_chunk(bi * c, c, io, states[bi], e_s, e_p, diag, neg_mask, bd_mask, dskip, g_y)
                for bi in range(nb))
    for bi in range(nb):
        st_out_ref[bi] = states[bi][0].T
    y_ref[...] = x + jnp.dot(o_ref[...], w_out_ref[...], preferred_element_type=F32)


def _bucket_table(q_off, k_off):
    n = q_off[:, None] - k_off[None, :]
    half = N_BUCKETS // 2
    max_exact = half // 2
    side = np.where(n < 0, half, 0)
    n = np.abs(n)
    nf = np.maximum(n, max_exact).astype(np.float32)
    large = max_exact + (np.log(nf / np.float32(max_exact)) / np.float32(math.log(MAX_DISTANCE / max_exact))
                         * np.float32(half - max_exact)).astype(np.int32)
    large = np.minimum(large, half - 1)
    return side + np.where(n < max_exact, n, large)


def _bias_rows(rel_bias, q_off, k_off):
    bucket = _bucket_table(q_off, k_off)
    onehot = jnp.asarray(np.eye(N_BUCKETS, dtype=np.float32)[bucket])
    bias = jnp.einsum('qkb,bh->hqk', onehot, rel_bias.astype(F32), precision=lax.Precision.HIGHEST)
    return bias.reshape(KVH_B, G_B * q_off.shape[0], k_off.shape[0])


def _group_sum_matrix(width, group):
    idx = np.arange(width) // group
    return jnp.asarray(idx[:, None] == idx[None, :], BF16)


def _kv_tile_matrices():
    m = np.zeros((KVH_B, KVH_B * HD_B, G_B * HD_B), np.float32)
    for kvh in range(KVH_B):
        for gq in range(G_B):
            m[kvh, kvh * HD_B + np.arange(HD_B), gq * HD_B + np.arange(HD_B)] = 1.0
    return jnp.asarray(m, BF16)


def _head_expand_matrix(per_head):
    m = np.zeros((LANES, H_C * per_head), np.float32)
    for h in range(H_C):
        m[h, h * per_head:(h + 1) * per_head] = 1.0
    return jnp.asarray(np.concatenate([m, m], axis=0), BF16)


def _full(shape):
    return pl.BlockSpec(shape, lambda *_: (0,) * len(shape))


def _const(shape):
    return pl.BlockSpec(shape, lambda *_: (0,) * len(shape), pipeline_mode=pl.Buffered(1))


def _lane_pad(x, width):
    return jnp.concatenate([x, jnp.zeros((x.shape[0], width - x.shape[1]), x.dtype)], axis=1)


def _ab_cast_kernel(w_in_ref, w_out_ref, w_in_bf_ref, w_out_bf_ref):
    g0 = A_Z
    w_in_bf_ref[:, 0:g0] = w_in_ref[:, 0:g0].astype(BF16)
    w_in_bf_ref[:, g0:A_G] = w_in_ref[:, g0 + GATE_RANK_A:A_G + GATE_RANK_A].astype(BF16)
    w_in_bf_ref[:, A_G:AB_COLS] = _lane_pad(w_in_ref[:, g0:g0 + GATE_RANK_A], AB_COLS - A_G).astype(BF16)
    w_out_bf_ref[...] = w_out_ref[...].astype(BF16)


def _c_cast_kernel(w_in_ref, w_out_ref, w_in_bf_ref, w_out_bf_ref):
    w_in_bf_ref[:, 0:C_DT] = w_in_ref[:, 0:C_DT].astype(BF16)
    w_in_bf_ref[:, C_DT:C_COLS] = _lane_pad(w_in_ref[:, C_DT:C_DT + H_C], C_COLS - C_DT).astype(BF16)
    w_out_bf_ref[...] = w_out_ref[...].astype(BF16)


def _cast_weights(body, w_in, w_out, cols, name):
    steps = 4
    rows_in, rows_out = w_in.shape[0] // steps, w_out.shape[0] // steps
    blk = lambda i: (i, 0)
    return pl.pallas_call(
        body, grid=(steps,),
        in_specs=[pl.BlockSpec((rows_in, w_in.shape[1]), blk), pl.BlockSpec((rows_out, w_out.shape[1]), blk)],
        out_specs=(pl.BlockSpec((rows_in, cols), blk), pl.BlockSpec((rows_out, w_out.shape[1]), blk)),
        out_shape=(jax.ShapeDtypeStruct((w_in.shape[0], cols), BF16),
                   jax.ShapeDtypeStruct(w_out.shape, BF16)),
        name=name, compiler_params=pltpu.CompilerParams(vmem_limit_bytes=VMEM_LIMIT),
    )(w_in, w_out)


def _ab_weights(rms_g, w_in, w_gate_up, b_gate, g_out, g_q, g_k, w_out):
    w_in_r, w_out_bf = _cast_weights(_ab_cast_kernel, w_in, w_out, AB_COLS, "ab_cast")
    w_gu = jnp.concatenate([w_gate_up, jnp.zeros((LANES - GATE_RANK_A, H_A * DK_A), w_gate_up.dtype)],
                           axis=0).astype(BF16)
    return (rms_g.reshape(1, D_MODEL), w_in_r, w_gu, b_gate.reshape(1, -1), g_out.reshape(1, DV_A),
            jnp.tile(g_q, H_B).reshape(1, -1), jnp.tile(g_k, KVH_B).reshape(1, -1), w_out_bf)


def _ab_prompt(x, sink, rel_bias, weights):
    rms_g, w_in, w_gu, b_gate, g_out, gq, gk, w_out = weights
    length = x.shape[0]
    t = AB_PROMPT_BLOCK
    c = CHUNK
    kb = WINDOW + c
    bias = _bias_rows(rel_bias, np.arange(c), np.arange(kb) - WINDOW)
    row_blk = lambda i: (i, 0)
    in_specs = [
        pl.BlockSpec(memory_space=pltpu.SMEM),
        pl.BlockSpec((t, D_MODEL), row_blk),
        _const((1, D_MODEL)), _const((D_MODEL, AB_COLS)), _const((LANES, 256)), _const((1, 256)),
        _const((1, DV_A)), _const((1, 512)), _const((1, 128)), _const((KVH_B, G_B * c, kb)),
        _const((D_MODEL, D_MODEL)), _const((512, 512)), _const((128, 128)), _const((KVH_B, 128, 256)),
    ]
    out_shape = (jax.ShapeDtypeStruct((length, D_MODEL), F32),
                 jax.ShapeDtypeStruct((H_A * DK_A, DV_A), F32),
                 jax.ShapeDtypeStruct((WINDOW, KVH_B * HD_B), F32),
                 jax.ShapeDtypeStruct((WINDOW, KVH_B * HD_B), F32))
    out_specs = (pl.BlockSpec((t, D_MODEL), row_blk), _full((H_A * DK_A, DV_A)),
                 _full((WINDOW, KVH_B * HD_B)), _full((WINDOW, KVH_B * HD_B)))
    scratch = [pltpu.VMEM((t, AB_COLS), F32), pltpu.VMEM((t, 512), F32), pltpu.VMEM((t, D_MODEL), BF16),
               pltpu.VMEM((KVH_B, t + WINDOW, 256), BF16), pltpu.VMEM((KVH_B, t + WINDOW, 256), BF16),
               pltpu.VMEM((H_A * DK_A, DV_A), F32)]
    return pl.pallas_call(
        _ab_prompt_kernel, grid=(length // t,), in_specs=in_specs, out_specs=out_specs,
        out_shape=out_shape, scratch_shapes=scratch, name="ab_prompt",
        compiler_params=pltpu.CompilerParams(dimension_semantics=("arbitrary",),
                                             vmem_limit_bytes=VMEM_LIMIT),
    )(sink, x, rms_g, w_in, w_gu, b_gate, g_out, gq, gk, bias, w_out,
      _group_sum_matrix(512, HD_B), _group_sum_matrix(128, HD_B), _kv_tile_matrices())


def _ab_sample(x, sink, rel_bias, weights, s_in, k_cache, v_cache):
    rms_g, w_in, w_gu, b_gate, g_out, gq, gk, w_out = weights
    nb, c, _ = x.shape
    wc = k_cache.shape[1]
    kb = wc + c
    t = nb * c
    bias = _bias_rows(rel_bias, np.arange(c), np.arange(kb) - wc)
    vm = pl.BlockSpec(memory_space=pltpu.VMEM)
    in_specs = [pl.BlockSpec(memory_space=pltpu.SMEM)] + [vm] * 16
    out_shape = (jax.ShapeDtypeStruct((t, D_MODEL), F32),
                 jax.ShapeDtypeStruct((nb, H_A * DK_A, DV_A), F32),
                 jax.ShapeDtypeStruct((t, KVH_B * HD_B), F32),
                 jax.ShapeDtypeStruct((t, KVH_B * HD_B), F32))
    scratch = [pltpu.VMEM((t, AB_COLS), F32), pltpu.VMEM((t, 512), F32), pltpu.VMEM((t, D_MODEL), BF16),
               pltpu.VMEM((KVH_B, nb, kb, 256), BF16), pltpu.VMEM((KVH_B, nb, kb, 256), BF16)]
    return pl.pallas_call(
        _ab_sample_kernel, in_specs=in_specs, out_specs=(vm,) * 4, out_shape=out_shape,
        scratch_shapes=scratch, name="ab_sample",
        compiler_params=pltpu.CompilerParams(vmem_limit_bytes=VMEM_LIMIT),
    )(sink, x.reshape(t, D_MODEL), rms_g, w_in, w_gu, b_gate, g_out, gq, gk, bias, w_out,
      _group_sum_matrix(512, HD_B), _group_sum_matrix(128, HD_B), _kv_tile_matrices(),
      s_in.reshape(nb, H_A * DK_A, DV_A), k_cache.reshape(nb, wc, KVH_B * HD_B),
      v_cache.reshape(nb, wc, KVH_B * HD_B))


def _c_weights(rms_g, w_in, conv_w, conv_b, dt_bias, a_log, d_skip, g_y, w_out):
    w_in_p, w_out_bf = _cast_weights(_c_cast_kernel, w_in, w_out, C_COLS, "c_cast")
    pad = lambda v: jnp.concatenate([v, jnp.zeros((LANES - H_C,), v.dtype)]).reshape(1, LANES)
    return (rms_g.reshape(1, D_MODEL), w_in_p, conv_w, conv_b.reshape(1, CONV_DIM), pad(dt_bias), pad(a_log),
            jnp.repeat(d_skip, P_C).reshape(1, D_INNER_C), g_y.reshape(1, D_INNER_C), w_out_bf)


def _ssd_masks(c):
    tok = np.arange(c)[:, None]
    src = np.arange(H_C * c)[None, :] % c
    diag = (tok == src).astype(np.float32)
    neg = np.where(src <= tok, 0.0, -np.inf).astype(np.float32)
    blk = (np.arange(4 * c)[:, None] // c) == (np.arange(256)[None, :] // P_C)
    return jnp.asarray(diag), jnp.asarray(neg), jnp.asarray(blk, BF16)


def _c_prompt(x, weights):
    length = x.shape[0]
    t = PROMPT_BLOCK
    c = CHUNK
    n_blocks = length // t
    in_blk = lambda s: (jnp.minimum(s, n_blocks - 1), 0)
    out_blk = lambda s: (jnp.clip(s - 2, 0, n_blocks - 1), 0)
    in_specs = [
        pl.BlockSpec((t, D_MODEL), in_blk), pl.BlockSpec((t, D_MODEL), out_blk),
        _const((1, D_MODEL)), _const((D_MODEL, C_COLS)), _const((CONV_W, CONV_DIM)), _const((1, CONV_DIM)),
        _const((1, LANES)), _const((1, LANES)), _const((1, D_INNER_C)), _const((1, D_INNER_C)),
        _const((D_INNER_C, D_MODEL)), _const((2 * LANES, H_C * c)),
        _const((c, H_C * c)), _const((c, H_C * c)), _const((4 * c, 256)),
    ]
    out_shape = (jax.ShapeDtypeStruct((length, D_MODEL), F32),
                 jax.ShapeDtypeStruct((H_C * P_C, N_C), F32),
                 jax.ShapeDtypeStruct((8, CONV_DIM), F32))
    out_specs = (pl.BlockSpec((t, D_MODEL), out_blk), _full((H_C * P_C, N_C)), _full((8, CONV_DIM)))
    scratch = [pltpu.VMEM((2, t, C_COLS), F32), pltpu.VMEM((CONV_TILES, CONV_ROWS + 8, LANES), F32),
               pltpu.VMEM((CONV_TILES, CONV_ROWS, LANES), F32), pltpu.VMEM((2, t, LANES), F32),
               pltpu.VMEM((2, t, LANES), F32), pltpu.VMEM((2, t, D_INNER_C), BF16),
               pltpu.VMEM((N_C, H_C * P_C), F32)]
    return pl.pallas_call(
        _c_prompt_kernel, grid=(n_blocks + 2,), in_specs=in_specs, out_specs=out_specs,
        out_shape=out_shape, scratch_shapes=scratch, name="c_prompt",
        compiler_params=pltpu.CompilerParams(dimension_semantics=("arbitrary",),
                                             vmem_limit_bytes=VMEM_LIMIT),
    )(x, x, *weights, _head_expand_matrix(c), *_ssd_masks(c))


def _c_sample(x, weights, st_in, conv_in):
    nb, c, _ = x.shape
    t = nb * c
    vm = pl.BlockSpec(memory_space=pltpu.VMEM)
    out_shape = (jax.ShapeDtypeStruct((t, D_MODEL), F32),
                 jax.ShapeDtypeStruct((nb, H_C * P_C, N_C), F32),
                 jax.ShapeDtypeStruct((nb, CONV_W - 1, CONV_DIM), F32))
    scratch = [pltpu.VMEM((t, C_COLS), F32), pltpu.VMEM((c + 8, CONV_DIM), F32),
               pltpu.VMEM((t, CONV_DIM), F32), pltpu.VMEM((t, LANES), F32), pltpu.VMEM((t, LANES), F32),
               pltpu.VMEM((t, D_INNER_C), BF16)]
    return pl.pallas_call(
        _c_sample_kernel, in_specs=[vm] * 17, out_specs=(vm,) * 3, out_shape=out_shape,
        scratch_shapes=scratch, name="c_sample",
        compiler_params=pltpu.CompilerParams(vmem_limit_bytes=VMEM_LIMIT),
    )(x.reshape(t, D_MODEL), *weights, _head_expand_matrix(c), _head_expand_matrix(P_C), *_ssd_masks(c),
      st_in.reshape(nb, H_C * P_C, N_C), conv_in)


def kernel(x_prompt, x_sample, cache_swa_k, cache_swa_v, state_gla, state_ssd, state_conv, rms_g, w_in_ab, w_gate_up_a, b_gate_a, g_out_a, g_q_b, g_k_b, sink_b, rel_bias, w_out_ab, w_in_c, conv_w_c, conv_b_c, dt_bias_c, a_log_c, d_skip_c, g_y_c, w_out_c):
    bp, seq_len, _ = x_prompt.shape
    nb, dec_len, _ = x_sample.shape
    assert bp == 1 and seq_len % PROMPT_BLOCK == 0 and seq_len % AB_PROMPT_BLOCK == 0 and seq_len >= WINDOW
    wab = _ab_weights(rms_g[0], w_in_ab[0], w_gate_up_a[0], b_gate_a[0], g_out_a[0], g_q_b[0], g_k_b[0],
                      w_out_ab[0])
    yp, gla_p, k_p, v_p = _ab_prompt(x_prompt[0], sink_b[0], rel_bias, wab)
    ys, gla_s, k_s, v_s = _ab_sample(x_sample, sink_b[0], rel_bias, wab, state_gla[0], cache_swa_k[0],
                                     cache_swa_v[0])
    wc = _c_weights(rms_g[1], w_in_c[0], conv_w_c[0], conv_b_c[0], dt_bias_c[0], a_log_c[0], d_skip_c[0],
                    g_y_c[0], w_out_c[0])
    yp, ssd_p, conv_p = _c_prompt(yp, wc)
    ys, ssd_s, conv_s = _c_sample(ys.reshape(nb, dec_len, D_MODEL), wc, state_ssd[0], state_conv[0])
    return (
        yp.reshape(1, seq_len, D_MODEL),
        ys.reshape(nb, dec_len, D_MODEL),
        gla_p.reshape(1, 1, H_A, DK_A, DV_A),
        k_p.reshape(1, 1, WINDOW, KVH_B, HD_B),
        v_p.reshape(1, 1, WINDOW, KVH_B, HD_B),
        ssd_p.reshape(1, 1, H_C, P_C, N_C),
        conv_p[8 - (CONV_W - 1):].reshape(1, 1, CONV_W - 1, CONV_DIM),
        gla_s.reshape(1, nb, H_A, DK_A, DV_A),
        k_s.reshape(1, nb, dec_len, KVH_B, HD_B),
        v_s.reshape(1, nb, dec_len, KVH_B, HD_B),
        ssd_s.reshape(1, nb, H_C, P_C, N_C),
        conv_s.reshape(1, nb, CONV_W - 1, CONV_DIM),
    )
```

```python
import math
import types

import numpy as np
import jax
import jax.numpy as jnp
from jax import lax
from jax.experimental import pallas as pl
from jax.experimental.pallas import tpu as pltpu

F32 = jnp.float32
BF16 = jnp.bfloat16

D_MODEL = 1024
CHUNK = 64
EPS = 1e-6
H_A = 4
DK_A = 64
DV_A = 128
GATE_RANK_A = 16
GATE_NORM_A = 16.0
H_B = 8
KVH_B = 2
G_B = H_B // KVH_B
HD_B = 64
WINDOW = 128
N_BUCKETS = 32
MAX_DISTANCE = 128
D_INNER_C = 2048
P_C = 64
H_C = 32
G_C = 4
HPG_C = 8
N_C = 128
CONV_W = 4
CONV_DIM = D_INNER_C + 2 * G_C * N_C

LOG2E = 1.4426950408889634
LANES = 128

A_Q, A_K, A_V, A_Z = 0, 256, 512, 1024
B_Q, B_K, B_V, B_Z = 1536, 2048, 2176, 2304
A_G = 2816
AB_COLS = 2944
C_Z, C_X, C_B, C_C, C_DT = 0, 2048, 4096, 4608, 5120
C_COLS = 5248

PROMPT_BLOCK = 256
AB_PROMPT_BLOCK = 512
VMEM_LIMIT = 60 * 1024 * 1024


def _dot(a, b):
    return jnp.dot(a.astype(BF16), b.astype(BF16), preferred_element_type=F32)


def _dot_nt(a, b):
    return lax.dot_general(a.astype(BF16), b.astype(BF16), (((1,), (1,)), ((), ())),
                           preferred_element_type=F32)


def _dot_tn(a, b):
    return lax.dot_general(a.astype(BF16), b.astype(BF16), (((0,), (0,)), ((), ())),
                           preferred_element_type=F32)


def _split_bf16(x, terms):
    out = []
    r = x
    for _ in range(terms):
        h = r.astype(BF16)
        out.append(h)
        r = r - h.astype(F32)
    return out


def _sel_dot(x, sel, terms):
    acc = None
    for h in _split_bf16(x, terms):
        d = jnp.dot(h, sel, preferred_element_type=F32)
        acc = d if acc is None else acc + d
    return acc


def _sel_dot_left(sel, x, terms):
    acc = None
    for h in _split_bf16(x, terms):
        d = jnp.dot(sel, h, preferred_element_type=F32)
        acc = d if acc is None else acc + d
    return acc


def _sel_dot_tn(x, sel, terms):
    acc = None
    for h in _split_bf16(x, terms):
        d = lax.dot_general(h, sel, (((0,), (0,)), ((), ())), preferred_element_type=F32)
        acc = d if acc is None else acc + d
    return acc


def _rms_rows(x):
    return x * lax.rsqrt(jnp.mean(x * x, axis=-1, keepdims=True) + EPS)


def _silu(x):
    return x * (1.0 / (1.0 + jnp.exp(-x)))


def _softplus(x):
    return jnp.maximum(x, 0.0) + jnp.log(1.0 + jnp.exp(-jnp.abs(x)))


def _log_sigmoid(x):
    return jnp.minimum(x, 0.0) - jnp.log(1.0 + jnp.exp(-jnp.abs(x)))


def _tril(n):
    r = lax.broadcasted_iota(jnp.int32, (n, n), 0)
    c = lax.broadcasted_iota(jnp.int32, (n, n), 1)
    return r >= c


def _ab_dense_in(x, rms_g, w_in_ref, gsum512, gsum128, gq, gk, proj_ref, qn_ref):
    h = (_rms_rows(x) * rms_g).astype(BF16)
    for lo in range(0, AB_COLS, 256):
        hi = min(lo + 256, AB_COLS)
        proj_ref[:, lo:hi] = jnp.dot(h, w_in_ref[:, lo:hi], preferred_element_type=F32)
    qb = proj_ref[:, B_Q:B_Q + 512]
    msq = _sel_dot(qb * qb, gsum512, 2) * (1.0 / HD_B)
    qn_ref[...] = qb * lax.rsqrt(msq + EPS) * gq
    kb = proj_ref[:, B_K:B_K + 128]
    msk = _sel_dot(kb * kb, gsum128, 2) * (1.0 / HD_B)
    kn = kb * lax.rsqrt(msk + EPS) * gk
    vb = proj_ref[:, B_V:B_V + 128]
    return kn, vb


def _advance(gens, yielded=None):
    alive = []
    for gen in gens:
        try:
            value = next(gen)
            alive.append(gen)
            if yielded is not None and value is not None:
                yielded.append(value)
        except StopIteration:
            pass
    return alive


def _interleave(gens):
    gens = list(gens)
    while gens:
        gens = _advance(gens)


def _ab_chunk(r, c, kb, proj_ref, qn_ref, o_ref, state, k_band, v_band, bias_ref, sink_ref,
              w_gu, b_gate, g_out, first_valid_col):
    rows = pl.ds(r, c)
    gate = _dot(proj_ref[rows, A_G:A_G + LANES], w_gu) + b_gate
    yield
    g = _log_sigmoid(gate) * (1.0 / GATE_NORM_A)
    tril = _tril(c)
    b = _sel_dot_left(tril.astype(BF16), g, 3)
    bl = b[c - 1:c, :]
    bl_tile = jnp.broadcast_to(bl, (LANES, H_A * DK_A))
    bl_rows = jnp.concatenate([bl_tile[:, i * LANES:(i + 1) * LANES].T
                               for i in range(H_A * DK_A // LANES)], axis=0)
    yield
    q = proj_ref[rows, A_Q:A_Q + 256] * (DK_A ** -0.5)
    k = proj_ref[rows, A_K:A_K + 256]
    v = proj_ref[rows, A_V:A_V + 512].astype(BF16)
    qe = q * jnp.exp(b)
    kd = (k * jnp.exp(-b)).astype(BF16)
    kd2 = (k * jnp.exp(bl - b)).astype(BF16)
    lane128 = lax.broadcasted_iota(jnp.int32, (c, LANES), 1)
    row128 = lax.broadcasted_iota(jnp.int32, (LANES, LANES), 0)
    qm, att = [], []
    for hd in range(H_A):
        p, j = divmod(hd, 2)
        qm.append(jnp.where((lane128 // DK_A) == j, qe[:, p * 128:(p + 1) * 128], 0.0).astype(BF16))
        att.append(_dot_nt(qm[hd], kd[:, p * 128:(p + 1) * 128]))
    upd = [_dot_tn(kd2[:, p * 128:(p + 1) * 128], v[:, p * 256:(p + 1) * 256]) for p in range(2)]
    yield
    oh = [_dot(jnp.where(tril, att[hd], 0.0), v[:, hd * 128:(hd + 1) * 128]) for hd in range(H_A)]
    yield
    s_prev = state[0]
    s_new = []
    for p in range(2):
        sp = s_prev[p * 128:(p + 1) * 128, :]
        sp_bf = sp.astype(BF16)
        for j in range(2):
            oh[2 * p + j] = oh[2 * p + j] + _dot(qm[2 * p + j], sp_bf)
        u = jnp.where(row128 < DK_A, upd[p][:, :128], upd[p][:, 128:])
        s_new.append(jnp.exp(bl_rows[p * 128:(p + 1) * 128, :]) * sp + u)
    state[0] = jnp.concatenate(s_new, axis=0)
    yield
    for hd in range(H_A):
        z = proj_ref[rows, A_Z + hd * 128:A_Z + (hd + 1) * 128]
        o_ref[rows, hd * 128:(hd + 1) * 128] = (_rms_rows(oh[hd]) * g_out * _silu(z)).astype(BF16)
    yield
    lane256 = lax.broadcasted_iota(jnp.int32, (c, 256), 1) // HD_B
    srow = lax.broadcasted_iota(jnp.int32, (G_B * c, 1), 0) // c
    scores = []
    for kvh in range(KVH_B):
        qn = qn_ref[rows, kvh * 256:(kvh + 1) * 256]
        qs = jnp.concatenate([jnp.where(lane256 == gq_, qn, 0.0) for gq_ in range(G_B)], axis=0)
        scores.append(_dot_nt(qs, k_band(kvh)))
    yield
    probs, dens = [], []
    for kvh in range(KVH_B):
        s = scores[kvh] * (HD_B ** -0.5) + bias_ref[kvh]
        if first_valid_col is not None:
            col = lax.broadcasted_iota(jnp.int32, (G_B * c, kb), 1)
            s = jnp.where(col >= first_valid_col, s, -jnp.inf)
        sink = jnp.zeros((G_B * c, 1), F32)
        for gq_ in range(G_B):
            sink = jnp.where(srow == gq_, sink_ref[kvh * G_B + gq_], sink)
        m = jnp.maximum(jnp.max(s, axis=-1, keepdims=True), sink)
        pr = jnp.exp(s - m)
        dens.append(jnp.sum(pr, axis=-1, keepdims=True) + jnp.exp(sink - m))
        probs.append(pr.astype(BF16))
    yield
    outs = [_dot(probs[kvh], v_band(kvh)) for kvh in range(KVH_B)]
    yield
    for kvh in range(KVH_B):
        ost = outs[kvh] / dens[kvh]
        ob = jnp.zeros((c, 256), F32)
        for gq_ in range(G_B):
            ob = ob + jnp.where(lane256 == gq_, ost[gq_ * c:(gq_ + 1) * c, :], 0.0)
        z = proj_ref[rows, B_Z + kvh * 256:B_Z + (kvh + 1) * 256]
        o_ref[rows, 512 + kvh * 256:512 + (kvh + 1) * 256] = (ob * _silu(z)).astype(BF16)


def _ab_prompt_kernel(sink_ref, x_ref, rms_g_ref, w_in_ref, w_gu_ref, b_gate_ref, g_out_ref, gq_ref,
                      gk_ref, bias_ref, w_out_ref, gsum512_ref, gsum128_ref, tile_ref,
                      y_ref, s_out_ref, k_out_ref, v_out_ref,
                      proj_ref, qn_ref, o_ref, kband_ref, vband_ref, s_ref):
    t = x_ref.shape[0]
    c = CHUNK
    kb = WINDOW + c
    nchunk = t // c
    step = pl.program_id(0)

    @pl.when(step == 0)
    def _():
        s_ref[...] = jnp.zeros_like(s_ref)
        kband_ref[:, t:t + WINDOW, :] = jnp.zeros((KVH_B, WINDOW, 256), BF16)
        vband_ref[:, t:t + WINDOW, :] = jnp.zeros((KVH_B, WINDOW, 256), BF16)

    for kvh in range(KVH_B):
        kband_ref[kvh, 0:WINDOW, :] = kband_ref[kvh, t:t + WINDOW, :]
        vband_ref[kvh, 0:WINDOW, :] = vband_ref[kvh, t:t + WINDOW, :]

    x = x_ref[...]
    kn, vb = _ab_dense_in(x, rms_g_ref[...], w_in_ref, gsum512_ref[...], gsum128_ref[...],
                          gq_ref[...], gk_ref[...], proj_ref, qn_ref)
    k_out_ref[...] = kn[t - WINDOW:, :]
    v_out_ref[...] = vb[t - WINDOW:, :]
    for kvh in range(KVH_B):
        kband_ref[kvh, WINDOW:WINDOW + t, :] = jnp.dot(
            kn.astype(BF16), tile_ref[kvh], preferred_element_type=F32).astype(BF16)
        vband_ref[kvh, WINDOW:WINDOW + t, :] = jnp.dot(
            vb.astype(BF16), tile_ref[kvh], preferred_element_type=F32).astype(BF16)

    w_gu = w_gu_ref[...]
    b_gate = b_gate_ref[...]
    g_out = g_out_ref[...]

    state = [s_ref[...]]

    def chunk(i):
        r = i * c
        return _ab_chunk(r, c, kb, proj_ref, qn_ref, o_ref, state,
                         lambda kvh: kband_ref[kvh, pl.ds(r, kb), :],
                         lambda kvh: vband_ref[kvh, pl.ds(r, kb), :],
                         bias_ref, sink_ref, w_gu, b_gate, g_out, (2 - (step * nchunk + i)) * c)

    _interleave(chunk(i) for i in range(nchunk))
    s_ref[...] = state[0]
    y_ref[...] = x + jnp.dot(o_ref[...], w_out_ref[...], preferred_element_type=F32)
    s_out_ref[...] = s_ref[...]


def _ab_sample_kernel(sink_ref, x_ref, rms_g_ref, w_in_ref, w_gu_ref, b_gate_ref, g_out_ref, gq_ref,
                      gk_ref, bias_ref, w_out_ref, gsum512_ref, gsum128_ref, tile_ref,
                      s_in_ref, kc_ref, vc_ref,
                      y_ref, s_out_ref, k_out_ref, v_out_ref,
                      proj_ref, qn_ref, o_ref, kband_ref, vband_ref):
    nb, wc = kc_ref.shape[0], kc_ref.shape[1]
    t = x_ref.shape[0]
    c = t // nb
    kb = wc + c
    x = x_ref[...]
    kn, vb = _ab_dense_in(x, rms_g_ref[...], w_in_ref, gsum512_ref[...], gsum128_ref[...],
                          gq_ref[...], gk_ref[...], proj_ref, qn_ref)
    k_out_ref[...] = kn
    v_out_ref[...] = vb
    for kvh in range(KVH_B):
        kt = jnp.dot(kn.astype(BF16), tile_ref[kvh], preferred_element_type=F32).astype(BF16)
        vt = jnp.dot(vb.astype(BF16), tile_ref[kvh], preferred_element_type=F32).astype(BF16)
        for bi in range(nb):
            kband_ref[kvh, bi, 0:wc, :] = jnp.dot(
                kc_ref[bi].astype(BF16), tile_ref[kvh], preferred_element_type=F32).astype(BF16)
            vband_ref[kvh, bi, 0:wc, :] = jnp.dot(
                vc_ref[bi].astype(BF16), tile_ref[kvh], preferred_element_type=F32).astype(BF16)
            kband_ref[kvh, bi, wc:kb, :] = kt[bi * c:(bi + 1) * c, :]
            vband_ref[kvh, bi, wc:kb, :] = vt[bi * c:(bi + 1) * c, :]

    w_gu = w_gu_ref[...]
    b_gate = b_gate_ref[...]
    g_out = g_out_ref[...]

    states = [[s_in_ref[bi]] for bi in range(nb)]

    def seq(bi):
        return _ab_chunk(bi * c, c, kb, proj_ref, qn_ref, o_ref, states[bi],
                         lambda kvh: kband_ref[kvh, bi], lambda kvh: vband_ref[kvh, bi],
                         bias_ref, sink_ref, w_gu, b_gate, g_out, None)

    _interleave(seq(bi) for bi in range(nb))
    for bi in range(nb):
        s_out_ref[bi] = states[bi][0]
    y_ref[...] = x + jnp.dot(o_ref[...], w_out_ref[...], preferred_element_type=F32)


GROUP_W = D_INNER_C // G_C


def _c_chunk(r, c, io, state, e_s, e_p, diag, neg_mask, bd_mask, dskip, g_y):
    dtc = io.dt(r)
    acum = _sel_dot_left(_tril(c).astype(BF16), io.da(r), 3) * LOG2E
    yield acum
    lhs = jnp.concatenate([jnp.concatenate(_split_bf16(acum, 2), axis=1),
                           jnp.concatenate(_split_bf16(dtc, 2), axis=1)], axis=0)
    both_p = jnp.dot(lhs, e_p, preferred_element_type=F32)
    xa_p, dt_p = both_p[:c], both_p[c:]
    xa_s = xa_p if c == P_C else jnp.dot(lhs[:c], e_s, preferred_element_type=F32)
    yield dt_p
    a_row = jnp.sum(xa_s * diag, axis=0, keepdims=True)
    wmat = jnp.exp2((xa_s - a_row) + neg_mask)
    al_p = xa_p[c - 1:c, :]
    ea_p = jnp.exp2(xa_p)
    xs = io.x(r)
    xdt = xs * dt_p
    xdt_bf = xdt.astype(BF16)
    xw = (xdt * jnp.exp2(al_p - xa_p)).astype(BF16)
    dec = jnp.exp2(al_p)
    bg, cg, cb = [], [], []
    for g in range(G_C):
        bg.append(io.b(r, g).astype(BF16))
        cg.append(io.c(r, g).astype(BF16))
        cb.append(_dot_nt(cg[g], jnp.concatenate([bg[g]] * HPG_C, axis=0)))
    yield cb[-1]
    ys = []
    for g in range(G_C):
        mg = (cb[g] * wmat[:, g * HPG_C * c:(g + 1) * HPG_C * c]).astype(BF16)
        for j in range(2):
            xj = xdt_bf[:, g * GROUP_W + j * 256:g * GROUP_W + (j + 1) * 256]
            bd = jnp.concatenate([xj] * 4, axis=0) * bd_mask
            ys.append(jnp.dot(mg[:, j * 4 * c:(j + 1) * 4 * c], bd, preferred_element_type=F32))
    upd = [_dot_tn(bg[g], xw[:, g * GROUP_W:(g + 1) * GROUP_W]) for g in range(G_C)]
    yield upd[-1]
    st_prev = state[0]
    y_inter, st_new = [], []
    for g in range(G_C):
        sl = slice(g * GROUP_W, (g + 1) * GROUP_W)
        y_inter.append(_dot(cg[g], st_prev[:, sl]))
        st_new.append(st_prev[:, sl] * dec[:, sl] + upd[g])
    state[0] = jnp.concatenate(st_new, axis=1)
    yield st_new[-1]
    for g in range(G_C):
        sl = slice(g * GROUP_W, (g + 1) * GROUP_W)
        y = jnp.concatenate(ys[2 * g:2 * g + 2], axis=1) + y_inter[g] * ea_p[:, sl]
        y = y + dskip[:, sl] * xs[:, sl]
        y = y * _silu(io.z(r, g))
        io.put_o(r, g, (_rms_rows(y) * g_y[:, sl]).astype(BF16))


def _c_dt(dt_cols, dt_bias, a_log):
    dt = _softplus(dt_cols + dt_bias)
    return dt, dt * (-jnp.exp(a_log))


CONV_PITCH = PROMPT_BLOCK // 8 + 1
CONV_ROWS = 8 * CONV_PITCH
CONV_TILES = CONV_DIM // LANES
C_ROUNDS = 12
C_IN_UNITS = (4, 4, 4, 4, 4, 4, 4, 2, 4, 2, 2, 3)
C_OUT_ROUNDS = (3, 5, 8, 10)
assert len(C_IN_UNITS) == C_ROUNDS and sum(C_IN_UNITS) * LANES == C_COLS


def _conv_tile(ubuf_ref, act_ref, j, conv_w, conv_b):
    w = [jnp.broadcast_to(conv_w[i:i + 1, j * LANES:(j + 1) * LANES], (8, LANES)) for i in range(CONV_W)]
    b = jnp.broadcast_to(conv_b[:, j * LANES:(j + 1) * LANES], (8, LANES))
    for a in range(CONV_PITCH):
        acc = b
        for i in range(CONV_W):
            acc = acc + w[i] * ubuf_ref[j, pl.ds(8 - (CONV_W - 1) + i + a, 8, stride=CONV_PITCH), :]
        act_ref[j, pl.ds(a, 8, stride=CONV_PITCH), :] = _silu(acc)


def _c_prompt_kernel(xin_ref, xres_ref, rms_g_ref, w_in_ref, conv_w_ref, conv_b_ref, dt_bias_ref,
                     a_log_ref, dskip_ref, g_y_ref, w_out_ref, e_s_ref, diag_ref, neg_mask_ref, bd_mask_ref,
                     y_ref, st_out_ref, conv_out_ref,
                     proj_ref, ubuf_ref, act_ref, dt_ref, da_ref, o_ref, st_ref):
    t = xin_ref.shape[0]
    c = CHUNK
    s = pl.program_id(0)
    n_blocks = pl.num_programs(0) - 2

    @pl.when(s == 0)
    def _():
        proj_ref[1] = jnp.zeros(proj_ref.shape[1:], F32)
        dt_ref[1] = jnp.zeros(dt_ref.shape[1:], F32)
        da_ref[1] = jnp.zeros(da_ref.shape[1:], F32)
        o_ref[...] = jnp.zeros_like(o_ref)
        ubuf_ref[...] = jnp.zeros_like(ubuf_ref)

    @pl.when(s <= 1)
    def _():
        st_ref[...] = jnp.zeros_like(st_ref)
        ubuf_ref[:, CONV_ROWS:CONV_ROWS + 8, :] = jnp.zeros((CONV_TILES, 8, LANES), F32)

    def in_stage(slot_in):
        h = (_rms_rows(xin_ref[...]) * rms_g_ref[...]).astype(BF16)
        lo = 0
        for units in C_IN_UNITS:
            if lo:
                yield
            hi = lo + units * LANES
            slab = jnp.dot(h, w_in_ref[:, lo:hi], preferred_element_type=F32)
            proj_ref[slot_in, :, lo:hi] = slab
            if hi == C_COLS:
                dt, da = _c_dt(slab[:, C_DT - lo:], dt_bias_ref[...], a_log_ref[...])
                dt_ref[slot_in] = dt
                da_ref[slot_in] = da
            lo = hi

    def act_rows(r, j):
        return act_ref[j, pl.ds(8 + r, c), :]

    n_x = D_INNER_C // LANES
    n_g = N_C // LANES

    def mix_stage(slot_mix):
        def put_o(r, g, value):
            o_ref[slot_mix, pl.ds(r, c), g * GROUP_W:(g + 1) * GROUP_W] = value

        io = types.SimpleNamespace(
            dt=lambda r: dt_ref[slot_mix, pl.ds(r, c), :],
            da=lambda r: da_ref[slot_mix, pl.ds(r, c), :],
            x=lambda r: jnp.concatenate([act_rows(r, j) for j in range(n_x)], axis=1),
            b=lambda r, g: act_rows(r, n_x + g * n_g),
            c=lambda r, g: act_rows(r, n_x + (G_C + g) * n_g),
            z=lambda r, g: proj_ref[slot_mix, pl.ds(r, c), C_Z + g * GROUP_W:C_Z + (g + 1) * GROUP_W],
            put_o=put_o)
        conv_w = conv_w_ref[...]
        conv_b = conv_b_ref[...]
        conv_rounds = C_ROUNDS - 6
        per_round = CONV_TILES // conv_rounds
        for rnd in range(conv_rounds):
            if rnd:
                yield
            for j in range(rnd * per_round, (rnd + 1) * per_round):
                ubuf_ref[j, 8:16, :] = ubuf_ref[j, CONV_ROWS:CONV_ROWS + 8, :]
                ubuf_ref[j, 16:16 + t, :] = proj_ref[slot_mix, :, C_X + j * LANES:C_X + (j + 1) * LANES]
                _conv_tile(ubuf_ref, act_ref, j, conv_w, conv_b)
        state = [st_ref[...]]
        e_s = e_s_ref[...]
        chunks = [_c_chunk(i * c, c, io, state, e_s, e_s, diag_ref[...], neg_mask_ref[...], bd_mask_ref[...],
                           dskip_ref[...], g_y_ref[...]) for i in range(t // c)]
        while chunks:
            yield
            chunks = _advance(chunks)
        st_ref[...] = state[0]

    def out_stage(slot_in):
        quarter = 0
        for rnd in range(C_ROUNDS):
            if rnd:
                yield
            if rnd in C_OUT_ROUNDS:
                cols = slice(quarter * 256, (quarter + 1) * 256)
                y_ref[:, cols] = xres_ref[:, cols] + jnp.dot(o_ref[slot_in], w_out_ref[:, cols],
                                                             preferred_element_type=F32)
                quarter += 1

    slot_in = s % 2
    _interleave([in_stage(slot_in), mix_stage(1 - slot_in), out_stage(slot_in)])

    @pl.when(s == n_blocks)
    def _():
        st_out_ref[...] = st_ref[...].T
        conv_out_ref[...] = proj_ref[1 - s % 2, t - 8:t, C_X:C_X + CONV_DIM]


def _conv_rows(ubuf, nrows, conv_w, conv_b):
    acc = conv_b
    for i in range(CONV_W):
        acc = acc + conv_w[i:i + 1, :] * ubuf[pl.ds(8 - (CONV_W - 1) + i, nrows), :]
    return _silu(acc)


def _c_sample_kernel(x_ref, rms_g_ref, w_in_ref, conv_w_ref, conv_b_ref, dt_bias_ref, a_log_ref,
                     dskip_ref, g_y_ref, w_out_ref, e_s_ref, e_p_ref, diag_ref, neg_mask_ref, bd_mask_ref,
                     st_in_ref, conv_in_ref,
                     y_ref, st_out_ref, conv_out_ref,
                     proj_ref, ubuf_ref, act_ref, dt_ref, da_ref, o_ref):
    nb = st_in_ref.shape[0]
    t = x_ref.shape[0]
    c = t // nb
    x = x_ref[...]
    h = (_rms_rows(x) * rms_g_ref[...]).astype(BF16)
    proj_ref[...] = jnp.dot(h, w_in_ref[...], preferred_element_type=F32)
    dt, da = _c_dt(proj_ref[:, C_DT:C_DT + LANES], dt_bias_ref[...], a_log_ref[...])
    dt_ref[...] = dt
    da_ref[...] = da
    conv_w = conv_w_ref[...]
    conv_b = conv_b_ref[...]
    ubuf_ref[0:8, :] = jnp.zeros((8, CONV_DIM), F32)
    for bi in range(nb):
        ubuf_ref[8 - (CONV_W - 1):8, :] = conv_in_ref[bi]
        ubuf_ref[8:8 + c, :] = proj_ref[bi * c:(bi + 1) * c, C_X:C_X + CONV_DIM]
        act_ref[bi * c:(bi + 1) * c, :] = _conv_rows(ubuf_ref, c, conv_w, conv_b)
        conv_out_ref[bi] = ubuf_ref[8 + c - (CONV_W - 1):8 + c, :]

    def put_o(r, g, value):
        o_ref[pl.ds(r, c), g * GROUP_W:(g + 1) * GROUP_W] = value

    io = types.SimpleNamespace(
        dt=lambda r: dt_ref[pl.ds(r, c), :],
        da=lambda r: da_ref[pl.ds(r, c), :],
        x=lambda r: act_ref[pl.ds(r, c), 0:D_INNER_C],
        b=lambda r, g: act_ref[pl.ds(r, c), D_INNER_C + g * N_C:D_INNER_C + (g + 1) * N_C],
        c=lambda r, g: act_ref[pl.ds(r, c), D_INNER_C + (G_C + g) * N_C:D_INNER_C + (G_C + g + 1) * N_C],
        z=lambda r, g: proj_ref[pl.ds(r, c), C_Z + g * GROUP_W:C_Z + (g + 1) * GROUP_W],
        put_o=put_o)
    e_s = e_s_ref[...]
    e_p = e_p_ref[...]
    diag = diag_ref[...]
    neg_mask = neg_mask_ref[...]
    bd_mask = bd_mask_ref[...]
    dskip = dskip_ref[...]
    g_y = g_y_ref[...]

    states = [[st_in_ref[bi].T] for bi in range(nb)]
    _interleave(_c_chunk(bi * c, c, io, states[bi], e_s, e_p, diag, neg_mask, bd_mask, dskip, g_y)
                for bi in range(nb))
    for bi in range(nb):
        st_out_ref[bi] = states[bi][0].T
    y_ref[...] = x + jnp.dot(o_ref[...], w_out_ref[...], preferred_element_type=F32)


def _bucket_table(q_off, k_off):
    n = q_off[:, None] - k_off[None, :]
    half = N_BUCKETS // 2
    max_exact = half // 2
    side = np.where(n < 0, half, 0)
    n = np.abs(n)
    nf = np.maximum(n, max_exact).astype(np.float32)
    large = max_exact + (np.log(nf / np.float32(max_exact)) / np.float32(math.log(MAX_DISTANCE / max_exact))
                         * np.float32(half - max_exact)).astype(np.int32)
    large = np.minimum(large, half - 1)
    return side + np.where(n < max_exact, n, large)


def _bias_rows(rel_bias, q_off, k_off):
    bucket = _bucket_table(q_off, k_off)
    onehot = jnp.asarray(np.eye(N_BUCKETS, dtype=np.float32)[bucket])
    bias = jnp.einsum('qkb,bh->hqk', onehot, rel_bias.astype(F32), precision=lax.Precision.HIGHEST)
    return bias.reshape(KVH_B, G_B * q_off.shape[0], k_off.shape[0])


def _group_sum_matrix(width, group):
    idx = np.arange(width) // group
    return jnp.asarray(idx[:, None] == idx[None, :], BF16)


def _kv_tile_matrices():
    m = np.zeros((KVH_B, KVH_B * HD_B, G_B * HD_B), np.float32)
    for kvh in range(KVH_B):
        for gq in range(G_B):
            m[kvh, kvh * HD_B + np.arange(HD_B), gq * HD_B + np.arange(HD_B)] = 1.0
    return jnp.asarray(m, BF16)


def _head_expand_matrix(per_head):
    m = np.zeros((LANES, H_C * per_head), np.float32)
    for h in range(H_C):
        m[h, h * per_head:(h + 1) * per_head] = 1.0
    return jnp.asarray(np.concatenate([m, m], axis=0), BF16)


def _full(shape):
    return pl.BlockSpec(shape, lambda *_: (0,) * len(shape))


def _const(shape):
    return pl.BlockSpec(shape, lambda *_: (0,) * len(shape), pipeline_mode=pl.Buffered(1))


CAST_ROWS = 256


def _cast_columns(wt_ref, w_bf_ref, src, dst, n):
    for off in range(0, n - n % LANES, CAST_ROWS):
        rows = min(CAST_ROWS, n - n % LANES - off)
        w_bf_ref[:, dst + off:dst + off + rows] = wt_ref[src + off:src + off + rows, :].T.astype(BF16)
    rest = n % LANES
    if rest:
        off = n - rest
        tail = jnp.concatenate([wt_ref[src + off:src + n, :], jnp.zeros((LANES - rest, wt_ref.shape[1]), F32)],
                               axis=0)
        w_bf_ref[:, dst + off:dst + off + LANES] = tail.T.astype(BF16)


def _ab_cast_kernel(wt_ref, w_out_ref, w_in_bf_ref, w_out_bf_ref):
    g0 = A_Z
    _cast_columns(wt_ref, w_in_bf_ref, 0, 0, g0)
    _cast_columns(wt_ref, w_in_bf_ref, g0 + GATE_RANK_A, g0, A_G - g0)
    _cast_columns(wt_ref, w_in_bf_ref, g0, A_G, GATE_RANK_A)
    w_out_bf_ref[...] = w_out_ref[...].astype(BF16)


def _c_cast_kernel(wt_ref, w_out_ref, w_in_bf_ref, w_out_bf_ref):
    _cast_columns(wt_ref, w_in_bf_ref, 0, 0, wt_ref.shape[0])
    w_out_bf_ref[...] = w_out_ref[...].astype(BF16)


def _cast_weights(body, w_in, w_out, cols, name):
    vm = pl.BlockSpec(memory_space=pltpu.VMEM)
    return pl.pallas_call(
        body, in_specs=[vm, vm], out_specs=(vm, vm),
        out_shape=(jax.ShapeDtypeStruct((w_in.shape[0], cols), BF16),
                   jax.ShapeDtypeStruct(w_out.shape, BF16)),
        name=name, compiler_params=pltpu.CompilerParams(vmem_limit_bytes=VMEM_LIMIT),
    )(w_in.T, w_out)


def _ab_weights(rms_g, w_in, w_gate_up, b_gate, g_out, g_q, g_k, w_out):
    w_in_r, w_out_bf = _cast_weights(_ab_cast_kernel, w_in, w_out, AB_COLS, "ab_cast")
    w_gu = jnp.concatenate([w_gate_up, jnp.zeros((LANES - GATE_RANK_A, H_A * DK_A), w_gate_up.dtype)],
                           axis=0).astype(BF16)
    return (rms_g.reshape(1, D_MODEL), w_in_r, w_gu, b_gate.reshape(1, -1), g_out.reshape(1, DV_A),
            jnp.tile(g_q, H_B).reshape(1, -1), jnp.tile(g_k, KVH_B).reshape(1, -1), w_out_bf)


def _ab_prompt(x, sink, rel_bias, weights):
    rms_g, w_in, w_gu, b_gate, g_out, gq, gk, w_out = weights
    length = x.shape[0]
    t = AB_PROMPT_BLOCK
    c = CHUNK
    kb = WINDOW + c
    bias = _bias_rows(rel_bias, np.arange(c), np.arange(kb) - WINDOW)
    row_blk = lambda i: (i, 0)
    in_specs = [
        pl.BlockSpec(memory_space=pltpu.SMEM),
        pl.BlockSpec((t, D_MODEL), row_blk),
        _const((1, D_MODEL)), _const((D_MODEL, AB_COLS)), _const((LANES, 256)), _const((1, 256)),
        _const((1, DV_A)), _const((1, 512)), _const((1, 128)), _const((KVH_B, G_B * c, kb)),
        _const((D_MODEL, D_MODEL)), _const((512, 512)), _const((128, 128)), _const((KVH_B, 128, 256)),
    ]
    out_shape = (jax.ShapeDtypeStruct((length, D_MODEL), F32),
                 jax.ShapeDtypeStruct((H_A * DK_A, DV_A), F32),
                 jax.ShapeDtypeStruct((WINDOW, KVH_B * HD_B), F32),
                 jax.ShapeDtypeStruct((WINDOW, KVH_B * HD_B), F32))
    out_specs = (pl.BlockSpec((t, D_MODEL), row_blk), _full((H_A * DK_A, DV_A)),
                 _full((WINDOW, KVH_B * HD_B)), _full((WINDOW, KVH_B * HD_B)))
    scratch = [pltpu.VMEM((t, AB_COLS), F32), pltpu.VMEM((t, 512), F32), pltpu.VMEM((t, D_MODEL), BF16),
               pltpu.VMEM((KVH_B, t + WINDOW, 256), BF16), pltpu.VMEM((KVH_B, t + WINDOW, 256), BF16),
               pltpu.VMEM((H_A * DK_A, DV_A), F32)]
    return pl.pallas_call(
        _ab_prompt_kernel, grid=(length // t,), in_specs=in_specs, out_specs=out_specs,
        out_shape=out_shape, scratch_shapes=scratch, name="ab_prompt",
        compiler_params=pltpu.CompilerParams(dimension_semantics=("arbitrary",),
                                             vmem_limit_bytes=VMEM_LIMIT),
    )(sink, x, rms_g, w_in, w_gu, b_gate, g_out, gq, gk, bias, w_out,
      _group_sum_matrix(512, HD_B), _group_sum_matrix(128, HD_B), _kv_tile_matrices())


def _ab_sample(x, sink, rel_bias, weights, s_in, k_cache, v_cache):
    rms_g, w_in, w_gu, b_gate, g_out, gq, gk, w_out = weights
    nb, c, _ = x.shape
    wc = k_cache.shape[1]
    kb = wc + c
    t = nb * c
    bias = _bias_rows(rel_bias, np.arange(c), np.arange(kb) - wc)
    vm = pl.BlockSpec(memory_space=pltpu.VMEM)
    in_specs = [pl.BlockSpec(memory_space=pltpu.SMEM)] + [vm] * 16
    out_shape = (jax.ShapeDtypeStruct((t, D_MODEL), F32),
                 jax.ShapeDtypeStruct((nb, H_A * DK_A, DV_A), F32),
                 jax.ShapeDtypeStruct((t, KVH_B * HD_B), F32),
                 jax.ShapeDtypeStruct((t, KVH_B * HD_B), F32))
    scratch = [pltpu.VMEM((t, AB_COLS), F32), pltpu.VMEM((t, 512), F32), pltpu.VMEM((t, D_MODEL), BF16),
               pltpu.VMEM((KVH_B, nb, kb, 256), BF16), pltpu.VMEM((KVH_B, nb, kb, 256), BF16)]
    return pl.pallas_call(
        _ab_sample_kernel, in_specs=in_specs, out_specs=(vm,) * 4, out_shape=out_shape,
        scratch_shapes=scratch, name="ab_sample",
        compiler_params=pltpu.CompilerParams(vmem_limit_bytes=VMEM_LIMIT),
    )(sink, x.reshape(t, D_MODEL), rms_g, w_in, w_gu, b_gate, g_out, gq, gk, bias, w_out,
      _group_sum_matrix(512, HD_B), _group_sum_matrix(128, HD_B), _kv_tile_matrices(),
      s_in.reshape(nb, H_A * DK_A, DV_A), k_cache.reshape(nb, wc, KVH_B * HD_B),
      v_cache.reshape(nb, wc, KVH_B * HD_B))


def _c_weights(rms_g, w_in, conv_w, conv_b, dt_bias, a_log, d_skip, g_y, w_out):
    w_in_p, w_out_bf = _cast_weights(_c_cast_kernel, w_in, w_out, C_COLS, "c_cast")
    pad = lambda v: jnp.concatenate([v, jnp.zeros((LANES - H_C,), v.dtype)]).reshape(1, LANES)
    return (rms_g.reshape(1, D_MODEL), w_in_p, conv_w, conv_b.reshape(1, CONV_DIM), pad(dt_bias), pad(a_log),
            jnp.repeat(d_skip, P_C).reshape(1, D_INNER_C), g_y.reshape(1, D_INNER_C), w_out_bf)


def _ssd_masks(c):
    tok = np.arange(c)[:, None]
    src = np.arange(H_C * c)[None, :] % c
    diag = (tok == src).astype(np.float32)
    neg = np.where(src <= tok, 0.0, -np.inf).astype(np.float32)
    blk = (np.arange(4 * c)[:, None] // c) == (np.arange(256)[None, :] // P_C)
    return jnp.asarray(diag), jnp.asarray(neg), jnp.asarray(blk, BF16)


def _c_prompt(x, weights):
    length = x.shape[0]
    t = PROMPT_BLOCK
    c = CHUNK
    n_blocks = length // t
    in_blk = lambda s: (jnp.minimum(s, n_blocks - 1), 0)
    out_blk = lambda s: (jnp.clip(s - 2, 0, n_blocks - 1), 0)
    in_specs = [
        pl.BlockSpec((t, D_MODEL), in_blk), pl.BlockSpec((t, D_MODEL), out_blk),
        _const((1, D_MODEL)), _const((D_MODEL, C_COLS)), _const((CONV_W, CONV_DIM)), _const((1, CONV_DIM)),
        _const((1, LANES)), _const((1, LANES)), _const((1, D_INNER_C)), _const((1, D_INNER_C)),
        _const((D_INNER_C, D_MODEL)), _const((2 * LANES, H_C * c)),
        _const((c, H_C * c)), _const((c, H_C * c)), _const((4 * c, 256)),
    ]
    out_shape = (jax.ShapeDtypeStruct((length, D_MODEL), F32),
                 jax.ShapeDtypeStruct((H_C * P_C, N_C), F32),
                 jax.ShapeDtypeStruct((8, CONV_DIM), F32))
    out_specs = (pl.BlockSpec((t, D_MODEL), out_blk), _full((H_C * P_C, N_C)), _full((8, CONV_DIM)))
    scratch = [pltpu.VMEM((2, t, C_COLS), F32), pltpu.VMEM((CONV_TILES, CONV_ROWS + 8, LANES), F32),
               pltpu.VMEM((CONV_TILES, CONV_ROWS, LANES), F32), pltpu.VMEM((2, t, LANES), F32),
               pltpu.VMEM((2, t, LANES), F32), pltpu.VMEM((2, t, D_INNER_C), BF16),
               pltpu.VMEM((N_C, H_C * P_C), F32)]
    return pl.pallas_call(
        _c_prompt_kernel, grid=(n_blocks + 2,), in_specs=in_specs, out_specs=out_specs,
        out_shape=out_shape, scratch_shapes=scratch, name="c_prompt",
        compiler_params=pltpu.CompilerParams(dimension_semantics=("arbitrary",),
                                             vmem_limit_bytes=VMEM_LIMIT),
    )(x, x, *weights, _head_expand_matrix(c), *_ssd_masks(c))


def _c_sample(x, weights, st_in, conv_in):
    nb, c, _ = x.shape
    t = nb * c
    vm = pl.BlockSpec(memory_space=pltpu.VMEM)
    out_shape = (jax.ShapeDtypeStruct((t, D_MODEL), F32),
                 jax.ShapeDtypeStruct((nb, H_C * P_C, N_C), F32),
                 jax.ShapeDtypeStruct((nb, CONV_W - 1, CONV_DIM), F32))
    scratch = [pltpu.VMEM((t, C_COLS), F32), pltpu.VMEM((c + 8, CONV_DIM), F32),
               pltpu.VMEM((t, CONV_DIM), F32), pltpu.VMEM((t, LANES), F32), pltpu.VMEM((t, LANES), F32),
               pltpu.VMEM((t, D_INNER_C), BF16)]
    return pl.pallas_call(
        _c_sample_kernel, in_specs=[vm] * 17, out_specs=(vm,) * 3, out_shape=out_shape,
        scratch_shapes=scratch, name="c_sample",
        compiler_params=pltpu.CompilerParams(vmem_limit_bytes=VMEM_LIMIT),
    )(x.reshape(t, D_MODEL), *weights, _head_expand_matrix(c), _head_expand_matrix(P_C), *_ssd_masks(c),
      st_in.reshape(nb, H_C * P_C, N_C), conv_in)


def kernel(x_prompt, x_sample, cache_swa_k, cache_swa_v, state_gla, state_ssd, state_conv, rms_g, w_in_ab, w_gate_up_a, b_gate_a, g_out_a, g_q_b, g_k_b, sink_b, rel_bias, w_out_ab, w_in_c, conv_w_c, conv_b_c, dt_bias_c, a_log_c, d_skip_c, g_y_c, w_out_c):
    bp, seq_len, _ = x_prompt.shape
    nb, dec_len, _ = x_sample.shape
    assert bp == 1 and seq_len % PROMPT_BLOCK == 0 and seq_len % AB_PROMPT_BLOCK == 0 and seq_len >= WINDOW
    wab = _ab_weights(rms_g[0], w_in_ab[0], w_gate_up_a[0], b_gate_a[0], g_out_a[0], g_q_b[0], g_k_b[0],
                      w_out_ab[0])
    yp, gla_p, k_p, v_p = _ab_prompt(x_prompt[0], sink_b[0], rel_bias, wab)
    ys, gla_s, k_s, v_s = _ab_sample(x_sample, sink_b[0], rel_bias, wab, state_gla[0], cache_swa_k[0],
                                     cache_swa_v[0])
    wc = _c_weights(rms_g[1], w_in_c[0], conv_w_c[0], conv_b_c[0], dt_bias_c[0], a_log_c[0], d_skip_c[0],
                    g_y_c[0], w_out_c[0])
    yp, ssd_p, conv_p = _c_prompt(yp, wc)
    ys, ssd_s, conv_s = _c_sample(ys.reshape(nb, dec_len, D_MODEL), wc, state_ssd[0], state_conv[0])
    return (
        yp.reshape(1, seq_len, D_MODEL),
        ys.reshape(nb, dec_len, D_MODEL),
        gla_p.reshape(1, 1, H_A, DK_A, DV_A),
        k_p.reshape(1, 1, WINDOW, KVH_B, HD_B),
        v_p.reshape(1, 1, WINDOW, KVH_B, HD_B),
        ssd_p.reshape(1, 1, H_C, P_C, N_C),
        conv_p[8 - (CONV_W - 1):].reshape(1, 1, CONV_W - 1, CONV_DIM),
        gla_s.reshape(1, nb, H_A, DK_A, DV_A),
        k_s.reshape(1, nb, dec_len, KVH_B, HD_B),
        v_s.reshape(1, nb, dec_len, KVH_B, HD_B),
        ssd_s.reshape(1, nb, H_C, P_C, N_C),
        conv_s.reshape(1, nb, CONV_W - 1, CONV_DIM),
    )
```

```python
import math
import types

import numpy as np
import jax
import jax.numpy as jnp
from jax import lax
from jax.experimental import pallas as pl
from jax.experimental.pallas import tpu as pltpu

F32 = jnp.float32
BF16 = jnp.bfloat16

D_MODEL = 1024
CHUNK = 64
EPS = 1e-6
H_A = 4
DK_A = 64
DV_A = 128
GATE_RANK_A = 16
GATE_NORM_A = 16.0
H_B = 8
KVH_B = 2
G_B = H_B // KVH_B
HD_B = 64
WINDOW = 128
N_BUCKETS = 32
MAX_DISTANCE = 128
D_INNER_C = 2048
P_C = 64
H_C = 32
G_C = 4
HPG_C = 8
N_C = 128
CONV_W = 4
CONV_DIM = D_INNER_C + 2 * G_C * N_C

LOG2E = 1.4426950408889634
LANES = 128

A_Q, A_K, A_V, A_Z = 0, 256, 512, 1024
B_Q, B_K, B_V, B_Z = 1536, 2048, 2176, 2304
A_G = 2816
AB_COLS = 2944
C_Z, C_X, C_B, C_C, C_DT = 0, 2048, 4096, 4608, 5120
C_COLS = 5248

PROMPT_BLOCK = 256
AB_PROMPT_BLOCK = 512
VMEM_LIMIT = 60 * 1024 * 1024


def _dot(a, b):
    return jnp.dot(a.astype(BF16), b.astype(BF16), preferred_element_type=F32)


def _dot_nt(a, b):
    return lax.dot_general(a.astype(BF16), b.astype(BF16), (((1,), (1,)), ((), ())),
                           preferred_element_type=F32)


def _dot_tn(a, b):
    return lax.dot_general(a.astype(BF16), b.astype(BF16), (((0,), (0,)), ((), ())),
                           preferred_element_type=F32)


def _split_bf16(x, terms):
    out = []
    r = x
    for _ in range(terms):
        h = r.astype(BF16)
        out.append(h)
        r = r - h.astype(F32)
    return out


def _sel_dot(x, sel, terms):
    acc = None
    for h in _split_bf16(x, terms):
        d = jnp.dot(h, sel, preferred_element_type=F32)
        acc = d if acc is None else acc + d
    return acc


def _sel_dot_left(sel, x, terms):
    acc = None
    for h in _split_bf16(x, terms):
        d = jnp.dot(sel, h, preferred_element_type=F32)
        acc = d if acc is None else acc + d
    return acc


def _rms_rows(x):
    return x * lax.rsqrt(jnp.mean(x * x, axis=-1, keepdims=True) + EPS)


def _exp_neg(x):
    return jnp.exp2(x * (-LOG2E))


def _silu(x):
    return x * (1.0 / (1.0 + _exp_neg(x)))


def _softplus(x):
    return jnp.maximum(x, 0.0) + jnp.log(1.0 + _exp_neg(jnp.abs(x)))


def _log_sigmoid(x):
    return jnp.minimum(x, 0.0) - jnp.log(1.0 + _exp_neg(jnp.abs(x)))


def _tril(n):
    r = lax.broadcasted_iota(jnp.int32, (n, n), 0)
    c = lax.broadcasted_iota(jnp.int32, (n, n), 1)
    return r >= c


def _ab_dense_in(x, rms_g, w_in_ref, gsum512, gsum128, gq, gk, proj_ref, qn_ref):
    h = (_rms_rows(x) * rms_g).astype(BF16)
    for lo in range(0, AB_COLS, 256):
        hi = min(lo + 256, AB_COLS)
        proj_ref[:, lo:hi] = jnp.dot(h, w_in_ref[:, lo:hi], preferred_element_type=F32)
    qb = proj_ref[:, B_Q:B_Q + 512]
    msq = _sel_dot(qb * qb, gsum512, 2) * (1.0 / HD_B)
    qn_ref[...] = qb * lax.rsqrt(msq + EPS) * gq
    kb = proj_ref[:, B_K:B_K + 128]
    msk = _sel_dot(kb * kb, gsum128, 2) * (1.0 / HD_B)
    kn = kb * lax.rsqrt(msk + EPS) * gk
    vb = proj_ref[:, B_V:B_V + 128]
    return kn, vb


def _advance(gens, yielded=None):
    alive = []
    for gen in gens:
        try:
            value = next(gen)
            alive.append(gen)
            if yielded is not None and value is not None:
                yielded.append(value)
        except StopIteration:
            pass
    return alive


def _interleave(gens):
    gens = list(gens)
    while gens:
        gens = _advance(gens)


def _ab_chunk(r, c, kb, proj_ref, qn_ref, o_ref, state, k_band, v_band, bias_ref, sink_ref,
              w_gu, b_gate, g_out, first_valid_col):
    rows = pl.ds(r, c)
    gate = _dot(proj_ref[rows, A_G:A_G + LANES], w_gu) + b_gate
    yield
    g = _log_sigmoid(gate) * (LOG2E / GATE_NORM_A)
    tril = _tril(c)
    b = _sel_dot_left(tril.astype(BF16), g, 3)
    bl = b[c - 1:c, :]
    bl_tile = jnp.broadcast_to(bl, (LANES, H_A * DK_A))
    bl_rows = jnp.concatenate([bl_tile[:, i * LANES:(i + 1) * LANES].T
                               for i in range(H_A * DK_A // LANES)], axis=0)
    yield
    q = proj_ref[rows, A_Q:A_Q + 256] * (DK_A ** -0.5)
    k = proj_ref[rows, A_K:A_K + 256]
    v = proj_ref[rows, A_V:A_V + 512].astype(BF16)
    qe = q * jnp.exp2(b)
    kd = (k * jnp.exp2(-b)).astype(BF16)
    kd2 = (k * jnp.exp2(bl - b)).astype(BF16)
    lane128 = lax.broadcasted_iota(jnp.int32, (c, LANES), 1)
    row128 = lax.broadcasted_iota(jnp.int32, (LANES, LANES), 0)
    qm, att = [], []
    for hd in range(H_A):
        p, j = divmod(hd, 2)
        qm.append(jnp.where((lane128 // DK_A) == j, qe[:, p * 128:(p + 1) * 128], 0.0).astype(BF16))
        att.append(_dot_nt(qm[hd], kd[:, p * 128:(p + 1) * 128]))
    upd = [_dot_tn(kd2[:, p * 128:(p + 1) * 128], v[:, p * 256:(p + 1) * 256]) for p in range(2)]
    yield
    oh = [_dot(jnp.where(tril, att[hd], 0.0), v[:, hd * 128:(hd + 1) * 128]) for hd in range(H_A)]
    yield
    s_prev = state[0]
    s_new = []
    for p in range(2):
        sp = s_prev[p * 128:(p + 1) * 128, :]
        sp_bf = sp.astype(BF16)
        for j in range(2):
            oh[2 * p + j] = oh[2 * p + j] + _dot(qm[2 * p + j], sp_bf)
        u = jnp.where(row128 < DK_A, upd[p][:, :128], upd[p][:, 128:])
        s_new.append(jnp.exp2(bl_rows[p * 128:(p + 1) * 128, :]) * sp + u)
    state[0] = jnp.concatenate(s_new, axis=0)
    yield
    for hd in range(H_A):
        z = proj_ref[rows, A_Z + hd * 128:A_Z + (hd + 1) * 128]
        o_ref[rows, hd * 128:(hd + 1) * 128] = (_rms_rows(oh[hd]) * g_out * _silu(z)).astype(BF16)
    yield
    lane256 = lax.broadcasted_iota(jnp.int32, (c, 256), 1) // HD_B
    srow = lax.broadcasted_iota(jnp.int32, (G_B * c, 1), 0) // c
    scores = []
    for kvh in range(KVH_B):
        qn = qn_ref[rows, kvh * 256:(kvh + 1) * 256]
        qs = jnp.concatenate([jnp.where(lane256 == gq_, qn, 0.0) for gq_ in range(G_B)], axis=0)
        scores.append(_dot_nt(qs, k_band(kvh)))
    yield
    probs, dens = [], []
    for kvh in range(KVH_B):
        s = scores[kvh] * (LOG2E * HD_B ** -0.5) + bias_ref[kvh]
        if first_valid_col is not None:
            col = lax.broadcasted_iota(jnp.int32, (G_B * c, kb), 1)
            s = jnp.where(col >= first_valid_col, s, -jnp.inf)
        sink = jnp.zeros((G_B * c, 1), F32)
        for gq_ in range(G_B):
            sink = jnp.where(srow == gq_, sink_ref[kvh * G_B + gq_] * LOG2E, sink)
        m = jnp.maximum(jnp.max(s, axis=-1, keepdims=True), sink)
        pr = jnp.exp2(s - m)
        dens.append(jnp.sum(pr, axis=-1, keepdims=True) + jnp.exp2(sink - m))
        probs.append(pr.astype(BF16))
    yield
    outs = [_dot(probs[kvh], v_band(kvh)) for kvh in range(KVH_B)]
    yield
    for kvh in range(KVH_B):
        ost = outs[kvh] / dens[kvh]
        ob = jnp.zeros((c, 256), F32)
        for gq_ in range(G_B):
            ob = ob + jnp.where(lane256 == gq_, ost[gq_ * c:(gq_ + 1) * c, :], 0.0)
        z = proj_ref[rows, B_Z + kvh * 256:B_Z + (kvh + 1) * 256]
        o_ref[rows, 512 + kvh * 256:512 + (kvh + 1) * 256] = (ob * _silu(z)).astype(BF16)


def _ab_prompt_kernel(sink_ref, x_ref, rms_g_ref, w_in_ref, w_gu_ref, b_gate_ref, g_out_ref, gq_ref,
                      gk_ref, bias_ref, w_out_ref, gsum512_ref, gsum128_ref, tile_ref,
                      y_ref, s_out_ref, k_out_ref, v_out_ref,
                      proj_ref, qn_ref, o_ref, kband_ref, vband_ref, s_ref):
    t = x_ref.shape[0]
    c = CHUNK
    kb = WINDOW + c
    nchunk = t // c
    step = pl.program_id(0)

    @pl.when(step == 0)
    def _():
        s_ref[...] = jnp.zeros_like(s_ref)
        kband_ref[:, t:t + WINDOW, :] = jnp.zeros((KVH_B, WINDOW, 256), BF16)
        vband_ref[:, t:t + WINDOW, :] = jnp.zeros((KVH_B, WINDOW, 256), BF16)

    for kvh in range(KVH_B):
        kband_ref[kvh, 0:WINDOW, :] = kband_ref[kvh, t:t + WINDOW, :]
        vband_ref[kvh, 0:WINDOW, :] = vband_ref[kvh, t:t + WINDOW, :]

    x = x_ref[...]
    kn, vb = _ab_dense_in(x, rms_g_ref[...], w_in_ref, gsum512_ref[...], gsum128_ref[...],
                          gq_ref[...], gk_ref[...], proj_ref, qn_ref)
    k_out_ref[...] = kn[t - WINDOW:, :]
    v_out_ref[...] = vb[t - WINDOW:, :]
    for kvh in range(KVH_B):
        kband_ref[kvh, WINDOW:WINDOW + t, :] = jnp.dot(
            kn.astype(BF16), tile_ref[kvh], preferred_element_type=F32).astype(BF16)
        vband_ref[kvh, WINDOW:WINDOW + t, :] = jnp.dot(
            vb.astype(BF16), tile_ref[kvh], preferred_element_type=F32).astype(BF16)

    w_gu = w_gu_ref[...]
    b_gate = b_gate_ref[...]
    g_out = g_out_ref[...]

    state = [s_ref[...]]

    def chunk(i):
        r = i * c
        return _ab_chunk(r, c, kb, proj_ref, qn_ref, o_ref, state,
                         lambda kvh: kband_ref[kvh, pl.ds(r, kb), :],
                         lambda kvh: vband_ref[kvh, pl.ds(r, kb), :],
                         bias_ref, sink_ref, w_gu, b_gate, g_out, (2 - (step * nchunk + i)) * c)

    _interleave(chunk(i) for i in range(nchunk))
    s_ref[...] = state[0]
    y_ref[...] = x + jnp.dot(o_ref[...], w_out_ref[...], preferred_element_type=F32)
    s_out_ref[...] = s_ref[...]


def _ab_sample_kernel(sink_ref, x_ref, rms_g_ref, w_in_ref, w_gu_ref, b_gate_ref, g_out_ref, gq_ref,
                      gk_ref, bias_ref, w_out_ref, gsum512_ref, gsum128_ref, tile_ref,
                      s_in_ref, kc_ref, vc_ref,
                      y_ref, s_out_ref, k_out_ref, v_out_ref,
                      proj_ref, qn_ref, o_ref, kband_ref, vband_ref):
    nb, wc = kc_ref.shape[0], kc_ref.shape[1]
    t = x_ref.shape[0]
    c = t // nb
    kb = wc + c
    x = x_ref[...]
    kn, vb = _ab_dense_in(x, rms_g_ref[...], w_in_ref, gsum512_ref[...], gsum128_ref[...],
                          gq_ref[...], gk_ref[...], proj_ref, qn_ref)
    k_out_ref[...] = kn
    v_out_ref[...] = vb
    for kvh in range(KVH_B):
        kt = jnp.dot(kn.astype(BF16), tile_ref[kvh], preferred_element_type=F32).astype(BF16)
        vt = jnp.dot(vb.astype(BF16), tile_ref[kvh], preferred_element_type=F32).astype(BF16)
        for bi in range(nb):
            kband_ref[kvh, bi, 0:wc, :] = jnp.dot(
                kc_ref[bi].astype(BF16), tile_ref[kvh], preferred_element_type=F32).astype(BF16)
            vband_ref[kvh, bi, 0:wc, :] = jnp.dot(
                vc_ref[bi].astype(BF16), tile_ref[kvh], preferred_element_type=F32).astype(BF16)
            kband_ref[kvh, bi, wc:kb, :] = kt[bi * c:(bi + 1) * c, :]
            vband_ref[kvh, bi, wc:kb, :] = vt[bi * c:(bi + 1) * c, :]

    w_gu = w_gu_ref[...]
    b_gate = b_gate_ref[...]
    g_out = g_out_ref[...]

    states = [[s_in_ref[bi]] for bi in range(nb)]

    def seq(bi):
        return _ab_chunk(bi * c, c, kb, proj_ref, qn_ref, o_ref, states[bi],
                         lambda kvh: kband_ref[kvh, bi], lambda kvh: vband_ref[kvh, bi],
                         bias_ref, sink_ref, w_gu, b_gate, g_out, None)

    _interleave(seq(bi) for bi in range(nb))
    for bi in range(nb):
        s_out_ref[bi] = states[bi][0]
    y_ref[...] = x + jnp.dot(o_ref[...], w_out_ref[...], preferred_element_type=F32)


GROUP_W = D_INNER_C // G_C


def _c_chunk(r, c, io, state, e_s, e_p, neg_mask, bd_mask, dskip, g_y):
    dtc = io.dt(r)
    acum = _sel_dot_left(_tril(c).astype(BF16), io.da(r), 3) * LOG2E
    yield acum
    lhs = jnp.concatenate([jnp.concatenate(_split_bf16(acum, 2), axis=1),
                           jnp.concatenate(_split_bf16(dtc, 2), axis=1)], axis=0)
    both_p = jnp.dot(lhs, e_p, preferred_element_type=F32)
    xa_p, dt_p = both_p[:c], both_p[c:]
    xa_s = xa_p if c == P_C else jnp.dot(lhs[:c], e_s, preferred_element_type=F32)
    yield dt_p
    acum_t = acum.T
    a_row = jnp.concatenate([acum_t[h:h + 1, :] for h in range(H_C)], axis=1)
    wmat = jnp.exp2((xa_s - a_row) + neg_mask)
    al_p = xa_p[c - 1:c, :]
    xs = io.x(r)
    xdt = xs * dt_p
    xdt_bf = xdt.astype(BF16)
    xw = (xdt * jnp.exp2(al_p - xa_p)).astype(BF16)
    dec = jnp.exp2(al_p)
    bg, cg, cb = [], [], []
    for g in range(G_C):
        bg.append(io.b(r, g).astype(BF16))
        cg.append(io.c(r, g).astype(BF16))
        cb.append(_dot_nt(cg[g], jnp.concatenate([bg[g]] * HPG_C, axis=0)))
    yield cb[-1]
    ys = []
    for g in range(G_C):
        mg = (cb[g] * wmat[:, g * HPG_C * c:(g + 1) * HPG_C * c]).astype(BF16)
        for j in range(2):
            xj = xdt_bf[:, g * GROUP_W + j * 256:g * GROUP_W + (j + 1) * 256]
            bd = jnp.concatenate([xj] * 4, axis=0) * bd_mask
            ys.append(jnp.dot(mg[:, j * 4 * c:(j + 1) * 4 * c], bd, preferred_element_type=F32))
    upd = [_dot_tn(bg[g], xw[:, g * GROUP_W:(g + 1) * GROUP_W]) for g in range(G_C)]
    yield upd[-1]
    st_prev = state[0]
    y_inter, st_new = [], []
    for g in range(G_C):
        sl = slice(g * GROUP_W, (g + 1) * GROUP_W)
        y_inter.append(_dot(cg[g], st_prev[:, sl]))
        st_new.append(st_prev[:, sl] * dec[:, sl] + upd[g])
    state[0] = jnp.concatenate(st_new, axis=1)
    yield st_new[-1]
    for g in range(G_C):
        sl = slice(g * GROUP_W, (g + 1) * GROUP_W)
        y = jnp.concatenate(ys[2 * g:2 * g + 2], axis=1) + y_inter[g] * jnp.exp2(xa_p[:, sl])
        y = y + dskip[:, sl] * xs[:, sl]
        y = y * _silu(io.z(r, g))
        io.put_o(r, g, (_rms_rows(y) * g_y[:, sl]).astype(BF16))


def _c_dt(dt_cols, dt_bias, a_log):
    dt = _softplus(dt_cols + dt_bias)
    return dt, dt * (-jnp.exp(a_log))


CONV_PITCH = PROMPT_BLOCK // 8 + 1
CONV_ROWS = 8 * CONV_PITCH
CONV_TILES = CONV_DIM // LANES
C_CONV_ROUNDS = 6
C_ROUNDS = C_CONV_ROUNDS + 6
C_IN_UNITS = (4, 4, 4, 4, 4, 4, 4, 2, 4, 2, 2, 3)
C_OUT_ROUNDS = (3, 5, 8, 10)
assert len(C_IN_UNITS) == C_ROUNDS and sum(C_IN_UNITS) * LANES == C_COLS


def _conv_tile(ubuf_ref, act_ref, j, conv_w, conv_b):
    w = [jnp.broadcast_to(conv_w[i:i + 1, j * LANES:(j + 1) * LANES], (8, LANES)) for i in range(CONV_W)]
    b = jnp.broadcast_to(conv_b[:, j * LANES:(j + 1) * LANES], (8, LANES))
    for a in range(CONV_PITCH):
        acc = b
        for i in range(CONV_W):
            acc = acc + w[i] * ubuf_ref[j, pl.ds(8 - (CONV_W - 1) + i + a, 8, stride=CONV_PITCH), :]
        act_ref[j, pl.ds(a, 8, stride=CONV_PITCH), :] = _silu(acc)


def _c_prompt_kernel(xin_ref, xres_ref, rms_g_ref, w_in_ref, conv_w_ref, conv_b_ref, dt_bias_ref,
                     a_log_ref, dskip_ref, g_y_ref, w_out_ref, e_s_ref, neg_mask_ref, bd_mask_ref,
                     y_ref, st_out_ref, conv_out_ref,
                     proj_ref, ubuf_ref, act_ref, dt_ref, da_ref, o_ref, st_ref):
    t = xin_ref.shape[0]
    c = CHUNK
    s = pl.program_id(0)
    n_blocks = pl.num_programs(0) - 2

    @pl.when(s == 0)
    def _():
        proj_ref[1] = jnp.zeros(proj_ref.shape[1:], F32)
        dt_ref[1] = jnp.zeros(dt_ref.shape[1:], F32)
        da_ref[1] = jnp.zeros(da_ref.shape[1:], F32)
        o_ref[...] = jnp.zeros_like(o_ref)
        ubuf_ref[...] = jnp.zeros_like(ubuf_ref)

    @pl.when(s <= 1)
    def _():
        st_ref[...] = jnp.zeros_like(st_ref)
        ubuf_ref[:, CONV_ROWS:CONV_ROWS + 8, :] = jnp.zeros((CONV_TILES, 8, LANES), F32)

    def in_stage(slot_in):
        h = (_rms_rows(xin_ref[...]) * rms_g_ref[...]).astype(BF16)
        lo = 0
        for units in C_IN_UNITS:
            if lo:
                yield
            hi = lo + units * LANES
            slab = jnp.dot(h, w_in_ref[:, lo:hi], preferred_element_type=F32)
            proj_ref[slot_in, :, lo:hi] = slab
            if hi == C_COLS:
                dt, da = _c_dt(slab[:, C_DT - lo:], dt_bias_ref[...], a_log_ref[...])
                dt_ref[slot_in] = dt
                da_ref[slot_in] = da
            lo = hi

    def act_rows(r, j):
        return act_ref[j, pl.ds(8 + r, c), :]

    n_x = D_INNER_C // LANES
    n_g = N_C // LANES

    def mix_stage(slot_mix):
        def put_o(r, g, value):
            o_ref[slot_mix, pl.ds(r, c), g * GROUP_W:(g + 1) * GROUP_W] = value

        io = types.SimpleNamespace(
            dt=lambda r: dt_ref[slot_mix, pl.ds(r, c), :],
            da=lambda r: da_ref[slot_mix, pl.ds(r, c), :],
            x=lambda r: jnp.concatenate([act_rows(r, j) for j in range(n_x)], axis=1),
            b=lambda r, g: act_rows(r, n_x + g * n_g),
            c=lambda r, g: act_rows(r, n_x + (G_C + g) * n_g),
            z=lambda r, g: proj_ref[slot_mix, pl.ds(r, c), C_Z + g * GROUP_W:C_Z + (g + 1) * GROUP_W],
            put_o=put_o)
        conv_w = conv_w_ref[...]
        conv_b = conv_b_ref[...]
        conv_rounds = C_CONV_ROUNDS
        per_round = CONV_TILES // conv_rounds
        for rnd in range(conv_rounds):
            if rnd:
                yield
            for j in range(rnd * per_round, (rnd + 1) * per_round):
                ubuf_ref[j, 8:16, :] = ubuf_ref[j, CONV_ROWS:CONV_ROWS + 8, :]
                ubuf_ref[j, 16:16 + t, :] = proj_ref[slot_mix, :, C_X + j * LANES:C_X + (j + 1) * LANES]
                _conv_tile(ubuf_ref, act_ref, j, conv_w, conv_b)
        state = [st_ref[...]]
        e_s = e_s_ref[...]
        chunks = [_c_chunk(i * c, c, io, state, e_s, e_s, neg_mask_ref[...], bd_mask_ref[...],
                           dskip_ref[...], g_y_ref[...]) for i in range(t // c)]
        while chunks:
            yield
            chunks = _advance(chunks)
        st_ref[...] = state[0]

    def out_stage(slot_in):
        quarter = 0
        for rnd in range(C_ROUNDS):
            if rnd:
                yield
            if rnd in C_OUT_ROUNDS:
                cols = slice(quarter * 256, (quarter + 1) * 256)
                y_ref[:, cols] = xres_ref[:, cols] + jnp.dot(o_ref[slot_in], w_out_ref[:, cols],
                                                             preferred_element_type=F32)
                quarter += 1

    slot_in = s % 2
    _interleave([in_stage(slot_in), mix_stage(1 - slot_in), out_stage(slot_in)])

    @pl.when(s == n_blocks)
    def _():
        st_out_ref[...] = st_ref[...].T
        conv_out_ref[...] = proj_ref[1 - s % 2, t - 8:t, C_X:C_X + CONV_DIM]


def _conv_rows(ubuf, nrows, conv_w, conv_b):
    acc = conv_b
    for i in range(CONV_W):
        acc = acc + conv_w[i:i + 1, :] * ubuf[pl.ds(8 - (CONV_W - 1) + i, nrows), :]
    return _silu(acc)


def _c_sample_kernel(x_ref, rms_g_ref, w_in_ref, conv_w_ref, conv_b_ref, dt_bias_ref, a_log_ref,
                     dskip_ref, g_y_ref, w_out_ref, e_s_ref, e_p_ref, neg_mask_ref, bd_mask_ref,
                     st_in_ref, conv_in_ref,
                     y_ref, st_out_ref, conv_out_ref,
                     proj_ref, ubuf_ref, act_ref, dt_ref, da_ref, o_ref):
    nb = st_in_ref.shape[0]
    t = x_ref.shape[0]
    c = t // nb
    x = x_ref[...]
    h = (_rms_rows(x) * rms_g_ref[...]).astype(BF16)
    proj_ref[...] = jnp.dot(h, w_in_ref[...], preferred_element_type=F32)
    dt, da = _c_dt(proj_ref[:, C_DT:C_DT + LANES], dt_bias_ref[...], a_log_ref[...])
    dt_ref[...] = dt
    da_ref[...] = da
    conv_w = conv_w_ref[...]
    conv_b = conv_b_ref[...]
    ubuf_ref[0:8, :] = jnp.zeros((8, CONV_DIM), F32)
    for bi in range(nb):
        ubuf_ref[8 - (CONV_W - 1):8, :] = conv_in_ref[bi]
        ubuf_ref[8:8 + c, :] = proj_ref[bi * c:(bi + 1) * c, C_X:C_X + CONV_DIM]
        act_ref[bi * c:(bi + 1) * c, :] = _conv_rows(ubuf_ref, c, conv_w, conv_b)
        conv_out_ref[bi] = ubuf_ref[8 + c - (CONV_W - 1):8 + c, :]

    def put_o(r, g, value):
        o_ref[pl.ds(r, c), g * GROUP_W:(g + 1) * GROUP_W] = value

    io = types.SimpleNamespace(
        dt=lambda r: dt_ref[pl.ds(r, c), :],
        da=lambda r: da_ref[pl.ds(r, c), :],
        x=lambda r: act_ref[pl.ds(r, c), 0:D_INNER_C],
        b=lambda r, g: act_ref[pl.ds(r, c), D_INNER_C + g * N_C:D_INNER_C + (g + 1) * N_C],
        c=lambda r, g: act_ref[pl.ds(r, c), D_INNER_C + (G_C + g) * N_C:D_INNER_C + (G_C + g + 1) * N_C],
        z=lambda r, g: proj_ref[pl.ds(r, c), C_Z + g * GROUP_W:C_Z + (g + 1) * GROUP_W],
        put_o=put_o)
    e_s = e_s_ref[...]
    e_p = e_p_ref[...]
    neg_mask = neg_mask_ref[...]
    bd_mask = bd_mask_ref[...]
    dskip = dskip_ref[...]
    g_y = g_y_ref[...]

    states = [[st_in_ref[bi].T] for bi in range(nb)]
    _interleave(_c_chunk(bi * c, c, io, states[bi], e_s, e_p, neg_mask, bd_mask, dskip, g_y)
                for bi in range(nb))
    for bi in range(nb):
        st_out_ref[bi] = states[bi][0].T
    y_ref[...] = x + jnp.dot(o_ref[...], w_out_ref[...], preferred_element_type=F32)


def _bucket_table(q_off, k_off):
    n = q_off[:, None] - k_off[None, :]
    half = N_BUCKETS // 2
    max_exact = half // 2
    side = np.where(n < 0, half, 0)
    n = np.abs(n)
    nf = np.maximum(n, max_exact).astype(np.float32)
    large = max_exact + (np.log(nf / np.float32(max_exact)) / np.float32(math.log(MAX_DISTANCE / max_exact))
                         * np.float32(half - max_exact)).astype(np.int32)
    large = np.minimum(large, half - 1)
    return side + np.where(n < max_exact, n, large)


def _bias_rows(rel_bias, q_off, k_off):
    bucket = _bucket_table(q_off, k_off)
    onehot = jnp.asarray(np.eye(N_BUCKETS, dtype=np.float32)[bucket])
    bias = jnp.einsum('qkb,bh->hqk', onehot, rel_bias.astype(F32), precision=lax.Precision.HIGHEST) * LOG2E
    return bias.reshape(KVH_B, G_B * q_off.shape[0], k_off.shape[0])


def _group_sum_matrix(width, group):
    idx = np.arange(width) // group
    return jnp.asarray(idx[:, None] == idx[None, :], BF16)


def _kv_tile_matrices():
    m = np.zeros((KVH_B, KVH_B * HD_B, G_B * HD_B), np.float32)
    for kvh in range(KVH_B):
        for gq in range(G_B):
            m[kvh, kvh * HD_B + np.arange(HD_B), gq * HD_B + np.arange(HD_B)] = 1.0
    return jnp.asarray(m, BF16)


def _head_expand_matrix(per_head):
    m = np.zeros((LANES, H_C * per_head), np.float32)
    for h in range(H_C):
        m[h, h * per_head:(h + 1) * per_head] = 1.0
    return jnp.asarray(np.concatenate([m, m], axis=0), BF16)


def _full(shape):
    return pl.BlockSpec(shape, lambda *_: (0,) * len(shape))


def _const(shape):
    return pl.BlockSpec(shape, lambda *_: (0,) * len(shape), pipeline_mode=pl.Buffered(1))


CAST_ROWS = 256


def _cast_columns(wt_ref, w_bf_ref, src, dst, n):
    for off in range(0, n - n % LANES, CAST_ROWS):
        rows = min(CAST_ROWS, n - n % LANES - off)
        w_bf_ref[:, dst + off:dst + off + rows] = wt_ref[src + off:src + off + rows, :].T.astype(BF16)
    rest = n % LANES
    if rest:
        off = n - rest
        tail = jnp.concatenate([wt_ref[src + off:src + n, :], jnp.zeros((LANES - rest, wt_ref.shape[1]), F32)],
                               axis=0)
        w_bf_ref[:, dst + off:dst + off + LANES] = tail.T.astype(BF16)


def _ab_cast_kernel(wt_ref, w_out_ref, w_in_bf_ref, w_out_bf_ref):
    g0 = A_Z
    _cast_columns(wt_ref, w_in_bf_ref, 0, 0, g0)
    _cast_columns(wt_ref, w_in_bf_ref, g0 + GATE_RANK_A, g0, A_G - g0)
    _cast_columns(wt_ref, w_in_bf_ref, g0, A_G, GATE_RANK_A)
    w_out_bf_ref[...] = w_out_ref[...].astype(BF16)


def _c_cast_kernel(wt_ref, w_out_ref, w_in_bf_ref, w_out_bf_ref):
    _cast_columns(wt_ref, w_in_bf_ref, 0, 0, wt_ref.shape[0])
    w_out_bf_ref[...] = w_out_ref[...].astype(BF16)


def _cast_weights(body, w_in, w_out, cols, name):
    vm = pl.BlockSpec(memory_space=pltpu.VMEM)
    return pl.pallas_call(
        body, in_specs=[vm, vm], out_specs=(vm, vm),
        out_shape=(jax.ShapeDtypeStruct((w_in.shape[0], cols), BF16),
                   jax.ShapeDtypeStruct(w_out.shape, BF16)),
        name=name, compiler_params=pltpu.CompilerParams(vmem_limit_bytes=VMEM_LIMIT),
    )(w_in.T, w_out)


def _ab_weights(rms_g, w_in, w_gate_up, b_gate, g_out, g_q, g_k, w_out):
    w_in_r, w_out_bf = _cast_weights(_ab_cast_kernel, w_in, w_out, AB_COLS, "ab_cast")
    w_gu = jnp.concatenate([w_gate_up, jnp.zeros((LANES - GATE_RANK_A, H_A * DK_A), w_gate_up.dtype)],
                           axis=0).astype(BF16)
    return (rms_g.reshape(1, D_MODEL), w_in_r, w_gu, b_gate.reshape(1, -1), g_out.reshape(1, DV_A),
            jnp.tile(g_q, H_B).reshape(1, -1), jnp.tile(g_k, KVH_B).reshape(1, -1), w_out_bf)


def _ab_prompt(x, sink, rel_bias, weights):
    rms_g, w_in, w_gu, b_gate, g_out, gq, gk, w_out = weights
    length = x.shape[0]
    t = AB_PROMPT_BLOCK
    c = CHUNK
    kb = WINDOW + c
    bias = _bias_rows(rel_bias, np.arange(c), np.arange(kb) - WINDOW)
    row_blk = lambda i: (i, 0)
    in_specs = [
        pl.BlockSpec(memory_space=pltpu.SMEM),
        pl.BlockSpec((t, D_MODEL), row_blk),
        _const((1, D_MODEL)), _const((D_MODEL, AB_COLS)), _const((LANES, 256)), _const((1, 256)),
        _const((1, DV_A)), _const((1, 512)), _const((1, 128)), _const((KVH_B, G_B * c, kb)),
        _const((D_MODEL, D_MODEL)), _const((512, 512)), _const((128, 128)), _const((KVH_B, 128, 256)),
    ]
    out_shape = (jax.ShapeDtypeStruct((length, D_MODEL), F32),
                 jax.ShapeDtypeStruct((H_A * DK_A, DV_A), F32),
                 jax.ShapeDtypeStruct((WINDOW, KVH_B * HD_B), F32),
                 jax.ShapeDtypeStruct((WINDOW, KVH_B * HD_B), F32))
    out_specs = (pl.BlockSpec((t, D_MODEL), row_blk), _full((H_A * DK_A, DV_A)),
                 _full((WINDOW, KVH_B * HD_B)), _full((WINDOW, KVH_B * HD_B)))
    scratch = [pltpu.VMEM((t, AB_COLS), F32), pltpu.VMEM((t, 512), F32), pltpu.VMEM((t, D_MODEL), BF16),
               pltpu.VMEM((KVH_B, t + WINDOW, 256), BF16), pltpu.VMEM((KVH_B, t + WINDOW, 256), BF16),
               pltpu.VMEM((H_A * DK_A, DV_A), F32)]
    return pl.pallas_call(
        _ab_prompt_kernel, grid=(length // t,), in_specs=in_specs, out_specs=out_specs,
        out_shape=out_shape, scratch_shapes=scratch, name="ab_prompt",
        compiler_params=pltpu.CompilerParams(dimension_semantics=("arbitrary",),
                                             vmem_limit_bytes=VMEM_LIMIT),
    )(sink, x, rms_g, w_in, w_gu, b_gate, g_out, gq, gk, bias, w_out,
      _group_sum_matrix(512, HD_B), _group_sum_matrix(128, HD_B), _kv_tile_matrices())


def _ab_sample(x, sink, rel_bias, weights, s_in, k_cache, v_cache):
    rms_g, w_in, w_gu, b_gate, g_out, gq, gk, w_out = weights
    nb, c, _ = x.shape
    wc = k_cache.shape[1]
    kb = wc + c
    t = nb * c
    bias = _bias_rows(rel_bias, np.arange(c), np.arange(kb) - wc)
    vm = pl.BlockSpec(memory_space=pltpu.VMEM)
    in_specs = [pl.BlockSpec(memory_space=pltpu.SMEM)] + [vm] * 16
    out_shape = (jax.ShapeDtypeStruct((t, D_MODEL), F32),
                 jax.ShapeDtypeStruct((nb, H_A * DK_A, DV_A), F32),
                 jax.ShapeDtypeStruct((t, KVH_B * HD_B), F32),
                 jax.ShapeDtypeStruct((t, KVH_B * HD_B), F32))
    scratch = [pltpu.VMEM((t, AB_COLS), F32), pltpu.VMEM((t, 512), F32), pltpu.VMEM((t, D_MODEL), BF16),
               pltpu.VMEM((KVH_B, nb, kb, 256), BF16), pltpu.VMEM((KVH_B, nb, kb, 256), BF16)]
    return pl.pallas_call(
        _ab_sample_kernel, in_specs=in_specs, out_specs=(vm,) * 4, out_shape=out_shape,
        scratch_shapes=scratch, name="ab_sample",
        compiler_params=pltpu.CompilerParams(vmem_limit_bytes=VMEM_LIMIT),
    )(sink, x.reshape(t, D_MODEL), rms_g, w_in, w_gu, b_gate, g_out, gq, gk, bias, w_out,
      _group_sum_matrix(512, HD_B), _group_sum_matrix(128, HD_B), _kv_tile_matrices(),
      s_in.reshape(nb, H_A * DK_A, DV_A), k_cache.reshape(nb, wc, KVH_B * HD_B),
      v_cache.reshape(nb, wc, KVH_B * HD_B))


def _c_weights(rms_g, w_in, conv_w, conv_b, dt_bias, a_log, d_skip, g_y, w_out):
    w_in_p, w_out_bf = _cast_weights(_c_cast_kernel, w_in, w_out, C_COLS, "c_cast")
    pad = lambda v: jnp.concatenate([v, jnp.zeros((LANES - H_C,), v.dtype)]).reshape(1, LANES)
    return (rms_g.reshape(1, D_MODEL), w_in_p, conv_w, conv_b.reshape(1, CONV_DIM), pad(dt_bias), pad(a_log),
            jnp.repeat(d_skip, P_C).reshape(1, D_INNER_C), g_y.reshape(1, D_INNER_C), w_out_bf)


def _ssd_masks(c):
    tok = np.arange(c)[:, None]
    src = np.arange(H_C * c)[None, :] % c
    neg = np.where(src <= tok, 0.0, -np.inf).astype(np.float32)
    blk = (np.arange(4 * c)[:, None] // c) == (np.arange(256)[None, :] // P_C)
    return jnp.asarray(neg), jnp.asarray(blk, BF16)


def _c_prompt(x, weights):
    length = x.shape[0]
    t = PROMPT_BLOCK
    c = CHUNK
    n_blocks = length // t
    in_blk = lambda s: (jnp.minimum(s, n_blocks - 1), 0)
    out_blk = lambda s: (jnp.clip(s - 2, 0, n_blocks - 1), 0)
    in_specs = [
        pl.BlockSpec((t, D_MODEL), in_blk), pl.BlockSpec((t, D_MODEL), out_blk),
        _const((1, D_MODEL)), _const((D_MODEL, C_COLS)), _const((CONV_W, CONV_DIM)), _const((1, CONV_DIM)),
        _const((1, LANES)), _const((1, LANES)), _const((1, D_INNER_C)), _const((1, D_INNER_C)),
        _const((D_INNER_C, D_MODEL)), _const((2 * LANES, H_C * c)),
        _const((c, H_C * c)), _const((4 * c, 256)),
    ]
    out_shape = (jax.ShapeDtypeStruct((length, D_MODEL), F32),
                 jax.ShapeDtypeStruct((H_C * P_C, N_C), F32),
                 jax.ShapeDtypeStruct((8, CONV_DIM), F32))
    out_specs = (pl.BlockSpec((t, D_MODEL), out_blk), _full((H_C * P_C, N_C)), _full((8, CONV_DIM)))
    scratch = [pltpu.VMEM((2, t, C_COLS), F32), pltpu.VMEM((CONV_TILES, CONV_ROWS + 8, LANES), F32),
               pltpu.VMEM((CONV_TILES, CONV_ROWS, LANES), F32), pltpu.VMEM((2, t, LANES), F32),
               pltpu.VMEM((2, t, LANES), F32), pltpu.VMEM((2, t, D_INNER_C), BF16),
               pltpu.VMEM((N_C, H_C * P_C), F32)]
    return pl.pallas_call(
        _c_prompt_kernel, grid=(n_blocks + 2,), in_specs=in_specs, out_specs=out_specs,
        out_shape=out_shape, scratch_shapes=scratch, name="c_prompt",
        compiler_params=pltpu.CompilerParams(dimension_semantics=("arbitrary",),
                                             vmem_limit_bytes=VMEM_LIMIT),
    )(x, x, *weights, _head_expand_matrix(c), *_ssd_masks(c))


def _c_sample(x, weights, st_in, conv_in):
    nb, c, _ = x.shape
    t = nb * c
    vm = pl.BlockSpec(memory_space=pltpu.VMEM)
    out_shape = (jax.ShapeDtypeStruct((t, D_MODEL), F32),
                 jax.ShapeDtypeStruct((nb, H_C * P_C, N_C), F32),
                 jax.ShapeDtypeStruct((nb, CONV_W - 1, CONV_DIM), F32))
    scratch = [pltpu.VMEM((t, C_COLS), F32), pltpu.VMEM((c + 8, CONV_DIM), F32),
               pltpu.VMEM((t, CONV_DIM), F32), pltpu.VMEM((t, LANES), F32), pltpu.VMEM((t, LANES), F32),
               pltpu.VMEM((t, D_INNER_C), BF16)]
    return pl.pallas_call(
        _c_sample_kernel, in_specs=[vm] * 16, out_specs=(vm,) * 3, out_shape=out_shape,
        scratch_shapes=scratch, name="c_sample",
        compiler_params=pltpu.CompilerParams(vmem_limit_bytes=VMEM_LIMIT),
    )(x.reshape(t, D_MODEL), *weights, _head_expand_matrix(c), _head_expand_matrix(P_C), *_ssd_masks(c),
      st_in.reshape(nb, H_C * P_C, N_C), conv_in)


def kernel(x_prompt, x_sample, cache_swa_k, cache_swa_v, state_gla, state_ssd, state_conv, rms_g, w_in_ab, w_gate_up_a, b_gate_a, g_out_a, g_q_b, g_k_b, sink_b, rel_bias, w_out_ab, w_in_c, conv_w_c, conv_b_c, dt_bias_c, a_log_c, d_skip_c, g_y_c, w_out_c):
    bp, seq_len, _ = x_prompt.shape
    nb, dec_len, _ = x_sample.shape
    assert bp == 1 and seq_len % PROMPT_BLOCK == 0 and seq_len % AB_PROMPT_BLOCK == 0 and seq_len >= WINDOW
    wab = _ab_weights(rms_g[0], w_in_ab[0], w_gate_up_a[0], b_gate_a[0], g_out_a[0], g_q_b[0], g_k_b[0],
                      w_out_ab[0])
    yp, gla_p, k_p, v_p = _ab_prompt(x_prompt[0], sink_b[0], rel_bias, wab)
    ys, gla_s, k_s, v_s = _ab_sample(x_sample, sink_b[0], rel_bias, wab, state_gla[0], cache_swa_k[0],
                                     cache_swa_v[0])
    wc = _c_weights(rms_g[1], w_in_c[0], conv_w_c[0], conv_b_c[0], dt_bias_c[0], a_log_c[0], d_skip_c[0],
                    g_y_c[0], w_out_c[0])
    yp, ssd_p, conv_p = _c_prompt(yp, wc)
    ys, ssd_s, conv_s = _c_sample(ys.reshape(nb, dec_len, D_MODEL), wc, state_ssd[0], state_conv[0])
    return (
        yp.reshape(1, seq_len, D_MODEL),
        ys.reshape(nb, dec_len, D_MODEL),
        gla_p.reshape(1, 1, H_A, DK_A, DV_A),
        k_p.reshape(1, 1, WINDOW, KVH_B, HD_B),
        v_p.reshape(1, 1, WINDOW, KVH_B, HD_B),
        ssd_p.reshape(1, 1, H_C, P_C, N_C),
        conv_p[8 - (CONV_W - 1):].reshape(1, 1, CONV_W - 1, CONV_DIM),
        gla_s.reshape(1, nb, H_A, DK_A, DV_A),
        k_s.reshape(1, nb, dec_len, KVH_B, HD_B),
        v_s.reshape(1, nb, dec_len, KVH_B, HD_B),
        ssd_s.reshape(1, nb, H_C, P_C, N_C),
        conv_s.reshape(1, nb, CONV_W - 1, CONV_DIM),
    )
```

```python
import math
import types

import numpy as np
import jax
import jax.numpy as jnp
from jax import lax
from jax.experimental import pallas as pl
from jax.experimental.pallas import tpu as pltpu

F32 = jnp.float32
BF16 = jnp.bfloat16

D_MODEL = 1024
CHUNK = 64
EPS = 1e-6
H_A = 4
DK_A = 64
DV_A = 128
GATE_RANK_A = 16
GATE_NORM_A = 16.0
H_B = 8
KVH_B = 2
G_B = H_B // KVH_B
HD_B = 64
WINDOW = 128
N_BUCKETS = 32
MAX_DISTANCE = 128
D_INNER_C = 2048
P_C = 64
H_C = 32
G_C = 4
HPG_C = 8
N_C = 128
CONV_W = 4
CONV_DIM = D_INNER_C + 2 * G_C * N_C

LOG2E = 1.4426950408889634
LANES = 128

A_Q, A_K, A_V, A_Z = 0, 256, 512, 1024
B_Q, B_K, B_V, B_Z = 1536, 2048, 2176, 2304
A_G = 2816
AB_COLS = 2944
C_Z, C_X, C_B, C_C, C_DT = 0, 2048, 4096, 4608, 5120
C_COLS = 5248

PROMPT_BLOCK = 256
AB_PROMPT_BLOCK = 512
VMEM_LIMIT = 60 * 1024 * 1024


def _dot(a, b):
    return jnp.dot(a.astype(BF16), b.astype(BF16), preferred_element_type=F32)


def _dot_nt(a, b):
    return lax.dot_general(a.astype(BF16), b.astype(BF16), (((1,), (1,)), ((), ())),
                           preferred_element_type=F32)


def _dot_tn(a, b):
    return lax.dot_general(a.astype(BF16), b.astype(BF16), (((0,), (0,)), ((), ())),
                           preferred_element_type=F32)


def _split_bf16(x, terms):
    out = []
    r = x
    for _ in range(terms):
        h = r.astype(BF16)
        out.append(h)
        r = r - h.astype(F32)
    return out


def _sel_dot_left(sel, x, terms):
    acc = None
    for h in _split_bf16(x, terms):
        d = jnp.dot(sel, h, preferred_element_type=F32)
        acc = d if acc is None else acc + d
    return acc


def _rms_rows(x):
    return x * lax.rsqrt(jnp.mean(x * x, axis=-1, keepdims=True) + EPS)


def _exp_neg(x):
    return jnp.exp2(x * (-LOG2E))


def _silu(x):
    return x * (1.0 / (1.0 + _exp_neg(x)))


def _softplus(x):
    return jnp.maximum(x, 0.0) + jnp.log(1.0 + _exp_neg(jnp.abs(x)))


def _log_sigmoid(x):
    return jnp.minimum(x, 0.0) - jnp.log(1.0 + _exp_neg(jnp.abs(x)))


def _tril(n):
    r = lax.broadcasted_iota(jnp.int32, (n, n), 0)
    c = lax.broadcasted_iota(jnp.int32, (n, n), 1)
    return r >= c


def _head_mean_sq(x):
    out = []
    for p in range(x.shape[1] // LANES):
        sq = x[:, p * LANES:(p + 1) * LANES]
        sq = sq * sq
        low = lax.broadcasted_iota(jnp.int32, sq.shape, 1) < HD_B
        lo = jnp.sum(jnp.where(low, sq, 0.0), axis=-1, keepdims=True)
        hi = jnp.sum(jnp.where(low, 0.0, sq), axis=-1, keepdims=True)
        out.append(jnp.where(low, lo, hi))
    return jnp.concatenate(out, axis=1) * (1.0 / HD_B)


def _tile_kv_heads(kv):
    low = lax.broadcasted_iota(jnp.int32, kv.shape, 1) < HD_B
    swapped = pltpu.roll(kv, HD_B, 1)
    out = []
    for base in (jnp.where(low, kv, swapped), jnp.where(low, swapped, kv)):
        base = base.astype(BF16)
        out.append(jnp.concatenate([base, base], axis=1))
    return out


def _ab_dense_in(x, rms_g, w_in_ref, gq, gk, proj_ref, qn_ref):
    h = (_rms_rows(x) * rms_g).astype(BF16)
    rows = min(256, x.shape[0])
    for m in range(0, x.shape[0], rows):
        proj_ref[m:m + rows, :] = jnp.dot(h[m:m + rows], w_in_ref[...], preferred_element_type=F32)
    qb = proj_ref[:, B_Q:B_Q + 512]
    qn_ref[...] = qb * lax.rsqrt(_head_mean_sq(qb) + EPS) * gq
    kb = proj_ref[:, B_K:B_K + 128]
    kn = kb * lax.rsqrt(_head_mean_sq(kb) + EPS) * gk
    vb = proj_ref[:, B_V:B_V + 128]
    return kn, vb


def _advance(gens, yielded=None):
    alive = []
    for gen in gens:
        try:
            value = next(gen)
            alive.append(gen)
            if yielded is not None and value is not None:
                yielded.append(value)
        except StopIteration:
            pass
    return alive


def _interleave(gens):
    gens = list(gens)
    while gens:
        gens = _advance(gens)


def _ab_chunk(r, c, kb, proj_ref, qn_ref, o_ref, state, k_band, v_band, bias_ref, sink_ref,
              w_gu, b_gate, g_out, first_valid_col):
    rows = pl.ds(r, c)
    gate = _dot(proj_ref[rows, A_G:A_G + LANES], w_gu) + b_gate
    yield
    g = _log_sigmoid(gate) * (LOG2E / GATE_NORM_A)
    tril = _tril(c)
    b = _sel_dot_left(tril.astype(BF16), g, 3)
    bl = b[c - 1:c, :]
    bl_tile = jnp.broadcast_to(bl, (LANES, H_A * DK_A))
    bl_rows = jnp.concatenate([bl_tile[:, i * LANES:(i + 1) * LANES].T
                               for i in range(H_A * DK_A // LANES)], axis=0)
    yield
    q = proj_ref[rows, A_Q:A_Q + 256] * (DK_A ** -0.5)
    k = proj_ref[rows, A_K:A_K + 256]
    v = proj_ref[rows, A_V:A_V + 512].astype(BF16)
    qe = (q * jnp.exp2(b)).astype(BF16)
    kd = k * jnp.exp2(-b)
    kd2 = (k * jnp.exp2(bl - b)).astype(BF16)
    lane_head = lax.broadcasted_iota(jnp.int32, (c, LANES), 1) // DK_A
    row_head = lax.broadcasted_iota(jnp.int32, (LANES, LANES), 0) // DK_A
    att = []
    for p in range(2):
        kp = kd[:, p * 128:(p + 1) * 128]
        kpair = jnp.concatenate([jnp.where(lane_head == j, kp, 0.0) for j in range(2)], axis=0).astype(BF16)
        att.append(_dot_nt(qe[:, p * 128:(p + 1) * 128], kpair))
    upd = [_dot_tn(kd2[:, p * 128:(p + 1) * 128], v[:, p * 256:(p + 1) * 256]) for p in range(2)]
    yield
    row = lax.broadcasted_iota(jnp.int32, (c, 2 * c), 0)
    col = lax.broadcasted_iota(jnp.int32, (c, 2 * c), 1) % c
    att = [jnp.where(row >= col, a, 0.0).astype(BF16) for a in att]
    yield
    s_prev = state[0]
    s_new, oh = [], []
    zeros_v = jnp.zeros((c, DV_A), BF16)
    for p in range(2):
        sp = s_prev[p * 128:(p + 1) * 128, :]
        s_bd = jnp.concatenate([jnp.where(row_head == j, sp, 0.0) for j in range(2)], axis=1).astype(BF16)
        v0, v1 = v[:, 2 * p * DV_A:(2 * p + 1) * DV_A], v[:, (2 * p + 1) * DV_A:(2 * p + 2) * DV_A]
        v_bd = jnp.concatenate([jnp.concatenate([v0, zeros_v], axis=1),
                                jnp.concatenate([zeros_v, v1], axis=1)], axis=0)
        qp = qe[:, p * 128:(p + 1) * 128]
        if (2 * c) % LANES == 0:
            o_pair = jnp.dot(jnp.concatenate([att[p], qp], axis=1), jnp.concatenate([v_bd, s_bd], axis=0),
                             preferred_element_type=F32)
        else:
            o_pair = (jnp.dot(att[p], v_bd, preferred_element_type=F32)
                      + jnp.dot(qp, s_bd, preferred_element_type=F32))
        oh += [o_pair[:, :DV_A], o_pair[:, DV_A:]]
        u = jnp.where(row_head == 0, upd[p][:, :128], upd[p][:, 128:])
        s_new.append(jnp.exp2(bl_rows[p * 128:(p + 1) * 128, :]) * sp + u)
    state[0] = jnp.concatenate(s_new, axis=0)
    yield
    for hd in range(H_A):
        z = proj_ref[rows, A_Z + hd * 128:A_Z + (hd + 1) * 128]
        o_ref[rows, hd * 128:(hd + 1) * 128] = (_rms_rows(oh[hd]) * g_out * _silu(z)).astype(BF16)
    yield
    lane256 = lax.broadcasted_iota(jnp.int32, (c, 256), 1) // HD_B
    srow = lax.broadcasted_iota(jnp.int32, (G_B * c, 1), 0) // c
    scores = []
    for kvh in range(KVH_B):
        qn = qn_ref[rows, kvh * 256:(kvh + 1) * 256]
        qs = jnp.concatenate([jnp.where(lane256 == gq_, qn, 0.0) for gq_ in range(G_B)], axis=0)
        scores.append(_dot_nt(qs, k_band(kvh)))
    yield
    probs, dens = [], []
    for kvh in range(KVH_B):
        s = scores[kvh] * (LOG2E * HD_B ** -0.5) + bias_ref[kvh]
        if first_valid_col is not None:
            col = lax.broadcasted_iota(jnp.int32, (G_B * c, kb), 1)
            s = jnp.where(col >= first_valid_col, s, -jnp.inf)
        sink = jnp.zeros((G_B * c, 1), F32)
        for gq_ in range(G_B):
            sink = jnp.where(srow == gq_, sink_ref[kvh * G_B + gq_] * LOG2E, sink)
        m = jnp.maximum(jnp.max(s, axis=-1, keepdims=True), sink)
        pr = jnp.exp2(s - m)
        dens.append(jnp.sum(pr, axis=-1, keepdims=True) + jnp.exp2(sink - m))
        probs.append(pr.astype(BF16))
    yield
    outs = [_dot(probs[kvh], v_band(kvh)) for kvh in range(KVH_B)]
    yield
    for kvh in range(KVH_B):
        ost = outs[kvh] / dens[kvh]
        ob = jnp.zeros((c, 256), F32)
        for gq_ in range(G_B):
            ob = ob + jnp.where(lane256 == gq_, ost[gq_ * c:(gq_ + 1) * c, :], 0.0)
        z = proj_ref[rows, B_Z + kvh * 256:B_Z + (kvh + 1) * 256]
        o_ref[rows, 512 + kvh * 256:512 + (kvh + 1) * 256] = (ob * _silu(z)).astype(BF16)


def _ab_prompt_kernel(sink_ref, x_ref, rms_g_ref, w_in_ref, w_gu_ref, b_gate_ref, g_out_ref, gq_ref,
                      gk_ref, bias_ref, w_out_ref,
                      y_ref, s_out_ref, k_out_ref, v_out_ref,
                      proj_ref, qn_ref, o_ref, kband_ref, vband_ref, s_ref):
    t = x_ref.shape[0]
    c = CHUNK
    kb = WINDOW + c
    nchunk = t // c
    step = pl.program_id(0)

    @pl.when(step == 0)
    def _():
        s_ref[...] = jnp.zeros_like(s_ref)
        kband_ref[:, t:t + WINDOW, :] = jnp.zeros((KVH_B, WINDOW, 256), BF16)
        vband_ref[:, t:t + WINDOW, :] = jnp.zeros((KVH_B, WINDOW, 256), BF16)

    for kvh in range(KVH_B):
        kband_ref[kvh, 0:WINDOW, :] = kband_ref[kvh, t:t + WINDOW, :]
        vband_ref[kvh, 0:WINDOW, :] = vband_ref[kvh, t:t + WINDOW, :]

    x = x_ref[...]
    kn, vb = _ab_dense_in(x, rms_g_ref[...], w_in_ref, gq_ref[...], gk_ref[...], proj_ref, qn_ref)
    k_out_ref[...] = kn[t - WINDOW:, :]
    v_out_ref[...] = vb[t - WINDOW:, :]
    for kvh, (kt, vt) in enumerate(zip(_tile_kv_heads(kn), _tile_kv_heads(vb))):
        kband_ref[kvh, WINDOW:WINDOW + t, :] = kt
        vband_ref[kvh, WINDOW:WINDOW + t, :] = vt

    w_gu = w_gu_ref[...]
    b_gate = b_gate_ref[...]
    g_out = g_out_ref[...]

    state = [s_ref[...]]

    def chunk(i):
        r = i * c
        return _ab_chunk(r, c, kb, proj_ref, qn_ref, o_ref, state,
                         lambda kvh: kband_ref[kvh, pl.ds(r, kb), :],
                         lambda kvh: vband_ref[kvh, pl.ds(r, kb), :],
                         bias_ref, sink_ref, w_gu, b_gate, g_out, (2 - (step * nchunk + i)) * c)

    _interleave(chunk(i) for i in range(nchunk))
    s_ref[...] = state[0]
    y_ref[...] = x + jnp.dot(o_ref[...], w_out_ref[...], preferred_element_type=F32)
    s_out_ref[...] = s_ref[...]


def _ab_sample_kernel(sink_ref, x_ref, rms_g_ref, w_in_ref, w_gu_ref, b_gate_ref, g_out_ref, gq_ref,
                      gk_ref, bias_ref, w_out_ref,
                      s_in_ref, kc_ref, vc_ref,
                      y_ref, s_out_ref, k_out_ref, v_out_ref,
                      proj_ref, qn_ref, o_ref, kband_ref, vband_ref):
    nb, wc = kc_ref.shape[0], kc_ref.shape[1]
    t = x_ref.shape[0]
    c = t // nb
    kb = wc + c
    x = x_ref[...]
    kn, vb = _ab_dense_in(x, rms_g_ref[...], w_in_ref, gq_ref[...], gk_ref[...], proj_ref, qn_ref)
    k_out_ref[...] = kn
    v_out_ref[...] = vb
    k_new, v_new = _tile_kv_heads(kn), _tile_kv_heads(vb)
    for bi in range(nb):
        k_old, v_old = _tile_kv_heads(kc_ref[bi]), _tile_kv_heads(vc_ref[bi])
        for kvh in range(KVH_B):
            kband_ref[kvh, bi, 0:wc, :] = k_old[kvh]
            vband_ref[kvh, bi, 0:wc, :] = v_old[kvh]
            kband_ref[kvh, bi, wc:kb, :] = k_new[kvh][bi * c:(bi + 1) * c, :]
            vband_ref[kvh, bi, wc:kb, :] = v_new[kvh][bi * c:(bi + 1) * c, :]

    w_gu = w_gu_ref[...]
    b_gate = b_gate_ref[...]
    g_out = g_out_ref[...]

    states = [[s_in_ref[bi]] for bi in range(nb)]

    def seq(bi):
        return _ab_chunk(bi * c, c, kb, proj_ref, qn_ref, o_ref, states[bi],
                         lambda kvh: kband_ref[kvh, bi], lambda kvh: vband_ref[kvh, bi],
                         bias_ref, sink_ref, w_gu, b_gate, g_out, None)

    _interleave(seq(bi) for bi in range(nb))
    for bi in range(nb):
        s_out_ref[bi] = states[bi][0]
    y_ref[...] = x + jnp.dot(o_ref[...], w_out_ref[...], preferred_element_type=F32)


GROUP_W = D_INNER_C // G_C


def _c_chunk(r, c, io, state, e_s, e_p, neg_mask, bd_mask, dskip, g_y):
    dtc = io.dt(r)
    acum = _sel_dot_left(_tril(c).astype(BF16), io.da(r), 3) * LOG2E
    yield acum
    lhs = jnp.concatenate([jnp.concatenate(_split_bf16(acum, 2), axis=1),
                           jnp.concatenate(_split_bf16(dtc, 2), axis=1)], axis=0)
    both_p = jnp.dot(lhs, e_p, preferred_element_type=F32)
    xa_p, dt_p = both_p[:c], both_p[c:]
    xa_s = xa_p if c == P_C else jnp.dot(lhs[:c], e_s, preferred_element_type=F32)
    yield dt_p
    acum_t = acum.T
    a_row = jnp.concatenate([acum_t[h:h + 1, :] for h in range(H_C)], axis=1)
    wmat = jnp.exp2((xa_s - a_row) + neg_mask)
    al_p = xa_p[c - 1:c, :]
    xs = io.x(r)
    xdt = xs * dt_p
    xdt_bf = xdt.astype(BF16)
    xw = (xdt * jnp.exp2(al_p - xa_p)).astype(BF16)
    dec = jnp.exp2(al_p)
    bg, cg, cb = [], [], []
    for g in range(G_C):
        bg.append(io.b(r, g).astype(BF16))
        cg.append(io.c(r, g).astype(BF16))
        cb.append(_dot_nt(cg[g], jnp.concatenate([bg[g]] * HPG_C, axis=0)))
    yield cb[-1]
    ys = []
    for g in range(G_C):
        mg = (cb[g] * wmat[:, g * HPG_C * c:(g + 1) * HPG_C * c]).astype(BF16)
        for j in range(2):
            xj = xdt_bf[:, g * GROUP_W + j * 256:g * GROUP_W + (j + 1) * 256]
            bd = jnp.concatenate([xj] * 4, axis=0) * bd_mask
            ys.append(jnp.dot(mg[:, j * 4 * c:(j + 1) * 4 * c], bd, preferred_element_type=F32))
    upd = [_dot_tn(bg[g], xw[:, g * GROUP_W:(g + 1) * GROUP_W]) for g in range(G_C)]
    yield upd[-1]
    st_prev = state[0]
    y_inter, st_new = [], []
    for g in range(G_C):
        sl = slice(g * GROUP_W, (g + 1) * GROUP_W)
        y_inter.append(_dot(cg[g], st_prev[:, sl]))
        st_new.append(st_prev[:, sl] * dec[:, sl] + upd[g])
    state[0] = jnp.concatenate(st_new, axis=1)
    yield st_new[-1]
    for g in range(G_C):
        sl = slice(g * GROUP_W, (g + 1) * GROUP_W)
        y = jnp.concatenate(ys[2 * g:2 * g + 2], axis=1) + y_inter[g] * jnp.exp2(xa_p[:, sl])
        y = y + dskip[:, sl] * xs[:, sl]
        y = y * _silu(io.z(r, g))
        io.put_o(r, g, (_rms_rows(y) * g_y[:, sl]).astype(BF16))


def _c_dt(dt_cols, dt_bias, a_log):
    dt = _softplus(dt_cols + dt_bias)
    return dt, dt * (-jnp.exp(a_log))


CONV_PITCH = PROMPT_BLOCK // 8 + 1
CONV_ROWS = 8 * CONV_PITCH
CONV_TILES = CONV_DIM // LANES
C_CONV_ROUNDS = 6
C_ROUNDS = C_CONV_ROUNDS + 6
C_IN_UNITS = (4, 4, 4, 4, 4, 4, 4, 2, 4, 2, 2, 3)
C_OUT_ROUNDS = (3, 5, 8, 10)
assert len(C_IN_UNITS) == C_ROUNDS and sum(C_IN_UNITS) * LANES == C_COLS


def _conv_tile(ubuf_ref, act_ref, j, conv_w, conv_b):
    w = [jnp.broadcast_to(conv_w[i:i + 1, j * LANES:(j + 1) * LANES], (8, LANES)) for i in range(CONV_W)]
    b = jnp.broadcast_to(conv_b[:, j * LANES:(j + 1) * LANES], (8, LANES))
    for a in range(CONV_PITCH):
        acc = b
        for i in range(CONV_W):
            acc = acc + w[i] * ubuf_ref[j, pl.ds(8 - (CONV_W - 1) + i + a, 8, stride=CONV_PITCH), :]
        act_ref[j, pl.ds(a, 8, stride=CONV_PITCH), :] = _silu(acc)


def _c_prompt_kernel(xin_ref, xres_ref, rms_g_ref, w_in_ref, conv_w_ref, conv_b_ref, dt_bias_ref,
                     a_log_ref, dskip_ref, g_y_ref, w_out_ref, e_s_ref, neg_mask_ref, bd_mask_ref,
                     y_ref, st_out_ref, conv_out_ref,
                     proj_ref, ubuf_ref, act_ref, dt_ref, da_ref, o_ref, st_ref):
    t = xin_ref.shape[0]
    c = CHUNK
    s = pl.program_id(0)
    n_blocks = pl.num_programs(0) - 2

    @pl.when(s == 0)
    def _():
        proj_ref[1] = jnp.zeros(proj_ref.shape[1:], F32)
        dt_ref[1] = jnp.zeros(dt_ref.shape[1:], F32)
        da_ref[1] = jnp.zeros(da_ref.shape[1:], F32)
        o_ref[...] = jnp.zeros_like(o_ref)
        ubuf_ref[...] = jnp.zeros_like(ubuf_ref)

    @pl.when(s <= 1)
    def _():
        st_ref[...] = jnp.zeros_like(st_ref)
        ubuf_ref[:, CONV_ROWS:CONV_ROWS + 8, :] = jnp.zeros((CONV_TILES, 8, LANES), F32)

    def in_stage(slot_in):
        h = (_rms_rows(xin_ref[...]) * rms_g_ref[...]).astype(BF16)
        lo = 0
        for units in C_IN_UNITS:
            if lo:
                yield
            hi = lo + units * LANES
            slab = jnp.dot(h, w_in_ref[:, lo:hi], preferred_element_type=F32)
            proj_ref[slot_in, :, lo:hi] = slab
            if hi == C_COLS:
                dt, da = _c_dt(slab[:, C_DT - lo:], dt_bias_ref[...], a_log_ref[...])
                dt_ref[slot_in] = dt
                da_ref[slot_in] = da
            lo = hi

    def act_rows(r, j):
        return act_ref[j, pl.ds(8 + r, c), :]

    n_x = D_INNER_C // LANES
    n_g = N_C // LANES

    def mix_stage(slot_mix):
        def put_o(r, g, value):
            o_ref[slot_mix, pl.ds(r, c), g * GROUP_W:(g + 1) * GROUP_W] = value

        io = types.SimpleNamespace(
            dt=lambda r: dt_ref[slot_mix, pl.ds(r, c), :],
            da=lambda r: da_ref[slot_mix, pl.ds(r, c), :],
            x=lambda r: jnp.concatenate([act_rows(r, j) for j in range(n_x)], axis=1),
            b=lambda r, g: act_rows(r, n_x + g * n_g),
            c=lambda r, g: act_rows(r, n_x + (G_C + g) * n_g),
            z=lambda r, g: proj_ref[slot_mix, pl.ds(r, c), C_Z + g * GROUP_W:C_Z + (g + 1) * GROUP_W],
            put_o=put_o)
        conv_w = conv_w_ref[...]
        conv_b = conv_b_ref[...]
        conv_rounds = C_CONV_ROUNDS
        per_round = CONV_TILES // conv_rounds
        for rnd in range(conv_rounds):
            if rnd:
                yield
            for j in range(rnd * per_round, (rnd + 1) * per_round):
                ubuf_ref[j, 8:16, :] = ubuf_ref[j, CONV_ROWS:CONV_ROWS + 8, :]
                ubuf_ref[j, 16:16 + t, :] = proj_ref[slot_mix, :, C_X + j * LANES:C_X + (j + 1) * LANES]
                _conv_tile(ubuf_ref, act_ref, j, conv_w, conv_b)
        state = [st_ref[...]]
        e_s = e_s_ref[...]
        chunks = [_c_chunk(i * c, c, io, state, e_s, e_s, neg_mask_ref[...], bd_mask_ref[...],
                           dskip_ref[...], g_y_ref[...]) for i in range(t // c)]
        while chunks:
            yield
            chunks = _advance(chunks)
        st_ref[...] = state[0]

    def out_stage(slot_in):
        quarter = 0
        for rnd in range(C_ROUNDS):
            if rnd:
                yield
            if rnd in C_OUT_ROUNDS:
                cols = slice(quarter * 256, (quarter + 1) * 256)
                y_ref[:, cols] = xres_ref[:, cols] + jnp.dot(o_ref[slot_in], w_out_ref[:, cols],
                                                             preferred_element_type=F32)
                quarter += 1

    slot_in = s % 2
    _interleave([in_stage(slot_in), mix_stage(1 - slot_in), out_stage(slot_in)])

    @pl.when(s == n_blocks)
    def _():
        st_out_ref[...] = st_ref[...].T
        conv_out_ref[...] = proj_ref[1 - s % 2, t - 8:t, C_X:C_X + CONV_DIM]


def _conv_rows(ubuf, nrows, conv_w, conv_b):
    acc = conv_b
    for i in range(CONV_W):
        acc = acc + conv_w[i:i + 1, :] * ubuf[pl.ds(8 - (CONV_W - 1) + i, nrows), :]
    return _silu(acc)


def _c_sample_kernel(x_ref, rms_g_ref, w_in_ref, conv_w_ref, conv_b_ref, dt_bias_ref, a_log_ref,
                     dskip_ref, g_y_ref, w_out_ref, e_s_ref, e_p_ref, neg_mask_ref, bd_mask_ref,
                     st_in_ref, conv_in_ref,
                     y_ref, st_out_ref, conv_out_ref,
                     proj_ref, ubuf_ref, act_ref, dt_ref, da_ref, o_ref):
    nb = st_in_ref.shape[0]
    t = x_ref.shape[0]
    c = t // nb
    x = x_ref[...]
    h = (_rms_rows(x) * rms_g_ref[...]).astype(BF16)
    proj_ref[...] = jnp.dot(h, w_in_ref[...], preferred_element_type=F32)
    dt, da = _c_dt(proj_ref[:, C_DT:C_DT + LANES], dt_bias_ref[...], a_log_ref[...])
    dt_ref[...] = dt
    da_ref[...] = da
    conv_w = conv_w_ref[...]
    conv_b = conv_b_ref[...]
    ubuf_ref[0:8, :] = jnp.zeros((8, CONV_DIM), F32)
    for bi in range(nb):
        ubuf_ref[8 - (CONV_W - 1):8, :] = conv_in_ref[bi]
        ubuf_ref[8:8 + c, :] = proj_ref[bi * c:(bi + 1) * c, C_X:C_X + CONV_DIM]
        act_ref[bi * c:(bi + 1) * c, :] = _conv_rows(ubuf_ref, c, conv_w, conv_b)
        conv_out_ref[bi] = ubuf_ref[8 + c - (CONV_W - 1):8 + c, :]

    def put_o(r, g, value):
        o_ref[pl.ds(r, c), g * GROUP_W:(g + 1) * GROUP_W] = value

    io = types.SimpleNamespace(
        dt=lambda r: dt_ref[pl.ds(r, c), :],
        da=lambda r: da_ref[pl.ds(r, c), :],
        x=lambda r: act_ref[pl.ds(r, c), 0:D_INNER_C],
        b=lambda r, g: act_ref[pl.ds(r, c), D_INNER_C + g * N_C:D_INNER_C + (g + 1) * N_C],
        c=lambda r, g: act_ref[pl.ds(r, c), D_INNER_C + (G_C + g) * N_C:D_INNER_C + (G_C + g + 1) * N_C],
        z=lambda r, g: proj_ref[pl.ds(r, c), C_Z + g * GROUP_W:C_Z + (g + 1) * GROUP_W],
        put_o=put_o)
    e_s = e_s_ref[...]
    e_p = e_p_ref[...]
    neg_mask = neg_mask_ref[...]
    bd_mask = bd_mask_ref[...]
    dskip = dskip_ref[...]
    g_y = g_y_ref[...]

    states = [[st_in_ref[bi].T] for bi in range(nb)]
    _interleave(_c_chunk(bi * c, c, io, states[bi], e_s, e_p, neg_mask, bd_mask, dskip, g_y)
                for bi in range(nb))
    for bi in range(nb):
        st_out_ref[bi] = states[bi][0].T
    y_ref[...] = x + jnp.dot(o_ref[...], w_out_ref[...], preferred_element_type=F32)


def _bucket_table(q_off, k_off):
    n = q_off[:, None] - k_off[None, :]
    half = N_BUCKETS // 2
    max_exact = half // 2
    side = np.where(n < 0, half, 0)
    n = np.abs(n)
    nf = np.maximum(n, max_exact).astype(np.float32)
    large = max_exact + (np.log(nf / np.float32(max_exact)) / np.float32(math.log(MAX_DISTANCE / max_exact))
                         * np.float32(half - max_exact)).astype(np.int32)
    large = np.minimum(large, half - 1)
    return side + np.where(n < max_exact, n, large)


def _bias_rows(rel_bias, q_off, k_off):
    bucket = _bucket_table(q_off, k_off)
    onehot = jnp.asarray(np.eye(N_BUCKETS, dtype=np.float32)[bucket])
    bias = jnp.einsum('qkb,bh->hqk', onehot, rel_bias.astype(F32), precision=lax.Precision.HIGHEST) * LOG2E
    return bias.reshape(KVH_B, G_B * q_off.shape[0], k_off.shape[0])


def _head_expand_matrix(per_head):
    m = np.zeros((LANES, H_C * per_head), np.float32)
    for h in range(H_C):
        m[h, h * per_head:(h + 1) * per_head] = 1.0
    return jnp.asarray(np.concatenate([m, m], axis=0), BF16)


def _full(shape):
    return pl.BlockSpec(shape, lambda *_: (0,) * len(shape))


def _const(shape):
    return pl.BlockSpec(shape, lambda *_: (0,) * len(shape), pipeline_mode=pl.Buffered(1))


CAST_ROWS = 256


def _cast_columns(wt_ref, w_bf_ref, src, dst, n):
    for off in range(0, n - n % LANES, CAST_ROWS):
        rows = min(CAST_ROWS, n - n % LANES - off)
        w_bf_ref[:, dst + off:dst + off + rows] = wt_ref[src + off:src + off + rows, :].T.astype(BF16)
    rest = n % LANES
    if rest:
        off = n - rest
        tail = jnp.concatenate([wt_ref[src + off:src + n, :], jnp.zeros((LANES - rest, wt_ref.shape[1]), F32)],
                               axis=0)
        w_bf_ref[:, dst + off:dst + off + LANES] = tail.T.astype(BF16)


def _ab_cast_kernel(wt_ref, w_out_ref, w_in_bf_ref, w_out_bf_ref):
    g0 = A_Z
    _cast_columns(wt_ref, w_in_bf_ref, 0, 0, g0)
    _cast_columns(wt_ref, w_in_bf_ref, g0 + GATE_RANK_A, g0, A_G - g0)
    _cast_columns(wt_ref, w_in_bf_ref, g0, A_G, GATE_RANK_A)
    w_out_bf_ref[...] = w_out_ref[...].astype(BF16)


def _c_cast_kernel(wt_ref, w_out_ref, w_in_bf_ref, w_out_bf_ref):
    _cast_columns(wt_ref, w_in_bf_ref, 0, 0, wt_ref.shape[0])
    w_out_bf_ref[...] = w_out_ref[...].astype(BF16)


def _cast_weights(body, w_in, w_out, cols, name):
    vm = pl.BlockSpec(memory_space=pltpu.VMEM)
    return pl.pallas_call(
        body, in_specs=[vm, vm], out_specs=(vm, vm),
        out_shape=(jax.ShapeDtypeStruct((w_in.shape[0], cols), BF16),
                   jax.ShapeDtypeStruct(w_out.shape, BF16)),
        name=name, compiler_params=pltpu.CompilerParams(vmem_limit_bytes=VMEM_LIMIT),
    )(w_in.T, w_out)


def _ab_weights(rms_g, w_in, w_gate_up, b_gate, g_out, g_q, g_k, w_out):
    w_in_r, w_out_bf = _cast_weights(_ab_cast_kernel, w_in, w_out, AB_COLS, "ab_cast")
    w_gu = jnp.concatenate([w_gate_up, jnp.zeros((LANES - GATE_RANK_A, H_A * DK_A), w_gate_up.dtype)],
                           axis=0).astype(BF16)
    return (rms_g.reshape(1, D_MODEL), w_in_r, w_gu, b_gate.reshape(1, -1), g_out.reshape(1, DV_A),
            jnp.tile(g_q, H_B).reshape(1, -1), jnp.tile(g_k, KVH_B).reshape(1, -1), w_out_bf)


def _ab_prompt(x, sink, rel_bias, weights):
    rms_g, w_in, w_gu, b_gate, g_out, gq, gk, w_out = weights
    length = x.shape[0]
    t = AB_PROMPT_BLOCK
    c = CHUNK
    kb = WINDOW + c
    bias = _bias_rows(rel_bias, np.arange(c), np.arange(kb) - WINDOW)
    row_blk = lambda i: (i, 0)
    in_specs = [
        pl.BlockSpec(memory_space=pltpu.SMEM),
        pl.BlockSpec((t, D_MODEL), row_blk),
        _const((1, D_MODEL)), _const((D_MODEL, AB_COLS)), _const((LANES, 256)), _const((1, 256)),
        _const((1, DV_A)), _const((1, 512)), _const((1, 128)), _const((KVH_B, G_B * c, kb)),
        _const((D_MODEL, D_MODEL)),
    ]
    out_shape = (jax.ShapeDtypeStruct((length, D_MODEL), F32),
                 jax.ShapeDtypeStruct((H_A * DK_A, DV_A), F32),
                 jax.ShapeDtypeStruct((WINDOW, KVH_B * HD_B), F32),
                 jax.ShapeDtypeStruct((WINDOW, KVH_B * HD_B), F32))
    out_specs = (pl.BlockSpec((t, D_MODEL), row_blk), _full((H_A * DK_A, DV_A)),
                 _full((WINDOW, KVH_B * HD_B)), _full((WINDOW, KVH_B * HD_B)))
    scratch = [pltpu.VMEM((t, AB_COLS), F32), pltpu.VMEM((t, 512), F32), pltpu.VMEM((t, D_MODEL), BF16),
               pltpu.VMEM((KVH_B, t + WINDOW, 256), BF16), pltpu.VMEM((KVH_B, t + WINDOW, 256), BF16),
               pltpu.VMEM((H_A * DK_A, DV_A), F32)]
    return pl.pallas_call(
        _ab_prompt_kernel, grid=(length // t,), in_specs=in_specs, out_specs=out_specs,
        out_shape=out_shape, scratch_shapes=scratch, name="ab_prompt",
        compiler_params=pltpu.CompilerParams(dimension_semantics=("arbitrary",),
                                             vmem_limit_bytes=VMEM_LIMIT),
    )(sink, x, rms_g, w_in, w_gu, b_gate, g_out, gq, gk, bias, w_out)


def _ab_sample(x, sink, rel_bias, weights, s_in, k_cache, v_cache):
    rms_g, w_in, w_gu, b_gate, g_out, gq, gk, w_out = weights
    nb, c, _ = x.shape
    wc = k_cache.shape[1]
    kb = wc + c
    t = nb * c
    bias = _bias_rows(rel_bias, np.arange(c), np.arange(kb) - wc)
    vm = pl.BlockSpec(memory_space=pltpu.VMEM)
    in_specs = [pl.BlockSpec(memory_space=pltpu.SMEM)] + [vm] * 13
    out_shape = (jax.ShapeDtypeStruct((t, D_MODEL), F32),
                 jax.ShapeDtypeStruct((nb, H_A * DK_A, DV_A), F32),
                 jax.ShapeDtypeStruct((t, KVH_B * HD_B), F32),
                 jax.ShapeDtypeStruct((t, KVH_B * HD_B), F32))
    scratch = [pltpu.VMEM((t, AB_COLS), F32), pltpu.VMEM((t, 512), F32), pltpu.VMEM((t, D_MODEL), BF16),
               pltpu.VMEM((KVH_B, nb, kb, 256), BF16), pltpu.VMEM((KVH_B, nb, kb, 256), BF16)]
    return pl.pallas_call(
        _ab_sample_kernel, in_specs=in_specs, out_specs=(vm,) * 4, out_shape=out_shape,
        scratch_shapes=scratch, name="ab_sample",
        compiler_params=pltpu.CompilerParams(vmem_limit_bytes=VMEM_LIMIT),
    )(sink, x.reshape(t, D_MODEL), rms_g, w_in, w_gu, b_gate, g_out, gq, gk, bias, w_out,
      s_in.reshape(nb, H_A * DK_A, DV_A), k_cache.reshape(nb, wc, KVH_B * HD_B),
      v_cache.reshape(nb, wc, KVH_B * HD_B))


def _c_weights(rms_g, w_in, conv_w, conv_b, dt_bias, a_log, d_skip, g_y, w_out):
    w_in_p, w_out_bf = _cast_weights(_c_cast_kernel, w_in, w_out, C_COLS, "c_cast")
    pad = lambda v: jnp.concatenate([v, jnp.zeros((LANES - H_C,), v.dtype)]).reshape(1, LANES)
    return (rms_g.reshape(1, D_MODEL), w_in_p, conv_w, conv_b.reshape(1, CONV_DIM), pad(dt_bias), pad(a_log),
            jnp.repeat(d_skip, P_C).reshape(1, D_INNER_C), g_y.reshape(1, D_INNER_C), w_out_bf)


def _ssd_masks(c):
    tok = np.arange(c)[:, None]
    src = np.arange(H_C * c)[None, :] % c
    neg = np.where(src <= tok, 0.0, -np.inf).astype(np.float32)
    blk = (np.arange(4 * c)[:, None] // c) == (np.arange(256)[None, :] // P_C)
    return jnp.asarray(neg), jnp.asarray(blk, BF16)


def _c_prompt(x, weights):
    length = x.shape[0]
    t = PROMPT_BLOCK
    c = CHUNK
    n_blocks = length // t
    in_blk = lambda s: (jnp.minimum(s, n_blocks - 1), 0)
    out_blk = lambda s: (jnp.clip(s - 2, 0, n_blocks - 1), 0)
    in_specs = [
        pl.BlockSpec((t, D_MODEL), in_blk), pl.BlockSpec((t, D_MODEL), out_blk),
        _const((1, D_MODEL)), _const((D_MODEL, C_COLS)), _const((CONV_W, CONV_DIM)), _const((1, CONV_DIM)),
        _const((1, LANES)), _const((1, LANES)), _const((1, D_INNER_C)), _const((1, D_INNER_C)),
        _const((D_INNER_C, D_MODEL)), _const((2 * LANES, H_C * c)),
        _const((c, H_C * c)), _const((4 * c, 256)),
    ]
    out_shape = (jax.ShapeDtypeStruct((length, D_MODEL), F32),
                 jax.ShapeDtypeStruct((H_C * P_C, N_C), F32),
                 jax.ShapeDtypeStruct((8, CONV_DIM), F32))
    out_specs = (pl.BlockSpec((t, D_MODEL), out_blk), _full((H_C * P_C, N_C)), _full((8, CONV_DIM)))
    scratch = [pltpu.VMEM((2, t, C_COLS), F32), pltpu.VMEM((CONV_TILES, CONV_ROWS + 8, LANES), F32),
               pltpu.VMEM((CONV_TILES, CONV_ROWS, LANES), F32), pltpu.VMEM((2, t, LANES), F32),
               pltpu.VMEM((2, t, LANES), F32), pltpu.VMEM((2, t, D_INNER_C), BF16),
               pltpu.VMEM((N_C, H_C * P_C), F32)]
    return pl.pallas_call(
        _c_prompt_kernel, grid=(n_blocks + 2,), in_specs=in_specs, out_specs=out_specs,
        out_shape=out_shape, scratch_shapes=scratch, name="c_prompt",
        compiler_params=pltpu.CompilerParams(dimension_semantics=("arbitrary",),
                                             vmem_limit_bytes=VMEM_LIMIT),
    )(x, x, *weights, _head_expand_matrix(c), *_ssd_masks(c))


def _c_sample(x, weights, st_in, conv_in):
    nb, c, _ = x.shape
    t = nb * c
    vm = pl.BlockSpec(memory_space=pltpu.VMEM)
    out_shape = (jax.ShapeDtypeStruct((t, D_MODEL), F32),
                 jax.ShapeDtypeStruct((nb, H_C * P_C, N_C), F32),
                 jax.ShapeDtypeStruct((nb, CONV_W - 1, CONV_DIM), F32))
    scratch = [pltpu.VMEM((t, C_COLS), F32), pltpu.VMEM((c + 8, CONV_DIM), F32),
               pltpu.VMEM((t, CONV_DIM), F32), pltpu.VMEM((t, LANES), F32), pltpu.VMEM((t, LANES), F32),
               pltpu.VMEM((t, D_INNER_C), BF16)]
    return pl.pallas_call(
        _c_sample_kernel, in_specs=[vm] * 16, out_specs=(vm,) * 3, out_shape=out_shape,
        scratch_shapes=scratch, name="c_sample",
        compiler_params=pltpu.CompilerParams(vmem_limit_bytes=VMEM_LIMIT),
    )(x.reshape(t, D_MODEL), *weights, _head_expand_matrix(c), _head_expand_matrix(P_C), *_ssd_masks(c),
      st_in.reshape(nb, H_C * P_C, N_C), conv_in)


def kernel(x_prompt, x_sample, cache_swa_k, cache_swa_v, state_gla, state_ssd, state_conv, rms_g, w_in_ab, w_gate_up_a, b_gate_a, g_out_a, g_q_b, g_k_b, sink_b, rel_bias, w_out_ab, w_in_c, conv_w_c, conv_b_c, dt_bias_c, a_log_c, d_skip_c, g_y_c, w_out_c):
    bp, seq_len, _ = x_prompt.shape
    nb, dec_len, _ = x_sample.shape
    assert bp == 1 and seq_len % PROMPT_BLOCK == 0 and seq_len % AB_PROMPT_BLOCK == 0 and seq_len >= WINDOW
    wab = _ab_weights(rms_g[0], w_in_ab[0], w_gate_up_a[0], b_gate_a[0], g_out_a[0], g_q_b[0], g_k_b[0],
                      w_out_ab[0])
    yp, gla_p, k_p, v_p = _ab_prompt(x_prompt[0], sink_b[0], rel_bias, wab)
    ys, gla_s, k_s, v_s = _ab_sample(x_sample, sink_b[0], rel_bias, wab, state_gla[0], cache_swa_k[0],
                                     cache_swa_v[0])
    wc = _c_weights(rms_g[1], w_in_c[0], conv_w_c[0], conv_b_c[0], dt_bias_c[0], a_log_c[0], d_skip_c[0],
                    g_y_c[0], w_out_c[0])
    yp, ssd_p, conv_p = _c_prompt(yp, wc)
    ys, ssd_s, conv_s = _c_sample(ys.reshape(nb, dec_len, D_MODEL), wc, state_ssd[0], state_conv[0])
    return (
        yp.reshape(1, seq_len, D_MODEL),
        ys.reshape(nb, dec_len, D_MODEL),
        gla_p.reshape(1, 1, H_A, DK_A, DV_A),
        k_p.reshape(1, 1, WINDOW, KVH_B, HD_B),
        v_p.reshape(1, 1, WINDOW, KVH_B, HD_B),
        ssd_p.reshape(1, 1, H_C, P_C, N_C),
        conv_p[8 - (CONV_W - 1):].reshape(1, 1, CONV_W - 1, CONV_DIM),
        gla_s.reshape(1, nb, H_A, DK_A, DV_A),
        k_s.reshape(1, nb, dec_len, KVH_B, HD_B),
        v_s.reshape(1, nb, dec_len, KVH_B, HD_B),
        ssd_s.reshape(1, nb, H_C, P_C, N_C),
        conv_s.reshape(1, nb, CONV_W - 1, CONV_DIM),
    )
```

```python
import math
import types

import numpy as np
import jax
import jax.numpy as jnp
from jax import lax
from jax.experimental import pallas as pl
from jax.experimental.pallas import tpu as pltpu

F32 = jnp.float32
BF16 = jnp.bfloat16

D_MODEL = 1024
CHUNK = 64
EPS = 1e-6
H_A = 4
DK_A = 64
DV_A = 128
GATE_RANK_A = 16
GATE_NORM_A = 16.0
H_B = 8
KVH_B = 2
G_B = H_B // KVH_B
HD_B = 64
WINDOW = 128
N_BUCKETS = 32
MAX_DISTANCE = 128
D_INNER_C = 2048
P_C = 64
H_C = 32
G_C = 4
HPG_C = 8
N_C = 128
CONV_W = 4
CONV_DIM = D_INNER_C + 2 * G_C * N_C

LOG2E = 1.4426950408889634
LANES = 128

A_Q, A_K, A_V, A_Z = 0, 256, 512, 1024
B_Q, B_K, B_V, B_Z = 1536, 2048, 2176, 2304
A_G = 2816
AB_COLS = 2944
C_Z, C_X, C_B, C_C, C_DT = 0, 2048, 4096, 4608, 5120
C_COLS = 5248

PROMPT_BLOCK = 256
AB_PROMPT_BLOCK = 512
VMEM_LIMIT = 60 * 1024 * 1024


def _dot(a, b):
    return jnp.dot(a.astype(BF16), b.astype(BF16), preferred_element_type=F32)


def _dot_nt(a, b):
    return lax.dot_general(a.astype(BF16), b.astype(BF16), (((1,), (1,)), ((), ())),
                           preferred_element_type=F32)


def _dot_tn(a, b):
    return lax.dot_general(a.astype(BF16), b.astype(BF16), (((0,), (0,)), ((), ())),
                           preferred_element_type=F32)


def _split_bf16(x, terms):
    out = []
    r = x
    for _ in range(terms):
        h = r.astype(BF16)
        out.append(h)
        r = r - h.astype(F32)
    return out


def _sel_dot_left(sel, x, terms):
    acc = None
    for h in _split_bf16(x, terms):
        d = jnp.dot(sel, h, preferred_element_type=F32)
        acc = d if acc is None else acc + d
    return acc


def _rms_rows(x):
    return x * lax.rsqrt(jnp.mean(x * x, axis=-1, keepdims=True) + EPS)


def _exp_neg(x):
    return jnp.exp2(x * (-LOG2E))


def _silu(x):
    return x * (1.0 / (1.0 + _exp_neg(x)))


def _softplus(x):
    return jnp.maximum(x, 0.0) + jnp.log(1.0 + _exp_neg(jnp.abs(x)))


def _log_sigmoid(x):
    return jnp.minimum(x, 0.0) - jnp.log(1.0 + _exp_neg(jnp.abs(x)))


def _tril(n):
    r = lax.broadcasted_iota(jnp.int32, (n, n), 0)
    c = lax.broadcasted_iota(jnp.int32, (n, n), 1)
    return r >= c


def _head_mean_sq(x):
    out = []
    for p in range(x.shape[1] // LANES):
        sq = x[:, p * LANES:(p + 1) * LANES]
        sq = sq * sq
        low = lax.broadcasted_iota(jnp.int32, sq.shape, 1) < HD_B
        lo = jnp.sum(jnp.where(low, sq, 0.0), axis=-1, keepdims=True)
        hi = jnp.sum(jnp.where(low, 0.0, sq), axis=-1, keepdims=True)
        out.append(jnp.where(low, lo, hi))
    return jnp.concatenate(out, axis=1) * (1.0 / HD_B)


def _tile_kv_heads(kv):
    low = lax.broadcasted_iota(jnp.int32, kv.shape, 1) < HD_B
    swapped = pltpu.roll(kv, HD_B, 1)
    out = []
    for base in (jnp.where(low, kv, swapped), jnp.where(low, swapped, kv)):
        base = base.astype(BF16)
        out.append(jnp.concatenate([base, base], axis=1))
    return out


def _ab_dense_in(x, rms_g, w_in_ref, gq, gk, proj_ref, qn_ref):
    h = (_rms_rows(x) * rms_g).astype(BF16)
    rows = min(256, x.shape[0])
    for m in range(0, x.shape[0], rows):
        proj_ref[m:m + rows, :] = jnp.dot(h[m:m + rows], w_in_ref[...], preferred_element_type=F32)
    qb = proj_ref[:, B_Q:B_Q + 512]
    qn_ref[...] = qb * lax.rsqrt(_head_mean_sq(qb) + EPS) * gq
    kb = proj_ref[:, B_K:B_K + 128]
    kn = kb * lax.rsqrt(_head_mean_sq(kb) + EPS) * gk
    vb = proj_ref[:, B_V:B_V + 128]
    return kn, vb


def _advance(gens, yielded=None):
    alive = []
    for gen in gens:
        try:
            value = next(gen)
            alive.append(gen)
            if yielded is not None and value is not None:
                yielded.append(value)
        except StopIteration:
            pass
    return alive


def _interleave(gens):
    gens = list(gens)
    while gens:
        gens = _advance(gens)


def _ab_chunk(r, c, kb, proj_ref, qn_ref, o_ref, state, k_band, v_band, bias_ref, sink_ref,
              w_gu, b_gate, g_out, first_valid_col):
    rows = pl.ds(r, c)
    gate = _dot(proj_ref[rows, A_G:A_G + LANES], w_gu) + b_gate
    yield
    g = _log_sigmoid(gate) * (LOG2E / GATE_NORM_A)
    tril = _tril(c)
    b = _sel_dot_left(tril.astype(BF16), g, 3)
    bl = b[c - 1:c, :]
    bl_tile = jnp.broadcast_to(bl, (LANES, H_A * DK_A))
    bl_rows = jnp.concatenate([bl_tile[:, i * LANES:(i + 1) * LANES].T
                               for i in range(H_A * DK_A // LANES)], axis=0)
    yield
    q = proj_ref[rows, A_Q:A_Q + 256] * (DK_A ** -0.5)
    k = proj_ref[rows, A_K:A_K + 256]
    v = proj_ref[rows, A_V:A_V + 512].astype(BF16)
    qe = (q * jnp.exp2(b)).astype(BF16)
    kd = k * jnp.exp2(-b)
    kd2 = (k * jnp.exp2(bl - b)).astype(BF16)
    lane_head = lax.broadcasted_iota(jnp.int32, (c, LANES), 1) // DK_A
    row_head = lax.broadcasted_iota(jnp.int32, (LANES, LANES), 0) // DK_A
    att = []
    for p in range(2):
        kp = kd[:, p * 128:(p + 1) * 128]
        kpair = jnp.concatenate([jnp.where(lane_head == j, kp, 0.0) for j in range(2)], axis=0).astype(BF16)
        att.append(_dot_nt(qe[:, p * 128:(p + 1) * 128], kpair))
    upd = [_dot_tn(kd2[:, p * 128:(p + 1) * 128], v[:, p * 256:(p + 1) * 256]) for p in range(2)]
    yield
    row = lax.broadcasted_iota(jnp.int32, (c, 2 * c), 0)
    col = lax.broadcasted_iota(jnp.int32, (c, 2 * c), 1) % c
    att = [jnp.where(row >= col, a, 0.0).astype(BF16) for a in att]
    yield
    s_prev = state[0]
    s_new, oh = [], []
    zeros_v = jnp.zeros((c, DV_A), BF16)
    for p in range(2):
        sp = s_prev[p * 128:(p + 1) * 128, :]
        s_bd = jnp.concatenate([jnp.where(row_head == j, sp, 0.0) for j in range(2)], axis=1).astype(BF16)
        v0, v1 = v[:, 2 * p * DV_A:(2 * p + 1) * DV_A], v[:, (2 * p + 1) * DV_A:(2 * p + 2) * DV_A]
        v_bd = jnp.concatenate([jnp.concatenate([v0, zeros_v], axis=1),
                                jnp.concatenate([zeros_v, v1], axis=1)], axis=0)
        qp = qe[:, p * 128:(p + 1) * 128]
        if (2 * c) % LANES == 0:
            o_pair = jnp.dot(jnp.concatenate([att[p], qp], axis=1), jnp.concatenate([v_bd, s_bd], axis=0),
                             preferred_element_type=F32)
        else:
            o_pair = (jnp.dot(att[p], v_bd, preferred_element_type=F32)
                      + jnp.dot(qp, s_bd, preferred_element_type=F32))
        oh += [o_pair[:, :DV_A], o_pair[:, DV_A:]]
        u = jnp.where(row_head == 0, upd[p][:, :128], upd[p][:, 128:])
        s_new.append(jnp.exp2(bl_rows[p * 128:(p + 1) * 128, :]) * sp + u)
    state[0] = jnp.concatenate(s_new, axis=0)
    yield
    for hd in range(H_A):
        z = proj_ref[rows, A_Z + hd * 128:A_Z + (hd + 1) * 128]
        o_ref[rows, hd * 128:(hd + 1) * 128] = (_rms_rows(oh[hd]) * g_out * _silu(z)).astype(BF16)
    yield
    lane256 = lax.broadcasted_iota(jnp.int32, (c, 256), 1) // HD_B
    srow = lax.broadcasted_iota(jnp.int32, (G_B * c, 1), 0) // c
    scores = []
    for kvh in range(KVH_B):
        qn = qn_ref[rows, kvh * 256:(kvh + 1) * 256]
        qs = jnp.concatenate([jnp.where(lane256 == gq_, qn, 0.0) for gq_ in range(G_B)], axis=0)
        scores.append(_dot_nt(qs, k_band(kvh)))
    yield
    probs, dens = [], []
    for kvh in range(KVH_B):
        s = scores[kvh] * (LOG2E * HD_B ** -0.5) + bias_ref[kvh]
        if first_valid_col is not None:
            col = lax.broadcasted_iota(jnp.int32, (G_B * c, kb), 1)
            s = jnp.where(col >= first_valid_col, s, -jnp.inf)
        sink = jnp.zeros((G_B * c, 1), F32)
        for gq_ in range(G_B):
            sink = jnp.where(srow == gq_, sink_ref[kvh * G_B + gq_] * LOG2E, sink)
        m = jnp.maximum(jnp.max(s, axis=-1, keepdims=True), sink)
        pr = jnp.exp2(s - m)
        dens.append(jnp.sum(pr, axis=-1, keepdims=True) + jnp.exp2(sink - m))
        probs.append(pr.astype(BF16))
    yield
    outs = [_dot(probs[kvh], v_band(kvh)) for kvh in range(KVH_B)]
    yield
    for kvh in range(KVH_B):
        ost = outs[kvh] / dens[kvh]
        ob = jnp.zeros((c, 256), F32)
        for gq_ in range(G_B):
            ob = ob + jnp.where(lane256 == gq_, ost[gq_ * c:(gq_ + 1) * c, :], 0.0)
        z = proj_ref[rows, B_Z + kvh * 256:B_Z + (kvh + 1) * 256]
        o_ref[rows, 512 + kvh * 256:512 + (kvh + 1) * 256] = (ob * _silu(z)).astype(BF16)


def _ab_prompt_kernel(sink_ref, x_ref, rms_g_ref, w_in_ref, w_gu_ref, b_gate_ref, g_out_ref, gq_ref,
                      gk_ref, bias_ref, w_out_ref,
                      y_ref, s_out_ref, k_out_ref, v_out_ref,
                      proj_ref, qn_ref, o_ref, kband_ref, vband_ref, s_ref):
    t = x_ref.shape[0]
    c = CHUNK
    kb = WINDOW + c
    nchunk = t // c
    step = pl.program_id(0)

    @pl.when(step == 0)
    def _():
        s_ref[...] = jnp.zeros_like(s_ref)
        kband_ref[:, t:t + WINDOW, :] = jnp.zeros((KVH_B, WINDOW, 256), BF16)
        vband_ref[:, t:t + WINDOW, :] = jnp.zeros((KVH_B, WINDOW, 256), BF16)

    for kvh in range(KVH_B):
        kband_ref[kvh, 0:WINDOW, :] = kband_ref[kvh, t:t + WINDOW, :]
        vband_ref[kvh, 0:WINDOW, :] = vband_ref[kvh, t:t + WINDOW, :]

    x = x_ref[...]
    kn, vb = _ab_dense_in(x, rms_g_ref[...], w_in_ref, gq_ref[...], gk_ref[...], proj_ref, qn_ref)
    k_out_ref[...] = kn[t - WINDOW:, :]
    v_out_ref[...] = vb[t - WINDOW:, :]
    for kvh, (kt, vt) in enumerate(zip(_tile_kv_heads(kn), _tile_kv_heads(vb))):
        kband_ref[kvh, WINDOW:WINDOW + t, :] = kt
        vband_ref[kvh, WINDOW:WINDOW + t, :] = vt

    w_gu = w_gu_ref[...]
    b_gate = b_gate_ref[...]
    g_out = g_out_ref[...]

    state = [s_ref[...]]

    def chunk(i):
        r = i * c
        return _ab_chunk(r, c, kb, proj_ref, qn_ref, o_ref, state,
                         lambda kvh: kband_ref[kvh, pl.ds(r, kb), :],
                         lambda kvh: vband_ref[kvh, pl.ds(r, kb), :],
                         bias_ref, sink_ref, w_gu, b_gate, g_out, (2 - (step * nchunk + i)) * c)

    _interleave(chunk(i) for i in range(nchunk))
    s_ref[...] = state[0]
    y_ref[...] = x + jnp.dot(o_ref[...], w_out_ref[...], preferred_element_type=F32)
    s_out_ref[...] = s_ref[...]


def _ab_sample_kernel(sink_ref, x_ref, rms_g_ref, w_in_ref, w_gu_ref, b_gate_ref, g_out_ref, gq_ref,
                      gk_ref, bias_ref, w_out_ref,
                      s_in_ref, kc_ref, vc_ref,
                      y_ref, s_out_ref, k_out_ref, v_out_ref,
                      proj_ref, qn_ref, o_ref, kband_ref, vband_ref):
    nb, wc = kc_ref.shape[0], kc_ref.shape[1]
    t = x_ref.shape[0]
    c = t // nb
    kb = wc + c
    x = x_ref[...]
    kn, vb = _ab_dense_in(x, rms_g_ref[...], w_in_ref, gq_ref[...], gk_ref[...], proj_ref, qn_ref)
    k_out_ref[...] = kn
    v_out_ref[...] = vb
    k_new, v_new = _tile_kv_heads(kn), _tile_kv_heads(vb)
    for bi in range(nb):
        k_old, v_old = _tile_kv_heads(kc_ref[bi]), _tile_kv_heads(vc_ref[bi])
        for kvh in range(KVH_B):
            kband_ref[kvh, bi, 0:wc, :] = k_old[kvh]
            vband_ref[kvh, bi, 0:wc, :] = v_old[kvh]
            kband_ref[kvh, bi, wc:kb, :] = k_new[kvh][bi * c:(bi + 1) * c, :]
            vband_ref[kvh, bi, wc:kb, :] = v_new[kvh][bi * c:(bi + 1) * c, :]

    w_gu = w_gu_ref[...]
    b_gate = b_gate_ref[...]
    g_out = g_out_ref[...]

    states = [[s_in_ref[bi]] for bi in range(nb)]

    def seq(bi):
        return _ab_chunk(bi * c, c, kb, proj_ref, qn_ref, o_ref, states[bi],
                         lambda kvh: kband_ref[kvh, bi], lambda kvh: vband_ref[kvh, bi],
                         bias_ref, sink_ref, w_gu, b_gate, g_out, None)

    _interleave(seq(bi) for bi in range(nb))
    for bi in range(nb):
        s_out_ref[bi] = states[bi][0]
    y_ref[...] = x + jnp.dot(o_ref[...], w_out_ref[...], preferred_element_type=F32)


GROUP_W = D_INNER_C // G_C


def _c_chunk(r, c, io, state, e_s, e_p, neg_mask, bd_mask, dskip, g_y):
    dtc = io.dt(r)
    acum = _sel_dot_left(_tril(c).astype(BF16), io.da(r), 3) * LOG2E
    yield acum
    lhs = jnp.concatenate([jnp.concatenate(_split_bf16(acum, 2), axis=1),
                           jnp.concatenate(_split_bf16(dtc, 2), axis=1)], axis=0)
    both_p = jnp.dot(lhs, e_p, preferred_element_type=F32)
    xa_p, dt_p = both_p[:c], both_p[c:]
    xa_s = xa_p if c == P_C else jnp.dot(lhs[:c], e_s, preferred_element_type=F32)
    yield dt_p
    acum_t = acum.T
    a_row = jnp.concatenate([acum_t[h:h + 1, :] for h in range(H_C)], axis=1)
    wmat = jnp.exp2((xa_s - a_row) + neg_mask)
    al_p = xa_p[c - 1:c, :]
    xs = io.x(r)
    xdt = xs * dt_p
    xdt_bf = xdt.astype(BF16)
    xw = (xdt * jnp.exp2(al_p - xa_p)).astype(BF16)
    dec = jnp.exp2(al_p)
    bg, cg, cb = [], [], []
    for g in range(G_C):
        bg.append(io.b(r, g).astype(BF16))
        cg.append(io.c(r, g).astype(BF16))
        cb.append(_dot_nt(cg[g], jnp.concatenate([bg[g]] * HPG_C, axis=0)))
    yield cb[-1]
    ys = []
    for g in range(G_C):
        mg = (cb[g] * wmat[:, g * HPG_C * c:(g + 1) * HPG_C * c]).astype(BF16)
        for j in range(2):
            xj = xdt_bf[:, g * GROUP_W + j * 256:g * GROUP_W + (j + 1) * 256]
            bd = jnp.concatenate([xj] * 4, axis=0) * bd_mask
            ys.append(jnp.dot(mg[:, j * 4 * c:(j + 1) * 4 * c], bd, preferred_element_type=F32))
    upd = [_dot_tn(bg[g], xw[:, g * GROUP_W:(g + 1) * GROUP_W]) for g in range(G_C)]
    yield upd[-1]
    st_prev = state[0]
    y_inter, st_new = [], []
    for g in range(G_C):
        sl = slice(g * GROUP_W, (g + 1) * GROUP_W)
        y_inter.append(_dot(cg[g], st_prev[:, sl]))
        st_new.append(st_prev[:, sl] * dec[:, sl] + upd[g])
    state[0] = jnp.concatenate(st_new, axis=1)
    yield st_new[-1]
    for g in range(G_C):
        sl = slice(g * GROUP_W, (g + 1) * GROUP_W)
        y = jnp.concatenate(ys[2 * g:2 * g + 2], axis=1) + y_inter[g] * jnp.exp2(xa_p[:, sl])
        y = y + dskip[:, sl] * xs[:, sl]
        y = y * _silu(io.z(r, g))
        io.put_o(r, g, (_rms_rows(y) * g_y[:, sl]).astype(BF16))


def _c_dt(dt_cols, dt_bias, a_log):
    dt = _softplus(dt_cols + dt_bias)
    return dt, dt * (-jnp.exp(a_log))


CONV_PITCH = PROMPT_BLOCK // 8 + 1
CONV_ROWS = 8 * CONV_PITCH
CONV_TILES = CONV_DIM // LANES
C_CONV_ROUNDS = 6
C_ROUNDS = C_CONV_ROUNDS + 6
C_IN_UNITS = (4, 4, 4, 4, 4, 4, 4, 2, 4, 2, 2, 3)
C_OUT_ROUNDS = (3, 5, 8, 10)
assert len(C_IN_UNITS) == C_ROUNDS and sum(C_IN_UNITS) * LANES == C_COLS


def _conv_tile(ubuf_ref, act_ref, j, conv_w, conv_b):
    w = [jnp.broadcast_to(conv_w[i:i + 1, j * LANES:(j + 1) * LANES], (8, LANES)) for i in range(CONV_W)]
    b = jnp.broadcast_to(conv_b[:, j * LANES:(j + 1) * LANES], (8, LANES))
    for a in range(CONV_PITCH):
        acc = b
        for i in range(CONV_W):
            acc = acc + w[i] * ubuf_ref[j, pl.ds(8 - (CONV_W - 1) + i + a, 8, stride=CONV_PITCH), :]
        act_ref[j, pl.ds(a, 8, stride=CONV_PITCH), :] = _silu(acc)


def _c_prompt_kernel(xin_ref, xres_ref, rms_g_ref, w_in_ref, conv_w_ref, conv_b_ref, dt_bias_ref,
                     a_log_ref, dskip_ref, g_y_ref, w_out_ref, e_s_ref, neg_mask_ref, bd_mask_ref,
                     y_ref, st_out_ref, conv_out_ref,
                     proj_ref, ubuf_ref, act_ref, dt_ref, da_ref, o_ref, st_ref):
    t = xin_ref.shape[0]
    c = CHUNK
    s = pl.program_id(0)
    n_blocks = pl.num_programs(0) - 2

    @pl.when(s == 0)
    def _():
        o_ref[1] = jnp.zeros(o_ref.shape[1:], BF16)
        ubuf_ref[...] = jnp.zeros_like(ubuf_ref)

    @pl.when(s <= 1)
    def _():
        st_ref[...] = jnp.zeros_like(st_ref)
        ubuf_ref[:, CONV_ROWS:CONV_ROWS + 8, :] = jnp.zeros((CONV_TILES, 8, LANES), F32)

    def in_stage(slot_in):
        h = (_rms_rows(xin_ref[...]) * rms_g_ref[...]).astype(BF16)
        lo = 0
        for units in C_IN_UNITS:
            if lo:
                yield
            hi = lo + units * LANES
            slab = jnp.dot(h, w_in_ref[:, lo:hi], preferred_element_type=F32)
            proj_ref[slot_in, :, lo:hi] = slab
            if hi == C_COLS:
                dt, da = _c_dt(slab[:, C_DT - lo:], dt_bias_ref[...], a_log_ref[...])
                dt_ref[slot_in] = dt
                da_ref[slot_in] = da
            lo = hi

    def act_rows(r, j):
        return act_ref[j, pl.ds(8 + r, c), :]

    n_x = D_INNER_C // LANES
    n_g = N_C // LANES

    def mix_stage(slot_mix):
        def put_o(r, g, value):
            o_ref[slot_mix, pl.ds(r, c), g * GROUP_W:(g + 1) * GROUP_W] = value

        io = types.SimpleNamespace(
            dt=lambda r: dt_ref[slot_mix, pl.ds(r, c), :],
            da=lambda r: da_ref[slot_mix, pl.ds(r, c), :],
            x=lambda r: jnp.concatenate([act_rows(r, j) for j in range(n_x)], axis=1),
            b=lambda r, g: act_rows(r, n_x + g * n_g),
            c=lambda r, g: act_rows(r, n_x + (G_C + g) * n_g),
            z=lambda r, g: proj_ref[slot_mix, pl.ds(r, c), C_Z + g * GROUP_W:C_Z + (g + 1) * GROUP_W],
            put_o=put_o)
        conv_w = conv_w_ref[...]
        conv_b = conv_b_ref[...]
        conv_rounds = C_CONV_ROUNDS
        per_round = CONV_TILES // conv_rounds
        for rnd in range(conv_rounds):
            if rnd:
                yield
            for j in range(rnd * per_round, (rnd + 1) * per_round):
                ubuf_ref[j, 8:16, :] = ubuf_ref[j, CONV_ROWS:CONV_ROWS + 8, :]
                ubuf_ref[j, 16:16 + t, :] = proj_ref[slot_mix, :, C_X + j * LANES:C_X + (j + 1) * LANES]
                _conv_tile(ubuf_ref, act_ref, j, conv_w, conv_b)
        state = [st_ref[...]]
        e_s = e_s_ref[...]
        chunks = [_c_chunk(i * c, c, io, state, e_s, e_s, neg_mask_ref[...], bd_mask_ref[...],
                           dskip_ref[...], g_y_ref[...]) for i in range(t // c)]
        while chunks:
            yield
            chunks = _advance(chunks)
        st_ref[...] = state[0]

    def out_stage(slot_in):
        quarter = 0
        for rnd in range(C_ROUNDS):
            if rnd:
                yield
            if rnd in C_OUT_ROUNDS:
                cols = slice(quarter * 256, (quarter + 1) * 256)
                y_ref[:, cols] = xres_ref[:, cols] + jnp.dot(o_ref[slot_in], w_out_ref[:, cols],
                                                             preferred_element_type=F32)
                quarter += 1

    slot_in = s % 2

    @pl.when(s == 0)
    def _():
        _interleave([in_stage(slot_in)])

    @pl.when(jnp.logical_and(s > 0, s <= n_blocks))
    def _():
        _interleave([in_stage(slot_in), mix_stage(1 - slot_in), out_stage(slot_in)])

    @pl.when(s == n_blocks + 1)
    def _():
        _interleave([out_stage(slot_in)])

    @pl.when(s == n_blocks)
    def _():
        st_out_ref[...] = st_ref[...].T
        conv_out_ref[...] = proj_ref[1 - s % 2, t - 8:t, C_X:C_X + CONV_DIM]


def _conv_rows(ubuf, nrows, conv_w, conv_b):
    acc = conv_b
    for i in range(CONV_W):
        acc = acc + conv_w[i:i + 1, :] * ubuf[pl.ds(8 - (CONV_W - 1) + i, nrows), :]
    return _silu(acc)


def _c_sample_kernel(x_ref, rms_g_ref, w_in_ref, conv_w_ref, conv_b_ref, dt_bias_ref, a_log_ref,
                     dskip_ref, g_y_ref, w_out_ref, e_s_ref, e_p_ref, neg_mask_ref, bd_mask_ref,
                     st_in_ref, conv_in_ref,
                     y_ref, st_out_ref, conv_out_ref,
                     proj_ref, ubuf_ref, act_ref, dt_ref, da_ref, o_ref):
    nb = st_in_ref.shape[0]
    t = x_ref.shape[0]
    c = t // nb
    x = x_ref[...]
    h = (_rms_rows(x) * rms_g_ref[...]).astype(BF16)
    proj_ref[...] = jnp.dot(h, w_in_ref[...], preferred_element_type=F32)
    dt, da = _c_dt(proj_ref[:, C_DT:C_DT + LANES], dt_bias_ref[...], a_log_ref[...])
    dt_ref[...] = dt
    da_ref[...] = da
    conv_w = conv_w_ref[...]
    conv_b = conv_b_ref[...]
    ubuf_ref[0:8, :] = jnp.zeros((8, CONV_DIM), F32)
    for bi in range(nb):
        ubuf_ref[8 - (CONV_W - 1):8, :] = conv_in_ref[bi]
        ubuf_ref[8:8 + c, :] = proj_ref[bi * c:(bi + 1) * c, C_X:C_X + CONV_DIM]
        act_ref[bi * c:(bi + 1) * c, :] = _conv_rows(ubuf_ref, c, conv_w, conv_b)
        conv_out_ref[bi] = ubuf_ref[8 + c - (CONV_W - 1):8 + c, :]

    def put_o(r, g, value):
        o_ref[pl.ds(r, c), g * GROUP_W:(g + 1) * GROUP_W] = value

    io = types.SimpleNamespace(
        dt=lambda r: dt_ref[pl.ds(r, c), :],
        da=lambda r: da_ref[pl.ds(r, c), :],
        x=lambda r: act_ref[pl.ds(r, c), 0:D_INNER_C],
        b=lambda r, g: act_ref[pl.ds(r, c), D_INNER_C + g * N_C:D_INNER_C + (g + 1) * N_C],
        c=lambda r, g: act_ref[pl.ds(r, c), D_INNER_C + (G_C + g) * N_C:D_INNER_C + (G_C + g + 1) * N_C],
        z=lambda r, g: proj_ref[pl.ds(r, c), C_Z + g * GROUP_W:C_Z + (g + 1) * GROUP_W],
        put_o=put_o)
    e_s = e_s_ref[...]
    e_p = e_p_ref[...]
    neg_mask = neg_mask_ref[...]
    bd_mask = bd_mask_ref[...]
    dskip = dskip_ref[...]
    g_y = g_y_ref[...]

    states = [[st_in_ref[bi].T] for bi in range(nb)]
    _interleave(_c_chunk(bi * c, c, io, states[bi], e_s, e_p, neg_mask, bd_mask, dskip, g_y)
                for bi in range(nb))
    for bi in range(nb):
        st_out_ref[bi] = states[bi][0].T
    y_ref[...] = x + jnp.dot(o_ref[...], w_out_ref[...], preferred_element_type=F32)


def _bucket_table(q_off, k_off):
    n = q_off[:, None] - k_off[None, :]
    half = N_BUCKETS // 2
    max_exact = half // 2
    side = np.where(n < 0, half, 0)
    n = np.abs(n)
    nf = np.maximum(n, max_exact).astype(np.float32)
    large = max_exact + (np.log(nf / np.float32(max_exact)) / np.float32(math.log(MAX_DISTANCE / max_exact))
                         * np.float32(half - max_exact)).astype(np.int32)
    large = np.minimum(large, half - 1)
    return side + np.where(n < max_exact, n, large)


def _bias_rows(rel_bias, q_off, k_off):
    bucket = _bucket_table(q_off, k_off)
    onehot = jnp.asarray(np.eye(N_BUCKETS, dtype=np.float32)[bucket])
    bias = jnp.einsum('qkb,bh->hqk', onehot, rel_bias.astype(F32), precision=lax.Precision.HIGHEST) * LOG2E
    return bias.reshape(KVH_B, G_B * q_off.shape[0], k_off.shape[0])


def _head_expand_matrix(per_head):
    m = np.zeros((LANES, H_C * per_head), np.float32)
    for h in range(H_C):
        m[h, h * per_head:(h + 1) * per_head] = 1.0
    return jnp.asarray(np.concatenate([m, m], axis=0), BF16)


def _full(shape):
    return pl.BlockSpec(shape, lambda *_: (0,) * len(shape))


def _const(shape):
    return pl.BlockSpec(shape, lambda *_: (0,) * len(shape), pipeline_mode=pl.Buffered(1))


CAST_ROWS = 256


def _cast_columns(wt_ref, w_bf_ref, src, dst, n):
    for off in range(0, n - n % LANES, CAST_ROWS):
        rows = min(CAST_ROWS, n - n % LANES - off)
        w_bf_ref[:, dst + off:dst + off + rows] = wt_ref[src + off:src + off + rows, :].T.astype(BF16)
    rest = n % LANES
    if rest:
        off = n - rest
        tail = jnp.concatenate([wt_ref[src + off:src + n, :], jnp.zeros((LANES - rest, wt_ref.shape[1]), F32)],
                               axis=0)
        w_bf_ref[:, dst + off:dst + off + LANES] = tail.T.astype(BF16)


def _ab_cast_kernel(wt_ref, w_out_ref, w_in_bf_ref, w_out_bf_ref):
    g0 = A_Z
    _cast_columns(wt_ref, w_in_bf_ref, 0, 0, g0)
    _cast_columns(wt_ref, w_in_bf_ref, g0 + GATE_RANK_A, g0, A_G - g0)
    _cast_columns(wt_ref, w_in_bf_ref, g0, A_G, GATE_RANK_A)
    w_out_bf_ref[...] = w_out_ref[...].astype(BF16)


def _c_cast_kernel(wt_ref, w_out_ref, w_in_bf_ref, w_out_bf_ref):
    _cast_columns(wt_ref, w_in_bf_ref, 0, 0, wt_ref.shape[0])
    w_out_bf_ref[...] = w_out_ref[...].astype(BF16)


def _cast_weights(body, w_in, w_out, cols, name):
    vm = pl.BlockSpec(memory_space=pltpu.VMEM)
    return pl.pallas_call(
        body, in_specs=[vm, vm], out_specs=(vm, vm),
        out_shape=(jax.ShapeDtypeStruct((w_in.shape[0], cols), BF16),
                   jax.ShapeDtypeStruct(w_out.shape, BF16)),
        name=name, compiler_params=pltpu.CompilerParams(vmem_limit_bytes=VMEM_LIMIT),
    )(w_in.T, w_out)


def _ab_weights(rms_g, w_in, w_gate_up, b_gate, g_out, g_q, g_k, w_out):
    w_in_r, w_out_bf = _cast_weights(_ab_cast_kernel, w_in, w_out, AB_COLS, "ab_cast")
    w_gu = jnp.concatenate([w_gate_up, jnp.zeros((LANES - GATE_RANK_A, H_A * DK_A), w_gate_up.dtype)],
                           axis=0).astype(BF16)
    return (rms_g.reshape(1, D_MODEL), w_in_r, w_gu, b_gate.reshape(1, -1), g_out.reshape(1, DV_A),
            jnp.tile(g_q, H_B).reshape(1, -1), jnp.tile(g_k, KVH_B).reshape(1, -1), w_out_bf)


def _ab_prompt(x, sink, rel_bias, weights):
    rms_g, w_in, w_gu, b_gate, g_out, gq, gk, w_out = weights
    length = x.shape[0]
    t = AB_PROMPT_BLOCK
    c = CHUNK
    kb = WINDOW + c
    bias = _bias_rows(rel_bias, np.arange(c), np.arange(kb) - WINDOW)
    row_blk = lambda i: (i, 0)
    in_specs = [
        pl.BlockSpec(memory_space=pltpu.SMEM),
        pl.BlockSpec((t, D_MODEL), row_blk),
        _const((1, D_MODEL)), _const((D_MODEL, AB_COLS)), _const((LANES, 256)), _const((1, 256)),
        _const((1, DV_A)), _const((1, 512)), _const((1, 128)), _const((KVH_B, G_B * c, kb)),
        _const((D_MODEL, D_MODEL)),
    ]
    out_shape = (jax.ShapeDtypeStruct((length, D_MODEL), F32),
                 jax.ShapeDtypeStruct((H_A * DK_A, DV_A), F32),
                 jax.ShapeDtypeStruct((WINDOW, KVH_B * HD_B), F32),
                 jax.ShapeDtypeStruct((WINDOW, KVH_B * HD_B), F32))
    out_specs = (pl.BlockSpec((t, D_MODEL), row_blk), _full((H_A * DK_A, DV_A)),
                 _full((WINDOW, KVH_B * HD_B)), _full((WINDOW, KVH_B * HD_B)))
    scratch = [pltpu.VMEM((t, AB_COLS), F32), pltpu.VMEM((t, 512), F32), pltpu.VMEM((t, D_MODEL), BF16),
               pltpu.VMEM((KVH_B, t + WINDOW, 256), BF16), pltpu.VMEM((KVH_B, t + WINDOW, 256), BF16),
               pltpu.VMEM((H_A * DK_A, DV_A), F32)]
    return pl.pallas_call(
        _ab_prompt_kernel, grid=(length // t,), in_specs=in_specs, out_specs=out_specs,
        out_shape=out_shape, scratch_shapes=scratch, name="ab_prompt",
        compiler_params=pltpu.CompilerParams(dimension_semantics=("arbitrary",),
                                             vmem_limit_bytes=VMEM_LIMIT),
    )(sink, x, rms_g, w_in, w_gu, b_gate, g_out, gq, gk, bias, w_out)


def _ab_sample(x, sink, rel_bias, weights, s_in, k_cache, v_cache):
    rms_g, w_in, w_gu, b_gate, g_out, gq, gk, w_out = weights
    nb, c, _ = x.shape
    wc = k_cache.shape[1]
    kb = wc + c
    t = nb * c
    bias = _bias_rows(rel_bias, np.arange(c), np.arange(kb) - wc)
    vm = pl.BlockSpec(memory_space=pltpu.VMEM)
    in_specs = [pl.BlockSpec(memory_space=pltpu.SMEM)] + [vm] * 13
    out_shape = (jax.ShapeDtypeStruct((t, D_MODEL), F32),
                 jax.ShapeDtypeStruct((nb, H_A * DK_A, DV_A), F32),
                 jax.ShapeDtypeStruct((t, KVH_B * HD_B), F32),
                 jax.ShapeDtypeStruct((t, KVH_B * HD_B), F32))
    scratch = [pltpu.VMEM((t, AB_COLS), F32), pltpu.VMEM((t, 512), F32), pltpu.VMEM((t, D_MODEL), BF16),
               pltpu.VMEM((KVH_B, nb, kb, 256), BF16), pltpu.VMEM((KVH_B, nb, kb, 256), BF16)]
    return pl.pallas_call(
        _ab_sample_kernel, in_specs=in_specs, out_specs=(vm,) * 4, out_shape=out_shape,
        scratch_shapes=scratch, name="ab_sample",
        compiler_params=pltpu.CompilerParams(vmem_limit_bytes=VMEM_LIMIT),
    )(sink, x.reshape(t, D_MODEL), rms_g, w_in, w_gu, b_gate, g_out, gq, gk, bias, w_out,
      s_in.reshape(nb, H_A * DK_A, DV_A), k_cache.reshape(nb, wc, KVH_B * HD_B),
      v_cache.reshape(nb, wc, KVH_B * HD_B))


def _c_weights(rms_g, w_in, conv_w, conv_b, dt_bias, a_log, d_skip, g_y, w_out):
    w_in_p, w_out_bf = _cast_weights(_c_cast_kernel, w_in, w_out, C_COLS, "c_cast")
    pad = lambda v: jnp.concatenate([v, jnp.zeros((LANES - H_C,), v.dtype)]).reshape(1, LANES)
    return (rms_g.reshape(1, D_MODEL), w_in_p, conv_w, conv_b.reshape(1, CONV_DIM), pad(dt_bias), pad(a_log),
            jnp.repeat(d_skip, P_C).reshape(1, D_INNER_C), g_y.reshape(1, D_INNER_C), w_out_bf)


def _ssd_masks(c):
    tok = np.arange(c)[:, None]
    src = np.arange(H_C * c)[None, :] % c
    neg = np.where(src <= tok, 0.0, -np.inf).astype(np.float32)
    blk = (np.arange(4 * c)[:, None] // c) == (np.arange(256)[None, :] // P_C)
    return jnp.asarray(neg), jnp.asarray(blk, BF16)


def _c_prompt(x, weights):
    length = x.shape[0]
    t = PROMPT_BLOCK
    c = CHUNK
    n_blocks = length // t
    in_blk = lambda s: (jnp.minimum(s, n_blocks - 1), 0)
    out_blk = lambda s: (jnp.clip(s - 2, 0, n_blocks - 1), 0)
    in_specs = [
        pl.BlockSpec((t, D_MODEL), in_blk), pl.BlockSpec((t, D_MODEL), out_blk),
        _const((1, D_MODEL)), _const((D_MODEL, C_COLS)), _const((CONV_W, CONV_DIM)), _const((1, CONV_DIM)),
        _const((1, LANES)), _const((1, LANES)), _const((1, D_INNER_C)), _const((1, D_INNER_C)),
        _const((D_INNER_C, D_MODEL)), _const((2 * LANES, H_C * c)),
        _const((c, H_C * c)), _const((4 * c, 256)),
    ]
    out_shape = (jax.ShapeDtypeStruct((length, D_MODEL), F32),
                 jax.ShapeDtypeStruct((H_C * P_C, N_C), F32),
                 jax.ShapeDtypeStruct((8, CONV_DIM), F32))
    out_specs = (pl.BlockSpec((t, D_MODEL), out_blk), _full((H_C * P_C, N_C)), _full((8, CONV_DIM)))
    scratch = [pltpu.VMEM((2, t, C_COLS), F32), pltpu.VMEM((CONV_TILES, CONV_ROWS + 8, LANES), F32),
               pltpu.VMEM((CONV_TILES, CONV_ROWS, LANES), F32), pltpu.VMEM((2, t, LANES), F32),
               pltpu.VMEM((2, t, LANES), F32), pltpu.VMEM((2, t, D_INNER_C), BF16),
               pltpu.VMEM((N_C, H_C * P_C), F32)]
    return pl.pallas_call(
        _c_prompt_kernel, grid=(n_blocks + 2,), in_specs=in_specs, out_specs=out_specs,
        out_shape=out_shape, scratch_shapes=scratch, name="c_prompt",
        compiler_params=pltpu.CompilerParams(dimension_semantics=("arbitrary",),
                                             vmem_limit_bytes=VMEM_LIMIT),
    )(x, x, *weights, _head_expand_matrix(c), *_ssd_masks(c))


def _c_sample(x, weights, st_in, conv_in):
    nb, c, _ = x.shape
    t = nb * c
    vm = pl.BlockSpec(memory_space=pltpu.VMEM)
    out_shape = (jax.ShapeDtypeStruct((t, D_MODEL), F32),
                 jax.ShapeDtypeStruct((nb, H_C * P_C, N_C), F32),
                 jax.ShapeDtypeStruct((nb, CONV_W - 1, CONV_DIM), F32))
    scratch = [pltpu.VMEM((t, C_COLS), F32), pltpu.VMEM((c + 8, CONV_DIM), F32),
               pltpu.VMEM((t, CONV_DIM), F32), pltpu.VMEM((t, LANES), F32), pltpu.VMEM((t, LANES), F32),
               pltpu.VMEM((t, D_INNER_C), BF16)]
    return pl.pallas_call(
        _c_sample_kernel, in_specs=[vm] * 16, out_specs=(vm,) * 3, out_shape=out_shape,
        scratch_shapes=scratch, name="c_sample",
        compiler_params=pltpu.CompilerParams(vmem_limit_bytes=VMEM_LIMIT),
    )(x.reshape(t, D_MODEL), *weights, _head_expand_matrix(c), _head_expand_matrix(P_C), *_ssd_masks(c),
      st_in.reshape(nb, H_C * P_C, N_C), conv_in)


def kernel(x_prompt, x_sample, cache_swa_k, cache_swa_v, state_gla, state_ssd, state_conv, rms_g, w_in_ab, w_gate_up_a, b_gate_a, g_out_a, g_q_b, g_k_b, sink_b, rel_bias, w_out_ab, w_in_c, conv_w_c, conv_b_c, dt_bias_c, a_log_c, d_skip_c, g_y_c, w_out_c):
    bp, seq_len, _ = x_prompt.shape
    nb, dec_len, _ = x_sample.shape
    assert bp == 1 and seq_len % PROMPT_BLOCK == 0 and seq_len % AB_PROMPT_BLOCK == 0 and seq_len >= WINDOW
    wab = _ab_weights(rms_g[0], w_in_ab[0], w_gate_up_a[0], b_gate_a[0], g_out_a[0], g_q_b[0], g_k_b[0],
                      w_out_ab[0])
    yp, gla_p, k_p, v_p = _ab_prompt(x_prompt[0], sink_b[0], rel_bias, wab)
    ys, gla_s, k_s, v_s = _ab_sample(x_sample, sink_b[0], rel_bias, wab, state_gla[0], cache_swa_k[0],
                                     cache_swa_v[0])
    wc = _c_weights(rms_g[1], w_in_c[0], conv_w_c[0], conv_b_c[0], dt_bias_c[0], a_log_c[0], d_skip_c[0],
                    g_y_c[0], w_out_c[0])
    yp, ssd_p, conv_p = _c_prompt(yp, wc)
    ys, ssd_s, conv_s = _c_sample(ys.reshape(nb, dec_len, D_MODEL), wc, state_ssd[0], state_conv[0])
    return (
        yp.reshape(1, seq_len, D_MODEL),
        ys.reshape(nb, dec_len, D_MODEL),
        gla_p.reshape(1, 1, H_A, DK_A, DV_A),
        k_p.reshape(1, 1, WINDOW, KVH_B, HD_B),
        v_p.reshape(1, 1, WINDOW, KVH_B, HD_B),
        ssd_p.reshape(1, 1, H_C, P_C, N_C),
        conv_p[8 - (CONV_W - 1):].reshape(1, 1, CONV_W - 1, CONV_DIM),
        gla_s.reshape(1, nb, H_A, DK_A, DV_A),
        k_s.reshape(1, nb, dec_len, KVH_B, HD_B),
        v_s.reshape(1, nb, dec_len, KVH_B, HD_B),
        ssd_s.reshape(1, nb, H_C, P_C, N_C),
        conv_s.reshape(1, nb, CONV_W - 1, CONV_DIM),
    )
```

```python
import math
import types

import numpy as np
import jax
import jax.numpy as jnp
from jax import lax
from jax.experimental import pallas as pl
from jax.experimental.pallas import tpu as pltpu

F32 = jnp.float32
BF16 = jnp.bfloat16

D_MODEL = 1024
CHUNK = 64
EPS = 1e-6
H_A = 4
DK_A = 64
DV_A = 128
GATE_RANK_A = 16
GATE_NORM_A = 16.0
H_B = 8
KVH_B = 2
G_B = H_B // KVH_B
HD_B = 64
WINDOW = 128
N_BUCKETS = 32
MAX_DISTANCE = 128
D_INNER_C = 2048
P_C = 64
H_C = 32
G_C = 4
HPG_C = 8
N_C = 128
CONV_W = 4
CONV_DIM = D_INNER_C + 2 * G_C * N_C

LOG2E = 1.4426950408889634
LANES = 128

A_Q, A_K, A_V, A_Z = 0, 256, 512, 1024
B_Q, B_K, B_V, B_Z = 1536, 2048, 2176, 2304
A_G = 2816
AB_COLS = 2944
C_Z, C_X, C_B, C_C, C_DT = 0, 2048, 4096, 4608, 5120
C_COLS = 5248

PROMPT_BLOCK = 256
AB_PROMPT_BLOCK = 512
VMEM_LIMIT = 60 * 1024 * 1024


def _dot(a, b):
    return jnp.dot(a.astype(BF16), b.astype(BF16), preferred_element_type=F32)


def _dot_nt(a, b):
    return lax.dot_general(a.astype(BF16), b.astype(BF16), (((1,), (1,)), ((), ())),
                           preferred_element_type=F32)


def _dot_tn(a, b):
    return lax.dot_general(a.astype(BF16), b.astype(BF16), (((0,), (0,)), ((), ())),
                           preferred_element_type=F32)


def _split_bf16(x, terms):
    out = []
    r = x
    for _ in range(terms):
        h = r.astype(BF16)
        out.append(h)
        r = r - h.astype(F32)
    return out


def _sel_dot_left(sel, x, terms):
    acc = None
    for h in _split_bf16(x, terms):
        d = jnp.dot(sel, h, preferred_element_type=F32)
        acc = d if acc is None else acc + d
    return acc


def _rms_rows(x):
    return x * lax.rsqrt(jnp.mean(x * x, axis=-1, keepdims=True) + EPS)


def _exp_neg(x):
    return jnp.exp2(x * (-LOG2E))


def _silu(x):
    return x * (1.0 / (1.0 + _exp_neg(x)))


def _softplus(x):
    return jnp.maximum(x, 0.0) + jnp.log(1.0 + _exp_neg(jnp.abs(x)))


def _log_sigmoid(x):
    return jnp.minimum(x, 0.0) - jnp.log(1.0 + _exp_neg(jnp.abs(x)))


def _tril(n):
    r = lax.broadcasted_iota(jnp.int32, (n, n), 0)
    c = lax.broadcasted_iota(jnp.int32, (n, n), 1)
    return r >= c


def _head_mean_sq(x):
    out = []
    for p in range(x.shape[1] // LANES):
        sq = x[:, p * LANES:(p + 1) * LANES]
        sq = sq * sq
        low = lax.broadcasted_iota(jnp.int32, sq.shape, 1) < HD_B
        lo = jnp.sum(jnp.where(low, sq, 0.0), axis=-1, keepdims=True)
        hi = jnp.sum(jnp.where(low, 0.0, sq), axis=-1, keepdims=True)
        out.append(jnp.where(low, lo, hi))
    return jnp.concatenate(out, axis=1) * (1.0 / HD_B)


def _tile_kv_heads(kv):
    low = lax.broadcasted_iota(jnp.int32, kv.shape, 1) < HD_B
    swapped = pltpu.roll(kv, HD_B, 1)
    out = []
    for base in (jnp.where(low, kv, swapped), jnp.where(low, swapped, kv)):
        base = base.astype(BF16)
        out.append(jnp.concatenate([base, base], axis=1))
    return out


def _ab_dense_in(x, rms_g, w_in_ref, gq, gk, proj_ref, qn_ref):
    h = (_rms_rows(x) * rms_g).astype(BF16)
    rows = min(256, x.shape[0])
    for m in range(0, x.shape[0], rows):
        proj_ref[m:m + rows, :] = jnp.dot(h[m:m + rows], w_in_ref[...], preferred_element_type=F32)
    qb = proj_ref[:, B_Q:B_Q + 512]
    qn_ref[...] = qb * lax.rsqrt(_head_mean_sq(qb) + EPS) * gq
    kb = proj_ref[:, B_K:B_K + 128]
    kn = kb * lax.rsqrt(_head_mean_sq(kb) + EPS) * gk
    vb = proj_ref[:, B_V:B_V + 128]
    return kn, vb


def _advance(gens, yielded=None):
    alive = []
    for gen in gens:
        try:
            value = next(gen)
            alive.append(gen)
            if yielded is not None and value is not None:
                yielded.append(value)
        except StopIteration:
            pass
    return alive


def _interleave(gens):
    gens = list(gens)
    while gens:
        gens = _advance(gens)


def _ab_chunk(r, c, kb, proj_ref, qn_ref, o_ref, state, k_band, v_band, bias_ref, sink_ref,
              w_gu, b_gate, g_out, first_valid_col):
    rows = pl.ds(r, c)
    gate = _dot(proj_ref[rows, A_G:A_G + LANES], w_gu) + b_gate
    yield
    g = _log_sigmoid(gate) * (LOG2E / GATE_NORM_A)
    tril = _tril(c)
    b = _sel_dot_left(tril.astype(BF16), g, 3)
    bl = b[c - 1:c, :]
    bl_tile = jnp.broadcast_to(bl, (LANES, H_A * DK_A))
    bl_rows = jnp.concatenate([bl_tile[:, i * LANES:(i + 1) * LANES].T
                               for i in range(H_A * DK_A // LANES)], axis=0)
    yield
    q = proj_ref[rows, A_Q:A_Q + 256] * (DK_A ** -0.5)
    k = proj_ref[rows, A_K:A_K + 256]
    v = proj_ref[rows, A_V:A_V + 512].astype(BF16)
    qe = (q * jnp.exp2(b)).astype(BF16)
    kd = k * jnp.exp2(-b)
    kd2 = (k * jnp.exp2(bl - b)).astype(BF16)
    lane_head = lax.broadcasted_iota(jnp.int32, (c, LANES), 1) // DK_A
    row_head = lax.broadcasted_iota(jnp.int32, (LANES, LANES), 0) // DK_A
    att = []
    for p in range(2):
        kp = kd[:, p * 128:(p + 1) * 128]
        kpair = jnp.concatenate([jnp.where(lane_head == j, kp, 0.0) for j in range(2)], axis=0).astype(BF16)
        att.append(_dot_nt(qe[:, p * 128:(p + 1) * 128], kpair))
    upd = [_dot_tn(kd2[:, p * 128:(p + 1) * 128], v[:, p * 256:(p + 1) * 256]) for p in range(2)]
    yield
    row = lax.broadcasted_iota(jnp.int32, (c, 2 * c), 0)
    col = lax.broadcasted_iota(jnp.int32, (c, 2 * c), 1) % c
    att = [jnp.where(row >= col, a, 0.0).astype(BF16) for a in att]
    yield
    s_prev = state[0]
    s_new, oh = [], []
    zeros_v = jnp.zeros((c, DV_A), BF16)
    for p in range(2):
        sp = s_prev[p * 128:(p + 1) * 128, :]
        s_bd = jnp.concatenate([jnp.where(row_head == j, sp, 0.0) for j in range(2)], axis=1).astype(BF16)
        v0, v1 = v[:, 2 * p * DV_A:(2 * p + 1) * DV_A], v[:, (2 * p + 1) * DV_A:(2 * p + 2) * DV_A]
        v_bd = jnp.concatenate([jnp.concatenate([v0, zeros_v], axis=1),
                                jnp.concatenate([zeros_v, v1], axis=1)], axis=0)
        qp = qe[:, p * 128:(p + 1) * 128]
        if (2 * c) % LANES == 0:
            o_pair = jnp.dot(jnp.concatenate([att[p], qp], axis=1), jnp.concatenate([v_bd, s_bd], axis=0),
                             preferred_element_type=F32)
        else:
            o_pair = (jnp.dot(att[p], v_bd, preferred_element_type=F32)
                      + jnp.dot(qp, s_bd, preferred_element_type=F32))
        oh += [o_pair[:, :DV_A], o_pair[:, DV_A:]]
        u = jnp.where(row_head == 0, upd[p][:, :128], upd[p][:, 128:])
        s_new.append(jnp.exp2(bl_rows[p * 128:(p + 1) * 128, :]) * sp + u)
    state[0] = jnp.concatenate(s_new, axis=0)
    yield
    for hd in range(H_A):
        z = proj_ref[rows, A_Z + hd * 128:A_Z + (hd + 1) * 128]
        o_ref[rows, hd * 128:(hd + 1) * 128] = (_rms_rows(oh[hd]) * g_out * _silu(z)).astype(BF16)
    yield
    lane256 = lax.broadcasted_iota(jnp.int32, (c, 256), 1) // HD_B
    srow = lax.broadcasted_iota(jnp.int32, (G_B * c, 1), 0) // c
    scores = []
    for kvh in range(KVH_B):
        qn = qn_ref[rows, kvh * 256:(kvh + 1) * 256]
        qs = jnp.concatenate([jnp.where(lane256 == gq_, qn, 0.0) for gq_ in range(G_B)], axis=0)
        scores.append(_dot_nt(qs, k_band(kvh)))
    yield
    probs, dens = [], []
    for kvh in range(KVH_B):
        s = scores[kvh] * (LOG2E * HD_B ** -0.5) + bias_ref[kvh]
        if first_valid_col is not None:
            col = lax.broadcasted_iota(jnp.int32, (G_B * c, kb), 1)
            s = jnp.where(col >= first_valid_col, s, -jnp.inf)
        sink = jnp.zeros((G_B * c, 1), F32)
        for gq_ in range(G_B):
            sink = jnp.where(srow == gq_, sink_ref[kvh * G_B + gq_] * LOG2E, sink)
        m = jnp.maximum(jnp.max(s, axis=-1, keepdims=True), sink)
        pr = jnp.exp2(s - m)
        dens.append(jnp.sum(pr, axis=-1, keepdims=True) + jnp.exp2(sink - m))
        probs.append(pr.astype(BF16))
    yield
    outs = [_dot(probs[kvh], v_band(kvh)) for kvh in range(KVH_B)]
    yield
    for kvh in range(KVH_B):
        ost = outs[kvh] / dens[kvh]
        ob = jnp.zeros((c, 256), F32)
        for gq_ in range(G_B):
            ob = ob + jnp.where(lane256 == gq_, ost[gq_ * c:(gq_ + 1) * c, :], 0.0)
        z = proj_ref[rows, B_Z + kvh * 256:B_Z + (kvh + 1) * 256]
        o_ref[rows, 512 + kvh * 256:512 + (kvh + 1) * 256] = (ob * _silu(z)).astype(BF16)


def _ab_prompt_kernel(sink_ref, x_ref, rms_g_ref, w_in_ref, w_gu_ref, b_gate_ref, g_out_ref, gq_ref,
                      gk_ref, bias_ref, w_out_ref,
                      y_ref, s_out_ref, k_out_ref, v_out_ref,
                      proj_ref, qn_ref, o_ref, kband_ref, vband_ref, s_ref):
    t = x_ref.shape[0]
    c = CHUNK
    kb = WINDOW + c
    nchunk = t // c
    step = pl.program_id(0)

    @pl.when(step == 0)
    def _():
        s_ref[...] = jnp.zeros_like(s_ref)
        kband_ref[:, t:t + WINDOW, :] = jnp.zeros((KVH_B, WINDOW, 256), BF16)
        vband_ref[:, t:t + WINDOW, :] = jnp.zeros((KVH_B, WINDOW, 256), BF16)

    for kvh in range(KVH_B):
        kband_ref[kvh, 0:WINDOW, :] = kband_ref[kvh, t:t + WINDOW, :]
        vband_ref[kvh, 0:WINDOW, :] = vband_ref[kvh, t:t + WINDOW, :]

    x = x_ref[...]
    kn, vb = _ab_dense_in(x, rms_g_ref[...], w_in_ref, gq_ref[...], gk_ref[...], proj_ref, qn_ref)
    k_out_ref[...] = kn[t - WINDOW:, :]
    v_out_ref[...] = vb[t - WINDOW:, :]
    for kvh, (kt, vt) in enumerate(zip(_tile_kv_heads(kn), _tile_kv_heads(vb))):
        kband_ref[kvh, WINDOW:WINDOW + t, :] = kt
        vband_ref[kvh, WINDOW:WINDOW + t, :] = vt

    w_gu = w_gu_ref[...]
    b_gate = b_gate_ref[...]
    g_out = g_out_ref[...]

    state = [s_ref[...]]

    def chunk(i):
        r = i * c
        return _ab_chunk(r, c, kb, proj_ref, qn_ref, o_ref, state,
                         lambda kvh: kband_ref[kvh, pl.ds(r, kb), :],
                         lambda kvh: vband_ref[kvh, pl.ds(r, kb), :],
                         bias_ref, sink_ref, w_gu, b_gate, g_out, (2 - (step * nchunk + i)) * c)

    _interleave(chunk(i) for i in range(nchunk))
    s_ref[...] = state[0]
    y_ref[...] = x + jnp.dot(o_ref[...], w_out_ref[...], preferred_element_type=F32)
    s_out_ref[...] = s_ref[...]


def _ab_sample_kernel(sink_ref, x_ref, rms_g_ref, w_in_ref, w_gu_ref, b_gate_ref, g_out_ref, gq_ref,
                      gk_ref, bias_ref, w_out_ref,
                      s_in_ref, kc_ref, vc_ref,
                      y_ref, s_out_ref, k_out_ref, v_out_ref,
                      proj_ref, qn_ref, o_ref, kband_ref, vband_ref):
    nb, wc = kc_ref.shape[0], kc_ref.shape[1]
    t = x_ref.shape[0]
    c = t // nb
    kb = wc + c
    x = x_ref[...]
    kn, vb = _ab_dense_in(x, rms_g_ref[...], w_in_ref, gq_ref[...], gk_ref[...], proj_ref, qn_ref)
    k_out_ref[...] = kn
    v_out_ref[...] = vb
    k_new, v_new = _tile_kv_heads(kn), _tile_kv_heads(vb)
    for bi in range(nb):
        k_old, v_old = _tile_kv_heads(kc_ref[bi]), _tile_kv_heads(vc_ref[bi])
        for kvh in range(KVH_B):
            kband_ref[kvh, bi, 0:wc, :] = k_old[kvh]
            vband_ref[kvh, bi, 0:wc, :] = v_old[kvh]
            kband_ref[kvh, bi, wc:kb, :] = k_new[kvh][bi * c:(bi + 1) * c, :]
            vband_ref[kvh, bi, wc:kb, :] = v_new[kvh][bi * c:(bi + 1) * c, :]

    w_gu = w_gu_ref[...]
    b_gate = b_gate_ref[...]
    g_out = g_out_ref[...]

    states = [[s_in_ref[bi]] for bi in range(nb)]

    def seq(bi):
        return _ab_chunk(bi * c, c, kb, proj_ref, qn_ref, o_ref, states[bi],
                         lambda kvh: kband_ref[kvh, bi], lambda kvh: vband_ref[kvh, bi],
                         bias_ref, sink_ref, w_gu, b_gate, g_out, None)

    _interleave(seq(bi) for bi in range(nb))
    for bi in range(nb):
        s_out_ref[bi] = states[bi][0]
    y_ref[...] = x + jnp.dot(o_ref[...], w_out_ref[...], preferred_element_type=F32)


GROUP_W = D_INNER_C // G_C


def _c_chunk(r, c, io, state, e_s, e_p, neg_mask, bd_mask, dskip, g_y):
    dtc = io.dt(r)
    acum = _sel_dot_left(_tril(c).astype(BF16), io.da(r), 3) * LOG2E
    yield acum
    lhs = jnp.concatenate([jnp.concatenate(_split_bf16(acum, 2), axis=1),
                           jnp.concatenate(_split_bf16(dtc, 2), axis=1)], axis=0)
    both_p = jnp.dot(lhs, e_p, preferred_element_type=F32)
    xa_p, dt_p = both_p[:c], both_p[c:]
    xa_s = xa_p if c == P_C else jnp.dot(lhs[:c], e_s, preferred_element_type=F32)
    yield dt_p
    acum_t = acum.T
    a_row = jnp.concatenate([acum_t[h:h + 1, :] for h in range(H_C)], axis=1)
    wmat = jnp.exp2((xa_s - a_row) + neg_mask)
    al_p = xa_p[c - 1:c, :]
    xs = io.x(r)
    xdt = xs * dt_p
    xdt_bf = xdt.astype(BF16)
    xw = (xdt * jnp.exp2(al_p - xa_p)).astype(BF16)
    dec = jnp.exp2(al_p)
    bg, cg, cb = [], [], []
    for g in range(G_C):
        bg.append(io.b(r, g).astype(BF16))
        cg.append(io.c(r, g).astype(BF16))
        cb.append(_dot_nt(cg[g], jnp.concatenate([bg[g]] * HPG_C, axis=0)))
    yield cb[-1]
    ys = []
    for g in range(G_C):
        mg = (cb[g] * wmat[:, g * HPG_C * c:(g + 1) * HPG_C * c]).astype(BF16)
        for j in range(2):
            xj = xdt_bf[:, g * GROUP_W + j * 256:g * GROUP_W + (j + 1) * 256]
            bd = jnp.concatenate([xj] * 4, axis=0) * bd_mask
            ys.append(jnp.dot(mg[:, j * 4 * c:(j + 1) * 4 * c], bd, preferred_element_type=F32))
    upd = [_dot_tn(bg[g], xw[:, g * GROUP_W:(g + 1) * GROUP_W]) for g in range(G_C)]
    yield upd[-1]
    st_prev = state[0]
    y_inter, st_new = [], []
    for g in range(G_C):
        sl = slice(g * GROUP_W, (g + 1) * GROUP_W)
        y_inter.append(_dot(cg[g], st_prev[:, sl]))
        st_new.append(st_prev[:, sl] * dec[:, sl] + upd[g])
    state[0] = jnp.concatenate(st_new, axis=1)
    yield st_new[-1]
    for g in range(G_C):
        sl = slice(g * GROUP_W, (g + 1) * GROUP_W)
        y = jnp.concatenate(ys[2 * g:2 * g + 2], axis=1) + y_inter[g] * jnp.exp2(xa_p[:, sl])
        y = y + dskip[:, sl] * xs[:, sl]
        y = y * _silu(io.z(r, g))
        io.put_o(r, g, (_rms_rows(y) * g_y[:, sl]).astype(BF16))


def _c_dt(dt_cols, dt_bias, a_log):
    dt = _softplus(dt_cols + dt_bias)
    return dt, dt * (-jnp.exp(a_log))


CONV_PITCH = PROMPT_BLOCK // 8 + 1
CONV_ROWS = 8 * CONV_PITCH
CONV_TILES = CONV_DIM // LANES
C_ROUNDS = 12
C_IN_SLABS = tuple((C_X + 512 * k, C_X + 512 * (k + 1)) for k in range(6)) + tuple(
    (C_Z + 512 * k, C_Z + 512 * (k + 1)) for k in range(4)) + ((C_DT, C_COLS),)
C_MIX_FIRST_ROUND = 5
C_OUT_ROUNDS = (1, 3, 7, 11)
assert len(C_IN_SLABS) <= C_ROUNDS and sum(hi - lo for lo, hi in C_IN_SLABS) == C_COLS


def _conv_tile(ubuf_ref, act_ref, j, conv_w, conv_b):
    w = [jnp.broadcast_to(conv_w[i:i + 1, j * LANES:(j + 1) * LANES], (8, LANES)) for i in range(CONV_W)]
    b = jnp.broadcast_to(conv_b[:, j * LANES:(j + 1) * LANES], (8, LANES))
    for a in range(CONV_PITCH):
        acc = b
        for i in range(CONV_W):
            acc = acc + w[i] * ubuf_ref[j, pl.ds(8 - (CONV_W - 1) + i + a, 8, stride=CONV_PITCH), :]
        act_ref[j, pl.ds(a, 8, stride=CONV_PITCH), :] = _silu(acc)


def _c_prompt_kernel(xin_ref, xres_ref, rms_g_ref, w_in_ref, conv_w_ref, conv_b_ref, dt_bias_ref,
                     a_log_ref, dskip_ref, g_y_ref, w_out_ref, e_s_ref, neg_mask_ref, bd_mask_ref,
                     y_ref, st_out_ref, conv_out_ref,
                     proj_ref, ubuf_ref, act_ref, dt_ref, da_ref, o_ref, st_ref):
    t = xin_ref.shape[0]
    c = CHUNK
    s = pl.program_id(0)
    n_blocks = pl.num_programs(0) - 2

    @pl.when(s == 0)
    def _():
        o_ref[1] = jnp.zeros(o_ref.shape[1:], BF16)
        ubuf_ref[...] = jnp.zeros_like(ubuf_ref)

    @pl.when(s == 0)
    def _():
        st_ref[...] = jnp.zeros_like(st_ref)

    def in_stage(slot_in):
        h = (_rms_rows(xin_ref[...]) * rms_g_ref[...]).astype(BF16)
        conv_w = conv_w_ref[...]
        conv_b = conv_b_ref[...]
        act = act_ref.at[slot_in]
        for rnd, (lo, hi) in enumerate(C_IN_SLABS):
            if rnd:
                yield
            slab = jnp.dot(h, w_in_ref[:, lo:hi], preferred_element_type=F32)
            proj_ref[slot_in, :, lo:hi] = slab
            if lo == C_DT:
                dt, da = _c_dt(slab, dt_bias_ref[...], a_log_ref[...])
                dt_ref[slot_in] = dt
                da_ref[slot_in] = da
            elif lo >= C_X:
                for j in range((lo - C_X) // LANES, (hi - C_X) // LANES):
                    ubuf_ref[j, 8:16, :] = ubuf_ref[j, CONV_ROWS:CONV_ROWS + 8, :]
                    ubuf_ref[j, 16:16 + t, :] = proj_ref[slot_in, :, C_X + j * LANES:C_X + (j + 1) * LANES]
                    _conv_tile(ubuf_ref, act, j, conv_w, conv_b)

    n_x = D_INNER_C // LANES
    n_g = N_C // LANES

    def mix_stage(slot_mix):
        def put_o(r, g, value):
            o_ref[slot_mix, pl.ds(r, c), g * GROUP_W:(g + 1) * GROUP_W] = value

        def act_rows(r, j):
            return act_ref[slot_mix, j, pl.ds(8 + r, c), :]

        io = types.SimpleNamespace(
            dt=lambda r: dt_ref[slot_mix, pl.ds(r, c), :],
            da=lambda r: da_ref[slot_mix, pl.ds(r, c), :],
            x=lambda r: jnp.concatenate([act_rows(r, j) for j in range(n_x)], axis=1),
            b=lambda r, g: act_rows(r, n_x + g * n_g),
            c=lambda r, g: act_rows(r, n_x + (G_C + g) * n_g),
            z=lambda r, g: proj_ref[slot_mix, pl.ds(r, c), C_Z + g * GROUP_W:C_Z + (g + 1) * GROUP_W],
            put_o=put_o)
        for _ in range(C_MIX_FIRST_ROUND):
            yield
        state = [st_ref[...]]
        e_s = e_s_ref[...]
        chunks = [_c_chunk(i * c, c, io, state, e_s, e_s, neg_mask_ref[...], bd_mask_ref[...],
                           dskip_ref[...], g_y_ref[...]) for i in range(t // c)]
        chunks = _advance(chunks)
        while chunks:
            yield
            chunks = _advance(chunks)
        st_ref[...] = state[0]

    def out_stage(slot_in):
        quarter = 0
        for rnd in range(C_ROUNDS):
            if rnd:
                yield
            if rnd in C_OUT_ROUNDS:
                cols = slice(quarter * 256, (quarter + 1) * 256)
                y_ref[:, cols] = xres_ref[:, cols] + jnp.dot(o_ref[slot_in], w_out_ref[:, cols],
                                                             preferred_element_type=F32)
                quarter += 1

    slot_in = s % 2

    @pl.when(s == 0)
    def _():
        _interleave([in_stage(slot_in)])

    @pl.when(jnp.logical_and(s > 0, s <= n_blocks))
    def _():
        _interleave([in_stage(slot_in), mix_stage(1 - slot_in), out_stage(slot_in)])

    @pl.when(s == n_blocks + 1)
    def _():
        _interleave([out_stage(slot_in)])

    @pl.when(s == n_blocks)
    def _():
        st_out_ref[...] = st_ref[...].T
        conv_out_ref[...] = proj_ref[1 - s % 2, t - 8:t, C_X:C_X + CONV_DIM]


def _conv_rows(ubuf, nrows, conv_w, conv_b):
    acc = conv_b
    for i in range(CONV_W):
        acc = acc + conv_w[i:i + 1, :] * ubuf[pl.ds(8 - (CONV_W - 1) + i, nrows), :]
    return _silu(acc)


def _c_sample_kernel(x_ref, rms_g_ref, w_in_ref, conv_w_ref, conv_b_ref, dt_bias_ref, a_log_ref,
                     dskip_ref, g_y_ref, w_out_ref, e_s_ref, e_p_ref, neg_mask_ref, bd_mask_ref,
                     st_in_ref, conv_in_ref,
                     y_ref, st_out_ref, conv_out_ref,
                     proj_ref, ubuf_ref, act_ref, dt_ref, da_ref, o_ref):
    nb = st_in_ref.shape[0]
    t = x_ref.shape[0]
    c = t // nb
    x = x_ref[...]
    h = (_rms_rows(x) * rms_g_ref[...]).astype(BF16)
    proj_ref[...] = jnp.dot(h, w_in_ref[...], preferred_element_type=F32)
    dt, da = _c_dt(proj_ref[:, C_DT:C_DT + LANES], dt_bias_ref[...], a_log_ref[...])
    dt_ref[...] = dt
    da_ref[...] = da
    conv_w = conv_w_ref[...]
    conv_b = conv_b_ref[...]
    ubuf_ref[0:8, :] = jnp.zeros((8, CONV_DIM), F32)
    for bi in range(nb):
        ubuf_ref[8 - (CONV_W - 1):8, :] = conv_in_ref[bi]
        ubuf_ref[8:8 + c, :] = proj_ref[bi * c:(bi + 1) * c, C_X:C_X + CONV_DIM]
        act_ref[bi * c:(bi + 1) * c, :] = _conv_rows(ubuf_ref, c, conv_w, conv_b)
        conv_out_ref[bi] = ubuf_ref[8 + c - (CONV_W - 1):8 + c, :]

    def put_o(r, g, value):
        o_ref[pl.ds(r, c), g * GROUP_W:(g + 1) * GROUP_W] = value

    io = types.SimpleNamespace(
        dt=lambda r: dt_ref[pl.ds(r, c), :],
        da=lambda r: da_ref[pl.ds(r, c), :],
        x=lambda r: act_ref[pl.ds(r, c), 0:D_INNER_C],
        b=lambda r, g: act_ref[pl.ds(r, c), D_INNER_C + g * N_C:D_INNER_C + (g + 1) * N_C],
        c=lambda r, g: act_ref[pl.ds(r, c), D_INNER_C + (G_C + g) * N_C:D_INNER_C + (G_C + g + 1) * N_C],
        z=lambda r, g: proj_ref[pl.ds(r, c), C_Z + g * GROUP_W:C_Z + (g + 1) * GROUP_W],
        put_o=put_o)
    e_s = e_s_ref[...]
    e_p = e_p_ref[...]
    neg_mask = neg_mask_ref[...]
    bd_mask = bd_mask_ref[...]
    dskip = dskip_ref[...]
    g_y = g_y_ref[...]

    states = [[st_in_ref[bi].T] for bi in range(nb)]
    _interleave(_c_chunk(bi * c, c, io, states[bi], e_s, e_p, neg_mask, bd_mask, dskip, g_y)
                for bi in range(nb))
    for bi in range(nb):
        st_out_ref[bi] = states[bi][0].T
    y_ref[...] = x + jnp.dot(o_ref[...], w_out_ref[...], preferred_element_type=F32)


def _bucket_table(q_off, k_off):
    n = q_off[:, None] - k_off[None, :]
    half = N_BUCKETS // 2
    max_exact = half // 2
    side = np.where(n < 0, half, 0)
    n = np.abs(n)
    nf = np.maximum(n, max_exact).astype(np.float32)
    large = max_exact + (np.log(nf / np.float32(max_exact)) / np.float32(math.log(MAX_DISTANCE / max_exact))
                         * np.float32(half - max_exact)).astype(np.int32)
    large = np.minimum(large, half - 1)
    return side + np.where(n < max_exact, n, large)


def _bias_rows(rel_bias, q_off, k_off):
    bucket = _bucket_table(q_off, k_off)
    onehot = jnp.asarray(np.eye(N_BUCKETS, dtype=np.float32)[bucket])
    bias = jnp.einsum('qkb,bh->hqk', onehot, rel_bias.astype(F32), precision=lax.Precision.HIGHEST) * LOG2E
    return bias.reshape(KVH_B, G_B * q_off.shape[0], k_off.shape[0])


def _head_expand_matrix(per_head):
    m = np.zeros((LANES, H_C * per_head), np.float32)
    for h in range(H_C):
        m[h, h * per_head:(h + 1) * per_head] = 1.0
    return jnp.asarray(np.concatenate([m, m], axis=0), BF16)


def _full(shape):
    return pl.BlockSpec(shape, lambda *_: (0,) * len(shape))


def _const(shape):
    return pl.BlockSpec(shape, lambda *_: (0,) * len(shape), pipeline_mode=pl.Buffered(1))


CAST_ROWS = 256


def _cast_columns(wt_ref, w_bf_ref, src, dst, n):
    for off in range(0, n - n % LANES, CAST_ROWS):
        rows = min(CAST_ROWS, n - n % LANES - off)
        w_bf_ref[:, dst + off:dst + off + rows] = wt_ref[src + off:src + off + rows, :].T.astype(BF16)
    rest = n % LANES
    if rest:
        off = n - rest
        tail = jnp.concatenate([wt_ref[src + off:src + n, :], jnp.zeros((LANES - rest, wt_ref.shape[1]), F32)],
                               axis=0)
        w_bf_ref[:, dst + off:dst + off + LANES] = tail.T.astype(BF16)


def _ab_cast_kernel(wt_ref, w_out_ref, w_in_bf_ref, w_out_bf_ref):
    g0 = A_Z
    _cast_columns(wt_ref, w_in_bf_ref, 0, 0, g0)
    _cast_columns(wt_ref, w_in_bf_ref, g0 + GATE_RANK_A, g0, A_G - g0)
    _cast_columns(wt_ref, w_in_bf_ref, g0, A_G, GATE_RANK_A)
    w_out_bf_ref[...] = w_out_ref[...].astype(BF16)


def _c_cast_kernel(wt_ref, w_out_ref, w_in_bf_ref, w_out_bf_ref):
    _cast_columns(wt_ref, w_in_bf_ref, 0, 0, wt_ref.shape[0])
    w_out_bf_ref[...] = w_out_ref[...].astype(BF16)


def _cast_weights(body, w_in, w_out, cols, name):
    vm = pl.BlockSpec(memory_space=pltpu.VMEM)
    return pl.pallas_call(
        body, in_specs=[vm, vm], out_specs=(vm, vm),
        out_shape=(jax.ShapeDtypeStruct((w_in.shape[0], cols), BF16),
                   jax.ShapeDtypeStruct(w_out.shape, BF16)),
        name=name, compiler_params=pltpu.CompilerParams(vmem_limit_bytes=VMEM_LIMIT),
    )(w_in.T, w_out)


def _ab_weights(rms_g, w_in, w_gate_up, b_gate, g_out, g_q, g_k, w_out):
    w_in_r, w_out_bf = _cast_weights(_ab_cast_kernel, w_in, w_out, AB_COLS, "ab_cast")
    w_gu = jnp.concatenate([w_gate_up, jnp.zeros((LANES - GATE_RANK_A, H_A * DK_A), w_gate_up.dtype)],
                           axis=0).astype(BF16)
    return (rms_g.reshape(1, D_MODEL), w_in_r, w_gu, b_gate.reshape(1, -1), g_out.reshape(1, DV_A),
            jnp.tile(g_q, H_B).reshape(1, -1), jnp.tile(g_k, KVH_B).reshape(1, -1), w_out_bf)


def _ab_prompt(x, sink, rel_bias, weights):
    rms_g, w_in, w_gu, b_gate, g_out, gq, gk, w_out = weights
    length = x.shape[0]
    t = AB_PROMPT_BLOCK
    c = CHUNK
    kb = WINDOW + c
    bias = _bias_rows(rel_bias, np.arange(c), np.arange(kb) - WINDOW)
    row_blk = lambda i: (i, 0)
    in_specs = [
        pl.BlockSpec(memory_space=pltpu.SMEM),
        pl.BlockSpec((t, D_MODEL), row_blk),
        _const((1, D_MODEL)), _const((D_MODEL, AB_COLS)), _const((LANES, 256)), _const((1, 256)),
        _const((1, DV_A)), _const((1, 512)), _const((1, 128)), _const((KVH_B, G_B * c, kb)),
        _const((D_MODEL, D_MODEL)),
    ]
    out_shape = (jax.ShapeDtypeStruct((length, D_MODEL), F32),
                 jax.ShapeDtypeStruct((H_A * DK_A, DV_A), F32),
                 jax.ShapeDtypeStruct((WINDOW, KVH_B * HD_B), F32),
                 jax.ShapeDtypeStruct((WINDOW, KVH_B * HD_B), F32))
    out_specs = (pl.BlockSpec((t, D_MODEL), row_blk), _full((H_A * DK_A, DV_A)),
                 _full((WINDOW, KVH_B * HD_B)), _full((WINDOW, KVH_B * HD_B)))
    scratch = [pltpu.VMEM((t, AB_COLS), F32), pltpu.VMEM((t, 512), F32), pltpu.VMEM((t, D_MODEL), BF16),
               pltpu.VMEM((KVH_B, t + WINDOW, 256), BF16), pltpu.VMEM((KVH_B, t + WINDOW, 256), BF16),
               pltpu.VMEM((H_A * DK_A, DV_A), F32)]
    return pl.pallas_call(
        _ab_prompt_kernel, grid=(length // t,), in_specs=in_specs, out_specs=out_specs,
        out_shape=out_shape, scratch_shapes=scratch, name="ab_prompt",
        compiler_params=pltpu.CompilerParams(dimension_semantics=("arbitrary",),
                                             vmem_limit_bytes=VMEM_LIMIT),
    )(sink, x, rms_g, w_in, w_gu, b_gate, g_out, gq, gk, bias, w_out)


def _ab_sample(x, sink, rel_bias, weights, s_in, k_cache, v_cache):
    rms_g, w_in, w_gu, b_gate, g_out, gq, gk, w_out = weights
    nb, c, _ = x.shape
    wc = k_cache.shape[1]
    kb = wc + c
    t = nb * c
    bias = _bias_rows(rel_bias, np.arange(c), np.arange(kb) - wc)
    vm = pl.BlockSpec(memory_space=pltpu.VMEM)
    in_specs = [pl.BlockSpec(memory_space=pltpu.SMEM)] + [vm] * 13
    out_shape = (jax.ShapeDtypeStruct((t, D_MODEL), F32),
                 jax.ShapeDtypeStruct((nb, H_A * DK_A, DV_A), F32),
                 jax.ShapeDtypeStruct((t, KVH_B * HD_B), F32),
                 jax.ShapeDtypeStruct((t, KVH_B * HD_B), F32))
    scratch = [pltpu.VMEM((t, AB_COLS), F32), pltpu.VMEM((t, 512), F32), pltpu.VMEM((t, D_MODEL), BF16),
               pltpu.VMEM((KVH_B, nb, kb, 256), BF16), pltpu.VMEM((KVH_B, nb, kb, 256), BF16)]
    return pl.pallas_call(
        _ab_sample_kernel, in_specs=in_specs, out_specs=(vm,) * 4, out_shape=out_shape,
        scratch_shapes=scratch, name="ab_sample",
        compiler_params=pltpu.CompilerParams(vmem_limit_bytes=VMEM_LIMIT),
    )(sink, x.reshape(t, D_MODEL), rms_g, w_in, w_gu, b_gate, g_out, gq, gk, bias, w_out,
      s_in.reshape(nb, H_A * DK_A, DV_A), k_cache.reshape(nb, wc, KVH_B * HD_B),
      v_cache.reshape(nb, wc, KVH_B * HD_B))


def _c_weights(rms_g, w_in, conv_w, conv_b, dt_bias, a_log, d_skip, g_y, w_out):
    w_in_p, w_out_bf = _cast_weights(_c_cast_kernel, w_in, w_out, C_COLS, "c_cast")
    pad = lambda v: jnp.concatenate([v, jnp.zeros((LANES - H_C,), v.dtype)]).reshape(1, LANES)
    return (rms_g.reshape(1, D_MODEL), w_in_p, conv_w, conv_b.reshape(1, CONV_DIM), pad(dt_bias), pad(a_log),
            jnp.repeat(d_skip, P_C).reshape(1, D_INNER_C), g_y.reshape(1, D_INNER_C), w_out_bf)


def _ssd_masks(c):
    tok = np.arange(c)[:, None]
    src = np.arange(H_C * c)[None, :] % c
    neg = np.where(src <= tok, 0.0, -np.inf).astype(np.float32)
    blk = (np.arange(4 * c)[:, None] // c) == (np.arange(256)[None, :] // P_C)
    return jnp.asarray(neg), jnp.asarray(blk, BF16)


def _c_prompt(x, weights):
    length = x.shape[0]
    t = PROMPT_BLOCK
    c = CHUNK
    n_blocks = length // t
    in_blk = lambda s: (jnp.minimum(s, n_blocks - 1), 0)
    out_blk = lambda s: (jnp.clip(s - 2, 0, n_blocks - 1), 0)
    in_specs = [
        pl.BlockSpec((t, D_MODEL), in_blk), pl.BlockSpec((t, D_MODEL), out_blk),
        _const((1, D_MODEL)), _const((D_MODEL, C_COLS)), _const((CONV_W, CONV_DIM)), _const((1, CONV_DIM)),
        _const((1, LANES)), _const((1, LANES)), _const((1, D_INNER_C)), _const((1, D_INNER_C)),
        _const((D_INNER_C, D_MODEL)), _const((2 * LANES, H_C * c)),
        _const((c, H_C * c)), _const((4 * c, 256)),
    ]
    out_shape = (jax.ShapeDtypeStruct((length, D_MODEL), F32),
                 jax.ShapeDtypeStruct((H_C * P_C, N_C), F32),
                 jax.ShapeDtypeStruct((8, CONV_DIM), F32))
    out_specs = (pl.BlockSpec((t, D_MODEL), out_blk), _full((H_C * P_C, N_C)), _full((8, CONV_DIM)))
    scratch = [pltpu.VMEM((2, t, C_COLS), F32), pltpu.VMEM((CONV_TILES, CONV_ROWS + 8, LANES), F32),
               pltpu.VMEM((2, CONV_TILES, CONV_ROWS, LANES), F32), pltpu.VMEM((2, t, LANES), F32),
               pltpu.VMEM((2, t, LANES), F32), pltpu.VMEM((2, t, D_INNER_C), BF16),
               pltpu.VMEM((N_C, H_C * P_C), F32)]
    return pl.pallas_call(
        _c_prompt_kernel, grid=(n_blocks + 2,), in_specs=in_specs, out_specs=out_specs,
        out_shape=out_shape, scratch_shapes=scratch, name="c_prompt",
        compiler_params=pltpu.CompilerParams(dimension_semantics=("arbitrary",),
                                             vmem_limit_bytes=VMEM_LIMIT),
    )(x, x, *weights, _head_expand_matrix(c), *_ssd_masks(c))


def _c_sample(x, weights, st_in, conv_in):
    nb, c, _ = x.shape
    t = nb * c
    vm = pl.BlockSpec(memory_space=pltpu.VMEM)
    out_shape = (jax.ShapeDtypeStruct((t, D_MODEL), F32),
                 jax.ShapeDtypeStruct((nb, H_C * P_C, N_C), F32),
                 jax.ShapeDtypeStruct((nb, CONV_W - 1, CONV_DIM), F32))
    scratch = [pltpu.VMEM((t, C_COLS), F32), pltpu.VMEM((c + 8, CONV_DIM), F32),
               pltpu.VMEM((t, CONV_DIM), F32), pltpu.VMEM((t, LANES), F32), pltpu.VMEM((t, LANES), F32),
               pltpu.VMEM((t, D_INNER_C), BF16)]
    return pl.pallas_call(
        _c_sample_kernel, in_specs=[vm] * 16, out_specs=(vm,) * 3, out_shape=out_shape,
        scratch_shapes=scratch, name="c_sample",
        compiler_params=pltpu.CompilerParams(vmem_limit_bytes=VMEM_LIMIT),
    )(x.reshape(t, D_MODEL), *weights, _head_expand_matrix(c), _head_expand_matrix(P_C), *_ssd_masks(c),
      st_in.reshape(nb, H_C * P_C, N_C), conv_in)


def kernel(x_prompt, x_sample, cache_swa_k, cache_swa_v, state_gla, state_ssd, state_conv, rms_g, w_in_ab, w_gate_up_a, b_gate_a, g_out_a, g_q_b, g_k_b, sink_b, rel_bias, w_out_ab, w_in_c, conv_w_c, conv_b_c, dt_bias_c, a_log_c, d_skip_c, g_y_c, w_out_c):
    bp, seq_len, _ = x_prompt.shape
    nb, dec_len, _ = x_sample.shape
    assert bp == 1 and seq_len % PROMPT_BLOCK == 0 and seq_len % AB_PROMPT_BLOCK == 0 and seq_len >= WINDOW
    wab = _ab_weights(rms_g[0], w_in_ab[0], w_gate_up_a[0], b_gate_a[0], g_out_a[0], g_q_b[0], g_k_b[0],
                      w_out_ab[0])
    yp, gla_p, k_p, v_p = _ab_prompt(x_prompt[0], sink_b[0], rel_bias, wab)
    ys, gla_s, k_s, v_s = _ab_sample(x_sample, sink_b[0], rel_bias, wab, state_gla[0], cache_swa_k[0],
                                     cache_swa_v[0])
    wc = _c_weights(rms_g[1], w_in_c[0], conv_w_c[0], conv_b_c[0], dt_bias_c[0], a_log_c[0], d_skip_c[0],
                    g_y_c[0], w_out_c[0])
    yp, ssd_p, conv_p = _c_prompt(yp, wc)
    ys, ssd_s, conv_s = _c_sample(ys.reshape(nb, dec_len, D_MODEL), wc, state_ssd[0], state_conv[0])
    return (
        yp.reshape(1, seq_len, D_MODEL),
        ys.reshape(nb, dec_len, D_MODEL),
        gla_p.reshape(1, 1, H_A, DK_A, DV_A),
        k_p.reshape(1, 1, WINDOW, KVH_B, HD_B),
        v_p.reshape(1, 1, WINDOW, KVH_B, HD_B),
        ssd_p.reshape(1, 1, H_C, P_C, N_C),
        conv_p[8 - (CONV_W - 1):].reshape(1, 1, CONV_W - 1, CONV_DIM),
        gla_s.reshape(1, nb, H_A, DK_A, DV_A),
        k_s.reshape(1, nb, dec_len, KVH_B, HD_B),
        v_s.reshape(1, nb, dec_len, KVH_B, HD_B),
        ssd_s.reshape(1, nb, H_C, P_C, N_C),
        conv_s.reshape(1, nb, CONV_W - 1, CONV_DIM),
    )
```

```python
import math
import types

import numpy as np
import jax
import jax.numpy as jnp
from jax import lax
from jax.experimental import pallas as pl
from jax.experimental.pallas import tpu as pltpu

F32 = jnp.float32
BF16 = jnp.bfloat16

D_MODEL = 1024
CHUNK = 64
EPS = 1e-6
H_A = 4
DK_A = 64
DV_A = 128
GATE_RANK_A = 16
GATE_NORM_A = 16.0
H_B = 8
KVH_B = 2
G_B = H_B // KVH_B
HD_B = 64
WINDOW = 128
N_BUCKETS = 32
MAX_DISTANCE = 128
D_INNER_C = 2048
P_C = 64
H_C = 32
G_C = 4
HPG_C = 8
N_C = 128
CONV_W = 4
CONV_DIM = D_INNER_C + 2 * G_C * N_C

LOG2E = 1.4426950408889634
LANES = 128

A_Q, A_K, A_V, A_Z = 0, 256, 512, 1024
B_Q, B_K, B_V, B_Z = 1536, 2048, 2176, 2304
A_G = 2816
AB_COLS = 2944
C_Z, C_X, C_B, C_C, C_DT = 0, 2048, 4096, 4608, 5120
C_COLS = 5248

PROMPT_BLOCK = 256
AB_PROMPT_BLOCK = 512
VMEM_LIMIT = 60 * 1024 * 1024


def _dot(a, b):
    return jnp.dot(a.astype(BF16), b.astype(BF16), preferred_element_type=F32)


def _dot_nt(a, b):
    return lax.dot_general(a.astype(BF16), b.astype(BF16), (((1,), (1,)), ((), ())),
                           preferred_element_type=F32)


def _dot_tn(a, b):
    return lax.dot_general(a.astype(BF16), b.astype(BF16), (((0,), (0,)), ((), ())),
                           preferred_element_type=F32)


def _split_bf16(x, terms):
    out = []
    r = x
    for _ in range(terms):
        h = r.astype(BF16)
        out.append(h)
        r = r - h.astype(F32)
    return out


def _sel_dot_left(sel, x, terms):
    acc = None
    for h in _split_bf16(x, terms):
        d = jnp.dot(sel, h, preferred_element_type=F32)
        acc = d if acc is None else acc + d
    return acc


def _rms_rows(x):
    return x * lax.rsqrt(jnp.mean(x * x, axis=-1, keepdims=True) + EPS)


def _exp_neg(x):
    return jnp.exp2(x * (-LOG2E))


def _silu(x):
    return x * (1.0 / (1.0 + _exp_neg(x)))


def _softplus(x):
    return jnp.maximum(x, 0.0) + jnp.log(1.0 + _exp_neg(jnp.abs(x)))


def _log_sigmoid(x):
    return jnp.minimum(x, 0.0) - jnp.log(1.0 + _exp_neg(jnp.abs(x)))


def _tril(n):
    r = lax.broadcasted_iota(jnp.int32, (n, n), 0)
    c = lax.broadcasted_iota(jnp.int32, (n, n), 1)
    return r >= c


def _head_mean_sq(x):
    out = []
    for p in range(x.shape[1] // LANES):
        sq = x[:, p * LANES:(p + 1) * LANES]
        sq = sq * sq
        low = lax.broadcasted_iota(jnp.int32, sq.shape, 1) < HD_B
        lo = jnp.sum(jnp.where(low, sq, 0.0), axis=-1, keepdims=True)
        hi = jnp.sum(jnp.where(low, 0.0, sq), axis=-1, keepdims=True)
        out.append(jnp.where(low, lo, hi))
    return jnp.concatenate(out, axis=1) * (1.0 / HD_B)


def _tile_kv_heads(kv):
    low = lax.broadcasted_iota(jnp.int32, kv.shape, 1) < HD_B
    swapped = pltpu.roll(kv, HD_B, 1)
    out = []
    for base in (jnp.where(low, kv, swapped), jnp.where(low, swapped, kv)):
        base = base.astype(BF16)
        out.append(jnp.concatenate([base, base], axis=1))
    return out


def _ab_dense_in(x, rms_g, w_in_ref, gq, gk, proj_ref, qn_ref):
    h = (_rms_rows(x) * rms_g).astype(BF16)
    rows = min(256, x.shape[0])
    for m in range(0, x.shape[0], rows):
        proj_ref[m:m + rows, :] = jnp.dot(h[m:m + rows], w_in_ref[...], preferred_element_type=F32)
    qb = proj_ref[:, B_Q:B_Q + 512]
    qn_ref[...] = qb * lax.rsqrt(_head_mean_sq(qb) + EPS) * gq
    kb = proj_ref[:, B_K:B_K + 128]
    kn = kb * lax.rsqrt(_head_mean_sq(kb) + EPS) * gk
    vb = proj_ref[:, B_V:B_V + 128]
    return kn, vb


def _advance(gens, yielded=None):
    alive = []
    for gen in gens:
        try:
            value = next(gen)
            alive.append(gen)
            if yielded is not None and value is not None:
                yielded.append(value)
        except StopIteration:
            pass
    return alive


def _interleave(gens):
    gens = list(gens)
    while gens:
        gens = _advance(gens)


def _ab_chunk(r, c, kb, proj_ref, qn_ref, o_ref, state, k_band, v_band, bias_ref, sink_ref,
              w_gu, b_gate, g_out, first_valid_col):
    rows = pl.ds(r, c)
    gate = _dot(proj_ref[rows, A_G:A_G + LANES], w_gu) + b_gate
    yield
    g = _log_sigmoid(gate) * (LOG2E / GATE_NORM_A)
    tril = _tril(c)
    b = _sel_dot_left(tril.astype(BF16), g, 3)
    bl = b[c - 1:c, :]
    bl_tile = jnp.broadcast_to(bl, (LANES, H_A * DK_A))
    bl_rows = jnp.concatenate([bl_tile[:, i * LANES:(i + 1) * LANES].T
                               for i in range(H_A * DK_A // LANES)], axis=0)
    yield
    q = proj_ref[rows, A_Q:A_Q + 256] * (DK_A ** -0.5)
    k = proj_ref[rows, A_K:A_K + 256]
    v = proj_ref[rows, A_V:A_V + 512].astype(BF16)
    qe = (q * jnp.exp2(b)).astype(BF16)
    kd = k * jnp.exp2(-b)
    kd2 = (k * jnp.exp2(bl - b)).astype(BF16)
    lane_head = lax.broadcasted_iota(jnp.int32, (c, LANES), 1) // DK_A
    row_head = lax.broadcasted_iota(jnp.int32, (LANES, LANES), 0) // DK_A
    att = []
    for p in range(2):
        kp = kd[:, p * 128:(p + 1) * 128]
        kpair = jnp.concatenate([jnp.where(lane_head == j, kp, 0.0) for j in range(2)], axis=0).astype(BF16)
        att.append(_dot_nt(qe[:, p * 128:(p + 1) * 128], kpair))
    upd = [_dot_tn(kd2[:, p * 128:(p + 1) * 128], v[:, p * 256:(p + 1) * 256]) for p in range(2)]
    yield
    row = lax.broadcasted_iota(jnp.int32, (c, 2 * c), 0)
    col = lax.broadcasted_iota(jnp.int32, (c, 2 * c), 1) % c
    att = [jnp.where(row >= col, a, 0.0).astype(BF16) for a in att]
    yield
    s_prev = state[0]
    s_new, oh = [], []
    zeros_v = jnp.zeros((c, DV_A), BF16)
    for p in range(2):
        sp = s_prev[p * 128:(p + 1) * 128, :]
        s_bd = jnp.concatenate([jnp.where(row_head == j, sp, 0.0) for j in range(2)], axis=1).astype(BF16)
        v0, v1 = v[:, 2 * p * DV_A:(2 * p + 1) * DV_A], v[:, (2 * p + 1) * DV_A:(2 * p + 2) * DV_A]
        v_bd = jnp.concatenate([jnp.concatenate([v0, zeros_v], axis=1),
                                jnp.concatenate([zeros_v, v1], axis=1)], axis=0)
        qp = qe[:, p * 128:(p + 1) * 128]
        if (2 * c) % LANES == 0:
            o_pair = jnp.dot(jnp.concatenate([att[p], qp], axis=1), jnp.concatenate([v_bd, s_bd], axis=0),
                             preferred_element_type=F32)
        else:
            o_pair = (jnp.dot(att[p], v_bd, preferred_element_type=F32)
                      + jnp.dot(qp, s_bd, preferred_element_type=F32))
        oh += [o_pair[:, :DV_A], o_pair[:, DV_A:]]
        u = jnp.where(row_head == 0, upd[p][:, :128], upd[p][:, 128:])
        s_new.append(jnp.exp2(bl_rows[p * 128:(p + 1) * 128, :]) * sp + u)
    state[0] = jnp.concatenate(s_new, axis=0)
    yield
    for hd in range(H_A):
        z = proj_ref[rows, A_Z + hd * 128:A_Z + (hd + 1) * 128]
        o_ref[rows, hd * 128:(hd + 1) * 128] = (_rms_rows(oh[hd]) * g_out * _silu(z)).astype(BF16)
    yield
    lane256 = lax.broadcasted_iota(jnp.int32, (c, 256), 1) // HD_B
    srow = lax.broadcasted_iota(jnp.int32, (G_B * c, 1), 0) // c
    scores = []
    for kvh in range(KVH_B):
        qn = qn_ref[rows, kvh * 256:(kvh + 1) * 256]
        qs = jnp.concatenate([jnp.where(lane256 == gq_, qn, 0.0) for gq_ in range(G_B)], axis=0)
        scores.append(_dot_nt(qs, k_band(kvh)))
    yield
    probs, dens = [], []
    for kvh in range(KVH_B):
        s = scores[kvh] * (LOG2E * HD_B ** -0.5) + bias_ref[kvh]
        if first_valid_col is not None:
            col = lax.broadcasted_iota(jnp.int32, (G_B * c, kb), 1)
            s = jnp.where(col >= first_valid_col, s, -jnp.inf)
        sink = jnp.zeros((G_B * c, 1), F32)
        for gq_ in range(G_B):
            sink = jnp.where(srow == gq_, sink_ref[kvh * G_B + gq_] * LOG2E, sink)
        m = jnp.maximum(jnp.max(s, axis=-1, keepdims=True), sink)
        pr = jnp.exp2(s - m)
        dens.append(jnp.sum(pr, axis=-1, keepdims=True) + jnp.exp2(sink - m))
        probs.append(pr.astype(BF16))
    yield
    outs = [_dot(probs[kvh], v_band(kvh)) for kvh in range(KVH_B)]
    yield
    for kvh in range(KVH_B):
        ost = outs[kvh] / dens[kvh]
        ob = jnp.zeros((c, 256), F32)
        for gq_ in range(G_B):
            ob = ob + jnp.where(lane256 == gq_, ost[gq_ * c:(gq_ + 1) * c, :], 0.0)
        z = proj_ref[rows, B_Z + kvh * 256:B_Z + (kvh + 1) * 256]
        o_ref[rows, 512 + kvh * 256:512 + (kvh + 1) * 256] = (ob * _silu(z)).astype(BF16)


def _ab_prompt_kernel(sink_ref, x_ref, rms_g_ref, w_in_ref, w_gu_ref, b_gate_ref, g_out_ref, gq_ref,
                      gk_ref, bias_ref, w_out_ref,
                      y_ref, s_out_ref, k_out_ref, v_out_ref,
                      proj_ref, qn_ref, o_ref, kband_ref, vband_ref, s_ref):
    t = x_ref.shape[0]
    c = CHUNK
    kb = WINDOW + c
    nchunk = t // c
    step = pl.program_id(0)

    @pl.when(step == 0)
    def _():
        s_ref[...] = jnp.zeros_like(s_ref)
        kband_ref[:, t:t + WINDOW, :] = jnp.zeros((KVH_B, WINDOW, 256), BF16)
        vband_ref[:, t:t + WINDOW, :] = jnp.zeros((KVH_B, WINDOW, 256), BF16)

    for kvh in range(KVH_B):
        kband_ref[kvh, 0:WINDOW, :] = kband_ref[kvh, t:t + WINDOW, :]
        vband_ref[kvh, 0:WINDOW, :] = vband_ref[kvh, t:t + WINDOW, :]

    x = x_ref[...]
    kn, vb = _ab_dense_in(x, rms_g_ref[...], w_in_ref, gq_ref[...], gk_ref[...], proj_ref, qn_ref)
    k_out_ref[...] = kn[t - WINDOW:, :]
    v_out_ref[...] = vb[t - WINDOW:, :]
    for kvh, (kt, vt) in enumerate(zip(_tile_kv_heads(kn), _tile_kv_heads(vb))):
        kband_ref[kvh, WINDOW:WINDOW + t, :] = kt
        vband_ref[kvh, WINDOW:WINDOW + t, :] = vt

    w_gu = w_gu_ref[...]
    b_gate = b_gate_ref[...]
    g_out = g_out_ref[...]

    state = [s_ref[...]]

    def chunk(i):
        r = i * c
        return _ab_chunk(r, c, kb, proj_ref, qn_ref, o_ref, state,
                         lambda kvh: kband_ref[kvh, pl.ds(r, kb), :],
                         lambda kvh: vband_ref[kvh, pl.ds(r, kb), :],
                         bias_ref, sink_ref, w_gu, b_gate, g_out, (2 - (step * nchunk + i)) * c)

    _interleave(chunk(i) for i in range(nchunk))
    s_ref[...] = state[0]
    y_ref[...] = x + jnp.dot(o_ref[...], w_out_ref[...], preferred_element_type=F32)
    s_out_ref[...] = s_ref[...]


def _ab_sample_kernel(sink_ref, x_ref, rms_g_ref, w_in_ref, w_gu_ref, b_gate_ref, g_out_ref, gq_ref,
                      gk_ref, bias_ref, w_out_ref,
                      s_in_ref, kc_ref, vc_ref,
                      y_ref, s_out_ref, k_out_ref, v_out_ref,
                      proj_ref, qn_ref, o_ref, kband_ref, vband_ref):
    nb, wc = kc_ref.shape[0], kc_ref.shape[1]
    t = x_ref.shape[0]
    c = t // nb
    kb = wc + c
    x = x_ref[...]
    kn, vb = _ab_dense_in(x, rms_g_ref[...], w_in_ref, gq_ref[...], gk_ref[...], proj_ref, qn_ref)
    k_out_ref[...] = kn
    v_out_ref[...] = vb
    k_new, v_new = _tile_kv_heads(kn), _tile_kv_heads(vb)
    for bi in range(nb):
        k_old, v_old = _tile_kv_heads(kc_ref[bi]), _tile_kv_heads(vc_ref[bi])
        for kvh in range(KVH_B):
            kband_ref[kvh, bi, 0:wc, :] = k_old[kvh]
            vband_ref[kvh, bi, 0:wc, :] = v_old[kvh]
            kband_ref[kvh, bi, wc:kb, :] = k_new[kvh][bi * c:(bi + 1) * c, :]
            vband_ref[kvh, bi, wc:kb, :] = v_new[kvh][bi * c:(bi + 1) * c, :]

    w_gu = w_gu_ref[...]
    b_gate = b_gate_ref[...]
    g_out = g_out_ref[...]

    states = [[s_in_ref[bi]] for bi in range(nb)]

    def seq(bi):
        return _ab_chunk(bi * c, c, kb, proj_ref, qn_ref, o_ref, states[bi],
                         lambda kvh: kband_ref[kvh, bi], lambda kvh: vband_ref[kvh, bi],
                         bias_ref, sink_ref, w_gu, b_gate, g_out, None)

    _interleave(seq(bi) for bi in range(nb))
    for bi in range(nb):
        s_out_ref[bi] = states[bi][0]
    y_ref[...] = x + jnp.dot(o_ref[...], w_out_ref[...], preferred_element_type=F32)


GROUP_W = D_INNER_C // G_C


def _c_chunk(r, c, io, state, e_s, e_p, neg_mask, bd_mask, dskip, g_y):
    dtc = io.dt(r)
    acum = _sel_dot_left(_tril(c).astype(BF16), io.da(r), 3) * LOG2E
    yield acum
    lhs = jnp.concatenate([jnp.concatenate(_split_bf16(acum, 2), axis=1),
                           jnp.concatenate(_split_bf16(dtc, 2), axis=1)], axis=0)
    both_p = jnp.dot(lhs, e_p, preferred_element_type=F32)
    xa_p, dt_p = both_p[:c], both_p[c:]
    xa_s = xa_p if c == P_C else jnp.dot(lhs[:c], e_s, preferred_element_type=F32)
    yield dt_p
    acum_t = acum.T
    a_row = jnp.concatenate([acum_t[h:h + 1, :] for h in range(H_C)], axis=1)
    wmat = jnp.exp2((xa_s - a_row) + neg_mask)
    al_p = xa_p[c - 1:c, :]
    xs = io.x(r)
    xdt = xs * dt_p
    xdt_bf = xdt.astype(BF16)
    xw = (xdt * jnp.exp2(al_p - xa_p)).astype(BF16)
    dec = jnp.exp2(al_p)
    bg, cg, cb = [], [], []
    for g in range(G_C):
        bg.append(io.b(r, g).astype(BF16))
        cg.append(io.c(r, g).astype(BF16))
        cb.append(_dot_nt(cg[g], jnp.concatenate([bg[g]] * HPG_C, axis=0)))
    yield cb[-1]
    ys = []
    for g in range(G_C):
        mg = (cb[g] * wmat[:, g * HPG_C * c:(g + 1) * HPG_C * c]).astype(BF16)
        for j in range(2):
            xj = xdt_bf[:, g * GROUP_W + j * 256:g * GROUP_W + (j + 1) * 256]
            bd = jnp.concatenate([xj] * 4, axis=0) * bd_mask
            ys.append(jnp.dot(mg[:, j * 4 * c:(j + 1) * 4 * c], bd, preferred_element_type=F32))
    upd = [_dot_tn(bg[g], xw[:, g * GROUP_W:(g + 1) * GROUP_W]) for g in range(G_C)]
    yield upd[-1]
    st_prev = state[0]
    y_inter, st_new = [], []
    for g in range(G_C):
        sl = slice(g * GROUP_W, (g + 1) * GROUP_W)
        y_inter.append(_dot(cg[g], st_prev[:, sl]))
        st_new.append(st_prev[:, sl] * dec[:, sl] + upd[g])
    state[0] = jnp.concatenate(st_new, axis=1)
    yield st_new[-1]
    for g in range(G_C):
        sl = slice(g * GROUP_W, (g + 1) * GROUP_W)
        y = jnp.concatenate(ys[2 * g:2 * g + 2], axis=1) + y_inter[g] * jnp.exp2(xa_p[:, sl])
        y = y + dskip[:, sl] * xs[:, sl]
        y = y * io.gate(r, g)
        io.put_o(r, g, (_rms_rows(y) * g_y[:, sl]).astype(BF16))


def _c_dt(dt_cols, dt_bias, a_log):
    dt = _softplus(dt_cols + dt_bias)
    return dt, dt * (-jnp.exp(a_log))


CONV_PITCH = PROMPT_BLOCK // 8 + 1
CONV_ROWS = 8 * CONV_PITCH
CONV_TILES = CONV_DIM // LANES
C_ROUNDS = 12
C_IN_SLABS = tuple((C_X + 512 * k, C_X + 512 * (k + 1)) for k in range(6)) + tuple(
    (C_Z + 512 * k, C_Z + 512 * (k + 1)) for k in range(4)) + ((C_DT, C_COLS),)
C_MIX_FIRST_ROUND = 5
C_OUT_ROUNDS = (1, 3, 7, 11)
assert len(C_IN_SLABS) <= C_ROUNDS and sum(hi - lo for lo, hi in C_IN_SLABS) == C_COLS


def _conv_tile(ubuf_ref, act_ref, j, conv_w, conv_b):
    w = [jnp.broadcast_to(conv_w[i:i + 1, j * LANES:(j + 1) * LANES], (8, LANES)) for i in range(CONV_W)]
    b = jnp.broadcast_to(conv_b[:, j * LANES:(j + 1) * LANES], (8, LANES))
    for a in range(CONV_PITCH):
        acc = b
        for i in range(CONV_W):
            acc = acc + w[i] * ubuf_ref[j, pl.ds(8 - (CONV_W - 1) + i + a, 8, stride=CONV_PITCH), :]
        act_ref[j, pl.ds(a, 8, stride=CONV_PITCH), :] = _silu(acc)


def _c_prompt_kernel(xin_ref, xres_ref, rms_g_ref, w_in_ref, conv_w_ref, conv_b_ref, dt_bias_ref,
                     a_log_ref, dskip_ref, g_y_ref, w_out_ref, e_s_ref, neg_mask_ref, bd_mask_ref,
                     y_ref, st_out_ref, conv_out_ref,
                     z_ref, ubuf_ref, act_ref, dt_ref, da_ref, o_ref, st_ref):
    t = xin_ref.shape[0]
    c = CHUNK
    s = pl.program_id(0)
    n_blocks = pl.num_programs(0) - 2

    @pl.when(s == 0)
    def _():
        o_ref[1] = jnp.zeros(o_ref.shape[1:], BF16)
        ubuf_ref[...] = jnp.zeros_like(ubuf_ref)

    @pl.when(s == 0)
    def _():
        st_ref[...] = jnp.zeros_like(st_ref)

    def in_stage(slot_in):
        h = (_rms_rows(xin_ref[...]) * rms_g_ref[...]).astype(BF16)
        conv_w = conv_w_ref[...]
        conv_b = conv_b_ref[...]
        act = act_ref.at[slot_in]
        for rnd, (lo, hi) in enumerate(C_IN_SLABS):
            if rnd:
                yield
            slab = jnp.dot(h, w_in_ref[:, lo:hi], preferred_element_type=F32)
            if lo == C_DT:
                dt, da = _c_dt(slab, dt_bias_ref[...], a_log_ref[...])
                dt_ref[slot_in] = dt
                da_ref[slot_in] = da
            elif lo >= C_X:
                for j in range((lo - C_X) // LANES, (hi - C_X) // LANES):
                    col = C_X + j * LANES - lo
                    ubuf_ref[j, 8:16, :] = ubuf_ref[j, CONV_ROWS:CONV_ROWS + 8, :]
                    ubuf_ref[j, 16:16 + t, :] = slab[:, col:col + LANES]
                    _conv_tile(ubuf_ref, act, j, conv_w, conv_b)
            else:
                z_ref[slot_in, :, lo:hi] = _silu(slab)

    n_x = D_INNER_C // LANES
    n_g = N_C // LANES

    def mix_stage(slot_mix):
        def put_o(r, g, value):
            o_ref[slot_mix, pl.ds(r, c), g * GROUP_W:(g + 1) * GROUP_W] = value

        def act_rows(r, j):
            return act_ref[slot_mix, j, pl.ds(8 + r, c), :]

        io = types.SimpleNamespace(
            dt=lambda r: dt_ref[slot_mix, pl.ds(r, c), :],
            da=lambda r: da_ref[slot_mix, pl.ds(r, c), :],
            x=lambda r: jnp.concatenate([act_rows(r, j) for j in range(n_x)], axis=1),
            b=lambda r, g: act_rows(r, n_x + g * n_g),
            c=lambda r, g: act_rows(r, n_x + (G_C + g) * n_g),
            gate=lambda r, g: z_ref[slot_mix, pl.ds(r, c), C_Z + g * GROUP_W:C_Z + (g + 1) * GROUP_W],
            put_o=put_o)
        for _ in range(C_MIX_FIRST_ROUND):
            yield
        state = [st_ref[...]]
        e_s = e_s_ref[...]
        chunks = [_c_chunk(i * c, c, io, state, e_s, e_s, neg_mask_ref[...], bd_mask_ref[...],
                           dskip_ref[...], g_y_ref[...]) for i in range(t // c)]
        chunks = _advance(chunks)
        while chunks:
            yield
            chunks = _advance(chunks)
        st_ref[...] = state[0]

    def out_stage(slot_in):
        quarter = 0
        for rnd in range(C_ROUNDS):
            if rnd:
                yield
            if rnd in C_OUT_ROUNDS:
                cols = slice(quarter * 256, (quarter + 1) * 256)
                y_ref[:, cols] = xres_ref[:, cols] + jnp.dot(o_ref[slot_in], w_out_ref[:, cols],
                                                             preferred_element_type=F32)
                quarter += 1

    slot_in = s % 2

    @pl.when(s == 0)
    def _():
        _interleave([in_stage(slot_in)])

    @pl.when(jnp.logical_and(s > 0, s <= n_blocks))
    def _():
        _interleave([in_stage(slot_in), mix_stage(1 - slot_in), out_stage(slot_in)])

    @pl.when(s == n_blocks + 1)
    def _():
        _interleave([out_stage(slot_in)])

    @pl.when(s == n_blocks)
    def _():
        st_out_ref[...] = st_ref[...].T
        for j in range(CONV_TILES):
            conv_out_ref[:, j * LANES:(j + 1) * LANES] = ubuf_ref[j, CONV_ROWS:CONV_ROWS + 8, :]


def _conv_rows(ubuf, nrows, conv_w, conv_b):
    acc = conv_b
    for i in range(CONV_W):
        acc = acc + conv_w[i:i + 1, :] * ubuf[pl.ds(8 - (CONV_W - 1) + i, nrows), :]
    return _silu(acc)


def _c_sample_kernel(x_ref, rms_g_ref, w_in_ref, conv_w_ref, conv_b_ref, dt_bias_ref, a_log_ref,
                     dskip_ref, g_y_ref, w_out_ref, e_s_ref, e_p_ref, neg_mask_ref, bd_mask_ref,
                     st_in_ref, conv_in_ref,
                     y_ref, st_out_ref, conv_out_ref,
                     proj_ref, ubuf_ref, act_ref, dt_ref, da_ref, o_ref):
    nb = st_in_ref.shape[0]
    t = x_ref.shape[0]
    c = t // nb
    x = x_ref[...]
    h = (_rms_rows(x) * rms_g_ref[...]).astype(BF16)
    proj_ref[...] = jnp.dot(h, w_in_ref[...], preferred_element_type=F32)
    dt, da = _c_dt(proj_ref[:, C_DT:C_DT + LANES], dt_bias_ref[...], a_log_ref[...])
    dt_ref[...] = dt
    da_ref[...] = da
    conv_w = conv_w_ref[...]
    conv_b = conv_b_ref[...]
    ubuf_ref[0:8, :] = jnp.zeros((8, CONV_DIM), F32)
    for bi in range(nb):
        ubuf_ref[8 - (CONV_W - 1):8, :] = conv_in_ref[bi]
        ubuf_ref[8:8 + c, :] = proj_ref[bi * c:(bi + 1) * c, C_X:C_X + CONV_DIM]
        act_ref[bi * c:(bi + 1) * c, :] = _conv_rows(ubuf_ref, c, conv_w, conv_b)
        conv_out_ref[bi] = ubuf_ref[8 + c - (CONV_W - 1):8 + c, :]

    def put_o(r, g, value):
        o_ref[pl.ds(r, c), g * GROUP_W:(g + 1) * GROUP_W] = value

    io = types.SimpleNamespace(
        dt=lambda r: dt_ref[pl.ds(r, c), :],
        da=lambda r: da_ref[pl.ds(r, c), :],
        x=lambda r: act_ref[pl.ds(r, c), 0:D_INNER_C],
        b=lambda r, g: act_ref[pl.ds(r, c), D_INNER_C + g * N_C:D_INNER_C + (g + 1) * N_C],
        c=lambda r, g: act_ref[pl.ds(r, c), D_INNER_C + (G_C + g) * N_C:D_INNER_C + (G_C + g + 1) * N_C],
        gate=lambda r, g: _silu(proj_ref[pl.ds(r, c), C_Z + g * GROUP_W:C_Z + (g + 1) * GROUP_W]),
        put_o=put_o)
    e_s = e_s_ref[...]
    e_p = e_p_ref[...]
    neg_mask = neg_mask_ref[...]
    bd_mask = bd_mask_ref[...]
    dskip = dskip_ref[...]
    g_y = g_y_ref[...]

    states = [[st_in_ref[bi].T] for bi in range(nb)]
    _interleave(_c_chunk(bi * c, c, io, states[bi], e_s, e_p, neg_mask, bd_mask, dskip, g_y)
                for bi in range(nb))
    for bi in range(nb):
        st_out_ref[bi] = states[bi][0].T
    y_ref[...] = x + jnp.dot(o_ref[...], w_out_ref[...], preferred_element_type=F32)


def _bucket_table(q_off, k_off):
    n = q_off[:, None] - k_off[None, :]
    half = N_BUCKETS // 2
    max_exact = half // 2
    side = np.where(n < 0, half, 0)
    n = np.abs(n)
    nf = np.maximum(n, max_exact).astype(np.float32)
    large = max_exact + (np.log(nf / np.float32(max_exact)) / np.float32(math.log(MAX_DISTANCE / max_exact))
                         * np.float32(half - max_exact)).astype(np.int32)
    large = np.minimum(large, half - 1)
    return side + np.where(n < max_exact, n, large)


def _bias_rows(rel_bias, q_off, k_off):
    bucket = _bucket_table(q_off, k_off)
    onehot = jnp.asarray(np.eye(N_BUCKETS, dtype=np.float32)[bucket])
    bias = jnp.einsum('qkb,bh->hqk', onehot, rel_bias.astype(F32), precision=lax.Precision.HIGHEST) * LOG2E
    return bias.reshape(KVH_B, G_B * q_off.shape[0], k_off.shape[0])


def _head_expand_matrix(per_head):
    m = np.zeros((LANES, H_C * per_head), np.float32)
    for h in range(H_C):
        m[h, h * per_head:(h + 1) * per_head] = 1.0
    return jnp.asarray(np.concatenate([m, m], axis=0), BF16)


def _full(shape):
    return pl.BlockSpec(shape, lambda *_: (0,) * len(shape))


def _const(shape):
    return pl.BlockSpec(shape, lambda *_: (0,) * len(shape), pipeline_mode=pl.Buffered(1))


CAST_ROWS = 256


def _cast_columns(wt_ref, w_bf_ref, src, dst, n):
    for off in range(0, n - n % LANES, CAST_ROWS):
        rows = min(CAST_ROWS, n - n % LANES - off)
        w_bf_ref[:, dst + off:dst + off + rows] = wt_ref[src + off:src + off + rows, :].T.astype(BF16)
    rest = n % LANES
    if rest:
        off = n - rest
        tail = jnp.concatenate([wt_ref[src + off:src + n, :], jnp.zeros((LANES - rest, wt_ref.shape[1]), F32)],
                               axis=0)
        w_bf_ref[:, dst + off:dst + off + LANES] = tail.T.astype(BF16)


def _ab_cast_kernel(wt_ref, w_out_ref, w_in_bf_ref, w_out_bf_ref):
    g0 = A_Z
    _cast_columns(wt_ref, w_in_bf_ref, 0, 0, g0)
    _cast_columns(wt_ref, w_in_bf_ref, g0 + GATE_RANK_A, g0, A_G - g0)
    _cast_columns(wt_ref, w_in_bf_ref, g0, A_G, GATE_RANK_A)
    w_out_bf_ref[...] = w_out_ref[...].astype(BF16)


def _c_cast_kernel(wt_ref, w_out_ref, w_in_bf_ref, w_out_bf_ref):
    _cast_columns(wt_ref, w_in_bf_ref, 0, 0, wt_ref.shape[0])
    w_out_bf_ref[...] = w_out_ref[...].astype(BF16)


def _cast_weights(body, w_in, w_out, cols, name):
    vm = pl.BlockSpec(memory_space=pltpu.VMEM)
    return pl.pallas_call(
        body, in_specs=[vm, vm], out_specs=(vm, vm),
        out_shape=(jax.ShapeDtypeStruct((w_in.shape[0], cols), BF16),
                   jax.ShapeDtypeStruct(w_out.shape, BF16)),
        name=name, compiler_params=pltpu.CompilerParams(vmem_limit_bytes=VMEM_LIMIT),
    )(w_in.T, w_out)


def _ab_weights(rms_g, w_in, w_gate_up, b_gate, g_out, g_q, g_k, w_out):
    w_in_r, w_out_bf = _cast_weights(_ab_cast_kernel, w_in, w_out, AB_COLS, "ab_cast")
    w_gu = jnp.concatenate([w_gate_up, jnp.zeros((LANES - GATE_RANK_A, H_A * DK_A), w_gate_up.dtype)],
                           axis=0).astype(BF16)
    return (rms_g.reshape(1, D_MODEL), w_in_r, w_gu, b_gate.reshape(1, -1), g_out.reshape(1, DV_A),
            jnp.tile(g_q, H_B).reshape(1, -1), jnp.tile(g_k, KVH_B).reshape(1, -1), w_out_bf)


def _ab_prompt(x, sink, rel_bias, weights):
    rms_g, w_in, w_gu, b_gate, g_out, gq, gk, w_out = weights
    length = x.shape[0]
    t = AB_PROMPT_BLOCK
    c = CHUNK
    kb = WINDOW + c
    bias = _bias_rows(rel_bias, np.arange(c), np.arange(kb) - WINDOW)
    row_blk = lambda i: (i, 0)
    in_specs = [
        pl.BlockSpec(memory_space=pltpu.SMEM),
        pl.BlockSpec((t, D_MODEL), row_blk),
        _const((1, D_MODEL)), _const((D_MODEL, AB_COLS)), _const((LANES, 256)), _const((1, 256)),
        _const((1, DV_A)), _const((1, 512)), _const((1, 128)), _const((KVH_B, G_B * c, kb)),
        _const((D_MODEL, D_MODEL)),
    ]
    out_shape = (jax.ShapeDtypeStruct((length, D_MODEL), F32),
                 jax.ShapeDtypeStruct((H_A * DK_A, DV_A), F32),
                 jax.ShapeDtypeStruct((WINDOW, KVH_B * HD_B), F32),
                 jax.ShapeDtypeStruct((WINDOW, KVH_B * HD_B), F32))
    out_specs = (pl.BlockSpec((t, D_MODEL), row_blk), _full((H_A * DK_A, DV_A)),
                 _full((WINDOW, KVH_B * HD_B)), _full((WINDOW, KVH_B * HD_B)))
    scratch = [pltpu.VMEM((t, AB_COLS), F32), pltpu.VMEM((t, 512), F32), pltpu.VMEM((t, D_MODEL), BF16),
               pltpu.VMEM((KVH_B, t + WINDOW, 256), BF16), pltpu.VMEM((KVH_B, t + WINDOW, 256), BF16),
               pltpu.VMEM((H_A * DK_A, DV_A), F32)]
    return pl.pallas_call(
        _ab_prompt_kernel, grid=(length // t,), in_specs=in_specs, out_specs=out_specs,
        out_shape=out_shape, scratch_shapes=scratch, name="ab_prompt",
        compiler_params=pltpu.CompilerParams(dimension_semantics=("arbitrary",),
                                             vmem_limit_bytes=VMEM_LIMIT),
    )(sink, x, rms_g, w_in, w_gu, b_gate, g_out, gq, gk, bias, w_out)


def _ab_sample(x, sink, rel_bias, weights, s_in, k_cache, v_cache):
    rms_g, w_in, w_gu, b_gate, g_out, gq, gk, w_out = weights
    nb, c, _ = x.shape
    wc = k_cache.shape[1]
    kb = wc + c
    t = nb * c
    bias = _bias_rows(rel_bias, np.arange(c), np.arange(kb) - wc)
    vm = pl.BlockSpec(memory_space=pltpu.VMEM)
    in_specs = [pl.BlockSpec(memory_space=pltpu.SMEM)] + [vm] * 13
    out_shape = (jax.ShapeDtypeStruct((t, D_MODEL), F32),
                 jax.ShapeDtypeStruct((nb, H_A * DK_A, DV_A), F32),
                 jax.ShapeDtypeStruct((t, KVH_B * HD_B), F32),
                 jax.ShapeDtypeStruct((t, KVH_B * HD_B), F32))
    scratch = [pltpu.VMEM((t, AB_COLS), F32), pltpu.VMEM((t, 512), F32), pltpu.VMEM((t, D_MODEL), BF16),
               pltpu.VMEM((KVH_B, nb, kb, 256), BF16), pltpu.VMEM((KVH_B, nb, kb, 256), BF16)]
    return pl.pallas_call(
        _ab_sample_kernel, in_specs=in_specs, out_specs=(vm,) * 4, out_shape=out_shape,
        scratch_shapes=scratch, name="ab_sample",
        compiler_params=pltpu.CompilerParams(vmem_limit_bytes=VMEM_LIMIT),
    )(sink, x.reshape(t, D_MODEL), rms_g, w_in, w_gu, b_gate, g_out, gq, gk, bias, w_out,
      s_in.reshape(nb, H_A * DK_A, DV_A), k_cache.reshape(nb, wc, KVH_B * HD_B),
      v_cache.reshape(nb, wc, KVH_B * HD_B))


def _c_weights(rms_g, w_in, conv_w, conv_b, dt_bias, a_log, d_skip, g_y, w_out):
    w_in_p, w_out_bf = _cast_weights(_c_cast_kernel, w_in, w_out, C_COLS, "c_cast")
    pad = lambda v: jnp.concatenate([v, jnp.zeros((LANES - H_C,), v.dtype)]).reshape(1, LANES)
    return (rms_g.reshape(1, D_MODEL), w_in_p, conv_w, conv_b.reshape(1, CONV_DIM), pad(dt_bias), pad(a_log),
            jnp.repeat(d_skip, P_C).reshape(1, D_INNER_C), g_y.reshape(1, D_INNER_C), w_out_bf)


def _ssd_masks(c):
    tok = np.arange(c)[:, None]
    src = np.arange(H_C * c)[None, :] % c
    neg = np.where(src <= tok, 0.0, -np.inf).astype(np.float32)
    blk = (np.arange(4 * c)[:, None] // c) == (np.arange(256)[None, :] // P_C)
    return jnp.asarray(neg), jnp.asarray(blk, BF16)


def _c_prompt(x, weights):
    length = x.shape[0]
    t = PROMPT_BLOCK
    c = CHUNK
    n_blocks = length // t
    in_blk = lambda s: (jnp.minimum(s, n_blocks - 1), 0)
    out_blk = lambda s: (jnp.clip(s - 2, 0, n_blocks - 1), 0)
    in_specs = [
        pl.BlockSpec((t, D_MODEL), in_blk), pl.BlockSpec((t, D_MODEL), out_blk),
        _const((1, D_MODEL)), _const((D_MODEL, C_COLS)), _const((CONV_W, CONV_DIM)), _const((1, CONV_DIM)),
        _const((1, LANES)), _const((1, LANES)), _const((1, D_INNER_C)), _const((1, D_INNER_C)),
        _const((D_INNER_C, D_MODEL)), _const((2 * LANES, H_C * c)),
        _const((c, H_C * c)), _const((4 * c, 256)),
    ]
    out_shape = (jax.ShapeDtypeStruct((length, D_MODEL), F32),
                 jax.ShapeDtypeStruct((H_C * P_C, N_C), F32),
                 jax.ShapeDtypeStruct((8, CONV_DIM), F32))
    out_specs = (pl.BlockSpec((t, D_MODEL), out_blk), _full((H_C * P_C, N_C)), _full((8, CONV_DIM)))
    scratch = [pltpu.VMEM((2, t, D_INNER_C), F32), pltpu.VMEM((CONV_TILES, CONV_ROWS + 8, LANES), F32),
               pltpu.VMEM((2, CONV_TILES, CONV_ROWS, LANES), F32), pltpu.VMEM((2, t, LANES), F32),
               pltpu.VMEM((2, t, LANES), F32), pltpu.VMEM((2, t, D_INNER_C), BF16),
               pltpu.VMEM((N_C, H_C * P_C), F32)]
    return pl.pallas_call(
        _c_prompt_kernel, grid=(n_blocks + 2,), in_specs=in_specs, out_specs=out_specs,
        out_shape=out_shape, scratch_shapes=scratch, name="c_prompt",
        compiler_params=pltpu.CompilerParams(dimension_semantics=("arbitrary",),
                                             vmem_limit_bytes=VMEM_LIMIT),
    )(x, x, *weights, _head_expand_matrix(c), *_ssd_masks(c))


def _c_sample(x, weights, st_in, conv_in):
    nb, c, _ = x.shape
    t = nb * c
    vm = pl.BlockSpec(memory_space=pltpu.VMEM)
    out_shape = (jax.ShapeDtypeStruct((t, D_MODEL), F32),
                 jax.ShapeDtypeStruct((nb, H_C * P_C, N_C), F32),
                 jax.ShapeDtypeStruct((nb, CONV_W - 1, CONV_DIM), F32))
    scratch = [pltpu.VMEM((t, C_COLS), F32), pltpu.VMEM((c + 8, CONV_DIM), F32),
               pltpu.VMEM((t, CONV_DIM), F32), pltpu.VMEM((t, LANES), F32), pltpu.VMEM((t, LANES), F32),
               pltpu.VMEM((t, D_INNER_C), BF16)]
    return pl.pallas_call(
        _c_sample_kernel, in_specs=[vm] * 16, out_specs=(vm,) * 3, out_shape=out_shape,
        scratch_shapes=scratch, name="c_sample",
        compiler_params=pltpu.CompilerParams(vmem_limit_bytes=VMEM_LIMIT),
    )(x.reshape(t, D_MODEL), *weights, _head_expand_matrix(c), _head_expand_matrix(P_C), *_ssd_masks(c),
      st_in.reshape(nb, H_C * P_C, N_C), conv_in)


def kernel(x_prompt, x_sample, cache_swa_k, cache_swa_v, state_gla, state_ssd, state_conv, rms_g, w_in_ab, w_gate_up_a, b_gate_a, g_out_a, g_q_b, g_k_b, sink_b, rel_bias, w_out_ab, w_in_c, conv_w_c, conv_b_c, dt_bias_c, a_log_c, d_skip_c, g_y_c, w_out_c):
    bp, seq_len, _ = x_prompt.shape
    nb, dec_len, _ = x_sample.shape
    assert bp == 1 and seq_len % PROMPT_BLOCK == 0 and seq_len % AB_PROMPT_BLOCK == 0 and seq_len >= WINDOW
    wab = _ab_weights(rms_g[0], w_in_ab[0], w_gate_up_a[0], b_gate_a[0], g_out_a[0], g_q_b[0], g_k_b[0],
                      w_out_ab[0])
    yp, gla_p, k_p, v_p = _ab_prompt(x_prompt[0], sink_b[0], rel_bias, wab)
    ys, gla_s, k_s, v_s = _ab_sample(x_sample, sink_b[0], rel_bias, wab, state_gla[0], cache_swa_k[0],
                                     cache_swa_v[0])
    wc = _c_weights(rms_g[1], w_in_c[0], conv_w_c[0], conv_b_c[0], dt_bias_c[0], a_log_c[0], d_skip_c[0],
                    g_y_c[0], w_out_c[0])
    yp, ssd_p, conv_p = _c_prompt(yp, wc)
    ys, ssd_s, conv_s = _c_sample(ys.reshape(nb, dec_len, D_MODEL), wc, state_ssd[0], state_conv[0])
    return (
        yp.reshape(1, seq_len, D_MODEL),
        ys.reshape(nb, dec_len, D_MODEL),
        gla_p.reshape(1, 1, H_A, DK_A, DV_A),
        k_p.reshape(1, 1, WINDOW, KVH_B, HD_B),
        v_p.reshape(1, 1, WINDOW, KVH_B, HD_B),
        ssd_p.reshape(1, 1, H_C, P_C, N_C),
        conv_p[8 - (CONV_W - 1):].reshape(1, 1, CONV_W - 1, CONV_DIM),
        gla_s.reshape(1, nb, H_A, DK_A, DV_A),
        k_s.reshape(1, nb, dec_len, KVH_B, HD_B),
        v_s.reshape(1, nb, dec_len, KVH_B, HD_B),
        ssd_s.reshape(1, nb, H_C, P_C, N_C),
        conv_s.reshape(1, nb, CONV_W - 1, CONV_DIM),
    )
```

```python
import math
import types

import numpy as np
import jax
import jax.numpy as jnp
from jax import lax
from jax.experimental import pallas as pl
from jax.experimental.pallas import tpu as pltpu

F32 = jnp.float32
BF16 = jnp.bfloat16

D_MODEL = 1024
CHUNK = 64
EPS = 1e-6
H_A = 4
DK_A = 64
DV_A = 128
GATE_RANK_A = 16
GATE_NORM_A = 16.0
H_B = 8
KVH_B = 2
G_B = H_B // KVH_B
HD_B = 64
WINDOW = 128
N_BUCKETS = 32
MAX_DISTANCE = 128
D_INNER_C = 2048
P_C = 64
H_C = 32
G_C = 4
HPG_C = 8
N_C = 128
CONV_W = 4
CONV_DIM = D_INNER_C + 2 * G_C * N_C

LOG2E = 1.4426950408889634
LANES = 128

A_Q, A_K, A_V, A_Z = 0, 256, 512, 1024
B_Q, B_K, B_V, B_Z = 1536, 2048, 2176, 2304
A_G = 2816
AB_COLS = 2944
C_Z, C_X, C_B, C_C, C_DT = 0, 2048, 4096, 4608, 5120
C_COLS = 5248

PROMPT_BLOCK = 256
AB_PROMPT_BLOCK = 512
VMEM_LIMIT = 60 * 1024 * 1024


def _dot(a, b):
    return jnp.dot(a.astype(BF16), b.astype(BF16), preferred_element_type=F32)


def _dot_nt(a, b):
    return lax.dot_general(a.astype(BF16), b.astype(BF16), (((1,), (1,)), ((), ())),
                           preferred_element_type=F32)


def _dot_tn(a, b):
    return lax.dot_general(a.astype(BF16), b.astype(BF16), (((0,), (0,)), ((), ())),
                           preferred_element_type=F32)


def _split_bf16(x, terms):
    out = []
    r = x
    for _ in range(terms):
        h = r.astype(BF16)
        out.append(h)
        r = r - h.astype(F32)
    return out


def _sel_dot_left(sel, x, terms):
    acc = None
    for h in _split_bf16(x, terms):
        d = jnp.dot(sel, h, preferred_element_type=F32)
        acc = d if acc is None else acc + d
    return acc


def _rms_rows(x):
    return x * lax.rsqrt(jnp.mean(x * x, axis=-1, keepdims=True) + EPS)


def _exp_neg(x):
    return jnp.exp2(x * (-LOG2E))


def _silu(x):
    return x * (1.0 / (1.0 + _exp_neg(x)))


def _softplus(x):
    return jnp.maximum(x, 0.0) + jnp.log(1.0 + _exp_neg(jnp.abs(x)))


def _log_sigmoid(x):
    return jnp.minimum(x, 0.0) - jnp.log(1.0 + _exp_neg(jnp.abs(x)))


def _tril(n):
    r = lax.broadcasted_iota(jnp.int32, (n, n), 0)
    c = lax.broadcasted_iota(jnp.int32, (n, n), 1)
    return r >= c


def _head_mean_sq(x):
    out = []
    for p in range(x.shape[1] // LANES):
        sq = x[:, p * LANES:(p + 1) * LANES]
        sq = sq * sq
        low = lax.broadcasted_iota(jnp.int32, sq.shape, 1) < HD_B
        lo = jnp.sum(jnp.where(low, sq, 0.0), axis=-1, keepdims=True)
        hi = jnp.sum(jnp.where(low, 0.0, sq), axis=-1, keepdims=True)
        out.append(jnp.where(low, lo, hi))
    return jnp.concatenate(out, axis=1) * (1.0 / HD_B)


def _tile_kv_heads(kv):
    low = lax.broadcasted_iota(jnp.int32, kv.shape, 1) < HD_B
    swapped = pltpu.roll(kv, HD_B, 1)
    out = []
    for base in (jnp.where(low, kv, swapped), jnp.where(low, swapped, kv)):
        base = base.astype(BF16)
        out.append(jnp.concatenate([base, base], axis=1))
    return out


def _ab_dense_in(x, rms_g, w_in_ref, gq, gk, proj_ref, qn_ref):
    h = (_rms_rows(x) * rms_g).astype(BF16)
    rows = min(256, x.shape[0])
    for m in range(0, x.shape[0], rows):
        proj_ref[m:m + rows, :] = jnp.dot(h[m:m + rows], w_in_ref[...], preferred_element_type=F32)
    qb = proj_ref[:, B_Q:B_Q + 512]
    qn_ref[...] = qb * lax.rsqrt(_head_mean_sq(qb) + EPS) * gq
    kb = proj_ref[:, B_K:B_K + 128]
    kn = kb * lax.rsqrt(_head_mean_sq(kb) + EPS) * gk
    vb = proj_ref[:, B_V:B_V + 128]
    return kn, vb


def _advance(gens, yielded=None):
    alive = []
    for gen in gens:
        try:
            value = next(gen)
            alive.append(gen)
            if yielded is not None and value is not None:
                yielded.append(value)
        except StopIteration:
            pass
    return alive


def _interleave(gens):
    gens = list(gens)
    while gens:
        gens = _advance(gens)


def _ab_chunk(r, c, kb, proj_ref, qn_ref, o_ref, state, k_band, v_band, bias_ref, sink_ref,
              w_gu, b_gate, g_out, first_valid_col):
    rows = pl.ds(r, c)
    gate = _dot(proj_ref[rows, A_G:A_G + LANES], w_gu) + b_gate
    yield
    g = _log_sigmoid(gate) * (LOG2E / GATE_NORM_A)
    tril = _tril(c)
    b = _sel_dot_left(tril.astype(BF16), g, 3)
    bl = b[c - 1:c, :]
    bl_tile = jnp.broadcast_to(bl, (LANES, H_A * DK_A))
    bl_rows = jnp.concatenate([bl_tile[:, i * LANES:(i + 1) * LANES].T
                               for i in range(H_A * DK_A // LANES)], axis=0)
    yield
    q = proj_ref[rows, A_Q:A_Q + 256] * (DK_A ** -0.5)
    k = proj_ref[rows, A_K:A_K + 256]
    v = proj_ref[rows, A_V:A_V + 512].astype(BF16)
    qe = (q * jnp.exp2(b)).astype(BF16)
    kd = k * jnp.exp2(-b)
    kd2 = (k * jnp.exp2(bl - b)).astype(BF16)
    lane_head = lax.broadcasted_iota(jnp.int32, (c, LANES), 1) // DK_A
    row_head = lax.broadcasted_iota(jnp.int32, (LANES, LANES), 0) // DK_A
    att = []
    for p in range(2):
        kp = kd[:, p * 128:(p + 1) * 128]
        kpair = jnp.concatenate([jnp.where(lane_head == j, kp, 0.0) for j in range(2)], axis=0).astype(BF16)
        att.append(_dot_nt(qe[:, p * 128:(p + 1) * 128], kpair))
    upd = [_dot_tn(kd2[:, p * 128:(p + 1) * 128], v[:, p * 256:(p + 1) * 256]) for p in range(2)]
    yield
    row = lax.broadcasted_iota(jnp.int32, (c, 2 * c), 0)
    col = lax.broadcasted_iota(jnp.int32, (c, 2 * c), 1) % c
    att = [jnp.where(row >= col, a, 0.0).astype(BF16) for a in att]
    yield
    s_prev = state[0]
    s_new, oh = [], []
    zeros_v = jnp.zeros((c, DV_A), BF16)
    for p in range(2):
        sp = s_prev[p * 128:(p + 1) * 128, :]
        s_bd = jnp.concatenate([jnp.where(row_head == j, sp, 0.0) for j in range(2)], axis=1).astype(BF16)
        v0, v1 = v[:, 2 * p * DV_A:(2 * p + 1) * DV_A], v[:, (2 * p + 1) * DV_A:(2 * p + 2) * DV_A]
        v_bd = jnp.concatenate([jnp.concatenate([v0, zeros_v], axis=1),
                                jnp.concatenate([zeros_v, v1], axis=1)], axis=0)
        qp = qe[:, p * 128:(p + 1) * 128]
        if (2 * c) % LANES == 0:
            o_pair = jnp.dot(jnp.concatenate([att[p], qp], axis=1), jnp.concatenate([v_bd, s_bd], axis=0),
                             preferred_element_type=F32)
        else:
            o_pair = (jnp.dot(att[p], v_bd, preferred_element_type=F32)
                      + jnp.dot(qp, s_bd, preferred_element_type=F32))
        oh += [o_pair[:, :DV_A], o_pair[:, DV_A:]]
        u = jnp.where(row_head == 0, upd[p][:, :128], upd[p][:, 128:])
        s_new.append(jnp.exp2(bl_rows[p * 128:(p + 1) * 128, :]) * sp + u)
    state[0] = jnp.concatenate(s_new, axis=0)
    yield
    for hd in range(H_A):
        z = proj_ref[rows, A_Z + hd * 128:A_Z + (hd + 1) * 128]
        o_ref[rows, hd * 128:(hd + 1) * 128] = (_rms_rows(oh[hd]) * g_out * _silu(z)).astype(BF16)
    yield
    lane256 = lax.broadcasted_iota(jnp.int32, (c, 256), 1) // HD_B
    srow = lax.broadcasted_iota(jnp.int32, (G_B * c, 1), 0) // c
    scores = []
    for kvh in range(KVH_B):
        qn = qn_ref[rows, kvh * 256:(kvh + 1) * 256]
        qs = jnp.concatenate([jnp.where(lane256 == gq_, qn, 0.0) for gq_ in range(G_B)], axis=0)
        scores.append(_dot_nt(qs, k_band(kvh)))
    yield
    probs, dens = [], []
    for kvh in range(KVH_B):
        s = scores[kvh] * (LOG2E * HD_B ** -0.5) + bias_ref[kvh]
        if first_valid_col is not None:
            col = lax.broadcasted_iota(jnp.int32, (G_B * c, kb), 1)
            s = jnp.where(col >= first_valid_col, s, -jnp.inf)
        sink = jnp.zeros((G_B * c, 1), F32)
        for gq_ in range(G_B):
            sink = jnp.where(srow == gq_, sink_ref[kvh * G_B + gq_] * LOG2E, sink)
        m = jnp.maximum(jnp.max(s, axis=-1, keepdims=True), sink)
        pr = jnp.exp2(s - m)
        dens.append(jnp.sum(pr, axis=-1, keepdims=True) + jnp.exp2(sink - m))
        probs.append(pr.astype(BF16))
    yield
    outs = [_dot(probs[kvh], v_band(kvh)) for kvh in range(KVH_B)]
    yield
    for kvh in range(KVH_B):
        ost = outs[kvh] / dens[kvh]
        ob = jnp.zeros((c, 256), F32)
        for gq_ in range(G_B):
            ob = ob + jnp.where(lane256 == gq_, ost[gq_ * c:(gq_ + 1) * c, :], 0.0)
        z = proj_ref[rows, B_Z + kvh * 256:B_Z + (kvh + 1) * 256]
        o_ref[rows, 512 + kvh * 256:512 + (kvh + 1) * 256] = (ob * _silu(z)).astype(BF16)


def _ab_prompt_kernel(sink_ref, x_ref, rms_g_ref, w_in_ref, w_gu_ref, b_gate_ref, g_out_ref, gq_ref,
                      gk_ref, bias_ref, w_out_ref,
                      y_ref, s_out_ref, k_out_ref, v_out_ref,
                      proj_ref, qn_ref, o_ref, kband_ref, vband_ref, s_ref):
    t = x_ref.shape[0]
    c = CHUNK
    kb = WINDOW + c
    nchunk = t // c
    step = pl.program_id(0)

    @pl.when(step == 0)
    def _():
        s_ref[...] = jnp.zeros_like(s_ref)
        kband_ref[:, t:t + WINDOW, :] = jnp.zeros((KVH_B, WINDOW, 256), BF16)
        vband_ref[:, t:t + WINDOW, :] = jnp.zeros((KVH_B, WINDOW, 256), BF16)

    for kvh in range(KVH_B):
        kband_ref[kvh, 0:WINDOW, :] = kband_ref[kvh, t:t + WINDOW, :]
        vband_ref[kvh, 0:WINDOW, :] = vband_ref[kvh, t:t + WINDOW, :]

    x = x_ref[...]
    kn, vb = _ab_dense_in(x, rms_g_ref[...], w_in_ref, gq_ref[...], gk_ref[...], proj_ref, qn_ref)
    k_out_ref[...] = kn[t - WINDOW:, :]
    v_out_ref[...] = vb[t - WINDOW:, :]
    for kvh, (kt, vt) in enumerate(zip(_tile_kv_heads(kn), _tile_kv_heads(vb))):
        kband_ref[kvh, WINDOW:WINDOW + t, :] = kt
        vband_ref[kvh, WINDOW:WINDOW + t, :] = vt

    w_gu = w_gu_ref[...]
    b_gate = b_gate_ref[...]
    g_out = g_out_ref[...]

    state = [s_ref[...]]

    def chunk(i):
        r = i * c
        return _ab_chunk(r, c, kb, proj_ref, qn_ref, o_ref, state,
                         lambda kvh: kband_ref[kvh, pl.ds(r, kb), :],
                         lambda kvh: vband_ref[kvh, pl.ds(r, kb), :],
                         bias_ref, sink_ref, w_gu, b_gate, g_out, (2 - (step * nchunk + i)) * c)

    _interleave(chunk(i) for i in range(nchunk))
    s_ref[...] = state[0]
    y_ref[...] = x + jnp.dot(o_ref[...], w_out_ref[...], preferred_element_type=F32)
    s_out_ref[...] = s_ref[...]


def _ab_sample_kernel(sink_ref, x_ref, rms_g_ref, w_in_ref, w_gu_ref, b_gate_ref, g_out_ref, gq_ref,
                      gk_ref, bias_ref, w_out_ref,
                      s_in_ref, kc_ref, vc_ref,
                      y_ref, s_out_ref, k_out_ref, v_out_ref,
                      proj_ref, qn_ref, o_ref, kband_ref, vband_ref):
    nb, wc = kc_ref.shape[0], kc_ref.shape[1]
    t = x_ref.shape[0]
    c = t // nb
    kb = wc + c
    x = x_ref[...]
    kn, vb = _ab_dense_in(x, rms_g_ref[...], w_in_ref, gq_ref[...], gk_ref[...], proj_ref, qn_ref)
    k_out_ref[...] = kn
    v_out_ref[...] = vb
    k_new, v_new = _tile_kv_heads(kn), _tile_kv_heads(vb)
    for bi in range(nb):
        k_old, v_old = _tile_kv_heads(kc_ref[bi]), _tile_kv_heads(vc_ref[bi])
        for kvh in range(KVH_B):
            kband_ref[kvh, bi, 0:wc, :] = k_old[kvh]
            vband_ref[kvh, bi, 0:wc, :] = v_old[kvh]
            kband_ref[kvh, bi, wc:kb, :] = k_new[kvh][bi * c:(bi + 1) * c, :]
            vband_ref[kvh, bi, wc:kb, :] = v_new[kvh][bi * c:(bi + 1) * c, :]

    w_gu = w_gu_ref[...]
    b_gate = b_gate_ref[...]
    g_out = g_out_ref[...]

    states = [[s_in_ref[bi]] for bi in range(nb)]

    def seq(bi):
        return _ab_chunk(bi * c, c, kb, proj_ref, qn_ref, o_ref, states[bi],
                         lambda kvh: kband_ref[kvh, bi], lambda kvh: vband_ref[kvh, bi],
                         bias_ref, sink_ref, w_gu, b_gate, g_out, None)

    _interleave(seq(bi) for bi in range(nb))
    for bi in range(nb):
        s_out_ref[bi] = states[bi][0]
    y_ref[...] = x + jnp.dot(o_ref[...], w_out_ref[...], preferred_element_type=F32)


GROUP_W = D_INNER_C // G_C


def _c_chunk(r, c, io, state, e_s, e_p, neg_mask, bd_mask, dskip, g_y):
    dtc = io.dt(r)
    acum = _sel_dot_left(_tril(c).astype(BF16), io.da(r), 3) * LOG2E
    yield acum
    lhs = jnp.concatenate([jnp.concatenate(_split_bf16(acum, 2), axis=1),
                           jnp.concatenate(_split_bf16(dtc, 2), axis=1)], axis=0)
    both_p = jnp.dot(lhs, e_p, preferred_element_type=F32)
    xa_p, dt_p = both_p[:c], both_p[c:]
    xa_s = xa_p if c == P_C else jnp.dot(lhs[:c], e_s, preferred_element_type=F32)
    yield dt_p
    acum_t = acum.T
    a_row = jnp.concatenate([acum_t[h:h + 1, :] for h in range(H_C)], axis=1)
    wmat = jnp.exp2((xa_s - a_row) + neg_mask)
    al_p = xa_p[c - 1:c, :]
    xs = io.x(r)
    xdt = xs * dt_p
    xdt_bf = xdt.astype(BF16)
    xw = (xdt * jnp.exp2(al_p - xa_p)).astype(BF16)
    dec = jnp.exp2(al_p)
    bg, cg, cb = [], [], []
    for g in range(G_C):
        bg.append(io.b(r, g).astype(BF16))
        cg.append(io.c(r, g).astype(BF16))
        cb.append(_dot_nt(cg[g], jnp.concatenate([bg[g]] * HPG_C, axis=0)))
    yield cb[-1]
    ys = []
    for g in range(G_C):
        mg = (cb[g] * wmat[:, g * HPG_C * c:(g + 1) * HPG_C * c]).astype(BF16)
        for j in range(2):
            xj = xdt_bf[:, g * GROUP_W + j * 256:g * GROUP_W + (j + 1) * 256]
            bd = jnp.concatenate([xj] * 4, axis=0) * bd_mask
            ys.append(jnp.dot(mg[:, j * 4 * c:(j + 1) * 4 * c], bd, preferred_element_type=F32))
    upd = [_dot_tn(bg[g], xw[:, g * GROUP_W:(g + 1) * GROUP_W]) for g in range(G_C)]
    yield upd[-1]
    st_prev = state[0]
    y_inter, st_new = [], []
    for g in range(G_C):
        sl = slice(g * GROUP_W, (g + 1) * GROUP_W)
        y_inter.append(_dot(cg[g], st_prev[:, sl]))
        st_new.append(st_prev[:, sl] * dec[:, sl] + upd[g])
    state[0] = jnp.concatenate(st_new, axis=1)
    yield st_new[-1]
    for g in range(G_C):
        sl = slice(g * GROUP_W, (g + 1) * GROUP_W)
        y = jnp.concatenate(ys[2 * g:2 * g + 2], axis=1) + y_inter[g] * jnp.exp2(xa_p[:, sl])
        y = y + dskip[:, sl] * xs[:, sl]
        y = y * io.gate(r, g)
        io.put_o(r, g, (_rms_rows(y) * g_y[:, sl]).astype(BF16))


def _c_dt(dt_cols, dt_bias, a_log):
    dt = _softplus(dt_cols + dt_bias)
    return dt, dt * (-jnp.exp(a_log))


CONV_PITCH = PROMPT_BLOCK // 8 + 1
CONV_ROWS = 8 * CONV_PITCH
CONV_TILES = CONV_DIM // LANES
C_ROUNDS = 12
C_IN_SLABS = tuple((C_X + 512 * k, C_X + 512 * (k + 1)) for k in range(6)) + tuple(
    (C_Z + 512 * k, C_Z + 512 * (k + 1)) for k in range(4)) + ((C_DT, C_COLS),)
C_MIX_FIRST_ROUND = 5
C_OUT_ROUNDS = (0, 1, 2, 3)
assert len(C_IN_SLABS) <= C_ROUNDS and sum(hi - lo for lo, hi in C_IN_SLABS) == C_COLS


def _conv_tile(ubuf_ref, act_ref, j, conv_w, conv_b):
    w = [jnp.broadcast_to(conv_w[i:i + 1, j * LANES:(j + 1) * LANES], (8, LANES)) for i in range(CONV_W)]
    b = jnp.broadcast_to(conv_b[:, j * LANES:(j + 1) * LANES], (8, LANES))
    for a in range(CONV_PITCH):
        acc = b
        for i in range(CONV_W):
            acc = acc + w[i] * ubuf_ref[j, pl.ds(8 - (CONV_W - 1) + i + a, 8, stride=CONV_PITCH), :]
        act_ref[j, pl.ds(a, 8, stride=CONV_PITCH), :] = _silu(acc)


def _c_prompt_kernel(xin_ref, xres_ref, rms_g_ref, w_in_ref, conv_w_ref, conv_b_ref, dt_bias_ref,
                     a_log_ref, dskip_ref, g_y_ref, w_out_ref, e_s_ref, neg_mask_ref, bd_mask_ref,
                     y_ref, st_out_ref, conv_out_ref,
                     z_ref, ubuf_ref, act_ref, dt_ref, da_ref, o_ref, st_ref):
    t = xin_ref.shape[0]
    c = CHUNK
    s = pl.program_id(0)
    n_blocks = pl.num_programs(0) - 2

    @pl.when(s == 0)
    def _():
        o_ref[1] = jnp.zeros(o_ref.shape[1:], BF16)
        ubuf_ref[...] = jnp.zeros_like(ubuf_ref)

    @pl.when(s == 0)
    def _():
        st_ref[...] = jnp.zeros_like(st_ref)

    def in_stage(slot_in):
        h = (_rms_rows(xin_ref[...]) * rms_g_ref[...]).astype(BF16)
        conv_w = conv_w_ref[...]
        conv_b = conv_b_ref[...]
        act = act_ref.at[slot_in]
        for rnd, (lo, hi) in enumerate(C_IN_SLABS):
            if rnd:
                yield
            slab = jnp.dot(h, w_in_ref[:, lo:hi], preferred_element_type=F32)
            if lo == C_DT:
                dt, da = _c_dt(slab, dt_bias_ref[...], a_log_ref[...])
                dt_ref[slot_in] = dt
                da_ref[slot_in] = da
            elif lo >= C_X:
                for j in range((lo - C_X) // LANES, (hi - C_X) // LANES):
                    col = C_X + j * LANES - lo
                    ubuf_ref[j, 8:16, :] = ubuf_ref[j, CONV_ROWS:CONV_ROWS + 8, :]
                    ubuf_ref[j, 16:16 + t, :] = slab[:, col:col + LANES]
                    _conv_tile(ubuf_ref, act, j, conv_w, conv_b)
            else:
                z_ref[slot_in, :, lo:hi] = _silu(slab)

    n_x = D_INNER_C // LANES
    n_g = N_C // LANES

    def mix_stage(slot_mix):
        def put_o(r, g, value):
            o_ref[slot_mix, pl.ds(r, c), g * GROUP_W:(g + 1) * GROUP_W] = value

        def act_rows(r, j):
            return act_ref[slot_mix, j, pl.ds(8 + r, c), :]

        io = types.SimpleNamespace(
            dt=lambda r: dt_ref[slot_mix, pl.ds(r, c), :],
            da=lambda r: da_ref[slot_mix, pl.ds(r, c), :],
            x=lambda r: jnp.concatenate([act_rows(r, j) for j in range(n_x)], axis=1),
            b=lambda r, g: act_rows(r, n_x + g * n_g),
            c=lambda r, g: act_rows(r, n_x + (G_C + g) * n_g),
            gate=lambda r, g: z_ref[slot_mix, pl.ds(r, c), C_Z + g * GROUP_W:C_Z + (g + 1) * GROUP_W],
            put_o=put_o)
        for _ in range(C_MIX_FIRST_ROUND):
            yield
        state = [st_ref[...]]
        e_s = e_s_ref[...]
        chunks = [_c_chunk(i * c, c, io, state, e_s, e_s, neg_mask_ref[...], bd_mask_ref[...],
                           dskip_ref[...], g_y_ref[...]) for i in range(t // c)]
        chunks = _advance(chunks)
        while chunks:
            yield
            chunks = _advance(chunks)
        st_ref[...] = state[0]

    def out_stage(slot_in):
        quarter = 0
        for rnd in range(C_ROUNDS):
            if rnd:
                yield
            if rnd in C_OUT_ROUNDS:
                cols = slice(quarter * 256, (quarter + 1) * 256)
                y_ref[:, cols] = xres_ref[:, cols] + jnp.dot(o_ref[slot_in], w_out_ref[:, cols],
                                                             preferred_element_type=F32)
                quarter += 1

    slot_in = s % 2

    @pl.when(s == 0)
    def _():
        _interleave([in_stage(slot_in)])

    @pl.when(jnp.logical_and(s > 0, s <= n_blocks))
    def _():
        _interleave([in_stage(slot_in), mix_stage(1 - slot_in), out_stage(slot_in)])

    @pl.when(s == n_blocks + 1)
    def _():
        _interleave([out_stage(slot_in)])

    @pl.when(s == n_blocks)
    def _():
        st_out_ref[...] = st_ref[...].T
        for j in range(CONV_TILES):
            conv_out_ref[:, j * LANES:(j + 1) * LANES] = ubuf_ref[j, CONV_ROWS:CONV_ROWS + 8, :]


def _conv_rows(ubuf, nrows, conv_w, conv_b):
    acc = conv_b
    for i in range(CONV_W):
        acc = acc + conv_w[i:i + 1, :] * ubuf[pl.ds(8 - (CONV_W - 1) + i, nrows), :]
    return _silu(acc)


def _c_sample_kernel(x_ref, rms_g_ref, w_in_ref, conv_w_ref, conv_b_ref, dt_bias_ref, a_log_ref,
                     dskip_ref, g_y_ref, w_out_ref, e_s_ref, e_p_ref, neg_mask_ref, bd_mask_ref,
                     st_in_ref, conv_in_ref,
                     y_ref, st_out_ref, conv_out_ref,
                     proj_ref, ubuf_ref, act_ref, dt_ref, da_ref, o_ref):
    nb = st_in_ref.shape[0]
    t = x_ref.shape[0]
    c = t // nb
    x = x_ref[...]
    h = (_rms_rows(x) * rms_g_ref[...]).astype(BF16)
    proj_ref[...] = jnp.dot(h, w_in_ref[...], preferred_element_type=F32)
    dt, da = _c_dt(proj_ref[:, C_DT:C_DT + LANES], dt_bias_ref[...], a_log_ref[...])
    dt_ref[...] = dt
    da_ref[...] = da
    conv_w = conv_w_ref[...]
    conv_b = conv_b_ref[...]
    ubuf_ref[0:8, :] = jnp.zeros((8, CONV_DIM), F32)
    for bi in range(nb):
        ubuf_ref[8 - (CONV_W - 1):8, :] = conv_in_ref[bi]
        ubuf_ref[8:8 + c, :] = proj_ref[bi * c:(bi + 1) * c, C_X:C_X + CONV_DIM]
        act_ref[bi * c:(bi + 1) * c, :] = _conv_rows(ubuf_ref, c, conv_w, conv_b)
        conv_out_ref[bi] = ubuf_ref[8 + c - (CONV_W - 1):8 + c, :]

    def put_o(r, g, value):
        o_ref[pl.ds(r, c), g * GROUP_W:(g + 1) * GROUP_W] = value

    io = types.SimpleNamespace(
        dt=lambda r: dt_ref[pl.ds(r, c), :],
        da=lambda r: da_ref[pl.ds(r, c), :],
        x=lambda r: act_ref[pl.ds(r, c), 0:D_INNER_C],
        b=lambda r, g: act_ref[pl.ds(r, c), D_INNER_C + g * N_C:D_INNER_C + (g + 1) * N_C],
        c=lambda r, g: act_ref[pl.ds(r, c), D_INNER_C + (G_C + g) * N_C:D_INNER_C + (G_C + g + 1) * N_C],
        gate=lambda r, g: _silu(proj_ref[pl.ds(r, c), C_Z + g * GROUP_W:C_Z + (g + 1) * GROUP_W]),
        put_o=put_o)
    e_s = e_s_ref[...]
    e_p = e_p_ref[...]
    neg_mask = neg_mask_ref[...]
    bd_mask = bd_mask_ref[...]
    dskip = dskip_ref[...]
    g_y = g_y_ref[...]

    states = [[st_in_ref[bi].T] for bi in range(nb)]
    _interleave(_c_chunk(bi * c, c, io, states[bi], e_s, e_p, neg_mask, bd_mask, dskip, g_y)
                for bi in range(nb))
    for bi in range(nb):
        st_out_ref[bi] = states[bi][0].T
    y_ref[...] = x + jnp.dot(o_ref[...], w_out_ref[...], preferred_element_type=F32)


def _bucket_table(q_off, k_off):
    n = q_off[:, None] - k_off[None, :]
    half = N_BUCKETS // 2
    max_exact = half // 2
    side = np.where(n < 0, half, 0)
    n = np.abs(n)
    nf = np.maximum(n, max_exact).astype(np.float32)
    large = max_exact + (np.log(nf / np.float32(max_exact)) / np.float32(math.log(MAX_DISTANCE / max_exact))
                         * np.float32(half - max_exact)).astype(np.int32)
    large = np.minimum(large, half - 1)
    return side + np.where(n < max_exact, n, large)


def _bias_rows(rel_bias, q_off, k_off):
    bucket = _bucket_table(q_off, k_off)
    onehot = jnp.asarray(np.eye(N_BUCKETS, dtype=np.float32)[bucket])
    bias = jnp.einsum('qkb,bh->hqk', onehot, rel_bias.astype(F32), precision=lax.Precision.HIGHEST) * LOG2E
    return bias.reshape(KVH_B, G_B * q_off.shape[0], k_off.shape[0])


def _head_expand_matrix(per_head):
    m = np.zeros((LANES, H_C * per_head), np.float32)
    for h in range(H_C):
        m[h, h * per_head:(h + 1) * per_head] = 1.0
    return jnp.asarray(np.concatenate([m, m], axis=0), BF16)


def _full(shape):
    return pl.BlockSpec(shape, lambda *_: (0,) * len(shape))


def _const(shape):
    return pl.BlockSpec(shape, lambda *_: (0,) * len(shape), pipeline_mode=pl.Buffered(1))


CAST_ROWS = 256


def _cast_columns(wt_ref, w_bf_ref, src, dst, n):
    for off in range(0, n - n % LANES, CAST_ROWS):
        rows = min(CAST_ROWS, n - n % LANES - off)
        w_bf_ref[:, dst + off:dst + off + rows] = wt_ref[src + off:src + off + rows, :].T.astype(BF16)
    rest = n % LANES
    if rest:
        off = n - rest
        tail = jnp.concatenate([wt_ref[src + off:src + n, :], jnp.zeros((LANES - rest, wt_ref.shape[1]), F32)],
                               axis=0)
        w_bf_ref[:, dst + off:dst + off + LANES] = tail.T.astype(BF16)


def _ab_cast_kernel(wt_ref, w_out_ref, w_in_bf_ref, w_out_bf_ref):
    g0 = A_Z
    _cast_columns(wt_ref, w_in_bf_ref, 0, 0, g0)
    _cast_columns(wt_ref, w_in_bf_ref, g0 + GATE_RANK_A, g0, A_G - g0)
    _cast_columns(wt_ref, w_in_bf_ref, g0, A_G, GATE_RANK_A)
    w_out_bf_ref[...] = w_out_ref[...].astype(BF16)


def _c_cast_kernel(wt_ref, w_out_ref, w_in_bf_ref, w_out_bf_ref):
    _cast_columns(wt_ref, w_in_bf_ref, 0, 0, wt_ref.shape[0])
    w_out_bf_ref[...] = w_out_ref[...].astype(BF16)


def _cast_weights(body, w_in, w_out, cols, name):
    vm = pl.BlockSpec(memory_space=pltpu.VMEM)
    return pl.pallas_call(
        body, in_specs=[vm, vm], out_specs=(vm, vm),
        out_shape=(jax.ShapeDtypeStruct((w_in.shape[0], cols), BF16),
                   jax.ShapeDtypeStruct(w_out.shape, BF16)),
        name=name, compiler_params=pltpu.CompilerParams(vmem_limit_bytes=VMEM_LIMIT),
    )(w_in.T, w_out)


def _ab_weights(rms_g, w_in, w_gate_up, b_gate, g_out, g_q, g_k, w_out):
    w_in_r, w_out_bf = _cast_weights(_ab_cast_kernel, w_in, w_out, AB_COLS, "ab_cast")
    w_gu = jnp.concatenate([w_gate_up, jnp.zeros((LANES - GATE_RANK_A, H_A * DK_A), w_gate_up.dtype)],
                           axis=0).astype(BF16)
    return (rms_g.reshape(1, D_MODEL), w_in_r, w_gu, b_gate.reshape(1, -1), g_out.reshape(1, DV_A),
            jnp.tile(g_q, H_B).reshape(1, -1), jnp.tile(g_k, KVH_B).reshape(1, -1), w_out_bf)


def _ab_prompt(x, sink, rel_bias, weights):
    rms_g, w_in, w_gu, b_gate, g_out, gq, gk, w_out = weights
    length = x.shape[0]
    t = AB_PROMPT_BLOCK
    c = CHUNK
    kb = WINDOW + c
    bias = _bias_rows(rel_bias, np.arange(c), np.arange(kb) - WINDOW)
    row_blk = lambda i: (i, 0)
    in_specs = [
        pl.BlockSpec(memory_space=pltpu.SMEM),
        pl.BlockSpec((t, D_MODEL), row_blk),
        _const((1, D_MODEL)), _const((D_MODEL, AB_COLS)), _const((LANES, 256)), _const((1, 256)),
        _const((1, DV_A)), _const((1, 512)), _const((1, 128)), _const((KVH_B, G_B * c, kb)),
        _const((D_MODEL, D_MODEL)),
    ]
    out_shape = (jax.ShapeDtypeStruct((length, D_MODEL), F32),
                 jax.ShapeDtypeStruct((H_A * DK_A, DV_A), F32),
                 jax.ShapeDtypeStruct((WINDOW, KVH_B * HD_B), F32),
                 jax.ShapeDtypeStruct((WINDOW, KVH_B * HD_B), F32))
    out_specs = (pl.BlockSpec((t, D_MODEL), row_blk), _full((H_A * DK_A, DV_A)),
                 _full((WINDOW, KVH_B * HD_B)), _full((WINDOW, KVH_B * HD_B)))
    scratch = [pltpu.VMEM((t, AB_COLS), F32), pltpu.VMEM((t, 512), F32), pltpu.VMEM((t, D_MODEL), BF16),
               pltpu.VMEM((KVH_B, t + WINDOW, 256), BF16), pltpu.VMEM((KVH_B, t + WINDOW, 256), BF16),
               pltpu.VMEM((H_A * DK_A, DV_A), F32)]
    return pl.pallas_call(
        _ab_prompt_kernel, grid=(length // t,), in_specs=in_specs, out_specs=out_specs,
        out_shape=out_shape, scratch_shapes=scratch, name="ab_prompt",
        compiler_params=pltpu.CompilerParams(dimension_semantics=("arbitrary",),
                                             vmem_limit_bytes=VMEM_LIMIT),
    )(sink, x, rms_g, w_in, w_gu, b_gate, g_out, gq, gk, bias, w_out)


def _ab_sample(x, sink, rel_bias, weights, s_in, k_cache, v_cache):
    rms_g, w_in, w_gu, b_gate, g_out, gq, gk, w_out = weights
    nb, c, _ = x.shape
    wc = k_cache.shape[1]
    kb = wc + c
    t = nb * c
    bias = _bias_rows(rel_bias, np.arange(c), np.arange(kb) - wc)
    vm = pl.BlockSpec(memory_space=pltpu.VMEM)
    in_specs = [pl.BlockSpec(memory_space=pltpu.SMEM)] + [vm] * 13
    out_shape = (jax.ShapeDtypeStruct((t, D_MODEL), F32),
                 jax.ShapeDtypeStruct((nb, H_A * DK_A, DV_A), F32),
                 jax.ShapeDtypeStruct((t, KVH_B * HD_B), F32),
                 jax.ShapeDtypeStruct((t, KVH_B * HD_B), F32))
    scratch = [pltpu.VMEM((t, AB_COLS), F32), pltpu.VMEM((t, 512), F32), pltpu.VMEM((t, D_MODEL), BF16),
               pltpu.VMEM((KVH_B, nb, kb, 256), BF16), pltpu.VMEM((KVH_B, nb, kb, 256), BF16)]
    return pl.pallas_call(
        _ab_sample_kernel, in_specs=in_specs, out_specs=(vm,) * 4, out_shape=out_shape,
        scratch_shapes=scratch, name="ab_sample",
        compiler_params=pltpu.CompilerParams(vmem_limit_bytes=VMEM_LIMIT),
    )(sink, x.reshape(t, D_MODEL), rms_g, w_in, w_gu, b_gate, g_out, gq, gk, bias, w_out,
      s_in.reshape(nb, H_A * DK_A, DV_A), k_cache.reshape(nb, wc, KVH_B * HD_B),
      v_cache.reshape(nb, wc, KVH_B * HD_B))


def _c_weights(rms_g, w_in, conv_w, conv_b, dt_bias, a_log, d_skip, g_y, w_out):
    w_in_p, w_out_bf = _cast_weights(_c_cast_kernel, w_in, w_out, C_COLS, "c_cast")
    pad = lambda v: jnp.concatenate([v, jnp.zeros((LANES - H_C,), v.dtype)]).reshape(1, LANES)
    return (rms_g.reshape(1, D_MODEL), w_in_p, conv_w, conv_b.reshape(1, CONV_DIM), pad(dt_bias), pad(a_log),
            jnp.repeat(d_skip, P_C).reshape(1, D_INNER_C), g_y.reshape(1, D_INNER_C), w_out_bf)


def _ssd_masks(c):
    tok = np.arange(c)[:, None]
    src = np.arange(H_C * c)[None, :] % c
    neg = np.where(src <= tok, 0.0, -np.inf).astype(np.float32)
    blk = (np.arange(4 * c)[:, None] // c) == (np.arange(256)[None, :] // P_C)
    return jnp.asarray(neg), jnp.asarray(blk, BF16)


def _c_prompt(x, weights):
    length = x.shape[0]
    t = PROMPT_BLOCK
    c = CHUNK
    n_blocks = length // t
    in_blk = lambda s: (jnp.minimum(s, n_blocks - 1), 0)
    out_blk = lambda s: (jnp.clip(s - 2, 0, n_blocks - 1), 0)
    in_specs = [
        pl.BlockSpec((t, D_MODEL), in_blk), pl.BlockSpec((t, D_MODEL), out_blk),
        _const((1, D_MODEL)), _const((D_MODEL, C_COLS)), _const((CONV_W, CONV_DIM)), _const((1, CONV_DIM)),
        _const((1, LANES)), _const((1, LANES)), _const((1, D_INNER_C)), _const((1, D_INNER_C)),
        _const((D_INNER_C, D_MODEL)), _const((2 * LANES, H_C * c)),
        _const((c, H_C * c)), _const((4 * c, 256)),
    ]
    out_shape = (jax.ShapeDtypeStruct((length, D_MODEL), F32),
                 jax.ShapeDtypeStruct((H_C * P_C, N_C), F32),
                 jax.ShapeDtypeStruct((8, CONV_DIM), F32))
    out_specs = (pl.BlockSpec((t, D_MODEL), out_blk), _full((H_C * P_C, N_C)), _full((8, CONV_DIM)))
    scratch = [pltpu.VMEM((2, t, D_INNER_C), F32), pltpu.VMEM((CONV_TILES, CONV_ROWS + 8, LANES), F32),
               pltpu.VMEM((2, CONV_TILES, CONV_ROWS, LANES), F32), pltpu.VMEM((2, t, LANES), F32),
               pltpu.VMEM((2, t, LANES), F32), pltpu.VMEM((2, t, D_INNER_C), BF16),
               pltpu.VMEM((N_C, H_C * P_C), F32)]
    return pl.pallas_call(
        _c_prompt_kernel, grid=(n_blocks + 2,), in_specs=in_specs, out_specs=out_specs,
        out_shape=out_shape, scratch_shapes=scratch, name="c_prompt",
        compiler_params=pltpu.CompilerParams(dimension_semantics=("arbitrary",),
                                             vmem_limit_bytes=VMEM_LIMIT),
    )(x, x, *weights, _head_expand_matrix(c), *_ssd_masks(c))


def _c_sample(x, weights, st_in, conv_in):
    nb, c, _ = x.shape
    t = nb * c
    vm = pl.BlockSpec(memory_space=pltpu.VMEM)
    out_shape = (jax.ShapeDtypeStruct((t, D_MODEL), F32),
                 jax.ShapeDtypeStruct((nb, H_C * P_C, N_C), F32),
                 jax.ShapeDtypeStruct((nb, CONV_W - 1, CONV_DIM), F32))
    scratch = [pltpu.VMEM((t, C_COLS), F32), pltpu.VMEM((c + 8, CONV_DIM), F32),
               pltpu.VMEM((t, CONV_DIM), F32), pltpu.VMEM((t, LANES), F32), pltpu.VMEM((t, LANES), F32),
               pltpu.VMEM((t, D_INNER_C), BF16)]
    return pl.pallas_call(
        _c_sample_kernel, in_specs=[vm] * 16, out_specs=(vm,) * 3, out_shape=out_shape,
        scratch_shapes=scratch, name="c_sample",
        compiler_params=pltpu.CompilerParams(vmem_limit_bytes=VMEM_LIMIT),
    )(x.reshape(t, D_MODEL), *weights, _head_expand_matrix(c), _head_expand_matrix(P_C), *_ssd_masks(c),
      st_in.reshape(nb, H_C * P_C, N_C), conv_in)


def kernel(x_prompt, x_sample, cache_swa_k, cache_swa_v, state_gla, state_ssd, state_conv, rms_g, w_in_ab, w_gate_up_a, b_gate_a, g_out_a, g_q_b, g_k_b, sink_b, rel_bias, w_out_ab, w_in_c, conv_w_c, conv_b_c, dt_bias_c, a_log_c, d_skip_c, g_y_c, w_out_c):
    bp, seq_len, _ = x_prompt.shape
    nb, dec_len, _ = x_sample.shape
    assert bp == 1 and seq_len % PROMPT_BLOCK == 0 and seq_len % AB_PROMPT_BLOCK == 0 and seq_len >= WINDOW
    wab = _ab_weights(rms_g[0], w_in_ab[0], w_gate_up_a[0], b_gate_a[0], g_out_a[0], g_q_b[0], g_k_b[0],
                      w_out_ab[0])
    yp, gla_p, k_p, v_p = _ab_prompt(x_prompt[0], sink_b[0], rel_bias, wab)
    ys, gla_s, k_s, v_s = _ab_sample(x_sample, sink_b[0], rel_bias, wab, state_gla[0], cache_swa_k[0],
                                     cache_swa_v[0])
    wc = _c_weights(rms_g[1], w_in_c[0], conv_w_c[0], conv_b_c[0], dt_bias_c[0], a_log_c[0], d_skip_c[0],
                    g_y_c[0], w_out_c[0])
    yp, ssd_p, conv_p = _c_prompt(yp, wc)
    ys, ssd_s, conv_s = _c_sample(ys.reshape(nb, dec_len, D_MODEL), wc, state_ssd[0], state_conv[0])
    return (
        yp.reshape(1, seq_len, D_MODEL),
        ys.reshape(nb, dec_len, D_MODEL),
        gla_p.reshape(1, 1, H_A, DK_A, DV_A),
        k_p.reshape(1, 1, WINDOW, KVH_B, HD_B),
        v_p.reshape(1, 1, WINDOW, KVH_B, HD_B),
        ssd_p.reshape(1, 1, H_C, P_C, N_C),
        conv_p[8 - (CONV_W - 1):].reshape(1, 1, CONV_W - 1, CONV_DIM),
        gla_s.reshape(1, nb, H_A, DK_A, DV_A),
        k_s.reshape(1, nb, dec_len, KVH_B, HD_B),
        v_s.reshape(1, nb, dec_len, KVH_B, HD_B),
        ssd_s.reshape(1, nb, H_C, P_C, N_C),
        conv_s.reshape(1, nb, CONV_W - 1, CONV_DIM),
    )
```

```python
import math
import types

import numpy as np
import jax
import jax.numpy as jnp
from jax import lax
from jax.experimental import pallas as pl
from jax.experimental.pallas import tpu as pltpu

F32 = jnp.float32
BF16 = jnp.bfloat16

D_MODEL = 1024
CHUNK = 64
EPS = 1e-6
H_A = 4
DK_A = 64
DV_A = 128
GATE_RANK_A = 16
GATE_NORM_A = 16.0
H_B = 8
KVH_B = 2
G_B = H_B // KVH_B
HD_B = 64
WINDOW = 128
N_BUCKETS = 32
MAX_DISTANCE = 128
D_INNER_C = 2048
P_C = 64
H_C = 32
G_C = 4
HPG_C = 8
N_C = 128
CONV_W = 4
CONV_DIM = D_INNER_C + 2 * G_C * N_C

LOG2E = 1.4426950408889634
LANES = 128

A_Q, A_K, A_V, A_Z = 0, 256, 512, 1024
B_Q, B_K, B_V, B_Z = 1536, 2048, 2176, 2304
A_G = 2816
AB_COLS = 2944
C_Z, C_X, C_B, C_C, C_DT = 0, 2048, 4096, 4608, 5120
C_COLS = 5248

PROMPT_BLOCK = 256
AB_PROMPT_BLOCK = 512
VMEM_LIMIT = 60 * 1024 * 1024


def _dot(a, b):
    return jnp.dot(a.astype(BF16), b.astype(BF16), preferred_element_type=F32)


def _dot_nt(a, b):
    return lax.dot_general(a.astype(BF16), b.astype(BF16), (((1,), (1,)), ((), ())),
                           preferred_element_type=F32)


def _dot_tn(a, b):
    return lax.dot_general(a.astype(BF16), b.astype(BF16), (((0,), (0,)), ((), ())),
                           preferred_element_type=F32)


def _split_bf16(x, terms):
    out = []
    r = x
    for _ in range(terms):
        h = r.astype(BF16)
        out.append(h)
        r = r - h.astype(F32)
    return out


def _sel_dot_left(sel, x, terms):
    acc = None
    for h in _split_bf16(x, terms):
        d = jnp.dot(sel, h, preferred_element_type=F32)
        acc = d if acc is None else acc + d
    return acc


def _rms_rows(x):
    return x * lax.rsqrt(jnp.mean(x * x, axis=-1, keepdims=True) + EPS)


def _exp_neg(x):
    return jnp.exp2(x * (-LOG2E))


def _silu(x):
    return x * (1.0 / (1.0 + _exp_neg(x)))


def _softplus(x):
    return jnp.maximum(x, 0.0) + jnp.log(1.0 + _exp_neg(jnp.abs(x)))


def _log_sigmoid(x):
    return jnp.minimum(x, 0.0) - jnp.log(1.0 + _exp_neg(jnp.abs(x)))


def _tril(n):
    r = lax.broadcasted_iota(jnp.int32, (n, n), 0)
    c = lax.broadcasted_iota(jnp.int32, (n, n), 1)
    return r >= c


def _head_mean_sq(x):
    out = []
    for p in range(x.shape[1] // LANES):
        sq = x[:, p * LANES:(p + 1) * LANES]
        sq = sq * sq
        low = lax.broadcasted_iota(jnp.int32, sq.shape, 1) < HD_B
        lo = jnp.sum(jnp.where(low, sq, 0.0), axis=-1, keepdims=True)
        hi = jnp.sum(jnp.where(low, 0.0, sq), axis=-1, keepdims=True)
        out.append(jnp.where(low, lo, hi))
    return jnp.concatenate(out, axis=1) * (1.0 / HD_B)


def _tile_kv_heads(kv):
    low = lax.broadcasted_iota(jnp.int32, kv.shape, 1) < HD_B
    swapped = pltpu.roll(kv, HD_B, 1)
    out = []
    for base in (jnp.where(low, kv, swapped), jnp.where(low, swapped, kv)):
        base = base.astype(BF16)
        out.append(jnp.concatenate([base, base], axis=1))
    return out


def _ab_dense_in(x, rms_g, w_in_ref, gq, gk, proj_ref, qn_ref):
    h = (_rms_rows(x) * rms_g).astype(BF16)
    rows = min(256, x.shape[0])
    for m in range(0, x.shape[0], rows):
        proj_ref[m:m + rows, :] = jnp.dot(h[m:m + rows], w_in_ref[...], preferred_element_type=F32)
    qb = proj_ref[:, B_Q:B_Q + 512]
    qn_ref[...] = qb * lax.rsqrt(_head_mean_sq(qb) + EPS) * gq
    kb = proj_ref[:, B_K:B_K + 128]
    kn = kb * lax.rsqrt(_head_mean_sq(kb) + EPS) * gk
    vb = proj_ref[:, B_V:B_V + 128]
    return kn, vb


def _advance(gens, yielded=None):
    alive = []
    for gen in gens:
        try:
            value = next(gen)
            alive.append(gen)
            if yielded is not None and value is not None:
                yielded.append(value)
        except StopIteration:
            pass
    return alive


def _interleave(gens):
    gens = list(gens)
    while gens:
        gens = _advance(gens)


def _ab_chunk(r, c, kb, proj_ref, qn_ref, o_ref, state, k_band, v_band, bias_ref, sink_ref,
              w_gu, b_gate, g_out, first_valid_col):
    rows = pl.ds(r, c)
    gate = _dot(proj_ref[rows, A_G:A_G + LANES], w_gu) + b_gate
    yield
    g = _log_sigmoid(gate) * (LOG2E / GATE_NORM_A)
    tril = _tril(c)
    b = _sel_dot_left(tril.astype(BF16), g, 3)
    bl = b[c - 1:c, :]
    bl_tile = jnp.broadcast_to(bl, (LANES, H_A * DK_A))
    bl_rows = jnp.concatenate([bl_tile[:, i * LANES:(i + 1) * LANES].T
                               for i in range(H_A * DK_A // LANES)], axis=0)
    yield
    q = proj_ref[rows, A_Q:A_Q + 256] * (DK_A ** -0.5)
    k = proj_ref[rows, A_K:A_K + 256]
    v = proj_ref[rows, A_V:A_V + 512].astype(BF16)
    qe = (q * jnp.exp2(b)).astype(BF16)
    kd = k * jnp.exp2(-b)
    kd2 = (k * jnp.exp2(bl - b)).astype(BF16)
    lane_head = lax.broadcasted_iota(jnp.int32, (c, LANES), 1) // DK_A
    row_head = lax.broadcasted_iota(jnp.int32, (LANES, LANES), 0) // DK_A
    att = []
    for p in range(2):
        kp = kd[:, p * 128:(p + 1) * 128]
        kpair = jnp.concatenate([jnp.where(lane_head == j, kp, 0.0) for j in range(2)], axis=0).astype(BF16)
        att.append(_dot_nt(qe[:, p * 128:(p + 1) * 128], kpair))
    upd = [_dot_tn(kd2[:, p * 128:(p + 1) * 128], v[:, p * 256:(p + 1) * 256]) for p in range(2)]
    yield
    row = lax.broadcasted_iota(jnp.int32, (c, 2 * c), 0)
    col = lax.broadcasted_iota(jnp.int32, (c, 2 * c), 1) % c
    att = [jnp.where(row >= col, a, 0.0).astype(BF16) for a in att]
    yield
    s_prev = state[0]
    s_new, oh = [], []
    zeros_v = jnp.zeros((c, DV_A), BF16)
    for p in range(2):
        sp = s_prev[p * 128:(p + 1) * 128, :]
        s_bd = jnp.concatenate([jnp.where(row_head == j, sp, 0.0) for j in range(2)], axis=1).astype(BF16)
        v0, v1 = v[:, 2 * p * DV_A:(2 * p + 1) * DV_A], v[:, (2 * p + 1) * DV_A:(2 * p + 2) * DV_A]
        v_bd = jnp.concatenate([jnp.concatenate([v0, zeros_v], axis=1),
                                jnp.concatenate([zeros_v, v1], axis=1)], axis=0)
        qp = qe[:, p * 128:(p + 1) * 128]
        if (2 * c) % LANES == 0:
            o_pair = jnp.dot(jnp.concatenate([att[p], qp], axis=1), jnp.concatenate([v_bd, s_bd], axis=0),
                             preferred_element_type=F32)
        else:
            o_pair = (jnp.dot(att[p], v_bd, preferred_element_type=F32)
                      + jnp.dot(qp, s_bd, preferred_element_type=F32))
        oh += [o_pair[:, :DV_A], o_pair[:, DV_A:]]
        u = jnp.where(row_head == 0, upd[p][:, :128], upd[p][:, 128:])
        s_new.append(jnp.exp2(bl_rows[p * 128:(p + 1) * 128, :]) * sp + u)
    state[0] = jnp.concatenate(s_new, axis=0)
    yield
    for hd in range(H_A):
        z = proj_ref[rows, A_Z + hd * 128:A_Z + (hd + 1) * 128]
        o_ref[rows, hd * 128:(hd + 1) * 128] = (_rms_rows(oh[hd]) * g_out * _silu(z)).astype(BF16)
    yield
    lane256 = lax.broadcasted_iota(jnp.int32, (c, 256), 1) // HD_B
    srow = lax.broadcasted_iota(jnp.int32, (G_B * c, 1), 0) // c
    scores = []
    for kvh in range(KVH_B):
        qn = qn_ref[rows, kvh * 256:(kvh + 1) * 256]
        qs = jnp.concatenate([jnp.where(lane256 == gq_, qn, 0.0) for gq_ in range(G_B)], axis=0)
        scores.append(_dot_nt(qs, k_band(kvh)))
    yield
    probs, dens = [], []
    for kvh in range(KVH_B):
        s = scores[kvh] * (LOG2E * HD_B ** -0.5) + bias_ref[kvh]
        if first_valid_col is not None:
            col = lax.broadcasted_iota(jnp.int32, (G_B * c, kb), 1)
            s = jnp.where(col >= first_valid_col, s, -jnp.inf)
        sink = jnp.zeros((G_B * c, 1), F32)
        for gq_ in range(G_B):
            sink = jnp.where(srow == gq_, sink_ref[kvh * G_B + gq_] * LOG2E, sink)
        m = jnp.maximum(jnp.max(s, axis=-1, keepdims=True), sink)
        pr = jnp.exp2(s - m)
        dens.append(jnp.sum(pr, axis=-1, keepdims=True) + jnp.exp2(sink - m))
        probs.append(pr.astype(BF16))
    yield
    outs = [_dot(probs[kvh], v_band(kvh)) for kvh in range(KVH_B)]
    yield
    for kvh in range(KVH_B):
        ost = outs[kvh] / dens[kvh]
        ob = jnp.zeros((c, 256), F32)
        for gq_ in range(G_B):
            ob = ob + jnp.where(lane256 == gq_, ost[gq_ * c:(gq_ + 1) * c, :], 0.0)
        z = proj_ref[rows, B_Z + kvh * 256:B_Z + (kvh + 1) * 256]
        o_ref[rows, 512 + kvh * 256:512 + (kvh + 1) * 256] = (ob * _silu(z)).astype(BF16)


def _ab_prompt_kernel(sink_ref, x_ref, rms_g_ref, w_in_ref, w_gu_ref, b_gate_ref, g_out_ref, gq_ref,
                      gk_ref, bias_ref, w_out_ref,
                      y_ref, s_out_ref, k_out_ref, v_out_ref,
                      proj_ref, qn_ref, o_ref, kband_ref, vband_ref, s_ref):
    t = x_ref.shape[0]
    c = CHUNK
    kb = WINDOW + c
    nchunk = t // c
    step = pl.program_id(0)

    @pl.when(step == 0)
    def _():
        s_ref[...] = jnp.zeros_like(s_ref)
        kband_ref[:, t:t + WINDOW, :] = jnp.zeros((KVH_B, WINDOW, 256), BF16)
        vband_ref[:, t:t + WINDOW, :] = jnp.zeros((KVH_B, WINDOW, 256), BF16)

    for kvh in range(KVH_B):
        kband_ref[kvh, 0:WINDOW, :] = kband_ref[kvh, t:t + WINDOW, :]
        vband_ref[kvh, 0:WINDOW, :] = vband_ref[kvh, t:t + WINDOW, :]

    x = x_ref[...]
    kn, vb = _ab_dense_in(x, rms_g_ref[...], w_in_ref, gq_ref[...], gk_ref[...], proj_ref, qn_ref)
    k_out_ref[...] = kn[t - WINDOW:, :]
    v_out_ref[...] = vb[t - WINDOW:, :]
    for kvh, (kt, vt) in enumerate(zip(_tile_kv_heads(kn), _tile_kv_heads(vb))):
        kband_ref[kvh, WINDOW:WINDOW + t, :] = kt
        vband_ref[kvh, WINDOW:WINDOW + t, :] = vt

    w_gu = w_gu_ref[...]
    b_gate = b_gate_ref[...]
    g_out = g_out_ref[...]

    state = [s_ref[...]]

    def chunk(i):
        r = i * c
        return _ab_chunk(r, c, kb, proj_ref, qn_ref, o_ref, state,
                         lambda kvh: kband_ref[kvh, pl.ds(r, kb), :],
                         lambda kvh: vband_ref[kvh, pl.ds(r, kb), :],
                         bias_ref, sink_ref, w_gu, b_gate, g_out, (2 - (step * nchunk + i)) * c)

    _interleave(chunk(i) for i in range(nchunk))
    s_ref[...] = state[0]
    y_ref[...] = x + jnp.dot(o_ref[...], w_out_ref[...], preferred_element_type=F32)
    s_out_ref[...] = s_ref[...]


def _ab_sample_kernel(sink_ref, x_ref, rms_g_ref, w_in_ref, w_gu_ref, b_gate_ref, g_out_ref, gq_ref,
                      gk_ref, bias_ref, w_out_ref,
                      s_in_ref, kc_ref, vc_ref,
                      y_ref, s_out_ref, k_out_ref, v_out_ref,
                      proj_ref, qn_ref, o_ref, kband_ref, vband_ref):
    nb, wc = kc_ref.shape[0], kc_ref.shape[1]
    t = x_ref.shape[0]
    c = t // nb
    kb = wc + c
    x = x_ref[...]
    kn, vb = _ab_dense_in(x, rms_g_ref[...], w_in_ref, gq_ref[...], gk_ref[...], proj_ref, qn_ref)
    k_out_ref[...] = kn
    v_out_ref[...] = vb
    k_new, v_new = _tile_kv_heads(kn), _tile_kv_heads(vb)
    for bi in range(nb):
        k_old, v_old = _tile_kv_heads(kc_ref[bi]), _tile_kv_heads(vc_ref[bi])
        for kvh in range(KVH_B):
            kband_ref[kvh, bi, 0:wc, :] = k_old[kvh]
            vband_ref[kvh, bi, 0:wc, :] = v_old[kvh]
            kband_ref[kvh, bi, wc:kb, :] = k_new[kvh][bi * c:(bi + 1) * c, :]
            vband_ref[kvh, bi, wc:kb, :] = v_new[kvh][bi * c:(bi + 1) * c, :]

    w_gu = w_gu_ref[...]
    b_gate = b_gate_ref[...]
    g_out = g_out_ref[...]

    states = [[s_in_ref[bi]] for bi in range(nb)]

    def seq(bi):
        return _ab_chunk(bi * c, c, kb, proj_ref, qn_ref, o_ref, states[bi],
                         lambda kvh: kband_ref[kvh, bi], lambda kvh: vband_ref[kvh, bi],
                         bias_ref, sink_ref, w_gu, b_gate, g_out, None)

    _interleave(seq(bi) for bi in range(nb))
    for bi in range(nb):
        s_out_ref[bi] = states[bi][0]
    y_ref[...] = x + jnp.dot(o_ref[...], w_out_ref[...], preferred_element_type=F32)


GROUP_W = D_INNER_C // G_C


def _c_chunk(r, c, io, state, e_s, e_p, neg_mask, bd_mask, dskip, g_y):
    dtc = io.dt(r)
    acum = _sel_dot_left(_tril(c).astype(BF16), io.da(r), 3) * LOG2E
    yield acum
    lhs = jnp.concatenate([jnp.concatenate(_split_bf16(acum, 2), axis=1),
                           jnp.concatenate(_split_bf16(dtc, 2), axis=1)], axis=0)
    both_p = jnp.dot(lhs, e_p, preferred_element_type=F32)
    xa_p, dt_p = both_p[:c], both_p[c:]
    xa_s = xa_p if c == P_C else jnp.dot(lhs[:c], e_s, preferred_element_type=F32)
    yield dt_p
    acum_t = acum.T
    a_row = jnp.concatenate([acum_t[h:h + 1, :] for h in range(H_C)], axis=1)
    wmat = jnp.exp2((xa_s - a_row) + neg_mask)
    al_p = xa_p[c - 1:c, :]
    xs = io.x(r)
    xdt = xs * dt_p
    xdt_bf = xdt.astype(BF16)
    xw = (xdt * jnp.exp2(al_p - xa_p)).astype(BF16)
    dec = jnp.exp2(al_p)
    bg, cg, cb = [], [], []
    for g in range(G_C):
        bg.append(io.b(r, g).astype(BF16))
        cg.append(io.c(r, g).astype(BF16))
        cb.append(_dot_nt(cg[g], jnp.concatenate([bg[g]] * HPG_C, axis=0)))
    yield cb[-1]
    ys = []
    for g in range(G_C):
        mg = (cb[g] * wmat[:, g * HPG_C * c:(g + 1) * HPG_C * c]).astype(BF16)
        for j in range(2):
            xj = xdt_bf[:, g * GROUP_W + j * 256:g * GROUP_W + (j + 1) * 256]
            bd = jnp.concatenate([xj] * 4, axis=0) * bd_mask
            ys.append(jnp.dot(mg[:, j * 4 * c:(j + 1) * 4 * c], bd, preferred_element_type=F32))
    upd = [_dot_tn(bg[g], xw[:, g * GROUP_W:(g + 1) * GROUP_W]) for g in range(G_C)]
    yield upd[-1]
    st_prev = state[0]
    y_inter, st_new = [], []
    for g in range(G_C):
        sl = slice(g * GROUP_W, (g + 1) * GROUP_W)
        y_inter.append(_dot(cg[g], st_prev[:, sl]))
        st_new.append(st_prev[:, sl] * dec[:, sl] + upd[g])
    state[0] = jnp.concatenate(st_new, axis=1)
    yield st_new[-1]
    for g in range(G_C):
        sl = slice(g * GROUP_W, (g + 1) * GROUP_W)
        y = jnp.concatenate(ys[2 * g:2 * g + 2], axis=1) + y_inter[g] * jnp.exp2(xa_p[:, sl])
        y = y + dskip[:, sl] * xs[:, sl]
        y = y * io.gate(r, g)
        io.put_o(r, g, (_rms_rows(y) * g_y[:, sl]).astype(BF16))


def _c_dt(dt_cols, dt_bias, a_log):
    dt = _softplus(dt_cols + dt_bias)
    return dt, dt * (-jnp.exp(a_log))


CONV_PITCH = PROMPT_BLOCK // 8 + 1
CONV_ROWS = 8 * CONV_PITCH
CONV_TILES = CONV_DIM // LANES
C_ROUNDS = 12
C_IN_SLABS = tuple((C_X + 512 * k, C_X + 512 * (k + 1)) for k in range(6)) + tuple(
    (C_Z + 512 * k, C_Z + 512 * (k + 1)) for k in range(4)) + ((C_DT, C_COLS),)
C_MIX_FIRST_ROUND = 5
C_OUT_ROUNDS = (5, 7, 9, 11)
assert len(C_IN_SLABS) <= C_ROUNDS and sum(hi - lo for lo, hi in C_IN_SLABS) == C_COLS


def _conv_tile(ubuf_ref, act_ref, j, conv_w, conv_b):
    w = [jnp.broadcast_to(conv_w[i:i + 1, j * LANES:(j + 1) * LANES], (8, LANES)) for i in range(CONV_W)]
    b = jnp.broadcast_to(conv_b[:, j * LANES:(j + 1) * LANES], (8, LANES))
    for a in range(CONV_PITCH):
        acc = b
        for i in range(CONV_W):
            acc = acc + w[i] * ubuf_ref[j, pl.ds(8 - (CONV_W - 1) + i + a, 8, stride=CONV_PITCH), :]
        act_ref[j, pl.ds(a, 8, stride=CONV_PITCH), :] = _silu(acc)


def _c_prompt_kernel(xin_ref, xres_ref, rms_g_ref, w_in_ref, conv_w_ref, conv_b_ref, dt_bias_ref,
                     a_log_ref, dskip_ref, g_y_ref, w_out_ref, e_s_ref, neg_mask_ref, bd_mask_ref,
                     y_ref, st_out_ref, conv_out_ref,
                     z_ref, ubuf_ref, act_ref, dt_ref, da_ref, o_ref, st_ref):
    t = xin_ref.shape[0]
    c = CHUNK
    s = pl.program_id(0)
    n_blocks = pl.num_programs(0) - 2

    @pl.when(s == 0)
    def _():
        o_ref[1] = jnp.zeros(o_ref.shape[1:], BF16)
        ubuf_ref[...] = jnp.zeros_like(ubuf_ref)

    @pl.when(s == 0)
    def _():
        st_ref[...] = jnp.zeros_like(st_ref)

    def in_stage(slot_in):
        h = (_rms_rows(xin_ref[...]) * rms_g_ref[...]).astype(BF16)
        conv_w = conv_w_ref[...]
        conv_b = conv_b_ref[...]
        act = act_ref.at[slot_in]
        for rnd, (lo, hi) in enumerate(C_IN_SLABS):
            if rnd:
                yield
            slab = jnp.dot(h, w_in_ref[:, lo:hi], preferred_element_type=F32)
            if lo == C_DT:
                dt, da = _c_dt(slab, dt_bias_ref[...], a_log_ref[...])
                dt_ref[slot_in] = dt
                da_ref[slot_in] = da
            elif lo >= C_X:
                for j in range((lo - C_X) // LANES, (hi - C_X) // LANES):
                    col = C_X + j * LANES - lo
                    ubuf_ref[j, 8:16, :] = ubuf_ref[j, CONV_ROWS:CONV_ROWS + 8, :]
                    ubuf_ref[j, 16:16 + t, :] = slab[:, col:col + LANES]
                    _conv_tile(ubuf_ref, act, j, conv_w, conv_b)
            else:
                z_ref[slot_in, :, lo:hi] = _silu(slab)

    n_x = D_INNER_C // LANES
    n_g = N_C // LANES

    def mix_stage(slot_mix):
        def put_o(r, g, value):
            o_ref[slot_mix, pl.ds(r, c), g * GROUP_W:(g + 1) * GROUP_W] = value

        def act_rows(r, j):
            return act_ref[slot_mix, j, pl.ds(8 + r, c), :]

        io = types.SimpleNamespace(
            dt=lambda r: dt_ref[slot_mix, pl.ds(r, c), :],
            da=lambda r: da_ref[slot_mix, pl.ds(r, c), :],
            x=lambda r: jnp.concatenate([act_rows(r, j) for j in range(n_x)], axis=1),
            b=lambda r, g: act_rows(r, n_x + g * n_g),
            c=lambda r, g: act_rows(r, n_x + (G_C + g) * n_g),
            gate=lambda r, g: z_ref[slot_mix, pl.ds(r, c), C_Z + g * GROUP_W:C_Z + (g + 1) * GROUP_W],
            put_o=put_o)
        for _ in range(C_MIX_FIRST_ROUND):
            yield
        state = [st_ref[...]]
        e_s = e_s_ref[...]
        chunks = [_c_chunk(i * c, c, io, state, e_s, e_s, neg_mask_ref[...], bd_mask_ref[...],
                           dskip_ref[...], g_y_ref[...]) for i in range(t // c)]
        chunks = _advance(chunks)
        while chunks:
            yield
            chunks = _advance(chunks)
        st_ref[...] = state[0]

    def out_stage(slot_in):
        quarter = 0
        for rnd in range(C_ROUNDS):
            if rnd:
                yield
            if rnd in C_OUT_ROUNDS:
                cols = slice(quarter * 256, (quarter + 1) * 256)
                y_ref[:, cols] = xres_ref[:, cols] + jnp.dot(o_ref[slot_in], w_out_ref[:, cols],
                                                             preferred_element_type=F32)
                quarter += 1

    slot_in = s % 2

    @pl.when(s == 0)
    def _():
        _interleave([in_stage(slot_in)])

    @pl.when(jnp.logical_and(s > 0, s <= n_blocks))
    def _():
        _interleave([in_stage(slot_in), mix_stage(1 - slot_in), out_stage(slot_in)])

    @pl.when(s == n_blocks + 1)
    def _():
        _interleave([out_stage(slot_in)])

    @pl.when(s == n_blocks)
    def _():
        st_out_ref[...] = st_ref[...].T
        for j in range(CONV_TILES):
            conv_out_ref[:, j * LANES:(j + 1) * LANES] = ubuf_ref[j, CONV_ROWS:CONV_ROWS + 8, :]


def _conv_rows(ubuf, nrows, conv_w, conv_b):
    acc = conv_b
    for i in range(CONV_W):
        acc = acc + conv_w[i:i + 1, :] * ubuf[pl.ds(8 - (CONV_W - 1) + i, nrows), :]
    return _silu(acc)


def _c_sample_kernel(x_ref, rms_g_ref, w_in_ref, conv_w_ref, conv_b_ref, dt_bias_ref, a_log_ref,
                     dskip_ref, g_y_ref, w_out_ref, e_s_ref, e_p_ref, neg_mask_ref, bd_mask_ref,
                     st_in_ref, conv_in_ref,
                     y_ref, st_out_ref, conv_out_ref,
                     proj_ref, ubuf_ref, act_ref, dt_ref, da_ref, o_ref):
    nb = st_in_ref.shape[0]
    t = x_ref.shape[0]
    c = t // nb
    x = x_ref[...]
    h = (_rms_rows(x) * rms_g_ref[...]).astype(BF16)
    proj_ref[...] = jnp.dot(h, w_in_ref[...], preferred_element_type=F32)
    dt, da = _c_dt(proj_ref[:, C_DT:C_DT + LANES], dt_bias_ref[...], a_log_ref[...])
    dt_ref[...] = dt
    da_ref[...] = da
    conv_w = conv_w_ref[...]
    conv_b = conv_b_ref[...]
    ubuf_ref[0:8, :] = jnp.zeros((8, CONV_DIM), F32)
    for bi in range(nb):
        ubuf_ref[8 - (CONV_W - 1):8, :] = conv_in_ref[bi]
        ubuf_ref[8:8 + c, :] = proj_ref[bi * c:(bi + 1) * c, C_X:C_X + CONV_DIM]
        act_ref[bi * c:(bi + 1) * c, :] = _conv_rows(ubuf_ref, c, conv_w, conv_b)
        conv_out_ref[bi] = ubuf_ref[8 + c - (CONV_W - 1):8 + c, :]

    def put_o(r, g, value):
        o_ref[pl.ds(r, c), g * GROUP_W:(g + 1) * GROUP_W] = value

    io = types.SimpleNamespace(
        dt=lambda r: dt_ref[pl.ds(r, c), :],
        da=lambda r: da_ref[pl.ds(r, c), :],
        x=lambda r: act_ref[pl.ds(r, c), 0:D_INNER_C],
        b=lambda r, g: act_ref[pl.ds(r, c), D_INNER_C + g * N_C:D_INNER_C + (g + 1) * N_C],
        c=lambda r, g: act_ref[pl.ds(r, c), D_INNER_C + (G_C + g) * N_C:D_INNER_C + (G_C + g + 1) * N_C],
        gate=lambda r, g: _silu(proj_ref[pl.ds(r, c), C_Z + g * GROUP_W:C_Z + (g + 1) * GROUP_W]),
        put_o=put_o)
    e_s = e_s_ref[...]
    e_p = e_p_ref[...]
    neg_mask = neg_mask_ref[...]
    bd_mask = bd_mask_ref[...]
    dskip = dskip_ref[...]
    g_y = g_y_ref[...]

    states = [[st_in_ref[bi].T] for bi in range(nb)]
    _interleave(_c_chunk(bi * c, c, io, states[bi], e_s, e_p, neg_mask, bd_mask, dskip, g_y)
                for bi in range(nb))
    for bi in range(nb):
        st_out_ref[bi] = states[bi][0].T
    y_ref[...] = x + jnp.dot(o_ref[...], w_out_ref[...], preferred_element_type=F32)


def _bucket_table(q_off, k_off):
    n = q_off[:, None] - k_off[None, :]
    half = N_BUCKETS // 2
    max_exact = half // 2
    side = np.where(n < 0, half, 0)
    n = np.abs(n)
    nf = np.maximum(n, max_exact).astype(np.float32)
    large = max_exact + (np.log(nf / np.float32(max_exact)) / np.float32(math.log(MAX_DISTANCE / max_exact))
                         * np.float32(half - max_exact)).astype(np.int32)
    large = np.minimum(large, half - 1)
    return side + np.where(n < max_exact, n, large)


def _bias_rows(rel_bias, q_off, k_off):
    bucket = _bucket_table(q_off, k_off)
    onehot = jnp.asarray(np.eye(N_BUCKETS, dtype=np.float32)[bucket])
    bias = jnp.einsum('qkb,bh->hqk', onehot, rel_bias.astype(F32), precision=lax.Precision.HIGHEST) * LOG2E
    return bias.reshape(KVH_B, G_B * q_off.shape[0], k_off.shape[0])


def _head_expand_matrix(per_head):
    m = np.zeros((LANES, H_C * per_head), np.float32)
    for h in range(H_C):
        m[h, h * per_head:(h + 1) * per_head] = 1.0
    return jnp.asarray(np.concatenate([m, m], axis=0), BF16)


def _full(shape):
    return pl.BlockSpec(shape, lambda *_: (0,) * len(shape))


def _const(shape):
    return pl.BlockSpec(shape, lambda *_: (0,) * len(shape), pipeline_mode=pl.Buffered(1))


CAST_ROWS = 256


def _cast_columns(wt_ref, w_bf_ref, src, dst, n):
    for off in range(0, n - n % LANES, CAST_ROWS):
        rows = min(CAST_ROWS, n - n % LANES - off)
        w_bf_ref[:, dst + off:dst + off + rows] = wt_ref[src + off:src + off + rows, :].T.astype(BF16)
    rest = n % LANES
    if rest:
        off = n - rest
        tail = jnp.concatenate([wt_ref[src + off:src + n, :], jnp.zeros((LANES - rest, wt_ref.shape[1]), F32)],
                               axis=0)
        w_bf_ref[:, dst + off:dst + off + LANES] = tail.T.astype(BF16)


def _ab_cast_kernel(wt_ref, w_out_ref, w_in_bf_ref, w_out_bf_ref):
    g0 = A_Z
    _cast_columns(wt_ref, w_in_bf_ref, 0, 0, g0)
    _cast_columns(wt_ref, w_in_bf_ref, g0 + GATE_RANK_A, g0, A_G - g0)
    _cast_columns(wt_ref, w_in_bf_ref, g0, A_G, GATE_RANK_A)
    w_out_bf_ref[...] = w_out_ref[...].astype(BF16)


def _c_cast_kernel(wt_ref, w_out_ref, w_in_bf_ref, w_out_bf_ref):
    _cast_columns(wt_ref, w_in_bf_ref, 0, 0, wt_ref.shape[0])
    w_out_bf_ref[...] = w_out_ref[...].astype(BF16)


def _cast_weights(body, w_in, w_out, cols, name):
    vm = pl.BlockSpec(memory_space=pltpu.VMEM)
    return pl.pallas_call(
        body, in_specs=[vm, vm], out_specs=(vm, vm),
        out_shape=(jax.ShapeDtypeStruct((w_in.shape[0], cols), BF16),
                   jax.ShapeDtypeStruct(w_out.shape, BF16)),
        name=name, compiler_params=pltpu.CompilerParams(vmem_limit_bytes=VMEM_LIMIT),
    )(w_in.T, w_out)


def _ab_weights(rms_g, w_in, w_gate_up, b_gate, g_out, g_q, g_k, w_out):
    w_in_r, w_out_bf = _cast_weights(_ab_cast_kernel, w_in, w_out, AB_COLS, "ab_cast")
    w_gu = jnp.concatenate([w_gate_up, jnp.zeros((LANES - GATE_RANK_A, H_A * DK_A), w_gate_up.dtype)],
                           axis=0).astype(BF16)
    return (rms_g.reshape(1, D_MODEL), w_in_r, w_gu, b_gate.reshape(1, -1), g_out.reshape(1, DV_A),
            jnp.tile(g_q, H_B).reshape(1, -1), jnp.tile(g_k, KVH_B).reshape(1, -1), w_out_bf)


def _ab_prompt(x, sink, rel_bias, weights):
    rms_g, w_in, w_gu, b_gate, g_out, gq, gk, w_out = weights
    length = x.shape[0]
    t = AB_PROMPT_BLOCK
    c = CHUNK
    kb = WINDOW + c
    bias = _bias_rows(rel_bias, np.arange(c), np.arange(kb) - WINDOW)
    row_blk = lambda i: (i, 0)
    in_specs = [
        pl.BlockSpec(memory_space=pltpu.SMEM),
        pl.BlockSpec((t, D_MODEL), row_blk),
        _const((1, D_MODEL)), _const((D_MODEL, AB_COLS)), _const((LANES, 256)), _const((1, 256)),
        _const((1, DV_A)), _const((1, 512)), _const((1, 128)), _const((KVH_B, G_B * c, kb)),
        _const((D_MODEL, D_MODEL)),
    ]
    out_shape = (jax.ShapeDtypeStruct((length, D_MODEL), F32),
                 jax.ShapeDtypeStruct((H_A * DK_A, DV_A), F32),
                 jax.ShapeDtypeStruct((WINDOW, KVH_B * HD_B), F32),
                 jax.ShapeDtypeStruct((WINDOW, KVH_B * HD_B), F32))
    out_specs = (pl.BlockSpec((t, D_MODEL), row_blk), _full((H_A * DK_A, DV_A)),
                 _full((WINDOW, KVH_B * HD_B)), _full((WINDOW, KVH_B * HD_B)))
    scratch = [pltpu.VMEM((t, AB_COLS), F32), pltpu.VMEM((t, 512), F32), pltpu.VMEM((t, D_MODEL), BF16),
               pltpu.VMEM((KVH_B, t + WINDOW, 256), BF16), pltpu.VMEM((KVH_B, t + WINDOW, 256), BF16),
               pltpu.VMEM((H_A * DK_A, DV_A), F32)]
    return pl.pallas_call(
        _ab_prompt_kernel, grid=(length // t,), in_specs=in_specs, out_specs=out_specs,
        out_shape=out_shape, scratch_shapes=scratch, name="ab_prompt",
        compiler_params=pltpu.CompilerParams(dimension_semantics=("arbitrary",),
                                             vmem_limit_bytes=VMEM_LIMIT),
    )(sink, x, rms_g, w_in, w_gu, b_gate, g_out, gq, gk, bias, w_out)


def _ab_sample(x, sink, rel_bias, weights, s_in, k_cache, v_cache):
    rms_g, w_in, w_gu, b_gate, g_out, gq, gk, w_out = weights
    nb, c, _ = x.shape
    wc = k_cache.shape[1]
    kb = wc + c
    t = nb * c
    bias = _bias_rows(rel_bias, np.arange(c), np.arange(kb) - wc)
    vm = pl.BlockSpec(memory_space=pltpu.VMEM)
    in_specs = [pl.BlockSpec(memory_space=pltpu.SMEM)] + [vm] * 13
    out_shape = (jax.ShapeDtypeStruct((t, D_MODEL), F32),
                 jax.ShapeDtypeStruct((nb, H_A * DK_A, DV_A), F32),
                 jax.ShapeDtypeStruct((t, KVH_B * HD_B), F32),
                 jax.ShapeDtypeStruct((t, KVH_B * HD_B), F32))
    scratch = [pltpu.VMEM((t, AB_COLS), F32), pltpu.VMEM((t, 512), F32), pltpu.VMEM((t, D_MODEL), BF16),
               pltpu.VMEM((KVH_B, nb, kb, 256), BF16), pltpu.VMEM((KVH_B, nb, kb, 256), BF16)]
    return pl.pallas_call(
        _ab_sample_kernel, in_specs=in_specs, out_specs=(vm,) * 4, out_shape=out_shape,
        scratch_shapes=scratch, name="ab_sample",
        compiler_params=pltpu.CompilerParams(vmem_limit_bytes=VMEM_LIMIT),
    )(sink, x.reshape(t, D_MODEL), rms_g, w_in, w_gu, b_gate, g_out, gq, gk, bias, w_out,
      s_in.reshape(nb, H_A * DK_A, DV_A), k_cache.reshape(nb, wc, KVH_B * HD_B),
      v_cache.reshape(nb, wc, KVH_B * HD_B))


def _c_weights(rms_g, w_in, conv_w, conv_b, dt_bias, a_log, d_skip, g_y, w_out):
    w_in_p, w_out_bf = _cast_weights(_c_cast_kernel, w_in, w_out, C_COLS, "c_cast")
    pad = lambda v: jnp.concatenate([v, jnp.zeros((LANES - H_C,), v.dtype)]).reshape(1, LANES)
    return (rms_g.reshape(1, D_MODEL), w_in_p, conv_w, conv_b.reshape(1, CONV_DIM), pad(dt_bias), pad(a_log),
            jnp.repeat(d_skip, P_C).reshape(1, D_INNER_C), g_y.reshape(1, D_INNER_C), w_out_bf)


def _ssd_masks(c):
    tok = np.arange(c)[:, None]
    src = np.arange(H_C * c)[None, :] % c
    neg = np.where(src <= tok, 0.0, -np.inf).astype(np.float32)
    blk = (np.arange(4 * c)[:, None] // c) == (np.arange(256)[None, :] // P_C)
    return jnp.asarray(neg), jnp.asarray(blk, BF16)


def _c_prompt(x, weights):
    length = x.shape[0]
    t = PROMPT_BLOCK
    c = CHUNK
    n_blocks = length // t
    in_blk = lambda s: (jnp.minimum(s, n_blocks - 1), 0)
    out_blk = lambda s: (jnp.clip(s - 2, 0, n_blocks - 1), 0)
    in_specs = [
        pl.BlockSpec((t, D_MODEL), in_blk), pl.BlockSpec((t, D_MODEL), out_blk),
        _const((1, D_MODEL)), _const((D_MODEL, C_COLS)), _const((CONV_W, CONV_DIM)), _const((1, CONV_DIM)),
        _const((1, LANES)), _const((1, LANES)), _const((1, D_INNER_C)), _const((1, D_INNER_C)),
        _const((D_INNER_C, D_MODEL)), _const((2 * LANES, H_C * c)),
        _const((c, H_C * c)), _const((4 * c, 256)),
    ]
    out_shape = (jax.ShapeDtypeStruct((length, D_MODEL), F32),
                 jax.ShapeDtypeStruct((H_C * P_C, N_C), F32),
                 jax.ShapeDtypeStruct((8, CONV_DIM), F32))
    out_specs = (pl.BlockSpec((t, D_MODEL), out_blk), _full((H_C * P_C, N_C)), _full((8, CONV_DIM)))
    scratch = [pltpu.VMEM((2, t, D_INNER_C), F32), pltpu.VMEM((CONV_TILES, CONV_ROWS + 8, LANES), F32),
               pltpu.VMEM((2, CONV_TILES, CONV_ROWS, LANES), F32), pltpu.VMEM((2, t, LANES), F32),
               pltpu.VMEM((2, t, LANES), F32), pltpu.VMEM((2, t, D_INNER_C), BF16),
               pltpu.VMEM((N_C, H_C * P_C), F32)]
    return pl.pallas_call(
        _c_prompt_kernel, grid=(n_blocks + 2,), in_specs=in_specs, out_specs=out_specs,
        out_shape=out_shape, scratch_shapes=scratch, name="c_prompt",
        compiler_params=pltpu.CompilerParams(dimension_semantics=("arbitrary",),
                                             vmem_limit_bytes=VMEM_LIMIT),
    )(x, x, *weights, _head_expand_matrix(c), *_ssd_masks(c))


def _c_sample(x, weights, st_in, conv_in):
    nb, c, _ = x.shape
    t = nb * c
    vm = pl.BlockSpec(memory_space=pltpu.VMEM)
    out_shape = (jax.ShapeDtypeStruct((t, D_MODEL), F32),
                 jax.ShapeDtypeStruct((nb, H_C * P_C, N_C), F32),
                 jax.ShapeDtypeStruct((nb, CONV_W - 1, CONV_DIM), F32))
    scratch = [pltpu.VMEM((t, C_COLS), F32), pltpu.VMEM((c + 8, CONV_DIM), F32),
               pltpu.VMEM((t, CONV_DIM), F32), pltpu.VMEM((t, LANES), F32), pltpu.VMEM((t, LANES), F32),
               pltpu.VMEM((t, D_INNER_C), BF16)]
    return pl.pallas_call(
        _c_sample_kernel, in_specs=[vm] * 16, out_specs=(vm,) * 3, out_shape=out_shape,
        scratch_shapes=scratch, name="c_sample",
        compiler_params=pltpu.CompilerParams(vmem_limit_bytes=VMEM_LIMIT),
    )(x.reshape(t, D_MODEL), *weights, _head_expand_matrix(c), _head_expand_matrix(P_C), *_ssd_masks(c),
      st_in.reshape(nb, H_C * P_C, N_C), conv_in)


def kernel(x_prompt, x_sample, cache_swa_k, cache_swa_v, state_gla, state_ssd, state_conv, rms_g, w_in_ab, w_gate_up_a, b_gate_a, g_out_a, g_q_b, g_k_b, sink_b, rel_bias, w_out_ab, w_in_c, conv_w_c, conv_b_c, dt_bias_c, a_log_c, d_skip_c, g_y_c, w_out_c):
    bp, seq_len, _ = x_prompt.shape
    nb, dec_len, _ = x_sample.shape
    assert bp == 1 and seq_len % PROMPT_BLOCK == 0 and seq_len % AB_PROMPT_BLOCK == 0 and seq_len >= WINDOW
    wab = _ab_weights(rms_g[0], w_in_ab[0], w_gate_up_a[0], b_gate_a[0], g_out_a[0], g_q_b[0], g_k_b[0],
                      w_out_ab[0])
    yp, gla_p, k_p, v_p = _ab_prompt(x_prompt[0], sink_b[0], rel_bias, wab)
    ys, gla_s, k_s, v_s = _ab_sample(x_sample, sink_b[0], rel_bias, wab, state_gla[0], cache_swa_k[0],
                                     cache_swa_v[0])
    wc = _c_weights(rms_g[1], w_in_c[0], conv_w_c[0], conv_b_c[0], dt_bias_c[0], a_log_c[0], d_skip_c[0],
                    g_y_c[0], w_out_c[0])
    yp, ssd_p, conv_p = _c_prompt(yp, wc)
    ys, ssd_s, conv_s = _c_sample(ys.reshape(nb, dec_len, D_MODEL), wc, state_ssd[0], state_conv[0])
    return (
        yp.reshape(1, seq_len, D_MODEL),
        ys.reshape(nb, dec_len, D_MODEL),
        gla_p.reshape(1, 1, H_A, DK_A, DV_A),
        k_p.reshape(1, 1, WINDOW, KVH_B, HD_B),
        v_p.reshape(1, 1, WINDOW, KVH_B, HD_B),
        ssd_p.reshape(1, 1, H_C, P_C, N_C),
        conv_p[8 - (CONV_W - 1):].reshape(1, 1, CONV_W - 1, CONV_DIM),
        gla_s.reshape(1, nb, H_A, DK_A, DV_A),
        k_s.reshape(1, nb, dec_len, KVH_B, HD_B),
        v_s.reshape(1, nb, dec_len, KVH_B, HD_B),
        ssd_s.reshape(1, nb, H_C, P_C, N_C),
        conv_s.reshape(1, nb, CONV_W - 1, CONV_DIM),
    )
```

```python
import math
import types

import numpy as np
import jax
import jax.numpy as jnp
from jax import lax
from jax.experimental import pallas as pl
from jax.experimental.pallas import tpu as pltpu

F32 = jnp.float32
BF16 = jnp.bfloat16

D_MODEL = 1024
CHUNK = 64
EPS = 1e-6
H_A = 4
DK_A = 64
DV_A = 128
GATE_RANK_A = 16
GATE_NORM_A = 16.0
H_B = 8
KVH_B = 2
G_B = H_B // KVH_B
HD_B = 64
WINDOW = 128
N_BUCKETS = 32
MAX_DISTANCE = 128
D_INNER_C = 2048
P_C = 64
H_C = 32
G_C = 4
HPG_C = 8
N_C = 128
CONV_W = 4
CONV_DIM = D_INNER_C + 2 * G_C * N_C

LOG2E = 1.4426950408889634
LANES = 128

A_Q, A_K, A_V, A_Z = 0, 256, 512, 1024
B_Q, B_K, B_V, B_Z = 1536, 2048, 2176, 2304
A_G = 2816
AB_COLS = 2944
C_Z, C_X, C_B, C_C, C_DT = 0, 2048, 4096, 4608, 5120
C_COLS = 5248

PROMPT_BLOCK = 256
AB_PROMPT_BLOCK = 512
VMEM_LIMIT = 60 * 1024 * 1024


def _dot(a, b):
    return jnp.dot(a.astype(BF16), b.astype(BF16), preferred_element_type=F32)


def _dot_nt(a, b):
    return lax.dot_general(a.astype(BF16), b.astype(BF16), (((1,), (1,)), ((), ())),
                           preferred_element_type=F32)


def _dot_tn(a, b):
    return lax.dot_general(a.astype(BF16), b.astype(BF16), (((0,), (0,)), ((), ())),
                           preferred_element_type=F32)


def _split_bf16(x, terms):
    out = []
    r = x
    for _ in range(terms):
        h = r.astype(BF16)
        out.append(h)
        r = r - h.astype(F32)
    return out


def _sel_dot_left(sel, x, terms):
    acc = None
    for h in _split_bf16(x, terms):
        d = jnp.dot(sel, h, preferred_element_type=F32)
        acc = d if acc is None else acc + d
    return acc


def _rms_rows(x):
    return x * lax.rsqrt(jnp.mean(x * x, axis=-1, keepdims=True) + EPS)


def _exp_neg(x):
    return jnp.exp2(x * (-LOG2E))


def _silu(x):
    return x * (1.0 / (1.0 + _exp_neg(x)))


def _softplus(x):
    return jnp.maximum(x, 0.0) + jnp.log(1.0 + _exp_neg(jnp.abs(x)))


def _log_sigmoid(x):
    return jnp.minimum(x, 0.0) - jnp.log(1.0 + _exp_neg(jnp.abs(x)))


def _tril(n):
    r = lax.broadcasted_iota(jnp.int32, (n, n), 0)
    c = lax.broadcasted_iota(jnp.int32, (n, n), 1)
    return r >= c


def _head_mean_sq(x):
    out = []
    for p in range(x.shape[1] // LANES):
        sq = x[:, p * LANES:(p + 1) * LANES]
        sq = sq * sq
        low = lax.broadcasted_iota(jnp.int32, sq.shape, 1) < HD_B
        lo = jnp.sum(jnp.where(low, sq, 0.0), axis=-1, keepdims=True)
        hi = jnp.sum(jnp.where(low, 0.0, sq), axis=-1, keepdims=True)
        out.append(jnp.where(low, lo, hi))
    return jnp.concatenate(out, axis=1) * (1.0 / HD_B)


def _tile_kv_heads(kv):
    low = lax.broadcasted_iota(jnp.int32, kv.shape, 1) < HD_B
    swapped = pltpu.roll(kv, HD_B, 1)
    out = []
    for base in (jnp.where(low, kv, swapped), jnp.where(low, swapped, kv)):
        base = base.astype(BF16)
        out.append(jnp.concatenate([base, base], axis=1))
    return out


def _ab_dense_in(x, rms_g, w_in_ref, gq, gk, proj_ref, qn_ref):
    h = (_rms_rows(x) * rms_g).astype(BF16)
    rows = min(256, x.shape[0])
    for m in range(0, x.shape[0], rows):
        proj_ref[m:m + rows, :] = jnp.dot(h[m:m + rows], w_in_ref[...], preferred_element_type=F32)
    qb = proj_ref[:, B_Q:B_Q + 512]
    qn_ref[...] = qb * lax.rsqrt(_head_mean_sq(qb) + EPS) * gq
    kb = proj_ref[:, B_K:B_K + 128]
    kn = kb * lax.rsqrt(_head_mean_sq(kb) + EPS) * gk
    vb = proj_ref[:, B_V:B_V + 128]
    return kn, vb


def _advance(gens, yielded=None):
    alive = []
    for gen in gens:
        try:
            value = next(gen)
            alive.append(gen)
            if yielded is not None and value is not None:
                yielded.append(value)
        except StopIteration:
            pass
    return alive


def _interleave(gens):
    gens = list(gens)
    while gens:
        gens = _advance(gens)


def _ab_chunk(r, c, kb, proj_ref, qn_ref, o_ref, state, k_band, v_band, bias_ref, sink_ref,
              w_gu, b_gate, g_out, first_valid_col):
    rows = pl.ds(r, c)
    gate = _dot(proj_ref[rows, A_G:A_G + LANES], w_gu) + b_gate
    yield
    g = _log_sigmoid(gate) * (LOG2E / GATE_NORM_A)
    tril = _tril(c)
    b = _sel_dot_left(tril.astype(BF16), g, 3)
    bl = b[c - 1:c, :]
    bl_tile = jnp.broadcast_to(bl, (LANES, H_A * DK_A))
    bl_rows = jnp.concatenate([bl_tile[:, i * LANES:(i + 1) * LANES].T
                               for i in range(H_A * DK_A // LANES)], axis=0)
    yield
    q = proj_ref[rows, A_Q:A_Q + 256] * (DK_A ** -0.5)
    k = proj_ref[rows, A_K:A_K + 256]
    v = proj_ref[rows, A_V:A_V + 512].astype(BF16)
    qe = (q * jnp.exp2(b)).astype(BF16)
    kd = k * jnp.exp2(-b)
    kd2 = (k * jnp.exp2(bl - b)).astype(BF16)
    lane_head = lax.broadcasted_iota(jnp.int32, (c, LANES), 1) // DK_A
    row_head = lax.broadcasted_iota(jnp.int32, (LANES, LANES), 0) // DK_A
    att = []
    for p in range(2):
        kp = kd[:, p * 128:(p + 1) * 128]
        kpair = jnp.concatenate([jnp.where(lane_head == j, kp, 0.0) for j in range(2)], axis=0).astype(BF16)
        att.append(_dot_nt(qe[:, p * 128:(p + 1) * 128], kpair))
    upd = [_dot_tn(kd2[:, p * 128:(p + 1) * 128], v[:, p * 256:(p + 1) * 256]) for p in range(2)]
    yield
    row = lax.broadcasted_iota(jnp.int32, (c, 2 * c), 0)
    col = lax.broadcasted_iota(jnp.int32, (c, 2 * c), 1) % c
    att = [jnp.where(row >= col, a, 0.0).astype(BF16) for a in att]
    yield
    s_prev = state[0]
    s_new, oh = [], []
    zeros_v = jnp.zeros((c, DV_A), BF16)
    for p in range(2):
        sp = s_prev[p * 128:(p + 1) * 128, :]
        s_bd = jnp.concatenate([jnp.where(row_head == j, sp, 0.0) for j in range(2)], axis=1).astype(BF16)
        v0, v1 = v[:, 2 * p * DV_A:(2 * p + 1) * DV_A], v[:, (2 * p + 1) * DV_A:(2 * p + 2) * DV_A]
        v_bd = jnp.concatenate([jnp.concatenate([v0, zeros_v], axis=1),
                                jnp.concatenate([zeros_v, v1], axis=1)], axis=0)
        qp = qe[:, p * 128:(p + 1) * 128]
        if (2 * c) % LANES == 0:
            o_pair = jnp.dot(jnp.concatenate([att[p], qp], axis=1), jnp.concatenate([v_bd, s_bd], axis=0),
                             preferred_element_type=F32)
        else:
            o_pair = (jnp.dot(att[p], v_bd, preferred_element_type=F32)
                      + jnp.dot(qp, s_bd, preferred_element_type=F32))
        oh += [o_pair[:, :DV_A], o_pair[:, DV_A:]]
        u = jnp.where(row_head == 0, upd[p][:, :128], upd[p][:, 128:])
        s_new.append(jnp.exp2(bl_rows[p * 128:(p + 1) * 128, :]) * sp + u)
    state[0] = jnp.concatenate(s_new, axis=0)
    yield
    for hd in range(H_A):
        z = proj_ref[rows, A_Z + hd * 128:A_Z + (hd + 1) * 128]
        o_ref[rows, hd * 128:(hd + 1) * 128] = (_rms_rows(oh[hd]) * g_out * _silu(z)).astype(BF16)
    yield
    lane256 = lax.broadcasted_iota(jnp.int32, (c, 256), 1) // HD_B
    srow = lax.broadcasted_iota(jnp.int32, (G_B * c, 1), 0) // c
    scores = []
    for kvh in range(KVH_B):
        qn = qn_ref[rows, kvh * 256:(kvh + 1) * 256]
        qs = jnp.concatenate([jnp.where(lane256 == gq_, qn, 0.0) for gq_ in range(G_B)], axis=0)
        scores.append(_dot_nt(qs, k_band(kvh)))
    yield
    probs, dens = [], []
    for kvh in range(KVH_B):
        s = scores[kvh] * (LOG2E * HD_B ** -0.5) + bias_ref[kvh]
        if first_valid_col is not None:
            col = lax.broadcasted_iota(jnp.int32, (G_B * c, kb), 1)
            s = jnp.where(col >= first_valid_col, s, -jnp.inf)
        sink = jnp.zeros((G_B * c, 1), F32)
        for gq_ in range(G_B):
            sink = jnp.where(srow == gq_, sink_ref[kvh * G_B + gq_] * LOG2E, sink)
        m = jnp.maximum(jnp.max(s, axis=-1, keepdims=True), sink)
        pr = jnp.exp2(s - m)
        dens.append(jnp.sum(pr, axis=-1, keepdims=True) + jnp.exp2(sink - m))
        probs.append(pr.astype(BF16))
    yield
    outs = [_dot(probs[kvh], v_band(kvh)) for kvh in range(KVH_B)]
    yield
    for kvh in range(KVH_B):
        ost = outs[kvh] / dens[kvh]
        ob = jnp.zeros((c, 256), F32)
        for gq_ in range(G_B):
            ob = ob + jnp.where(lane256 == gq_, ost[gq_ * c:(gq_ + 1) * c, :], 0.0)
        z = proj_ref[rows, B_Z + kvh * 256:B_Z + (kvh + 1) * 256]
        o_ref[rows, 512 + kvh * 256:512 + (kvh + 1) * 256] = (ob * _silu(z)).astype(BF16)


def _ab_prompt_kernel(sink_ref, x_ref, rms_g_ref, w_in_ref, w_gu_ref, b_gate_ref, g_out_ref, gq_ref,
                      gk_ref, bias_ref, w_out_ref,
                      y_ref, s_out_ref, k_out_ref, v_out_ref,
                      proj_ref, qn_ref, o_ref, kband_ref, vband_ref, s_ref):
    t = x_ref.shape[0]
    c = CHUNK
    kb = WINDOW + c
    nchunk = t // c
    step = pl.program_id(0)

    @pl.when(step == 0)
    def _():
        s_ref[...] = jnp.zeros_like(s_ref)
        kband_ref[:, t:t + WINDOW, :] = jnp.zeros((KVH_B, WINDOW, 256), BF16)
        vband_ref[:, t:t + WINDOW, :] = jnp.zeros((KVH_B, WINDOW, 256), BF16)

    for kvh in range(KVH_B):
        kband_ref[kvh, 0:WINDOW, :] = kband_ref[kvh, t:t + WINDOW, :]
        vband_ref[kvh, 0:WINDOW, :] = vband_ref[kvh, t:t + WINDOW, :]

    x = x_ref[...]
    kn, vb = _ab_dense_in(x, rms_g_ref[...], w_in_ref, gq_ref[...], gk_ref[...], proj_ref, qn_ref)
    k_out_ref[...] = kn[t - WINDOW:, :]
    v_out_ref[...] = vb[t - WINDOW:, :]
    for kvh, (kt, vt) in enumerate(zip(_tile_kv_heads(kn), _tile_kv_heads(vb))):
        kband_ref[kvh, WINDOW:WINDOW + t, :] = kt
        vband_ref[kvh, WINDOW:WINDOW + t, :] = vt

    w_gu = w_gu_ref[...]
    b_gate = b_gate_ref[...]
    g_out = g_out_ref[...]

    state = [s_ref[...]]

    def chunk(i):
        r = i * c
        return _ab_chunk(r, c, kb, proj_ref, qn_ref, o_ref, state,
                         lambda kvh: kband_ref[kvh, pl.ds(r, kb), :],
                         lambda kvh: vband_ref[kvh, pl.ds(r, kb), :],
                         bias_ref, sink_ref, w_gu, b_gate, g_out, (2 - (step * nchunk + i)) * c)

    _interleave(chunk(i) for i in range(nchunk))
    s_ref[...] = state[0]
    y_ref[...] = x + jnp.dot(o_ref[...], w_out_ref[...], preferred_element_type=F32)
    s_out_ref[...] = s_ref[...]


def _ab_sample_kernel(sink_ref, x_ref, rms_g_ref, w_in_ref, w_gu_ref, b_gate_ref, g_out_ref, gq_ref,
                      gk_ref, bias_ref, w_out_ref,
                      s_in_ref, kc_ref, vc_ref,
                      y_ref, s_out_ref, k_out_ref, v_out_ref,
                      proj_ref, qn_ref, o_ref, kband_ref, vband_ref):
    nb, wc = kc_ref.shape[0], kc_ref.shape[1]
    t = x_ref.shape[0]
    c = t // nb
    kb = wc + c
    x = x_ref[...]
    kn, vb = _ab_dense_in(x, rms_g_ref[...], w_in_ref, gq_ref[...], gk_ref[...], proj_ref, qn_ref)
    k_out_ref[...] = kn
    v_out_ref[...] = vb
    k_new, v_new = _tile_kv_heads(kn), _tile_kv_heads(vb)
    for bi in range(nb):
        k_old, v_old = _tile_kv_heads(kc_ref[bi]), _tile_kv_heads(vc_ref[bi])
        for kvh in range(KVH_B):
            kband_ref[kvh, bi, 0:wc, :] = k_old[kvh]
            vband_ref[kvh, bi, 0:wc, :] = v_old[kvh]
            kband_ref[kvh, bi, wc:kb, :] = k_new[kvh][bi * c:(bi + 1) * c, :]
            vband_ref[kvh, bi, wc:kb, :] = v_new[kvh][bi * c:(bi + 1) * c, :]

    w_gu = w_gu_ref[...]
    b_gate = b_gate_ref[...]
    g_out = g_out_ref[...]

    states = [[s_in_ref[bi]] for bi in range(nb)]

    def seq(bi):
        return _ab_chunk(bi * c, c, kb, proj_ref, qn_ref, o_ref, states[bi],
                         lambda kvh: kband_ref[kvh, bi], lambda kvh: vband_ref[kvh, bi],
                         bias_ref, sink_ref, w_gu, b_gate, g_out, None)

    _interleave(seq(bi) for bi in range(nb))
    for bi in range(nb):
        s_out_ref[bi] = states[bi][0]
    y_ref[...] = x + jnp.dot(o_ref[...], w_out_ref[...], preferred_element_type=F32)


GROUP_W = D_INNER_C // G_C


def _c_chunk(r, c, io, state, e_s, e_p, neg_mask, bd_mask, dskip, g_y):
    dtc = io.dt(r)
    acum = _sel_dot_left(_tril(c).astype(BF16), io.da(r), 3) * LOG2E
    yield acum
    lhs = jnp.concatenate([jnp.concatenate(_split_bf16(acum, 2), axis=1),
                           jnp.concatenate(_split_bf16(dtc, 2), axis=1)], axis=0)
    both_p = jnp.dot(lhs, e_p, preferred_element_type=F32)
    xa_p, dt_p = both_p[:c], both_p[c:]
    xa_s = xa_p if c == P_C else jnp.dot(lhs[:c], e_s, preferred_element_type=F32)
    yield dt_p
    acum_t = acum.T
    al_p = xa_p[c - 1:c, :]
    xs = io.x(r)
    dec = jnp.exp2(al_p)
    gs = HPG_C * c
    bg, cg, cb, wmat, xdt_bf, xw = [], [], [], [], [], []
    for g in range(G_C):
        sl = slice(g * GROUP_W, (g + 1) * GROUP_W)
        a_row = jnp.concatenate([acum_t[h:h + 1, :] for h in range(g * HPG_C, (g + 1) * HPG_C)], axis=1)
        wmat.append(jnp.exp2((xa_s[:, g * gs:(g + 1) * gs] - a_row) + neg_mask[:, g * gs:(g + 1) * gs]))
        xdt = xs[:, sl] * dt_p[:, sl]
        xdt_bf.append(xdt.astype(BF16))
        xw.append((xdt * jnp.exp2(al_p[:, sl] - xa_p[:, sl])).astype(BF16))
        bg.append(io.b(r, g).astype(BF16))
        cg.append(io.c(r, g).astype(BF16))
        cb.append(_dot_nt(cg[g], jnp.concatenate([bg[g]] * HPG_C, axis=0)))
    yield cb[-1]
    ys, upd = [], []
    for g in range(G_C):
        mg = (cb[g] * wmat[g]).astype(BF16)
        for j in range(2):
            bd = jnp.concatenate([xdt_bf[g][:, j * 256:(j + 1) * 256]] * 4, axis=0) * bd_mask
            ys.append(jnp.dot(mg[:, j * 4 * c:(j + 1) * 4 * c], bd, preferred_element_type=F32))
        upd.append(_dot_tn(bg[g], xw[g]))
    yield upd[-1]
    st_prev = state[0]
    y_inter, st_new = [], []
    for g in range(G_C):
        sl = slice(g * GROUP_W, (g + 1) * GROUP_W)
        y_inter.append(_dot(cg[g], st_prev[:, sl]))
        st_new.append(st_prev[:, sl] * dec[:, sl] + upd[g])
    state[0] = jnp.concatenate(st_new, axis=1)
    yield st_new[-1]
    for g in range(G_C):
        sl = slice(g * GROUP_W, (g + 1) * GROUP_W)
        y = jnp.concatenate(ys[2 * g:2 * g + 2], axis=1) + y_inter[g] * jnp.exp2(xa_p[:, sl])
        y = y + dskip[:, sl] * xs[:, sl]
        y = y * io.gate(r, g)
        io.put_o(r, g, (_rms_rows(y) * g_y[:, sl]).astype(BF16))


def _c_dt(dt_cols, dt_bias, a_log):
    dt = _softplus(dt_cols + dt_bias)
    return dt, dt * (-jnp.exp(a_log))


CONV_PITCH = PROMPT_BLOCK // 8 + 1
CONV_ROWS = 8 * CONV_PITCH
CONV_TILES = CONV_DIM // LANES
C_ROUNDS = 12
C_IN_SLABS = tuple((C_X + 512 * k, C_X + 512 * (k + 1)) for k in range(6)) + tuple(
    (C_Z + 512 * k, C_Z + 512 * (k + 1)) for k in range(4)) + ((C_DT, C_COLS),)
C_MIX_FIRST_ROUND = 5
C_OUT_ROUNDS = (1, 3, 7, 11)
assert len(C_IN_SLABS) <= C_ROUNDS and sum(hi - lo for lo, hi in C_IN_SLABS) == C_COLS


def _conv_tile(ubuf_ref, act_ref, j, conv_w, conv_b):
    w = [jnp.broadcast_to(conv_w[i:i + 1, j * LANES:(j + 1) * LANES], (8, LANES)) for i in range(CONV_W)]
    b = jnp.broadcast_to(conv_b[:, j * LANES:(j + 1) * LANES], (8, LANES))
    for a in range(CONV_PITCH):
        acc = b
        for i in range(CONV_W):
            acc = acc + w[i] * ubuf_ref[j, pl.ds(8 - (CONV_W - 1) + i + a, 8, stride=CONV_PITCH), :]
        act_ref[j, pl.ds(a, 8, stride=CONV_PITCH), :] = _silu(acc)


def _c_prompt_kernel(xin_ref, xres_ref, rms_g_ref, w_in_ref, conv_w_ref, conv_b_ref, dt_bias_ref,
                     a_log_ref, dskip_ref, g_y_ref, w_out_ref, e_s_ref, neg_mask_ref, bd_mask_ref,
                     y_ref, st_out_ref, conv_out_ref,
                     z_ref, ubuf_ref, act_ref, dt_ref, da_ref, o_ref, st_ref):
    t = xin_ref.shape[0]
    c = CHUNK
    s = pl.program_id(0)
    n_blocks = pl.num_programs(0) - 2

    @pl.when(s == 0)
    def _():
        o_ref[1] = jnp.zeros(o_ref.shape[1:], BF16)
        ubuf_ref[...] = jnp.zeros_like(ubuf_ref)

    @pl.when(s == 0)
    def _():
        st_ref[...] = jnp.zeros_like(st_ref)

    def in_stage(slot_in):
        h = (_rms_rows(xin_ref[...]) * rms_g_ref[...]).astype(BF16)
        conv_w = conv_w_ref[...]
        conv_b = conv_b_ref[...]
        act = act_ref.at[slot_in]
        for rnd, (lo, hi) in enumerate(C_IN_SLABS):
            if rnd:
                yield
            slab = jnp.dot(h, w_in_ref[:, lo:hi], preferred_element_type=F32)
            if lo == C_DT:
                dt, da = _c_dt(slab, dt_bias_ref[...], a_log_ref[...])
                dt_ref[slot_in] = dt
                da_ref[slot_in] = da
            elif lo >= C_X:
                for j in range((lo - C_X) // LANES, (hi - C_X) // LANES):
                    col = C_X + j * LANES - lo
                    ubuf_ref[j, 8:16, :] = ubuf_ref[j, CONV_ROWS:CONV_ROWS + 8, :]
                    ubuf_ref[j, 16:16 + t, :] = slab[:, col:col + LANES]
                    _conv_tile(ubuf_ref, act, j, conv_w, conv_b)
            else:
                z_ref[slot_in, :, lo:hi] = _silu(slab)

    n_x = D_INNER_C // LANES
    n_g = N_C // LANES

    def mix_stage(slot_mix):
        def put_o(r, g, value):
            o_ref[slot_mix, pl.ds(r, c), g * GROUP_W:(g + 1) * GROUP_W] = value

        def act_rows(r, j):
            return act_ref[slot_mix, j, pl.ds(8 + r, c), :]

        io = types.SimpleNamespace(
            dt=lambda r: dt_ref[slot_mix, pl.ds(r, c), :],
            da=lambda r: da_ref[slot_mix, pl.ds(r, c), :],
            x=lambda r: jnp.concatenate([act_rows(r, j) for j in range(n_x)], axis=1),
            b=lambda r, g: act_rows(r, n_x + g * n_g),
            c=lambda r, g: act_rows(r, n_x + (G_C + g) * n_g),
            gate=lambda r, g: z_ref[slot_mix, pl.ds(r, c), C_Z + g * GROUP_W:C_Z + (g + 1) * GROUP_W],
            put_o=put_o)
        for _ in range(C_MIX_FIRST_ROUND):
            yield
        state = [st_ref[...]]
        e_s = e_s_ref[...]
        chunks = [_c_chunk(i * c, c, io, state, e_s, e_s, neg_mask_ref[...], bd_mask_ref[...],
                           dskip_ref[...], g_y_ref[...]) for i in range(t // c)]
        chunks = _advance(chunks)
        while chunks:
            yield
            chunks = _advance(chunks)
        st_ref[...] = state[0]

    def out_stage(slot_in):
        quarter = 0
        for rnd in range(C_ROUNDS):
            if rnd:
                yield
            if rnd in C_OUT_ROUNDS:
                cols = slice(quarter * 256, (quarter + 1) * 256)
                y_ref[:, cols] = xres_ref[:, cols] + jnp.dot(o_ref[slot_in], w_out_ref[:, cols],
                                                             preferred_element_type=F32)
                quarter += 1

    slot_in = s % 2

    @pl.when(s == 0)
    def _():
        _interleave([in_stage(slot_in)])

    @pl.when(jnp.logical_and(s > 0, s <= n_blocks))
    def _():
        _interleave([in_stage(slot_in), mix_stage(1 - slot_in), out_stage(slot_in)])

    @pl.when(s == n_blocks + 1)
    def _():
        _interleave([out_stage(slot_in)])

    @pl.when(s == n_blocks)
    def _():
        st_out_ref[...] = st_ref[...].T
        for j in range(CONV_TILES):
            conv_out_ref[:, j * LANES:(j + 1) * LANES] = ubuf_ref[j, CONV_ROWS:CONV_ROWS + 8, :]


def _conv_rows(ubuf, nrows, conv_w, conv_b):
    acc = conv_b
    for i in range(CONV_W):
        acc = acc + conv_w[i:i + 1, :] * ubuf[pl.ds(8 - (CONV_W - 1) + i, nrows), :]
    return _silu(acc)


def _c_sample_kernel(x_ref, rms_g_ref, w_in_ref, conv_w_ref, conv_b_ref, dt_bias_ref, a_log_ref,
                     dskip_ref, g_y_ref, w_out_ref, e_s_ref, e_p_ref, neg_mask_ref, bd_mask_ref,
                     st_in_ref, conv_in_ref,
                     y_ref, st_out_ref, conv_out_ref,
                     proj_ref, ubuf_ref, act_ref, dt_ref, da_ref, o_ref):
    nb = st_in_ref.shape[0]
    t = x_ref.shape[0]
    c = t // nb
    x = x_ref[...]
    h = (_rms_rows(x) * rms_g_ref[...]).astype(BF16)
    proj_ref[...] = jnp.dot(h, w_in_ref[...], preferred_element_type=F32)
    dt, da = _c_dt(proj_ref[:, C_DT:C_DT + LANES], dt_bias_ref[...], a_log_ref[...])
    dt_ref[...] = dt
    da_ref[...] = da
    conv_w = conv_w_ref[...]
    conv_b = conv_b_ref[...]
    ubuf_ref[0:8, :] = jnp.zeros((8, CONV_DIM), F32)
    for bi in range(nb):
        ubuf_ref[8 - (CONV_W - 1):8, :] = conv_in_ref[bi]
        ubuf_ref[8:8 + c, :] = proj_ref[bi * c:(bi + 1) * c, C_X:C_X + CONV_DIM]
        act_ref[bi * c:(bi + 1) * c, :] = _conv_rows(ubuf_ref, c, conv_w, conv_b)
        conv_out_ref[bi] = ubuf_ref[8 + c - (CONV_W - 1):8 + c, :]

    def put_o(r, g, value):
        o_ref[pl.ds(r, c), g * GROUP_W:(g + 1) * GROUP_W] = value

    io = types.SimpleNamespace(
        dt=lambda r: dt_ref[pl.ds(r, c), :],
        da=lambda r: da_ref[pl.ds(r, c), :],
        x=lambda r: act_ref[pl.ds(r, c), 0:D_INNER_C],
        b=lambda r, g: act_ref[pl.ds(r, c), D_INNER_C + g * N_C:D_INNER_C + (g + 1) * N_C],
        c=lambda r, g: act_ref[pl.ds(r, c), D_INNER_C + (G_C + g) * N_C:D_INNER_C + (G_C + g + 1) * N_C],
        gate=lambda r, g: _silu(proj_ref[pl.ds(r, c), C_Z + g * GROUP_W:C_Z + (g + 1) * GROUP_W]),
        put_o=put_o)
    e_s = e_s_ref[...]
    e_p = e_p_ref[...]
    neg_mask = neg_mask_ref[...]
    bd_mask = bd_mask_ref[...]
    dskip = dskip_ref[...]
    g_y = g_y_ref[...]

    states = [[st_in_ref[bi].T] for bi in range(nb)]
    _interleave(_c_chunk(bi * c, c, io, states[bi], e_s, e_p, neg_mask, bd_mask, dskip, g_y)
                for bi in range(nb))
    for bi in range(nb):
        st_out_ref[bi] = states[bi][0].T
    y_ref[...] = x + jnp.dot(o_ref[...], w_out_ref[...], preferred_element_type=F32)


def _bucket_table(q_off, k_off):
    n = q_off[:, None] - k_off[None, :]
    half = N_BUCKETS // 2
    max_exact = half // 2
    side = np.where(n < 0, half, 0)
    n = np.abs(n)
    nf = np.maximum(n, max_exact).astype(np.float32)
    large = max_exact + (np.log(nf / np.float32(max_exact)) / np.float32(math.log(MAX_DISTANCE / max_exact))
                         * np.float32(half - max_exact)).astype(np.int32)
    large = np.minimum(large, half - 1)
    return side + np.where(n < max_exact, n, large)


def _bias_rows(rel_bias, q_off, k_off):
    bucket = _bucket_table(q_off, k_off)
    onehot = jnp.asarray(np.eye(N_BUCKETS, dtype=np.float32)[bucket])
    bias = jnp.einsum('qkb,bh->hqk', onehot, rel_bias.astype(F32), precision=lax.Precision.HIGHEST) * LOG2E
    return bias.reshape(KVH_B, G_B * q_off.shape[0], k_off.shape[0])


def _head_expand_matrix(per_head):
    m = np.zeros((LANES, H_C * per_head), np.float32)
    for h in range(H_C):
        m[h, h * per_head:(h + 1) * per_head] = 1.0
    return jnp.asarray(np.concatenate([m, m], axis=0), BF16)


def _full(shape):
    return pl.BlockSpec(shape, lambda *_: (0,) * len(shape))


def _const(shape):
    return pl.BlockSpec(shape, lambda *_: (0,) * len(shape), pipeline_mode=pl.Buffered(1))


CAST_ROWS = 256


def _cast_columns(wt_ref, w_bf_ref, src, dst, n):
    for off in range(0, n - n % LANES, CAST_ROWS):
        rows = min(CAST_ROWS, n - n % LANES - off)
        w_bf_ref[:, dst + off:dst + off + rows] = wt_ref[src + off:src + off + rows, :].T.astype(BF16)
    rest = n % LANES
    if rest:
        off = n - rest
        tail = jnp.concatenate([wt_ref[src + off:src + n, :], jnp.zeros((LANES - rest, wt_ref.shape[1]), F32)],
                               axis=0)
        w_bf_ref[:, dst + off:dst + off + LANES] = tail.T.astype(BF16)


def _ab_cast_kernel(wt_ref, w_out_ref, w_in_bf_ref, w_out_bf_ref):
    g0 = A_Z
    _cast_columns(wt_ref, w_in_bf_ref, 0, 0, g0)
    _cast_columns(wt_ref, w_in_bf_ref, g0 + GATE_RANK_A, g0, A_G - g0)
    _cast_columns(wt_ref, w_in_bf_ref, g0, A_G, GATE_RANK_A)
    w_out_bf_ref[...] = w_out_ref[...].astype(BF16)


def _c_cast_kernel(wt_ref, w_out_ref, w_in_bf_ref, w_out_bf_ref):
    _cast_columns(wt_ref, w_in_bf_ref, 0, 0, wt_ref.shape[0])
    w_out_bf_ref[...] = w_out_ref[...].astype(BF16)


def _cast_weights(body, w_in, w_out, cols, name):
    vm = pl.BlockSpec(memory_space=pltpu.VMEM)
    return pl.pallas_call(
        body, in_specs=[vm, vm], out_specs=(vm, vm),
        out_shape=(jax.ShapeDtypeStruct((w_in.shape[0], cols), BF16),
                   jax.ShapeDtypeStruct(w_out.shape, BF16)),
        name=name, compiler_params=pltpu.CompilerParams(vmem_limit_bytes=VMEM_LIMIT),
    )(w_in.T, w_out)


def _ab_weights(rms_g, w_in, w_gate_up, b_gate, g_out, g_q, g_k, w_out):
    w_in_r, w_out_bf = _cast_weights(_ab_cast_kernel, w_in, w_out, AB_COLS, "ab_cast")
    w_gu = jnp.concatenate([w_gate_up, jnp.zeros((LANES - GATE_RANK_A, H_A * DK_A), w_gate_up.dtype)],
                           axis=0).astype(BF16)
    return (rms_g.reshape(1, D_MODEL), w_in_r, w_gu, b_gate.reshape(1, -1), g_out.reshape(1, DV_A),
            jnp.tile(g_q, H_B).reshape(1, -1), jnp.tile(g_k, KVH_B).reshape(1, -1), w_out_bf)


def _ab_prompt(x, sink, rel_bias, weights):
    rms_g, w_in, w_gu, b_gate, g_out, gq, gk, w_out = weights
    length = x.shape[0]
    t = AB_PROMPT_BLOCK
    c = CHUNK
    kb = WINDOW + c
    bias = _bias_rows(rel_bias, np.arange(c), np.arange(kb) - WINDOW)
    row_blk = lambda i: (i, 0)
    in_specs = [
        pl.BlockSpec(memory_space=pltpu.SMEM),
        pl.BlockSpec((t, D_MODEL), row_blk),
        _const((1, D_MODEL)), _const((D_MODEL, AB_COLS)), _const((LANES, 256)), _const((1, 256)),
        _const((1, DV_A)), _const((1, 512)), _const((1, 128)), _const((KVH_B, G_B * c, kb)),
        _const((D_MODEL, D_MODEL)),
    ]
    out_shape = (jax.ShapeDtypeStruct((length, D_MODEL), F32),
                 jax.ShapeDtypeStruct((H_A * DK_A, DV_A), F32),
                 jax.ShapeDtypeStruct((WINDOW, KVH_B * HD_B), F32),
                 jax.ShapeDtypeStruct((WINDOW, KVH_B * HD_B), F32))
    out_specs = (pl.BlockSpec((t, D_MODEL), row_blk), _full((H_A * DK_A, DV_A)),
                 _full((WINDOW, KVH_B * HD_B)), _full((WINDOW, KVH_B * HD_B)))
    scratch = [pltpu.VMEM((t, AB_COLS), F32), pltpu.VMEM((t, 512), F32), pltpu.VMEM((t, D_MODEL), BF16),
               pltpu.VMEM((KVH_B, t + WINDOW, 256), BF16), pltpu.VMEM((KVH_B, t + WINDOW, 256), BF16),
               pltpu.VMEM((H_A * DK_A, DV_A), F32)]
    return pl.pallas_call(
        _ab_prompt_kernel, grid=(length // t,), in_specs=in_specs, out_specs=out_specs,
        out_shape=out_shape, scratch_shapes=scratch, name="ab_prompt",
        compiler_params=pltpu.CompilerParams(dimension_semantics=("arbitrary",),
                                             vmem_limit_bytes=VMEM_LIMIT),
    )(sink, x, rms_g, w_in, w_gu, b_gate, g_out, gq, gk, bias, w_out)


def _ab_sample(x, sink, rel_bias, weights, s_in, k_cache, v_cache):
    rms_g, w_in, w_gu, b_gate, g_out, gq, gk, w_out = weights
    nb, c, _ = x.shape
    wc = k_cache.shape[1]
    kb = wc + c
    t = nb * c
    bias = _bias_rows(rel_bias, np.arange(c), np.arange(kb) - wc)
    vm = pl.BlockSpec(memory_space=pltpu.VMEM)
    in_specs = [pl.BlockSpec(memory_space=pltpu.SMEM)] + [vm] * 13
    out_shape = (jax.ShapeDtypeStruct((t, D_MODEL), F32),
                 jax.ShapeDtypeStruct((nb, H_A * DK_A, DV_A), F32),
                 jax.ShapeDtypeStruct((t, KVH_B * HD_B), F32),
                 jax.ShapeDtypeStruct((t, KVH_B * HD_B), F32))
    scratch = [pltpu.VMEM((t, AB_COLS), F32), pltpu.VMEM((t, 512), F32), pltpu.VMEM((t, D_MODEL), BF16),
               pltpu.VMEM((KVH_B, nb, kb, 256), BF16), pltpu.VMEM((KVH_B, nb, kb, 256), BF16)]
    return pl.pallas_call(
        _ab_sample_kernel, in_specs=in_specs, out_specs=(vm,) * 4, out_shape=out_shape,
        scratch_shapes=scratch, name="ab_sample",
        compiler_params=pltpu.CompilerParams(vmem_limit_bytes=VMEM_LIMIT),
    )(sink, x.reshape(t, D_MODEL), rms_g, w_in, w_gu, b_gate, g_out, gq, gk, bias, w_out,
      s_in.reshape(nb, H_A * DK_A, DV_A), k_cache.reshape(nb, wc, KVH_B * HD_B),
      v_cache.reshape(nb, wc, KVH_B * HD_B))


def _c_weights(rms_g, w_in, conv_w, conv_b, dt_bias, a_log, d_skip, g_y, w_out):
    w_in_p, w_out_bf = _cast_weights(_c_cast_kernel, w_in, w_out, C_COLS, "c_cast")
    pad = lambda v: jnp.concatenate([v, jnp.zeros((LANES - H_C,), v.dtype)]).reshape(1, LANES)
    return (rms_g.reshape(1, D_MODEL), w_in_p, conv_w, conv_b.reshape(1, CONV_DIM), pad(dt_bias), pad(a_log),
            jnp.repeat(d_skip, P_C).reshape(1, D_INNER_C), g_y.reshape(1, D_INNER_C), w_out_bf)


def _ssd_masks(c):
    tok = np.arange(c)[:, None]
    src = np.arange(H_C * c)[None, :] % c
    neg = np.where(src <= tok, 0.0, -np.inf).astype(np.float32)
    blk = (np.arange(4 * c)[:, None] // c) == (np.arange(256)[None, :] // P_C)
    return jnp.asarray(neg), jnp.asarray(blk, BF16)


def _c_prompt(x, weights):
    length = x.shape[0]
    t = PROMPT_BLOCK
    c = CHUNK
    n_blocks = length // t
    in_blk = lambda s: (jnp.minimum(s, n_blocks - 1), 0)
    out_blk = lambda s: (jnp.clip(s - 2, 0, n_blocks - 1), 0)
    in_specs = [
        pl.BlockSpec((t, D_MODEL), in_blk), pl.BlockSpec((t, D_MODEL), out_blk),
        _const((1, D_MODEL)), _const((D_MODEL, C_COLS)), _const((CONV_W, CONV_DIM)), _const((1, CONV_DIM)),
        _const((1, LANES)), _const((1, LANES)), _const((1, D_INNER_C)), _const((1, D_INNER_C)),
        _const((D_INNER_C, D_MODEL)), _const((2 * LANES, H_C * c)),
        _const((c, H_C * c)), _const((4 * c, 256)),
    ]
    out_shape = (jax.ShapeDtypeStruct((length, D_MODEL), F32),
                 jax.ShapeDtypeStruct((H_C * P_C, N_C), F32),
                 jax.ShapeDtypeStruct((8, CONV_DIM), F32))
    out_specs = (pl.BlockSpec((t, D_MODEL), out_blk), _full((H_C * P_C, N_C)), _full((8, CONV_DIM)))
    scratch = [pltpu.VMEM((2, t, D_INNER_C), F32), pltpu.VMEM((CONV_TILES, CONV_ROWS + 8, LANES), F32),
               pltpu.VMEM((2, CONV_TILES, CONV_ROWS, LANES), F32), pltpu.VMEM((2, t, LANES), F32),
               pltpu.VMEM((2, t, LANES), F32), pltpu.VMEM((2, t, D_INNER_C), BF16),
               pltpu.VMEM((N_C, H_C * P_C), F32)]
    return pl.pallas_call(
        _c_prompt_kernel, grid=(n_blocks + 2,), in_specs=in_specs, out_specs=out_specs,
        out_shape=out_shape, scratch_shapes=scratch, name="c_prompt",
        compiler_params=pltpu.CompilerParams(dimension_semantics=("arbitrary",),
                                             vmem_limit_bytes=VMEM_LIMIT),
    )(x, x, *weights, _head_expand_matrix(c), *_ssd_masks(c))


def _c_sample(x, weights, st_in, conv_in):
    nb, c, _ = x.shape
    t = nb * c
    vm = pl.BlockSpec(memory_space=pltpu.VMEM)
    out_shape = (jax.ShapeDtypeStruct((t, D_MODEL), F32),
                 jax.ShapeDtypeStruct((nb, H_C * P_C, N_C), F32),
                 jax.ShapeDtypeStruct((nb, CONV_W - 1, CONV_DIM), F32))
    scratch = [pltpu.VMEM((t, C_COLS), F32), pltpu.VMEM((c + 8, CONV_DIM), F32),
               pltpu.VMEM((t, CONV_DIM), F32), pltpu.VMEM((t, LANES), F32), pltpu.VMEM((t, LANES), F32),
               pltpu.VMEM((t, D_INNER_C), BF16)]
    return pl.pallas_call(
        _c_sample_kernel, in_specs=[vm] * 16, out_specs=(vm,) * 3, out_shape=out_shape,
        scratch_shapes=scratch, name="c_sample",
        compiler_params=pltpu.CompilerParams(vmem_limit_bytes=VMEM_LIMIT),
    )(x.reshape(t, D_MODEL), *weights, _head_expand_matrix(c), _head_expand_matrix(P_C), *_ssd_masks(c),
      st_in.reshape(nb, H_C * P_C, N_C), conv_in)


def kernel(x_prompt, x_sample, cache_swa_k, cache_swa_v, state_gla, state_ssd, state_conv, rms_g, w_in_ab, w_gate_up_a, b_gate_a, g_out_a, g_q_b, g_k_b, sink_b, rel_bias, w_out_ab, w_in_c, conv_w_c, conv_b_c, dt_bias_c, a_log_c, d_skip_c, g_y_c, w_out_c):
    bp, seq_len, _ = x_prompt.shape
    nb, dec_len, _ = x_sample.shape
    assert bp == 1 and seq_len % PROMPT_BLOCK == 0 and seq_len % AB_PROMPT_BLOCK == 0 and seq_len >= WINDOW
    wab = _ab_weights(rms_g[0], w_in_ab[0], w_gate_up_a[0], b_gate_a[0], g_out_a[0], g_q_b[0], g_k_b[0],
                      w_out_ab[0])
    yp, gla_p, k_p, v_p = _ab_prompt(x_prompt[0], sink_b[0], rel_bias, wab)
    ys, gla_s, k_s, v_s = _ab_sample(x_sample, sink_b[0], rel_bias, wab, state_gla[0], cache_swa_k[0],
                                     cache_swa_v[0])
    wc = _c_weights(rms_g[1], w_in_c[0], conv_w_c[0], conv_b_c[0], dt_bias_c[0], a_log_c[0], d_skip_c[0],
                    g_y_c[0], w_out_c[0])
    yp, ssd_p, conv_p = _c_prompt(yp, wc)
    ys, ssd_s, conv_s = _c_sample(ys.reshape(nb, dec_len, D_MODEL), wc, state_ssd[0], state_conv[0])
    return (
        yp.reshape(1, seq_len, D_MODEL),
        ys.reshape(nb, dec_len, D_MODEL),
        gla_p.reshape(1, 1, H_A, DK_A, DV_A),
        k_p.reshape(1, 1, WINDOW, KVH_B, HD_B),
        v_p.reshape(1, 1, WINDOW, KVH_B, HD_B),
        ssd_p.reshape(1, 1, H_C, P_C, N_C),
        conv_p[8 - (CONV_W - 1):].reshape(1, 1, CONV_W - 1, CONV_DIM),
        gla_s.reshape(1, nb, H_A, DK_A, DV_A),
        k_s.reshape(1, nb, dec_len, KVH_B, HD_B),
        v_s.reshape(1, nb, dec_len, KVH_B, HD_B),
        ssd_s.reshape(1, nb, H_C, P_C, N_C),
        conv_s.reshape(1, nb, CONV_W - 1, CONV_DIM),
    )
```

```python
import math
import types

import numpy as np
import jax
import jax.numpy as jnp
from jax import lax
from jax.experimental import pallas as pl
from jax.experimental.pallas import tpu as pltpu

F32 = jnp.float32
BF16 = jnp.bfloat16

D_MODEL = 1024
CHUNK = 64
EPS = 1e-6
H_A = 4
DK_A = 64
DV_A = 128
GATE_RANK_A = 16
GATE_NORM_A = 16.0
H_B = 8
KVH_B = 2
G_B = H_B // KVH_B
HD_B = 64
WINDOW = 128
N_BUCKETS = 32
MAX_DISTANCE = 128
D_INNER_C = 2048
P_C = 64
H_C = 32
G_C = 4
HPG_C = 8
N_C = 128
CONV_W = 4
CONV_DIM = D_INNER_C + 2 * G_C * N_C

LOG2E = 1.4426950408889634
LANES = 128

A_Q, A_K, A_V, A_Z = 0, 256, 512, 1024
B_Q, B_K, B_V, B_Z = 1536, 2048, 2176, 2304
A_G = 2816
AB_COLS = 2944
C_Z, C_X, C_B, C_C, C_DT = 0, 2048, 4096, 4608, 5120
C_COLS = 5248

PROMPT_BLOCK = 256
AB_PROMPT_BLOCK = 512
VMEM_LIMIT = 60 * 1024 * 1024


def _dot(a, b):
    return jnp.dot(a.astype(BF16), b.astype(BF16), preferred_element_type=F32)


def _dot_nt(a, b):
    return lax.dot_general(a.astype(BF16), b.astype(BF16), (((1,), (1,)), ((), ())),
                           preferred_element_type=F32)


def _dot_tn(a, b):
    return lax.dot_general(a.astype(BF16), b.astype(BF16), (((0,), (0,)), ((), ())),
                           preferred_element_type=F32)


def _split_bf16(x, terms):
    out = []
    r = x
    for _ in range(terms):
        h = r.astype(BF16)
        out.append(h)
        r = r - h.astype(F32)
    return out


def _sel_dot_left(sel, x, terms):
    acc = None
    for h in _split_bf16(x, terms):
        d = jnp.dot(sel, h, preferred_element_type=F32)
        acc = d if acc is None else acc + d
    return acc


def _rms_rows(x):
    return x * lax.rsqrt(jnp.mean(x * x, axis=-1, keepdims=True) + EPS)


def _exp_neg(x):
    return jnp.exp2(x * (-LOG2E))


def _silu(x):
    return x * (1.0 / (1.0 + _exp_neg(x)))


def _softplus(x):
    return jnp.maximum(x, 0.0) + jnp.log(1.0 + _exp_neg(jnp.abs(x)))


def _log_sigmoid(x):
    return jnp.minimum(x, 0.0) - jnp.log(1.0 + _exp_neg(jnp.abs(x)))


def _tril(n):
    r = lax.broadcasted_iota(jnp.int32, (n, n), 0)
    c = lax.broadcasted_iota(jnp.int32, (n, n), 1)
    return r >= c


def _head_mean_sq(x):
    out = []
    for p in range(x.shape[1] // LANES):
        sq = x[:, p * LANES:(p + 1) * LANES]
        sq = sq * sq
        low = lax.broadcasted_iota(jnp.int32, sq.shape, 1) < HD_B
        lo = jnp.sum(jnp.where(low, sq, 0.0), axis=-1, keepdims=True)
        hi = jnp.sum(jnp.where(low, 0.0, sq), axis=-1, keepdims=True)
        out.append(jnp.where(low, lo, hi))
    return jnp.concatenate(out, axis=1) * (1.0 / HD_B)


def _tile_kv_heads(kv):
    low = lax.broadcasted_iota(jnp.int32, kv.shape, 1) < HD_B
    swapped = pltpu.roll(kv, HD_B, 1)
    out = []
    for base in (jnp.where(low, kv, swapped), jnp.where(low, swapped, kv)):
        base = base.astype(BF16)
        out.append(jnp.concatenate([base, base], axis=1))
    return out


def _ab_dense_in(x, rms_g, w_in_ref, gq, gk, proj_ref, qn_ref):
    h = (_rms_rows(x) * rms_g).astype(BF16)
    rows = min(256, x.shape[0])
    for m in range(0, x.shape[0], rows):
        proj_ref[m:m + rows, :] = jnp.dot(h[m:m + rows], w_in_ref[...], preferred_element_type=F32)
    qb = proj_ref[:, B_Q:B_Q + 512]
    qn_ref[...] = qb * lax.rsqrt(_head_mean_sq(qb) + EPS) * gq
    kb = proj_ref[:, B_K:B_K + 128]
    kn = kb * lax.rsqrt(_head_mean_sq(kb) + EPS) * gk
    vb = proj_ref[:, B_V:B_V + 128]
    return kn, vb


def _advance(gens, yielded=None):
    alive = []
    for gen in gens:
        try:
            value = next(gen)
            alive.append(gen)
            if yielded is not None and value is not None:
                yielded.append(value)
        except StopIteration:
            pass
    return alive


def _interleave(gens):
    gens = list(gens)
    while gens:
        gens = _advance(gens)


def _ab_chunk(r, c, kb, proj_ref, qn_ref, o_ref, state, k_band, v_band, bias_ref, sink_ref,
              w_gu, b_gate, g_out, first_valid_col):
    rows = pl.ds(r, c)
    gate = _dot(proj_ref[rows, A_G:A_G + LANES], w_gu) + b_gate
    yield
    g = _log_sigmoid(gate) * (LOG2E / GATE_NORM_A)
    tril = _tril(c)
    b = _sel_dot_left(tril.astype(BF16), g, 3)
    bl = b[c - 1:c, :]
    bl_tile = jnp.broadcast_to(bl, (LANES, H_A * DK_A))
    bl_rows = jnp.concatenate([bl_tile[:, i * LANES:(i + 1) * LANES].T
                               for i in range(H_A * DK_A // LANES)], axis=0)
    yield
    q = proj_ref[rows, A_Q:A_Q + 256] * (DK_A ** -0.5)
    k = proj_ref[rows, A_K:A_K + 256]
    v = proj_ref[rows, A_V:A_V + 512].astype(BF16)
    qe = (q * jnp.exp2(b)).astype(BF16)
    kd = k * jnp.exp2(-b)
    kd2 = (k * jnp.exp2(bl - b)).astype(BF16)
    lane_head = lax.broadcasted_iota(jnp.int32, (c, LANES), 1) // DK_A
    row_head = lax.broadcasted_iota(jnp.int32, (LANES, LANES), 0) // DK_A
    att = []
    for p in range(2):
        kp = kd[:, p * 128:(p + 1) * 128]
        kpair = jnp.concatenate([jnp.where(lane_head == j, kp, 0.0) for j in range(2)], axis=0).astype(BF16)
        att.append(_dot_nt(qe[:, p * 128:(p + 1) * 128], kpair))
    upd = [_dot_tn(kd2[:, p * 128:(p + 1) * 128], v[:, p * 256:(p + 1) * 256]) for p in range(2)]
    yield
    row = lax.broadcasted_iota(jnp.int32, (c, 2 * c), 0)
    col = lax.broadcasted_iota(jnp.int32, (c, 2 * c), 1) % c
    att = [jnp.where(row >= col, a, 0.0).astype(BF16) for a in att]
    yield
    s_prev = state[0]
    s_new, oh = [], []
    zeros_v = jnp.zeros((c, DV_A), BF16)
    for p in range(2):
        sp = s_prev[p * 128:(p + 1) * 128, :]
        s_bd = jnp.concatenate([jnp.where(row_head == j, sp, 0.0) for j in range(2)], axis=1).astype(BF16)
        v0, v1 = v[:, 2 * p * DV_A:(2 * p + 1) * DV_A], v[:, (2 * p + 1) * DV_A:(2 * p + 2) * DV_A]
        v_bd = jnp.concatenate([jnp.concatenate([v0, zeros_v], axis=1),
                                jnp.concatenate([zeros_v, v1], axis=1)], axis=0)
        qp = qe[:, p * 128:(p + 1) * 128]
        if (2 * c) % LANES == 0:
            o_pair = jnp.dot(jnp.concatenate([att[p], qp], axis=1), jnp.concatenate([v_bd, s_bd], axis=0),
                             preferred_element_type=F32)
        else:
            o_pair = (jnp.dot(att[p], v_bd, preferred_element_type=F32)
                      + jnp.dot(qp, s_bd, preferred_element_type=F32))
        oh += [o_pair[:, :DV_A], o_pair[:, DV_A:]]
        u = jnp.where(row_head == 0, upd[p][:, :128], upd[p][:, 128:])
        s_new.append(jnp.exp2(bl_rows[p * 128:(p + 1) * 128, :]) * sp + u)
    state[0] = jnp.concatenate(s_new, axis=0)
    yield
    for hd in range(H_A):
        z = proj_ref[rows, A_Z + hd * 128:A_Z + (hd + 1) * 128]
        o_ref[rows, hd * 128:(hd + 1) * 128] = (_rms_rows(oh[hd]) * g_out * _silu(z)).astype(BF16)
    yield
    lane256 = lax.broadcasted_iota(jnp.int32, (c, 256), 1) // HD_B
    srow = lax.broadcasted_iota(jnp.int32, (G_B * c, 1), 0) // c
    scores = []
    for kvh in range(KVH_B):
        qn = qn_ref[rows, kvh * 256:(kvh + 1) * 256]
        qs = jnp.concatenate([jnp.where(lane256 == gq_, qn, 0.0) for gq_ in range(G_B)], axis=0)
        scores.append(_dot_nt(qs, k_band(kvh)))
    yield
    probs, dens = [], []
    for kvh in range(KVH_B):
        s = scores[kvh] * (LOG2E * HD_B ** -0.5) + bias_ref[kvh]
        if first_valid_col is not None:
            col = lax.broadcasted_iota(jnp.int32, (G_B * c, kb), 1)
            s = jnp.where(col >= first_valid_col, s, -jnp.inf)
        sink = jnp.zeros((G_B * c, 1), F32)
        for gq_ in range(G_B):
            sink = jnp.where(srow == gq_, sink_ref[kvh * G_B + gq_] * LOG2E, sink)
        m = jnp.maximum(jnp.max(s, axis=-1, keepdims=True), sink)
        pr = jnp.exp2(s - m)
        dens.append(jnp.sum(pr, axis=-1, keepdims=True) + jnp.exp2(sink - m))
        probs.append(pr.astype(BF16))
    yield
    outs = [_dot(probs[kvh], v_band(kvh)) for kvh in range(KVH_B)]
    yield
    for kvh in range(KVH_B):
        ost = outs[kvh] / dens[kvh]
        ob = jnp.zeros((c, 256), F32)
        for gq_ in range(G_B):
            ob = ob + jnp.where(lane256 == gq_, ost[gq_ * c:(gq_ + 1) * c, :], 0.0)
        z = proj_ref[rows, B_Z + kvh * 256:B_Z + (kvh + 1) * 256]
        o_ref[rows, 512 + kvh * 256:512 + (kvh + 1) * 256] = (ob * _silu(z)).astype(BF16)


def _ab_prompt_kernel(sink_ref, x_ref, rms_g_ref, w_in_ref, w_gu_ref, b_gate_ref, g_out_ref, gq_ref,
                      gk_ref, bias_ref, w_out_ref,
                      y_ref, s_out_ref, k_out_ref, v_out_ref,
                      proj_ref, qn_ref, o_ref, kband_ref, vband_ref, s_ref):
    t = x_ref.shape[0]
    c = CHUNK
    kb = WINDOW + c
    nchunk = t // c
    step = pl.program_id(0)

    @pl.when(step == 0)
    def _():
        s_ref[...] = jnp.zeros_like(s_ref)
        kband_ref[:, t:t + WINDOW, :] = jnp.zeros((KVH_B, WINDOW, 256), BF16)
        vband_ref[:, t:t + WINDOW, :] = jnp.zeros((KVH_B, WINDOW, 256), BF16)

    for kvh in range(KVH_B):
        kband_ref[kvh, 0:WINDOW, :] = kband_ref[kvh, t:t + WINDOW, :]
        vband_ref[kvh, 0:WINDOW, :] = vband_ref[kvh, t:t + WINDOW, :]

    x = x_ref[...]
    kn, vb = _ab_dense_in(x, rms_g_ref[...], w_in_ref, gq_ref[...], gk_ref[...], proj_ref, qn_ref)
    k_out_ref[...] = kn[t - WINDOW:, :]
    v_out_ref[...] = vb[t - WINDOW:, :]
    for kvh, (kt, vt) in enumerate(zip(_tile_kv_heads(kn), _tile_kv_heads(vb))):
        kband_ref[kvh, WINDOW:WINDOW + t, :] = kt
        vband_ref[kvh, WINDOW:WINDOW + t, :] = vt

    w_gu = w_gu_ref[...]
    b_gate = b_gate_ref[...]
    g_out = g_out_ref[...]

    state = [s_ref[...]]

    def chunk(i):
        r = i * c
        return _ab_chunk(r, c, kb, proj_ref, qn_ref, o_ref, state,
                         lambda kvh: kband_ref[kvh, pl.ds(r, kb), :],
                         lambda kvh: vband_ref[kvh, pl.ds(r, kb), :],
                         bias_ref, sink_ref, w_gu, b_gate, g_out, (2 - (step * nchunk + i)) * c)

    _interleave(chunk(i) for i in range(nchunk))
    s_ref[...] = state[0]
    y_ref[...] = x + jnp.dot(o_ref[...], w_out_ref[...], preferred_element_type=F32)
    s_out_ref[...] = s_ref[...]


def _ab_sample_kernel(sink_ref, x_ref, rms_g_ref, w_in_ref, w_gu_ref, b_gate_ref, g_out_ref, gq_ref,
                      gk_ref, bias_ref, w_out_ref,
                      s_in_ref, kc_ref, vc_ref,
                      y_ref, s_out_ref, k_out_ref, v_out_ref,
                      proj_ref, qn_ref, o_ref, kband_ref, vband_ref):
    nb, wc = kc_ref.shape[0], kc_ref.shape[1]
    t = x_ref.shape[0]
    c = t // nb
    kb = wc + c
    x = x_ref[...]
    kn, vb = _ab_dense_in(x, rms_g_ref[...], w_in_ref, gq_ref[...], gk_ref[...], proj_ref, qn_ref)
    k_out_ref[...] = kn
    v_out_ref[...] = vb
    k_new, v_new = _tile_kv_heads(kn), _tile_kv_heads(vb)
    for bi in range(nb):
        k_old, v_old = _tile_kv_heads(kc_ref[bi]), _tile_kv_heads(vc_ref[bi])
        for kvh in range(KVH_B):
            kband_ref[kvh, bi, 0:wc, :] = k_old[kvh]
            vband_ref[kvh, bi, 0:wc, :] = v_old[kvh]
            kband_ref[kvh, bi, wc:kb, :] = k_new[kvh][bi * c:(bi + 1) * c, :]
            vband_ref[kvh, bi, wc:kb, :] = v_new[kvh][bi * c:(bi + 1) * c, :]

    w_gu = w_gu_ref[...]
    b_gate = b_gate_ref[...]
    g_out = g_out_ref[...]

    states = [[s_in_ref[bi]] for bi in range(nb)]

    def seq(bi):
        return _ab_chunk(bi * c, c, kb, proj_ref, qn_ref, o_ref, states[bi],
                         lambda kvh: kband_ref[kvh, bi], lambda kvh: vband_ref[kvh, bi],
                         bias_ref, sink_ref, w_gu, b_gate, g_out, None)

    _interleave(seq(bi) for bi in range(nb))
    for bi in range(nb):
        s_out_ref[bi] = states[bi][0]
    y_ref[...] = x + jnp.dot(o_ref[...], w_out_ref[...], preferred_element_type=F32)


GROUP_W = D_INNER_C // G_C


def _c_chunk(r, c, io, state, e_s, e_p, neg_mask, bd_mask, dskip, g_y):
    dtc = io.dt(r)
    acum = _sel_dot_left(_tril(c).astype(BF16), io.da(r), 3) * LOG2E
    yield acum
    lhs = jnp.concatenate([jnp.concatenate(_split_bf16(acum, 2), axis=1),
                           jnp.concatenate(_split_bf16(dtc, 2), axis=1)], axis=0)
    both_p = jnp.dot(lhs, e_p, preferred_element_type=F32)
    xa_p, dt_p = both_p[:c], both_p[c:]
    xa_s = xa_p if c == P_C else jnp.dot(lhs[:c], e_s, preferred_element_type=F32)
    yield dt_p
    acum_t = acum.T
    al_p = xa_p[c - 1:c, :]
    dec = jnp.exp2(al_p)
    gs = HPG_C * c
    bg, cg, cb, wmat, xdt_bf, xw = [], [], [], [], [], []
    for g in range(G_C):
        sl = slice(g * GROUP_W, (g + 1) * GROUP_W)
        a_row = jnp.concatenate([acum_t[h:h + 1, :] for h in range(g * HPG_C, (g + 1) * HPG_C)], axis=1)
        wmat.append(jnp.exp2((xa_s[:, g * gs:(g + 1) * gs] - a_row) + neg_mask[:, g * gs:(g + 1) * gs]))
        xdt = io.x(r, g) * dt_p[:, sl]
        xdt_bf.append(xdt.astype(BF16))
        xw.append((xdt * jnp.exp2(al_p[:, sl] - xa_p[:, sl])).astype(BF16))
        bg.append(io.b(r, g).astype(BF16))
        cg.append(io.c(r, g).astype(BF16))
        cb.append(_dot_nt(cg[g], jnp.concatenate([bg[g]] * HPG_C, axis=0)))
    yield cb[-1]
    ys, upd = [], []
    for g in range(G_C):
        mg = (cb[g] * wmat[g]).astype(BF16)
        for j in range(2):
            bd = jnp.concatenate([xdt_bf[g][:, j * 256:(j + 1) * 256]] * 4, axis=0) * bd_mask
            ys.append(jnp.dot(mg[:, j * 4 * c:(j + 1) * 4 * c], bd, preferred_element_type=F32))
        upd.append(_dot_tn(bg[g], xw[g]))
    yield upd[-1]
    st_prev = state[0]
    y_inter, st_new = [], []
    for g in range(G_C):
        sl = slice(g * GROUP_W, (g + 1) * GROUP_W)
        y_inter.append(_dot(cg[g], st_prev[:, sl]))
        st_new.append(st_prev[:, sl] * dec[:, sl] + upd[g])
    state[0] = jnp.concatenate(st_new, axis=1)
    yield st_new[-1]
    for g in range(G_C):
        sl = slice(g * GROUP_W, (g + 1) * GROUP_W)
        y = jnp.concatenate(ys[2 * g:2 * g + 2], axis=1) + y_inter[g] * jnp.exp2(xa_p[:, sl])
        y = y + dskip[:, sl] * io.x(r, g)
        y = y * io.gate(r, g)
        io.put_o(r, g, (_rms_rows(y) * g_y[:, sl]).astype(BF16))


def _c_dt(dt_cols, dt_bias, a_log):
    dt = _softplus(dt_cols + dt_bias)
    return dt, dt * (-jnp.exp(a_log))


CONV_PITCH = PROMPT_BLOCK // 8 + 1
CONV_ROWS = 8 * CONV_PITCH
CONV_TILES = CONV_DIM // LANES
C_ROUNDS = 12
C_IN_SLABS = tuple((C_X + 512 * k, C_X + 512 * (k + 1)) for k in range(6)) + tuple(
    (C_Z + 512 * k, C_Z + 512 * (k + 1)) for k in range(4)) + ((C_DT, C_COLS),)
C_MIX_FIRST_ROUND = 5
C_OUT_ROUNDS = (1, 3, 7, 11)
assert len(C_IN_SLABS) <= C_ROUNDS and sum(hi - lo for lo, hi in C_IN_SLABS) == C_COLS


def _conv_tile(ubuf_ref, act_ref, j, conv_w, conv_b):
    w = [jnp.broadcast_to(conv_w[i:i + 1, j * LANES:(j + 1) * LANES], (8, LANES)) for i in range(CONV_W)]
    b = jnp.broadcast_to(conv_b[:, j * LANES:(j + 1) * LANES], (8, LANES))
    for a in range(CONV_PITCH):
        acc = b
        for i in range(CONV_W):
            acc = acc + w[i] * ubuf_ref[j, pl.ds(8 - (CONV_W - 1) + i + a, 8, stride=CONV_PITCH), :]
        act_ref[j, pl.ds(a, 8, stride=CONV_PITCH), :] = _silu(acc)


def _c_prompt_kernel(xin_ref, xres_ref, rms_g_ref, w_in_ref, conv_w_ref, conv_b_ref, dt_bias_ref,
                     a_log_ref, dskip_ref, g_y_ref, w_out_ref, e_s_ref, neg_mask_ref, bd_mask_ref,
                     y_ref, st_out_ref, conv_out_ref,
                     z_ref, ubuf_ref, act_ref, dt_ref, da_ref, o_ref, st_ref):
    t = xin_ref.shape[0]
    c = CHUNK
    s = pl.program_id(0)
    n_blocks = pl.num_programs(0) - 2

    @pl.when(s == 0)
    def _():
        o_ref[1] = jnp.zeros(o_ref.shape[1:], BF16)
        ubuf_ref[...] = jnp.zeros_like(ubuf_ref)

    @pl.when(s == 0)
    def _():
        st_ref[...] = jnp.zeros_like(st_ref)

    def in_stage(slot_in):
        h = (_rms_rows(xin_ref[...]) * rms_g_ref[...]).astype(BF16)
        conv_w = conv_w_ref[...]
        conv_b = conv_b_ref[...]
        act = act_ref.at[slot_in]
        for rnd, (lo, hi) in enumerate(C_IN_SLABS):
            if rnd:
                yield
            slab = jnp.dot(h, w_in_ref[:, lo:hi], preferred_element_type=F32)
            if lo == C_DT:
                dt, da = _c_dt(slab, dt_bias_ref[...], a_log_ref[...])
                dt_ref[slot_in] = dt
                da_ref[slot_in] = da
            elif lo >= C_X:
                for j in range((lo - C_X) // LANES, (hi - C_X) // LANES):
                    col = C_X + j * LANES - lo
                    ubuf_ref[j, 8:16, :] = ubuf_ref[j, CONV_ROWS:CONV_ROWS + 8, :]
                    ubuf_ref[j, 16:16 + t, :] = slab[:, col:col + LANES]
                    _conv_tile(ubuf_ref, act, j, conv_w, conv_b)
            else:
                z_ref[slot_in, :, lo:hi] = _silu(slab)

    n_x = D_INNER_C // LANES
    n_g = N_C // LANES

    def mix_stage(slot_mix):
        def put_o(r, g, value):
            o_ref[slot_mix, pl.ds(r, c), g * GROUP_W:(g + 1) * GROUP_W] = value

        def act_rows(r, j):
            return act_ref[slot_mix, j, pl.ds(8 + r, c), :]

        io = types.SimpleNamespace(
            dt=lambda r: dt_ref[slot_mix, pl.ds(r, c), :],
            da=lambda r: da_ref[slot_mix, pl.ds(r, c), :],
            x=lambda r, g: jnp.concatenate(
                [act_rows(r, g * (n_x // G_C) + j) for j in range(n_x // G_C)], axis=1),
            b=lambda r, g: act_rows(r, n_x + g * n_g),
            c=lambda r, g: act_rows(r, n_x + (G_C + g) * n_g),
            gate=lambda r, g: z_ref[slot_mix, pl.ds(r, c), C_Z + g * GROUP_W:C_Z + (g + 1) * GROUP_W],
            put_o=put_o)
        for _ in range(C_MIX_FIRST_ROUND):
            yield
        state = [st_ref[...]]
        e_s = e_s_ref[...]
        chunks = [_c_chunk(i * c, c, io, state, e_s, e_s, neg_mask_ref[...], bd_mask_ref[...],
                           dskip_ref[...], g_y_ref[...]) for i in range(t // c)]
        chunks = _advance(chunks)
        while chunks:
            yield
            chunks = _advance(chunks)
        st_ref[...] = state[0]

    def out_stage(slot_in):
        quarter = 0
        for rnd in range(C_ROUNDS):
            if rnd:
                yield
            if rnd in C_OUT_ROUNDS:
                cols = slice(quarter * 256, (quarter + 1) * 256)
                y_ref[:, cols] = xres_ref[:, cols] + jnp.dot(o_ref[slot_in], w_out_ref[:, cols],
                                                             preferred_element_type=F32)
                quarter += 1

    slot_in = s % 2

    @pl.when(s == 0)
    def _():
        _interleave([in_stage(slot_in)])

    @pl.when(jnp.logical_and(s > 0, s <= n_blocks))
    def _():
        _interleave([in_stage(slot_in), mix_stage(1 - slot_in), out_stage(slot_in)])

    @pl.when(s == n_blocks + 1)
    def _():
        _interleave([out_stage(slot_in)])

    @pl.when(s == n_blocks)
    def _():
        st_out_ref[...] = st_ref[...].T
        for j in range(CONV_TILES):
            conv_out_ref[:, j * LANES:(j + 1) * LANES] = ubuf_ref[j, CONV_ROWS:CONV_ROWS + 8, :]


def _conv_rows(ubuf, nrows, conv_w, conv_b):
    acc = conv_b
    for i in range(CONV_W):
        acc = acc + conv_w[i:i + 1, :] * ubuf[pl.ds(8 - (CONV_W - 1) + i, nrows), :]
    return _silu(acc)


def _c_sample_kernel(x_ref, rms_g_ref, w_in_ref, conv_w_ref, conv_b_ref, dt_bias_ref, a_log_ref,
                     dskip_ref, g_y_ref, w_out_ref, e_s_ref, e_p_ref, neg_mask_ref, bd_mask_ref,
                     st_in_ref, conv_in_ref,
                     y_ref, st_out_ref, conv_out_ref,
                     proj_ref, ubuf_ref, act_ref, dt_ref, da_ref, o_ref):
    nb = st_in_ref.shape[0]
    t = x_ref.shape[0]
    c = t // nb
    x = x_ref[...]
    h = (_rms_rows(x) * rms_g_ref[...]).astype(BF16)
    proj_ref[...] = jnp.dot(h, w_in_ref[...], preferred_element_type=F32)
    dt, da = _c_dt(proj_ref[:, C_DT:C_DT + LANES], dt_bias_ref[...], a_log_ref[...])
    dt_ref[...] = dt
    da_ref[...] = da
    conv_w = conv_w_ref[...]
    conv_b = conv_b_ref[...]
    ubuf_ref[0:8, :] = jnp.zeros((8, CONV_DIM), F32)
    for bi in range(nb):
        ubuf_ref[8 - (CONV_W - 1):8, :] = conv_in_ref[bi]
        ubuf_ref[8:8 + c, :] = proj_ref[bi * c:(bi + 1) * c, C_X:C_X + CONV_DIM]
        act_ref[bi * c:(bi + 1) * c, :] = _conv_rows(ubuf_ref, c, conv_w, conv_b)
        conv_out_ref[bi] = ubuf_ref[8 + c - (CONV_W - 1):8 + c, :]

    def put_o(r, g, value):
        o_ref[pl.ds(r, c), g * GROUP_W:(g + 1) * GROUP_W] = value

    io = types.SimpleNamespace(
        dt=lambda r: dt_ref[pl.ds(r, c), :],
        da=lambda r: da_ref[pl.ds(r, c), :],
        x=lambda r, g: act_ref[pl.ds(r, c), g * GROUP_W:(g + 1) * GROUP_W],
        b=lambda r, g: act_ref[pl.ds(r, c), D_INNER_C + g * N_C:D_INNER_C + (g + 1) * N_C],
        c=lambda r, g: act_ref[pl.ds(r, c), D_INNER_C + (G_C + g) * N_C:D_INNER_C + (G_C + g + 1) * N_C],
        gate=lambda r, g: _silu(proj_ref[pl.ds(r, c), C_Z + g * GROUP_W:C_Z + (g + 1) * GROUP_W]),
        put_o=put_o)
    e_s = e_s_ref[...]
    e_p = e_p_ref[...]
    neg_mask = neg_mask_ref[...]
    bd_mask = bd_mask_ref[...]
    dskip = dskip_ref[...]
    g_y = g_y_ref[...]

    states = [[st_in_ref[bi].T] for bi in range(nb)]
    _interleave(_c_chunk(bi * c, c, io, states[bi], e_s, e_p, neg_mask, bd_mask, dskip, g_y)
                for bi in range(nb))
    for bi in range(nb):
        st_out_ref[bi] = states[bi][0].T
    y_ref[...] = x + jnp.dot(o_ref[...], w_out_ref[...], preferred_element_type=F32)


def _bucket_table(q_off, k_off):
    n = q_off[:, None] - k_off[None, :]
    half = N_BUCKETS // 2
    max_exact = half // 2
    side = np.where(n < 0, half, 0)
    n = np.abs(n)
    nf = np.maximum(n, max_exact).astype(np.float32)
    large = max_exact + (np.log(nf / np.float32(max_exact)) / np.float32(math.log(MAX_DISTANCE / max_exact))
                         * np.float32(half - max_exact)).astype(np.int32)
    large = np.minimum(large, half - 1)
    return side + np.where(n < max_exact, n, large)


def _bias_rows(rel_bias, q_off, k_off):
    bucket = _bucket_table(q_off, k_off)
    onehot = jnp.asarray(np.eye(N_BUCKETS, dtype=np.float32)[bucket])
    bias = jnp.einsum('qkb,bh->hqk', onehot, rel_bias.astype(F32), precision=lax.Precision.HIGHEST) * LOG2E
    return bias.reshape(KVH_B, G_B * q_off.shape[0], k_off.shape[0])


def _head_expand_matrix(per_head):
    m = np.zeros((LANES, H_C * per_head), np.float32)
    for h in range(H_C):
        m[h, h * per_head:(h + 1) * per_head] = 1.0
    return jnp.asarray(np.concatenate([m, m], axis=0), BF16)


def _full(shape):
    return pl.BlockSpec(shape, lambda *_: (0,) * len(shape))


def _const(shape):
    return pl.BlockSpec(shape, lambda *_: (0,) * len(shape), pipeline_mode=pl.Buffered(1))


CAST_ROWS = 256


def _cast_columns(wt_ref, w_bf_ref, src, dst, n):
    for off in range(0, n - n % LANES, CAST_ROWS):
        rows = min(CAST_ROWS, n - n % LANES - off)
        w_bf_ref[:, dst + off:dst + off + rows] = wt_ref[src + off:src + off + rows, :].T.astype(BF16)
    rest = n % LANES
    if rest:
        off = n - rest
        tail = jnp.concatenate([wt_ref[src + off:src + n, :], jnp.zeros((LANES - rest, wt_ref.shape[1]), F32)],
                               axis=0)
        w_bf_ref[:, dst + off:dst + off + LANES] = tail.T.astype(BF16)


def _ab_cast_kernel(wt_ref, w_out_ref, w_in_bf_ref, w_out_bf_ref):
    g0 = A_Z
    _cast_columns(wt_ref, w_in_bf_ref, 0, 0, g0)
    _cast_columns(wt_ref, w_in_bf_ref, g0 + GATE_RANK_A, g0, A_G - g0)
    _cast_columns(wt_ref, w_in_bf_ref, g0, A_G, GATE_RANK_A)
    w_out_bf_ref[...] = w_out_ref[...].astype(BF16)


def _c_cast_kernel(wt_ref, w_out_ref, w_in_bf_ref, w_out_bf_ref):
    _cast_columns(wt_ref, w_in_bf_ref, 0, 0, wt_ref.shape[0])
    w_out_bf_ref[...] = w_out_ref[...].astype(BF16)


def _cast_weights(body, w_in, w_out, cols, name):
    vm = pl.BlockSpec(memory_space=pltpu.VMEM)
    return pl.pallas_call(
        body, in_specs=[vm, vm], out_specs=(vm, vm),
        out_shape=(jax.ShapeDtypeStruct((w_in.shape[0], cols), BF16),
                   jax.ShapeDtypeStruct(w_out.shape, BF16)),
        name=name, compiler_params=pltpu.CompilerParams(vmem_limit_bytes=VMEM_LIMIT),
    )(w_in.T, w_out)


def _ab_weights(rms_g, w_in, w_gate_up, b_gate, g_out, g_q, g_k, w_out):
    w_in_r, w_out_bf = _cast_weights(_ab_cast_kernel, w_in, w_out, AB_COLS, "ab_cast")
    w_gu = jnp.concatenate([w_gate_up, jnp.zeros((LANES - GATE_RANK_A, H_A * DK_A), w_gate_up.dtype)],
                           axis=0).astype(BF16)
    return (rms_g.reshape(1, D_MODEL), w_in_r, w_gu, b_gate.reshape(1, -1), g_out.reshape(1, DV_A),
            jnp.tile(g_q, H_B).reshape(1, -1), jnp.tile(g_k, KVH_B).reshape(1, -1), w_out_bf)


def _ab_prompt(x, sink, rel_bias, weights):
    rms_g, w_in, w_gu, b_gate, g_out, gq, gk, w_out = weights
    length = x.shape[0]
    t = AB_PROMPT_BLOCK
    c = CHUNK
    kb = WINDOW + c
    bias = _bias_rows(rel_bias, np.arange(c), np.arange(kb) - WINDOW)
    row_blk = lambda i: (i, 0)
    in_specs = [
        pl.BlockSpec(memory_space=pltpu.SMEM),
        pl.BlockSpec((t, D_MODEL), row_blk),
        _const((1, D_MODEL)), _const((D_MODEL, AB_COLS)), _const((LANES, 256)), _const((1, 256)),
        _const((1, DV_A)), _const((1, 512)), _const((1, 128)), _const((KVH_B, G_B * c, kb)),
        _const((D_MODEL, D_MODEL)),
    ]
    out_shape = (jax.ShapeDtypeStruct((length, D_MODEL), F32),
                 jax.ShapeDtypeStruct((H_A * DK_A, DV_A), F32),
                 jax.ShapeDtypeStruct((WINDOW, KVH_B * HD_B), F32),
                 jax.ShapeDtypeStruct((WINDOW, KVH_B * HD_B), F32))
    out_specs = (pl.BlockSpec((t, D_MODEL), row_blk), _full((H_A * DK_A, DV_A)),
                 _full((WINDOW, KVH_B * HD_B)), _full((WINDOW, KVH_B * HD_B)))
    scratch = [pltpu.VMEM((t, AB_COLS), F32), pltpu.VMEM((t, 512), F32), pltpu.VMEM((t, D_MODEL), BF16),
               pltpu.VMEM((KVH_B, t + WINDOW, 256), BF16), pltpu.VMEM((KVH_B, t + WINDOW, 256), BF16),
               pltpu.VMEM((H_A * DK_A, DV_A), F32)]
    return pl.pallas_call(
        _ab_prompt_kernel, grid=(length // t,), in_specs=in_specs, out_specs=out_specs,
        out_shape=out_shape, scratch_shapes=scratch, name="ab_prompt",
        compiler_params=pltpu.CompilerParams(dimension_semantics=("arbitrary",),
                                             vmem_limit_bytes=VMEM_LIMIT),
    )(sink, x, rms_g, w_in, w_gu, b_gate, g_out, gq, gk, bias, w_out)


def _ab_sample(x, sink, rel_bias, weights, s_in, k_cache, v_cache):
    rms_g, w_in, w_gu, b_gate, g_out, gq, gk, w_out = weights
    nb, c, _ = x.shape
    wc = k_cache.shape[1]
    kb = wc + c
    t = nb * c
    bias = _bias_rows(rel_bias, np.arange(c), np.arange(kb) - wc)
    vm = pl.BlockSpec(memory_space=pltpu.VMEM)
    in_specs = [pl.BlockSpec(memory_space=pltpu.SMEM)] + [vm] * 13
    out_shape = (jax.ShapeDtypeStruct((t, D_MODEL), F32),
                 jax.ShapeDtypeStruct((nb, H_A * DK_A, DV_A), F32),
                 jax.ShapeDtypeStruct((t, KVH_B * HD_B), F32),
                 jax.ShapeDtypeStruct((t, KVH_B * HD_B), F32))
    scratch = [pltpu.VMEM((t, AB_COLS), F32), pltpu.VMEM((t, 512), F32), pltpu.VMEM((t, D_MODEL), BF16),
               pltpu.VMEM((KVH_B, nb, kb, 256), BF16), pltpu.VMEM((KVH_B, nb, kb, 256), BF16)]
    return pl.pallas_call(
        _ab_sample_kernel, in_specs=in_specs, out_specs=(vm,) * 4, out_shape=out_shape,
        scratch_shapes=scratch, name="ab_sample",
        compiler_params=pltpu.CompilerParams(vmem_limit_bytes=VMEM_LIMIT),
    )(sink, x.reshape(t, D_MODEL), rms_g, w_in, w_gu, b_gate, g_out, gq, gk, bias, w_out,
      s_in.reshape(nb, H_A * DK_A, DV_A), k_cache.reshape(nb, wc, KVH_B * HD_B),
      v_cache.reshape(nb, wc, KVH_B * HD_B))


def _c_weights(rms_g, w_in, conv_w, conv_b, dt_bias, a_log, d_skip, g_y, w_out):
    w_in_p, w_out_bf = _cast_weights(_c_cast_kernel, w_in, w_out, C_COLS, "c_cast")
    pad = lambda v: jnp.concatenate([v, jnp.zeros((LANES - H_C,), v.dtype)]).reshape(1, LANES)
    return (rms_g.reshape(1, D_MODEL), w_in_p, conv_w, conv_b.reshape(1, CONV_DIM), pad(dt_bias), pad(a_log),
            jnp.repeat(d_skip, P_C).reshape(1, D_INNER_C), g_y.reshape(1, D_INNER_C), w_out_bf)


def _ssd_masks(c):
    tok = np.arange(c)[:, None]
    src = np.arange(H_C * c)[None, :] % c
    neg = np.where(src <= tok, 0.0, -np.inf).astype(np.float32)
    blk = (np.arange(4 * c)[:, None] // c) == (np.arange(256)[None, :] // P_C)
    return jnp.asarray(neg), jnp.asarray(blk, BF16)


def _c_prompt(x, weights):
    length = x.shape[0]
    t = PROMPT_BLOCK
    c = CHUNK
    n_blocks = length // t
    in_blk = lambda s: (jnp.minimum(s, n_blocks - 1), 0)
    out_blk = lambda s: (jnp.clip(s - 2, 0, n_blocks - 1), 0)
    in_specs = [
        pl.BlockSpec((t, D_MODEL), in_blk), pl.BlockSpec((t, D_MODEL), out_blk),
        _const((1, D_MODEL)), _const((D_MODEL, C_COLS)), _const((CONV_W, CONV_DIM)), _const((1, CONV_DIM)),
        _const((1, LANES)), _const((1, LANES)), _const((1, D_INNER_C)), _const((1, D_INNER_C)),
        _const((D_INNER_C, D_MODEL)), _const((2 * LANES, H_C * c)),
        _const((c, H_C * c)), _const((4 * c, 256)),
    ]
    out_shape = (jax.ShapeDtypeStruct((length, D_MODEL), F32),
                 jax.ShapeDtypeStruct((H_C * P_C, N_C), F32),
                 jax.ShapeDtypeStruct((8, CONV_DIM), F32))
    out_specs = (pl.BlockSpec((t, D_MODEL), out_blk), _full((H_C * P_C, N_C)), _full((8, CONV_DIM)))
    scratch = [pltpu.VMEM((2, t, D_INNER_C), F32), pltpu.VMEM((CONV_TILES, CONV_ROWS + 8, LANES), F32),
               pltpu.VMEM((2, CONV_TILES, CONV_ROWS, LANES), F32), pltpu.VMEM((2, t, LANES), F32),
               pltpu.VMEM((2, t, LANES), F32), pltpu.VMEM((2, t, D_INNER_C), BF16),
               pltpu.VMEM((N_C, H_C * P_C), F32)]
    return pl.pallas_call(
        _c_prompt_kernel, grid=(n_blocks + 2,), in_specs=in_specs, out_specs=out_specs,
        out_shape=out_shape, scratch_shapes=scratch, name="c_prompt",
        compiler_params=pltpu.CompilerParams(dimension_semantics=("arbitrary",),
                                             vmem_limit_bytes=VMEM_LIMIT),
    )(x, x, *weights, _head_expand_matrix(c), *_ssd_masks(c))


def _c_sample(x, weights, st_in, conv_in):
    nb, c, _ = x.shape
    t = nb * c
    vm = pl.BlockSpec(memory_space=pltpu.VMEM)
    out_shape = (jax.ShapeDtypeStruct((t, D_MODEL), F32),
                 jax.ShapeDtypeStruct((nb, H_C * P_C, N_C), F32),
                 jax.ShapeDtypeStruct((nb, CONV_W - 1, CONV_DIM), F32))
    scratch = [pltpu.VMEM((t, C_COLS), F32), pltpu.VMEM((c + 8, CONV_DIM), F32),
               pltpu.VMEM((t, CONV_DIM), F32), pltpu.VMEM((t, LANES), F32), pltpu.VMEM((t, LANES), F32),
               pltpu.VMEM((t, D_INNER_C), BF16)]
    return pl.pallas_call(
        _c_sample_kernel, in_specs=[vm] * 16, out_specs=(vm,) * 3, out_shape=out_shape,
        scratch_shapes=scratch, name="c_sample",
        compiler_params=pltpu.CompilerParams(vmem_limit_bytes=VMEM_LIMIT),
    )(x.reshape(t, D_MODEL), *weights, _head_expand_matrix(c), _head_expand_matrix(P_C), *_ssd_masks(c),
      st_in.reshape(nb, H_C * P_C, N_C), conv_in)


def kernel(x_prompt, x_sample, cache_swa_k, cache_swa_v, state_gla, state_ssd, state_conv, rms_g, w_in_ab, w_gate_up_a, b_gate_a, g_out_a, g_q_b, g_k_b, sink_b, rel_bias, w_out_ab, w_in_c, conv_w_c, conv_b_c, dt_bias_c, a_log_c, d_skip_c, g_y_c, w_out_c):
    bp, seq_len, _ = x_prompt.shape
    nb, dec_len, _ = x_sample.shape
    assert bp == 1 and seq_len % PROMPT_BLOCK == 0 and seq_len % AB_PROMPT_BLOCK == 0 and seq_len >= WINDOW
    wab = _ab_weights(rms_g[0], w_in_ab[0], w_gate_up_a[0], b_gate_a[0], g_out_a[0], g_q_b[0], g_k_b[0],
                      w_out_ab[0])
    yp, gla_p, k_p, v_p = _ab_prompt(x_prompt[0], sink_b[0], rel_bias, wab)
    ys, gla_s, k_s, v_s = _ab_sample(x_sample, sink_b[0], rel_bias, wab, state_gla[0], cache_swa_k[0],
                                     cache_swa_v[0])
    wc = _c_weights(rms_g[1], w_in_c[0], conv_w_c[0], conv_b_c[0], dt_bias_c[0], a_log_c[0], d_skip_c[0],
                    g_y_c[0], w_out_c[0])
    yp, ssd_p, conv_p = _c_prompt(yp, wc)
    ys, ssd_s, conv_s = _c_sample(ys.reshape(nb, dec_len, D_MODEL), wc, state_ssd[0], state_conv[0])
    return (
        yp.reshape(1, seq_len, D_MODEL),
        ys.reshape(nb, dec_len, D_MODEL),
        gla_p.reshape(1, 1, H_A, DK_A, DV_A),
        k_p.reshape(1, 1, WINDOW, KVH_B, HD_B),
        v_p.reshape(1, 1, WINDOW, KVH_B, HD_B),
        ssd_p.reshape(1, 1, H_C, P_C, N_C),
        conv_p[8 - (CONV_W - 1):].reshape(1, 1, CONV_W - 1, CONV_DIM),
        gla_s.reshape(1, nb, H_A, DK_A, DV_A),
        k_s.reshape(1, nb, dec_len, KVH_B, HD_B),
        v_s.reshape(1, nb, dec_len, KVH_B, HD_B),
        ssd_s.reshape(1, nb, H_C, P_C, N_C),
        conv_s.reshape(1, nb, CONV_W - 1, CONV_DIM),
    )
```
